```python
import math
import jax, jax.numpy as jnp
from jax import lax
import numpy as np

D_MODEL = 1024
BATCH = 8
SEQ = 4096
DEPTH = 4

MLA_HEADS = 8
Q_RANK = 384
KV_RANK = 256
NOPE_DIM = 128
ROPE_DIM = 64
V_DIM = 128
ROPE_THETA = 10000.0
Q_BLOCK = 128
SSM_HEADS = 32
SSM_HEAD_DIM = 64
D_INNER = SSM_HEADS * SSM_HEAD_DIM
SSM_GROUPS = 8
D_STATE = 128
CONV_K = 4
SSD_CHUNK = 256
CONV_CH = D_INNER + 2 * SSM_GROUPS * D_STATE
XA_HEADS = 4
XA_HEAD_DIM = 256
MEM_TOKENS = 256
N_BRANCHES = 3
N_EXPERTS = 16
N_EXPERT_GROUPS = 4
EXPERTS_PER_GROUP = N_EXPERTS // N_EXPERT_GROUPS
TOP_K = 2
D_EXPERT = 512
MOE_BLOCK = 128
DN_ALPHA = (2 * DEPTH) ** 0.25
DN_BETA = (8 * DEPTH) ** -0.25
NORM_EPS = 1e-5
RMS_EPS = 1e-6
IN_SIZES = (Q_RANK, KV_RANK + ROPE_DIM, D_INNER, CONV_CH, SSM_HEADS, XA_HEADS * XA_HEAD_DIM, N_BRANCHES * D_MODEL)
N_IN = Q_RANK + KV_RANK + ROPE_DIM + D_INNER + CONV_CH + SSM_HEADS + XA_HEADS * XA_HEAD_DIM + N_BRANCHES * D_MODEL

kernel_name = "hybrid_mla_ssd_memxattn_grouped_moe_deepnorm"

F32 = jnp.float32


def layer_norm(x, g, b):
    xf = x.astype(F32)
    mu = xf.mean(-1, keepdims=True)
    var = jnp.square(xf - mu).mean(-1, keepdims=True)
    return ((xf - mu) * lax.rsqrt(var + NORM_EPS) * g + b).astype(x.dtype)


def rms_norm(x, g):
    xf = x.astype(F32)
    return (xf * lax.rsqrt(jnp.mean(xf * xf, -1, keepdims=True) + RMS_EPS) * g).astype(x.dtype)


def rope_tables(positions):
    inv = ROPE_THETA ** (-jnp.arange(0, ROPE_DIM, 2, dtype=F32) / ROPE_DIM)
    ang = positions.astype(F32)[..., None] * inv
    return jnp.cos(ang), jnp.sin(ang)


def apply_rope(t, cos, sin):
    t1, t2 = jnp.split(t.astype(F32), 2, axis=-1)
    return jnp.concatenate([t1 * cos - t2 * sin, t1 * sin + t2 * cos], axis=-1).astype(t.dtype)


def mla_branch(dq, dkv, cos, sin, q_norm, w_uq, kv_norm, w_ukv):
    bsz, seq, _ = dq.shape
    c_q = rms_norm(dq, q_norm)
    q = (c_q @ w_uq).reshape(bsz, seq, MLA_HEADS, NOPE_DIM + ROPE_DIM)
    q_nope, q_rope = q[..., :NOPE_DIM], q[..., NOPE_DIM:]
    q_rope = apply_rope(q_rope, cos[:, :, None], sin[:, :, None])
    c_kv = rms_norm(dkv[..., :KV_RANK], kv_norm)
    k_rope = apply_rope(dkv[..., KV_RANK:], cos, sin)
    w_uk, w_uv = w_ukv[..., :NOPE_DIM], w_ukv[..., NOPE_DIM:]
    q_lat = jnp.einsum('bshn,chn->bshc', q_nope, w_uk)
    nb = seq // Q_BLOCK

    def to_blocks(t):
        return jnp.moveaxis(t.reshape(bsz, nb, Q_BLOCK, *t.shape[2:]), 1, 0)

    key_pos = jnp.arange(seq)
    scale = (NOPE_DIM + ROPE_DIM) ** -0.5

    def attend_block(args):
        ql, qr, blk = args
        s = (jnp.einsum('bqhc,bkc->bhqk', ql, c_kv, preferred_element_type=F32)
             + jnp.einsum('bqhr,bkr->bhqk', qr, k_rope, preferred_element_type=F32))
        q_pos = blk * Q_BLOCK + jnp.arange(Q_BLOCK)
        causal = key_pos[None, :] <= q_pos[:, None]
        p = jax.nn.softmax(jnp.where(causal, s * scale, -jnp.inf), axis=-1).astype(c_kv.dtype)
        return jnp.einsum('bhqk,bkc->bqhc', p, c_kv)

    o_lat = lax.map(attend_block, (to_blocks(q_lat), to_blocks(q_rope), jnp.arange(nb)))
    o_lat = jnp.moveaxis(o_lat, 0, 1).reshape(bsz, seq, MLA_HEADS, KV_RANK)
    o = jnp.einsum('bshc,chv->bshv', o_lat, w_uv)
    return o.reshape(bsz, seq, MLA_HEADS * V_DIM)


def causal_depthwise_conv(u, w, b):
    out = lax.conv_general_dilated(u, w[:, None, :].astype(u.dtype), window_strides=(1,),
                                   padding=[(CONV_K - 1, 0)], dimension_numbers=('NWC', 'WIO', 'NWC'),
                                   feature_group_count=u.shape[-1])
    return out + b


def ssd_chunked_scan(xs, dt, a, bm, cm):
    bsz, seq, nh, hp = xs.shape
    rep = nh // SSM_GROUPS
    pad = (-seq) % SSD_CHUNK
    nc = (seq + pad) // SSD_CHUNK

    def chunks(t):
        t = jnp.pad(t.astype(F32), [(0, 0), (0, pad)] + [(0, 0)] * (t.ndim - 2))
        return jnp.moveaxis(t.reshape(bsz, nc, SSD_CHUNK, *t.shape[2:]), 1, 0)

    x_dt = chunks((xs.astype(F32) * dt[..., None]).reshape(bsz, seq, SSM_GROUPS, rep, hp))
    da = chunks((dt * a).reshape(bsz, seq, SSM_GROUPS, rep))
    bc, cc = chunks(bm), chunks(cm)
    causal = jnp.tril(jnp.ones((SSD_CHUNK, SSD_CHUNK), dtype=bool))[None, :, :, None, None]

    def step(state, inp):
        xq, daq, bq, cq = inp
        acum = jnp.cumsum(daq, axis=1)
        seg = acum[:, :, None] - acum[:, None, :]
        decay = jnp.exp(jnp.where(causal, seg, -jnp.inf))
        cb = jnp.einsum('blgn,bsgn->blsg', cq, bq)
        y = jnp.einsum('blsg,blsgr,bsgrp->blgrp', cb, decay, xq)
        y = y + jnp.einsum('blgn,bgrpn,blgr->blgrp', cq, state, jnp.exp(acum))
        a_end = acum[:, -1]
        to_end = jnp.exp(a_end[:, None] - acum)
        state = state * jnp.exp(a_end)[..., None, None] + jnp.einsum('bsgn,bsgr,bsgrp->bgrpn', bq, to_end, xq)
        return state, y

    state0 = jnp.zeros((bsz, SSM_GROUPS, rep, hp, D_STATE), F32)
    _, y = lax.scan(step, state0, (x_dt, da, bc, cc))
    return jnp.moveaxis(y, 0, 1).reshape(bsz, nc * SSD_CHUNK, nh, hp)[:, :seq]


def ssd_branch(z, xbc, dt_raw, conv_w, conv_b, dt_bias, a_log, d_skip, norm_g):
    bsz, seq, _ = z.shape
    xbc = jax.nn.silu(causal_depthwise_conv(xbc, conv_w, conv_b))
    gn = SSM_GROUPS * D_STATE
    xs = xbc[..., :D_INNER].reshape(bsz, seq, SSM_HEADS, SSM_HEAD_DIM)
    bm = xbc[..., D_INNER:D_INNER + gn].reshape(bsz, seq, SSM_GROUPS, D_STATE)
    cm = xbc[..., D_INNER + gn:].reshape(bsz, seq, SSM_GROUPS, D_STATE)
    dt = jax.nn.softplus(dt_raw.astype(F32) + dt_bias.astype(F32))
    a = -jnp.exp(a_log.astype(F32))
    y = ssd_chunked_scan(xs, dt, a, bm, cm)
    y = y + xs.astype(F32) * d_skip.astype(F32)[:, None]
    y = y.reshape(bsz, seq, D_INNER) * jax.nn.silu(z.astype(F32))
    yg = y.reshape(bsz, seq, SSM_GROUPS, D_INNER // SSM_GROUPS)
    yg = yg * lax.rsqrt(jnp.mean(yg * yg, -1, keepdims=True) + RMS_EPS)
    return (yg.reshape(bsz, seq, D_INNER) * norm_g).astype(z.dtype)


def memory_branch(q_in, mem, w_mem_kv):
    bsz, seq, _ = q_in.shape
    q = q_in.reshape(bsz, seq, XA_HEADS, XA_HEAD_DIM)
    kv = (mem @ w_mem_kv).reshape(bsz, mem.shape[1], 2, XA_HEADS, XA_HEAD_DIM)
    k, v = kv[:, :, 0], kv[:, :, 1]
    s = jnp.einsum('bshd,bmhd->bhsm', q, k, preferred_element_type=F32) * (XA_HEAD_DIM ** -0.5)
    p = jax.nn.softmax(s, axis=-1).astype(v.dtype)
    return jnp.einsum('bhsm,bmhd->bshd', p, v).reshape(bsz, seq, XA_HEADS * XA_HEAD_DIM)


def moe_ffn(x, router_w, router_bias, w1, w3, w2):
    bsz, seq, d = x.shape
    scores = jax.nn.sigmoid(jnp.einsum('bsd,de->bse', x, router_w, preferred_element_type=F32))
    sel = scores + router_bias.astype(F32)
    grp = sel.reshape(bsz, seq, N_EXPERT_GROUPS, EXPERTS_PER_GROUP)
    group_score = lax.top_k(grp, TOP_K)[0].sum(-1)
    best = jnp.argmax(group_score, axis=-1)
    in_group = jnp.repeat(jax.nn.one_hot(best, N_EXPERT_GROUPS, dtype=F32), EXPERTS_PER_GROUP, axis=-1) > 0
    _, idx = lax.top_k(jnp.where(in_group, sel, -jnp.inf), TOP_K)
    w = jnp.take_along_axis(scores, idx, axis=-1)
    w = w / w.sum(-1, keepdims=True)
    gates = jnp.einsum('bsk,bske->bse', w, jax.nn.one_hot(idx, N_EXPERTS, dtype=F32)).astype(x.dtype)
    nb = seq // MOE_BLOCK

    def to_blocks(t):
        return jnp.moveaxis(t.reshape(bsz, nb, MOE_BLOCK, t.shape[-1]), 1, 0)

    def expert_block(args):
        xb, gb = args
        h = jax.nn.silu(jnp.einsum('bqd,edf->bqef', xb, w1)) * jnp.einsum('bqd,edf->bqef', xb, w3)
        return jnp.einsum('bqef,efd->bqd', h * gb[..., None], w2)

    out = lax.map(expert_block, (to_blocks(x), to_blocks(gates)))
    return jnp.moveaxis(out, 0, 1).reshape(bsz, seq, d)


def setup_inputs(seed: int = 0) -> dict:
    key = jax.random.key(seed)
    ks = jax.random.split(key, 32)
    L = DEPTH

    def normal(k, shape, fan_in, gain=1.0):
        return jax.random.normal(k, shape, F32) * (gain * fan_in ** -0.5)

    def gain_vec(k, shape):
        return 1.0 + 0.05 * jax.random.normal(k, shape, F32)

    def small(k, shape):
        return 0.02 * jax.random.normal(k, shape, F32)

    x = jax.random.normal(ks[0], (BATCH, SEQ, D_MODEL), F32)
    mem = jax.random.normal(ks[1], (BATCH, MEM_TOKENS, D_MODEL), F32)
    start = jax.random.randint(ks[2], (BATCH, 1), 0, 2048, dtype=jnp.int32)
    positions = start + jnp.arange(SEQ, dtype=jnp.int32)[None, :]
    dt0 = jnp.exp(jax.random.uniform(ks[11], (L, SSM_HEADS), F32, math.log(1e-3), math.log(1e-1)))
    dt_bias = dt0 + jnp.log(-jnp.expm1(-dt0))
    a_log = jnp.log(jax.random.uniform(ks[12], (L, SSM_HEADS), F32, 1.0, 16.0))
    return {
        "x": x,
        "mem": mem,
        "positions": positions,
        "w_in": normal(ks[3], (L, D_MODEL, N_IN), D_MODEL),
        "q_norm": gain_vec(ks[4], (L, Q_RANK)),
        "w_uq": normal(ks[5], (L, Q_RANK, MLA_HEADS * (NOPE_DIM + ROPE_DIM)), Q_RANK),
        "kv_norm": gain_vec(ks[6], (L, KV_RANK)),
        "w_ukv": normal(ks[7], (L, KV_RANK, MLA_HEADS, NOPE_DIM + V_DIM), KV_RANK),
        "w_proj_a": normal(ks[8], (L, MLA_HEADS * V_DIM, D_MODEL), MLA_HEADS * V_DIM, DN_BETA),
        "conv_w": normal(ks[9], (L, CONV_K, CONV_CH), CONV_K),
        "conv_b": small(ks[10], (L, CONV_CH)),
        "dt_bias": dt_bias,
        "a_log": a_log,
        "d_skip": 1.0 + 0.1 * jax.random.normal(ks[13], (L, SSM_HEADS), F32),
        "ssm_norm": gain_vec(ks[14], (L, D_INNER)),
        "w_proj_b": normal(ks[15], (L, D_INNER, D_MODEL), D_INNER, DN_BETA),
        "w_mem_kv": normal(ks[16], (L, D_MODEL, 2 * XA_HEADS * XA_HEAD_DIM), D_MODEL),
        "w_proj_c": normal(ks[17], (L, XA_HEADS * XA_HEAD_DIM, D_MODEL), XA_HEADS * XA_HEAD_DIM, DN_BETA),
        "w_out": normal(ks[18], (L, D_MODEL, D_MODEL), D_MODEL, DN_BETA),
        "ln1_g": gain_vec(ks[19], (L, D_MODEL)),
        "ln1_b": small(ks[20], (L, D_MODEL)),
        "router_w": normal(ks[21], (D_MODEL, N_EXPERTS), D_MODEL),
        "router_bias": 0.01 * jax.random.normal(ks[22], (N_EXPERTS,), F32),
        "exp_w1": normal(ks[23], (L, N_EXPERTS, D_MODEL, D_EXPERT), D_MODEL),
        "exp_w3": normal(ks[24], (L, N_EXPERTS, D_MODEL, D_EXPERT), D_MODEL),
        "exp_w2": normal(ks[25], (L, N_EXPERTS, D_EXPERT, D_MODEL), D_EXPERT, DN_BETA),
        "ln2_g": gain_vec(ks[26], (L, D_MODEL)),
        "ln2_b": small(ks[27], (L, D_MODEL)),
    }


def reference(x, mem, positions, w_in, q_norm, w_uq, kv_norm, w_ukv, w_proj_a, conv_w, conv_b,
              dt_bias, a_log, d_skip, ssm_norm, w_proj_b, w_mem_kv, w_proj_c, w_out, ln1_g, ln1_b,
              router_w, router_bias, exp_w1, exp_w3, exp_w2, ln2_g, ln2_b):
    cos, sin = rope_tables(positions)
    split_at = [int(i) for i in np.cumsum(IN_SIZES)[:-1]]
    for l in range(DEPTH):
        h = x @ w_in[l]
        dq, dkv, z, xbc, dt_raw, q_mem, g_raw = jnp.split(h, split_at, axis=-1)
        o_a = mla_branch(dq, dkv, cos, sin, q_norm[l], w_uq[l], kv_norm[l], w_ukv[l]) @ w_proj_a[l]
        o_b = ssd_branch(z, xbc, dt_raw, conv_w[l], conv_b[l], dt_bias[l], a_log[l], d_skip[l],
                         ssm_norm[l]) @ w_proj_b[l]
        o_c = memory_branch(q_mem, mem, w_mem_kv[l]) @ w_proj_c[l]
        g = jax.nn.sigmoid(g_raw.astype(F32)).astype(x.dtype)
        merged = (g[..., :D_MODEL] * o_a + g[..., D_MODEL:2 * D_MODEL] * o_b
                  + g[..., 2 * D_MODEL:] * o_c)
        x = layer_norm(DN_ALPHA * x + merged @ w_out[l], ln1_g[l], ln1_b[l])
        x = layer_norm(DN_ALPHA * x + moe_ffn(x, router_w, router_bias, exp_w1[l], exp_w3[l], exp_w2[l]),
                       ln2_g[l], ln2_b[l])
    return x
```

```python
import functools

import jax
import jax.numpy as jnp
from jax import lax
from jax.experimental import pallas as pl
from jax.experimental.pallas import tpu as pltpu

F32 = jnp.float32
BF16 = jnp.bfloat16
I32 = jnp.int32

MLA_HEADS = 8
Q_RANK = 384
KV_RANK = 256
NOPE_DIM = 128
ROPE_DIM = 64
V_DIM = 128
ROPE_THETA = 10000.0
SSM_HEADS = 32
SSM_HEAD_DIM = 64
SSM_GROUPS = 8
D_STATE = 128
CONV_K = 4
XA_HEADS = 4
XA_HEAD_DIM = 256
N_EXPERTS = 16
N_EXPERT_GROUPS = 4
EXPERTS_PER_GROUP = 4
TOP_K = 2
NORM_EPS = 1e-5
RMS_EPS = 1e-6

LANES = 128
V7X_VMEM_LIMIT = 56 * 1024 * 1024

MM_TM = 1024
MM_TN = 1024
PREP_TM = 512
ATT_TQ = 128
ATT_TK = 512
CONV_TM = 512
CONV_TC = 1024
CONV_HALO = 16
SSD_CHUNK = 256
XA_TQ = 512
MERGE_TM = 512
ROUTE_TM = 512
MOE_TM = 256
PERM_TM = 256

HA_W = 1024
HA_DQ = 0
HA_C = 384
HA_KR = 640
HA_KRS = 768
HA_DT = 896


def _cp(*sem):
    return pltpu.CompilerParams(dimension_semantics=sem, vmem_limit_bytes=V7X_VMEM_LIMIT)


def _sigmoid(x):
    return 1.0 / (1.0 + jnp.exp(-x))


def _mm_kernel(x_ref, w_ref, o_ref):
    o_ref[...] = jnp.dot(x_ref[...], w_ref[...], preferred_element_type=F32).astype(o_ref.dtype)


def matmul(x, w, out_dtype, name):
    m, k = x.shape
    n = w.shape[1]
    tm = min(MM_TM, m)
    tn = min(MM_TN, n)
    return pl.pallas_call(
        _mm_kernel,
        grid=(n // tn, m // tm),
        in_specs=[pl.BlockSpec((tm, k), lambda j, i: (i, 0)),
                  pl.BlockSpec((k, tn), lambda j, i: (0, j))],
        out_specs=pl.BlockSpec((tm, tn), lambda j, i: (i, j)),
        out_shape=jax.ShapeDtypeStruct((m, n), out_dtype),
        compiler_params=_cp("parallel", "parallel"),
        name=name,
    )(x, w)


def _mla_prep_kernel(ha_ref, cos_ref, sin_ref, qn_ref, kvn_ref, wq_ref, wuk_ref,
                     ql_ref, qr_ref, ck_ref, kr_ref, *, scale):
    ha = ha_ref[...]
    dq = ha[:, HA_DQ:HA_DQ + Q_RANK]
    c_q = dq * lax.rsqrt(jnp.mean(dq * dq, axis=-1, keepdims=True) + RMS_EPS) * qn_ref[...]
    q = jnp.dot(c_q.astype(BF16), wq_ref[...], preferred_element_type=F32)
    n_nope = MLA_HEADS * NOPE_DIM
    n_rope = MLA_HEADS * ROPE_DIM
    cosq = cos_ref[...]
    sinq = sin_ref[...]
    q_rope = q[:, n_nope:n_nope + n_rope] * cosq + q[:, n_nope + n_rope:] * sinq
    qr_ref[...] = (q_rope * scale).astype(BF16)
    for h in range(MLA_HEADS):
        qh = q[:, h * NOPE_DIM:(h + 1) * NOPE_DIM].astype(BF16)
        ql = jnp.dot(qh, wuk_ref[h], preferred_element_type=F32)
        ql_ref[:, h * KV_RANK:(h + 1) * KV_RANK] = (ql * scale).astype(BF16)
    c = ha[:, HA_C:HA_C + KV_RANK]
    c_kv = c * lax.rsqrt(jnp.mean(c * c, axis=-1, keepdims=True) + RMS_EPS) * kvn_ref[...]
    ck_ref[...] = c_kv.astype(BF16)
    k_rope = (ha[:, HA_KR:HA_KR + LANES] * cosq[:, :LANES]
              + ha[:, HA_KRS:HA_KRS + LANES] * sinq[:, :LANES])
    kr_ref[...] = k_rope[:, :ROPE_DIM].astype(BF16)


def mla_prep(ha, cosq, sinq, q_norm, kv_norm, wq, wuk_t):
    t = ha.shape[0]
    tm = min(PREP_TM, t)
    scale = float((NOPE_DIM + ROPE_DIM) ** -0.5)
    n_rope = MLA_HEADS * ROPE_DIM
    full = lambda shape: pl.BlockSpec(shape, lambda i: (0,) * len(shape))
    return pl.pallas_call(
        functools.partial(_mla_prep_kernel, scale=scale),
        grid=(t // tm,),
        in_specs=[pl.BlockSpec((tm, HA_W), lambda i: (i, 0)),
                  pl.BlockSpec((tm, n_rope), lambda i: (i, 0)),
                  pl.BlockSpec((tm, n_rope), lambda i: (i, 0)),
                  full((1, Q_RANK)), full((1, KV_RANK)),
                  full(wq.shape), full(wuk_t.shape)],
        out_specs=[pl.BlockSpec((tm, MLA_HEADS * KV_RANK), lambda i: (i, 0)),
                   pl.BlockSpec((tm, n_rope), lambda i: (i, 0)),
                   pl.BlockSpec((tm, KV_RANK), lambda i: (i, 0)),
                   pl.BlockSpec((tm, ROPE_DIM), lambda i: (i, 0))],
        out_shape=[jax.ShapeDtypeStruct((t, MLA_HEADS * KV_RANK), BF16),
                   jax.ShapeDtypeStruct((t, n_rope), BF16),
                   jax.ShapeDtypeStruct((t, KV_RANK), BF16),
                   jax.ShapeDtypeStruct((t, ROPE_DIM), BF16)],
        compiler_params=_cp("parallel"),
        name="mla_prep",
    )(ha, cosq, sinq, q_norm, kv_norm, wq, wuk_t)


def _mla_attn_kernel(ql_ref, qr_ref, ck_ref, kr_ref, o_ref, m_scr, l_scr, acc_scr, *, tq, tk):
    rows = tq * MLA_HEADS
    q_start = pl.program_id(1) * tq
    n_full = q_start // tk
    m_scr[...] = jnp.full(m_scr.shape, -jnp.inf, F32)
    l_scr[...] = jnp.zeros(l_scr.shape, F32)
    acc_scr[...] = jnp.zeros(acc_scr.shape, F32)
    ql = ql_ref[0]
    qr = qr_ref[0]
    nt = (((1,), (1,)), ((), ()))

    def step(j, masked):
        ks = pl.multiple_of(j * tk, tk)
        ck = ck_ref[0, pl.ds(ks, tk), :]
        kr = kr_ref[0, pl.ds(ks, tk), :]
        s = (lax.dot_general(ql, ck, nt, preferred_element_type=F32)
             + lax.dot_general(qr, kr, nt, preferred_element_type=F32))
        if masked:
            q_pos = q_start + lax.broadcasted_iota(I32, (rows, tk), 0) // MLA_HEADS
            k_pos = ks + lax.broadcasted_iota(I32, (rows, tk), 1)
            s = jnp.where(k_pos <= q_pos, s, -jnp.inf)
        m_prev = m_scr[...]
        m_new = jnp.maximum(m_prev, jnp.max(s, axis=-1, keepdims=True))
        alpha = jnp.exp(m_prev - m_new)
        p = jnp.exp(s - m_new)
        l_scr[...] = alpha * l_scr[...] + jnp.sum(p, axis=-1, keepdims=True)
        acc_scr[...] = alpha * acc_scr[...] + jnp.dot(p.astype(BF16), ck, preferred_element_type=F32)
        m_scr[...] = m_new

    def body(j, carry):
        step(j, False)
        return carry

    lax.fori_loop(0, n_full, body, 0)
    step(n_full, True)
    o_ref[0] = (acc_scr[...] / l_scr[...]).astype(o_ref.dtype)


def mla_attention(ql, qr, ck, kr):
    b, sh, _ = ql.shape
    s = ck.shape[1]
    tq = min(ATT_TQ, s)
    tk = min(ATT_TK, s)
    assert tk % tq == 0 and s % tk == 0
    rows = tq * MLA_HEADS
    return pl.pallas_call(
        functools.partial(_mla_attn_kernel, tq=tq, tk=tk),
        grid=(b, s // tq),
        in_specs=[pl.BlockSpec((1, rows, KV_RANK), lambda bi, i: (bi, i, 0)),
                  pl.BlockSpec((1, rows, ROPE_DIM), lambda bi, i: (bi, i, 0)),
                  pl.BlockSpec((1, s, KV_RANK), lambda bi, i: (bi, 0, 0)),
                  pl.BlockSpec((1, s, ROPE_DIM), lambda bi, i: (bi, 0, 0))],
        out_specs=pl.BlockSpec((1, rows, KV_RANK), lambda bi, i: (bi, i, 0)),
        out_shape=jax.ShapeDtypeStruct((b, sh, KV_RANK), BF16),
        scratch_shapes=[pltpu.VMEM((rows, 1), F32), pltpu.VMEM((rows, 1), F32),
                        pltpu.VMEM((rows, KV_RANK), F32)],
        compiler_params=_cp("parallel", "parallel"),
        name="mla_attention",
    )(ql, qr, ck, kr)


def _conv_kernel(u_ref, halo_ref, w_ref, b_ref, o_ref):
    i = pl.program_id(1)
    u = u_ref[0].astype(F32)
    halo = halo_ref[0].astype(F32)
    halo = jnp.where(i > 0, halo, jnp.zeros_like(halo))
    ext = jnp.concatenate([halo, u], axis=0)
    w = w_ref[...]
    acc = b_ref[...] + w[CONV_K - 1:CONV_K, :] * u
    for d in range(1, CONV_K):
        shifted = pltpu.roll(ext, d, 0)[CONV_HALO:, :]
        acc = acc + w[CONV_K - 1 - d:CONV_K - d, :] * shifted
    o_ref[0] = (acc * _sigmoid(acc)).astype(o_ref.dtype)


def conv_silu(u, w, b):
    bsz, s, c = u.shape
    tm = min(CONV_TM, s)
    tc = min(CONV_TC, c)
    hb = tm // CONV_HALO
    return pl.pallas_call(
        _conv_kernel,
        grid=(bsz, s // tm, c // tc),
        in_specs=[pl.BlockSpec((1, tm, tc), lambda bi, i, j: (bi, i, j)),
                  pl.BlockSpec((1, CONV_HALO, tc), lambda bi, i, j: (bi, jnp.maximum(i * hb - 1, 0), j)),
                  pl.BlockSpec((CONV_K, tc), lambda bi, i, j: (0, j)),
                  pl.BlockSpec((1, tc), lambda bi, i, j: (0, j))],
        out_specs=pl.BlockSpec((1, tm, tc), lambda bi, i, j: (bi, i, j)),
        out_shape=jax.ShapeDtypeStruct((bsz, s, c), BF16),
        compiler_params=_cp("parallel", "parallel", "parallel"),
        name="conv_silu",
    )(u, u, w, b)


def _ssd_kernel(xbc_ref, z_ref, dt_ref, dtb_ref, a_ref, dsk_ref, ng_ref, o_ref,
                state_scr, y_scr, xw_scr, *, chunk):
    d_inner = SSM_HEADS * SSM_HEAD_DIM
    gn = SSM_GROUPS * D_STATE
    rep = SSM_HEADS // SSM_GROUPS
    gw = rep * SSM_HEAD_DIM

    @pl.when(pl.program_id(1) == 0)
    def _():
        state_scr[...] = jnp.zeros(state_scr.shape, F32)

    x_raw = dt_ref[0] + dtb_ref[...]
    dt = jnp.maximum(x_raw, 0.0) + jnp.log(1.0 + jnp.exp(-jnp.abs(x_raw)))
    da = dt * a_ref[...]
    row = lax.broadcasted_iota(I32, (chunk, chunk), 0)
    col = lax.broadcasted_iota(I32, (chunk, chunk), 1)
    causal = row >= col
    tri = jnp.where(causal, 1.0, 0.0).astype(F32)
    acum = jnp.dot(tri, da, preferred_element_type=F32, precision=lax.Precision.HIGHEST)
    acum_t = acum.T
    a_end = acum[chunk - 1:chunk, :]
    to_end = jnp.exp(a_end - acum)
    e_acum = jnp.exp(acum)
    e_end = jnp.exp(a_end)
    lane = lax.broadcasted_iota(I32, (1, gw), 1) // SSM_HEAD_DIM

    for g in range(SSM_GROUPS):
        bg = xbc_ref[0, :, d_inner + g * D_STATE:d_inner + (g + 1) * D_STATE]
        cg = xbc_ref[0, :, d_inner + gn + g * D_STATE:d_inner + gn + (g + 1) * D_STATE]
        bt = bg.astype(F32).T.astype(BF16)
        cb = jnp.dot(cg, bt, preferred_element_type=F32)
        st = state_scr[g]
        ys = jnp.dot(cg, st.astype(BF16), preferred_element_type=F32)
        xg = xbc_ref[0, :, g * gw:(g + 1) * gw].astype(F32)
        sc = jnp.zeros((1, gw), F32)
        for r in range(rep):
            h = g * rep + r
            sl = slice(r * SSM_HEAD_DIM, (r + 1) * SSM_HEAD_DIM)
            seg = acum[:, h:h + 1] - acum_t[h:h + 1, :]
            decay = jnp.exp(jnp.where(causal, seg, -jnp.inf))
            m = (cb * decay).astype(BF16)
            xdt = xg[:, sl] * dt[:, h:h + 1]
            yh = jnp.dot(m, xdt.astype(BF16), preferred_element_type=F32)
            y_scr[:, sl] = yh + ys[:, sl] * e_acum[:, h:h + 1]
            xw_scr[:, sl] = xdt * to_end[:, h:h + 1]
            sc = jnp.where(lane == r, e_end[:, h:h + 1], sc)
        state_scr[g] = st * sc + jnp.dot(bt, xw_scr[...].astype(BF16), preferred_element_type=F32)
        gs = slice(g * gw, (g + 1) * gw)
        y = y_scr[...] + xg * dsk_ref[:, gs]
        zg = z_ref[0, :, gs].astype(F32)
        y = y * (zg * _sigmoid(zg))
        y = y * lax.rsqrt(jnp.mean(y * y, axis=-1, keepdims=True) + RMS_EPS) * ng_ref[:, gs]
        o_ref[0, :, gs] = y.astype(o_ref.dtype)


def ssd(xbc, z, ha3, dt_bias, a_neg, d_skip, norm_g):
    bsz, s, c = xbc.shape
    d_inner = SSM_HEADS * SSM_HEAD_DIM
    chunk = min(SSD_CHUNK, s)
    gw = d_inner // SSM_GROUPS
    vec = lambda n: pl.BlockSpec((1, n), lambda bi, ci: (0, 0))
    return pl.pallas_call(
        functools.partial(_ssd_kernel, chunk=chunk),
        grid=(bsz, s // chunk),
        in_specs=[pl.BlockSpec((1, chunk, c), lambda bi, ci: (bi, ci, 0)),
                  pl.BlockSpec((1, chunk, d_inner), lambda bi, ci: (bi, ci, 0)),
                  pl.BlockSpec((1, chunk, LANES), lambda bi, ci: (bi, ci, HA_DT // LANES)),
                  vec(LANES), vec(LANES), vec(d_inner), vec(d_inner)],
        out_specs=pl.BlockSpec((1, chunk, d_inner), lambda bi, ci: (bi, ci, 0)),
        out_shape=jax.ShapeDtypeStruct((bsz, s, d_inner), BF16),
        scratch_shapes=[pltpu.VMEM((SSM_GROUPS, D_STATE, gw), F32),
                        pltpu.VMEM((chunk, gw), F32),
                        pltpu.VMEM((chunk, gw), F32)],
        compiler_params=_cp("parallel", "arbitrary"),
        name="ssd_scan",
    )(xbc, z, ha3, dt_bias, a_neg, d_skip, norm_g)


def _xattn_kernel(q_ref, kv_ref, o_ref, *, scale):
    hd = XA_HEADS * XA_HEAD_DIM
    nt = (((1,), (1,)), ((), ()))
    for h in range(XA_HEADS):
        sl = slice(h * XA_HEAD_DIM, (h + 1) * XA_HEAD_DIM)
        q = q_ref[0, :, sl]
        k = kv_ref[0, :, sl]
        v = kv_ref[0, :, hd + h * XA_HEAD_DIM:hd + (h + 1) * XA_HEAD_DIM]
        s = lax.dot_general(q, k, nt, preferred_element_type=F32) * scale
        p = jnp.exp(s - jnp.max(s, axis=-1, keepdims=True))
        l = jnp.sum(p, axis=-1, keepdims=True)
        o = jnp.dot(p.astype(BF16), v, preferred_element_type=F32) / l
        o_ref[0, :, sl] = o.astype(o_ref.dtype)


def mem_attention(q, kv):
    bsz, s, hd = q.shape
    m = kv.shape[1]
    tq = min(XA_TQ, s)
    return pl.pallas_call(
        functools.partial(_xattn_kernel, scale=float(XA_HEAD_DIM ** -0.5)),
        grid=(bsz, s // tq),
        in_specs=[pl.BlockSpec((1, tq, hd), lambda bi, i: (bi, i, 0)),
                  pl.BlockSpec((1, m, 2 * hd), lambda bi, i: (bi, 0, 0))],
        out_specs=pl.BlockSpec((1, tq, hd), lambda bi, i: (bi, i, 0)),
        out_shape=jax.ShapeDtypeStruct((bsz, s, hd), BF16),
        compiler_params=_cp("parallel", "parallel"),
        name="mem_attention",
    )(q, kv)


def _layer_norm(v, g, b):
    mu = jnp.mean(v, axis=-1, keepdims=True)
    d = v - mu
    var = jnp.mean(d * d, axis=-1, keepdims=True)
    return d * lax.rsqrt(var + NORM_EPS) * g + b


def _merge_kernel(ol_ref, yn_ref, cm_ref, g_ref, x_ref, wuv_ref, wa_ref, wb_ref, wc_ref, wo_ref,
                  lg_ref, lb_ref, xo_ref, a_scr, *, alpha):
    d = x_ref.shape[1]
    for h in range(MLA_HEADS):
        a_scr[:, h * V_DIM:(h + 1) * V_DIM] = jnp.dot(
            ol_ref[:, h * KV_RANK:(h + 1) * KV_RANK], wuv_ref[h],
            preferred_element_type=F32).astype(BF16)
    o_a = jnp.dot(a_scr[...], wa_ref[...], preferred_element_type=F32)
    o_b = jnp.dot(yn_ref[...], wb_ref[...], preferred_element_type=F32)
    o_c = jnp.dot(cm_ref[...], wc_ref[...], preferred_element_type=F32)
    g = g_ref[...].astype(F32)
    merged = (_sigmoid(g[:, :d]) * o_a + _sigmoid(g[:, d:2 * d]) * o_b + _sigmoid(g[:, 2 * d:]) * o_c)
    y = alpha * x_ref[...] + jnp.dot(merged.astype(BF16), wo_ref[...], preferred_element_type=F32)
    xo_ref[...] = _layer_norm(y, lg_ref[...], lb_ref[...])


def merge_ln(ol, yn, cm, g, x, wuv, wa, wb, wc, wo, ln_g, ln_b, alpha):
    t, d = x.shape
    tm = min(MERGE_TM, t)
    row = lambda n: pl.BlockSpec((tm, n), lambda i: (i, 0))
    full = lambda a: pl.BlockSpec(a.shape, lambda i: (0,) * a.ndim, pipeline_mode=pl.Buffered(1))
    return pl.pallas_call(
        functools.partial(_merge_kernel, alpha=alpha),
        grid=(t // tm,),
        in_specs=[row(ol.shape[1]), row(yn.shape[1]), row(cm.shape[1]), row(g.shape[1]), row(d),
                  full(wuv), full(wa), full(wb), full(wc), full(wo), full(ln_g), full(ln_b)],
        out_specs=row(d),
        out_shape=jax.ShapeDtypeStruct((t, d), F32),
        scratch_shapes=[pltpu.VMEM((tm, MLA_HEADS * V_DIM), BF16)],
        compiler_params=_cp("parallel"),
        name="merge_ln",
    )(ol, yn, cm, g, x, wuv, wa, wb, wc, wo, ln_g, ln_b)


def _first_max(v, lane):
    m = jnp.max(v, axis=-1, keepdims=True)
    idx = jnp.min(jnp.where(v == m, lane, LANES), axis=-1, keepdims=True)
    return m, idx


def _route_kernel(x_ref, rw_ref, rb_ref, info_ref, cnt_ref, carry_scr):
    tm = x_ref.shape[0]

    @pl.when(pl.program_id(0) == 0)
    def _():
        carry_scr[...] = jnp.zeros(carry_scr.shape, F32)

    logits = jnp.dot(x_ref[...], rw_ref[...], preferred_element_type=F32,
                     precision=lax.Precision.HIGHEST)
    scores = _sigmoid(logits)
    sel = scores + rb_ref[...]
    lane = lax.broadcasted_iota(I32, (tm, LANES), 1)
    neg = -jnp.inf
    best_score = None
    best_group = None
    for j in range(N_EXPERT_GROUPS):
        in_j = (lane >= j * EXPERTS_PER_GROUP) & (lane < (j + 1) * EXPERTS_PER_GROUP)
        v = jnp.where(in_j, sel, neg)
        m1, i1 = _first_max(v, lane)
        m2, _ = _first_max(jnp.where(lane == i1, neg, v), lane)
        gs = m1 + m2
        if j == 0:
            best_score, best_group = gs, jnp.zeros_like(i1)
        else:
            better = gs > best_score
            best_score = jnp.where(better, gs, best_score)
            best_group = jnp.where(better, j, best_group)
    lo = best_group * EXPERTS_PER_GROUP
    v = jnp.where((lane >= lo) & (lane < lo + EXPERTS_PER_GROUP), sel, neg)
    _, e1 = _first_max(v, lane)
    _, e2 = _first_max(jnp.where(lane == e1, neg, v), lane)
    w1 = jnp.sum(jnp.where(lane == e1, scores, 0.0), axis=-1, keepdims=True)
    w2 = jnp.sum(jnp.where(lane == e2, scores, 0.0), axis=-1, keepdims=True)
    wsum = w1 + w2
    member = jnp.where((lane == e1) | (lane == e2), 1.0, 0.0)
    row = lax.broadcasted_iota(I32, (tm, tm), 0)
    col = lax.broadcasted_iota(I32, (tm, tm), 1)
    before = jnp.where(row > col, 1.0, 0.0).astype(BF16)
    carry = carry_scr[0:1, :]
    rank = jnp.dot(before, member.astype(BF16), preferred_element_type=F32) + carry
    r1 = jnp.sum(jnp.where(lane == e1, rank, 0.0), axis=-1, keepdims=True)
    r2 = jnp.sum(jnp.where(lane == e2, rank, 0.0), axis=-1, keepdims=True)
    new_carry = carry + jnp.sum(member, axis=0, keepdims=True)
    carry_scr[...] = jnp.broadcast_to(new_carry, carry_scr.shape)
    cnt_ref[...] = jnp.broadcast_to(new_carry, cnt_ref.shape)
    info = jnp.where(lane == 0, e1.astype(F32),
           jnp.where(lane == 1, e2.astype(F32),
           jnp.where(lane == 2, w1 / wsum,
           jnp.where(lane == 3, w2 / wsum,
           jnp.where(lane == 4, r1,
           jnp.where(lane == 5, r2, 0.0))))))
    info_ref[...] = info


def route(x, rw, rb):
    t, d = x.shape
    tm = min(ROUTE_TM, t)
    return pl.pallas_call(
        _route_kernel,
        grid=(t // tm,),
        in_specs=[pl.BlockSpec((tm, d), lambda i: (i, 0)),
                  pl.BlockSpec((d, LANES), lambda i: (0, 0)),
                  pl.BlockSpec((1, LANES), lambda i: (0, 0))],
        out_specs=[pl.BlockSpec((tm, LANES), lambda i: (i, 0)),
                   pl.BlockSpec((8, LANES), lambda i: (0, 0))],
        out_shape=[jax.ShapeDtypeStruct((t, LANES), F32), jax.ShapeDtypeStruct((8, LANES), F32)],
        scratch_shapes=[pltpu.VMEM((8, LANES), F32)],
        compiler_params=_cp("arbitrary"),
        name="route",
    )(x, rw, rb)


def _scatter_kernel(pos_ref, x_ref, init_ref, o_ref, sem):
    del init_ref
    tm = x_ref.shape[0]

    def copy(r, k):
        p = pos_ref[0, 0, TOP_K * r + k]
        return pltpu.make_async_copy(x_ref.at[pl.ds(r, 1)], o_ref.at[pl.ds(p, 1)], sem)

    def start(r, c):
        for k in range(TOP_K):
            copy(r, k).start()
        return c

    def wait(r, c):
        for k in range(TOP_K):
            copy(r, k).wait()
        return c

    lax.fori_loop(0, tm, start, 0)
    lax.fori_loop(0, tm, wait, 0)


def scatter_rows(x, pos3, n_rows):
    t, d = x.shape
    tm = pos3.shape[2] // TOP_K
    init = jnp.zeros((n_rows, d), x.dtype)
    return pl.pallas_call(
        _scatter_kernel,
        grid=(t // tm,),
        in_specs=[pl.BlockSpec((1, 1, TOP_K * tm), lambda i: (i, 0, 0), memory_space=pltpu.SMEM),
                  pl.BlockSpec((tm, d), lambda i: (i, 0)),
                  pl.BlockSpec(memory_space=pl.ANY)],
        out_specs=pl.BlockSpec(memory_space=pl.ANY),
        out_shape=jax.ShapeDtypeStruct((n_rows, d), x.dtype),
        scratch_shapes=[pltpu.SemaphoreType.DMA(())],
        input_output_aliases={2: 0},
        compiler_params=_cp("arbitrary"),
        name="moe_scatter",
    )(pos3, x, init)


def _expert_kernel(te_ref, nu_ref, xs_ref, w13_ref, w2_ref, o_ref):
    del te_ref
    i = pl.program_id(0)
    f = w2_ref.shape[1]

    @pl.when(i < nu_ref[0])
    def _():
        xb = xs_ref[...].astype(BF16)
        h = jnp.dot(xb, w13_ref[0], preferred_element_type=F32)
        h1 = h[:, :f]
        act = (h1 * _sigmoid(h1) * h[:, f:]).astype(BF16)
        o_ref[...] = jnp.dot(act, w2_ref[0], preferred_element_type=F32)

    @pl.when(i >= nu_ref[0])
    def _():
        o_ref[...] = jnp.zeros(o_ref.shape, o_ref.dtype)


def expert_ffn(xs, w13, w2, tile_expert, n_used):
    p, d = xs.shape
    f = w2.shape[1]
    n_tiles = p // MOE_TM
    grid_spec = pltpu.PrefetchScalarGridSpec(
        num_scalar_prefetch=2,
        grid=(n_tiles,),
        in_specs=[pl.BlockSpec((MOE_TM, d), lambda i, te, nu: (i, 0)),
                  pl.BlockSpec((1, d, 2 * f), lambda i, te, nu: (te[i], 0, 0)),
                  pl.BlockSpec((1, f, d), lambda i, te, nu: (te[i], 0, 0))],
        out_specs=pl.BlockSpec((MOE_TM, d), lambda i, te, nu: (i, 0)),
    )
    return pl.pallas_call(
        _expert_kernel,
        grid_spec=grid_spec,
        out_shape=jax.ShapeDtypeStruct((p, d), F32),
        compiler_params=_cp("arbitrary"),
        name="expert_ffn",
    )(tile_expert, n_used, xs, w13, w2)


def _combine_kernel(pos_ref, info_ref, x_ref, ys_ref, lg_ref, lb_ref, xo_ref, xb_ref, ybuf, sem, *, alpha):
    tm = x_ref.shape[0]

    def copy(r, k):
        p = pos_ref[0, 0, TOP_K * r + k]
        return pltpu.make_async_copy(ys_ref.at[pl.ds(p, 1)], ybuf.at[k, pl.ds(r, 1)], sem)

    def start(r, c):
        for k in range(TOP_K):
            copy(r, k).start()
        return c

    def wait(r, c):
        for k in range(TOP_K):
            copy(r, k).wait()
        return c

    lax.fori_loop(0, tm, start, 0)
    lax.fori_loop(0, tm, wait, 0)
    info = info_ref[...]
    moe = info[:, 2:3] * ybuf[0] + info[:, 3:4] * ybuf[1]
    out = _layer_norm(alpha * x_ref[...] + moe, lg_ref[...], lb_ref[...])
    xo_ref[...] = out
    xb_ref[...] = out.astype(BF16)


def combine_ln(pos3, info, x, ys, ln_g, ln_b, alpha):
    t, d = x.shape
    tm = pos3.shape[2] // TOP_K
    return pl.pallas_call(
        functools.partial(_combine_kernel, alpha=alpha),
        grid=(t // tm,),
        in_specs=[pl.BlockSpec((1, 1, TOP_K * tm), lambda i: (i, 0, 0), memory_space=pltpu.SMEM),
                  pl.BlockSpec((tm, LANES), lambda i: (i, 0)),
                  pl.BlockSpec((tm, d), lambda i: (i, 0)),
                  pl.BlockSpec(memory_space=pl.ANY),
                  pl.BlockSpec((1, d), lambda i: (0, 0)),
                  pl.BlockSpec((1, d), lambda i: (0, 0))],
        out_specs=[pl.BlockSpec((tm, d), lambda i: (i, 0)), pl.BlockSpec((tm, d), lambda i: (i, 0))],
        out_shape=[jax.ShapeDtypeStruct((t, d), F32), jax.ShapeDtypeStruct((t, d), BF16)],
        scratch_shapes=[pltpu.VMEM((TOP_K, tm, d), F32), pltpu.SemaphoreType.DMA(())],
        compiler_params=_cp("arbitrary"),
        name="moe_combine_ln",
    )(pos3, info, x, ys, ln_g, ln_b)


def _prep_layer(l, p):
    w_in = p["w_in"][l]
    d = w_in.shape[0]
    o_dq = 0
    o_dkv = o_dq + Q_RANK
    o_z = o_dkv + KV_RANK + ROPE_DIM
    d_inner = SSM_HEADS * SSM_HEAD_DIM
    conv_ch = d_inner + 2 * SSM_GROUPS * D_STATE
    o_xbc = o_z + d_inner
    o_dt = o_xbc + conv_ch
    o_qm = o_dt + SSM_HEADS
    o_g = o_qm + XA_HEADS * XA_HEAD_DIM
    half = ROPE_DIM // 2
    kr0 = o_dkv + KV_RANK
    zeros = lambda n: jnp.zeros((d, n), F32)
    w_small = jnp.concatenate([
        w_in[:, o_dq:o_dq + Q_RANK],
        w_in[:, o_dkv:o_dkv + KV_RANK],
        w_in[:, kr0:kr0 + ROPE_DIM], zeros(LANES - ROPE_DIM),
        w_in[:, kr0 + half:kr0 + ROPE_DIM], w_in[:, kr0:kr0 + half], zeros(LANES - ROPE_DIM),
        w_in[:, o_dt:o_dt + SSM_HEADS], zeros(LANES - SSM_HEADS)], axis=1)
    w_uq = p["w_uq"][l].reshape(Q_RANK, MLA_HEADS, NOPE_DIM + ROPE_DIM)
    wq_nope = w_uq[:, :, :NOPE_DIM].reshape(Q_RANK, -1)
    wq_rope = w_uq[:, :, NOPE_DIM:]
    wq_rope_sw = jnp.concatenate([wq_rope[:, :, half:], wq_rope[:, :, :half]], axis=-1)
    wq = jnp.concatenate([wq_nope, wq_rope.reshape(Q_RANK, -1), wq_rope_sw.reshape(Q_RANK, -1)], axis=1)
    w_ukv = p["w_ukv"][l]
    wuk_t = jnp.transpose(w_ukv[:, :, :NOPE_DIM], (1, 2, 0))
    wuv = jnp.transpose(w_ukv[:, :, NOPE_DIM:], (1, 0, 2))
    pad_heads = lambda v, fill: jnp.concatenate(
        [v.astype(F32), jnp.full((LANES - SSM_HEADS,), fill, F32)]).reshape(1, LANES)
    bf = lambda a: a.astype(BF16)
    return dict(
        w_small=bf(w_small), w_z=bf(w_in[:, o_z:o_z + d_inner]), w_xbc=bf(w_in[:, o_xbc:o_xbc + conv_ch]),
        w_qm=bf(w_in[:, o_qm:o_g]), w_g=bf(w_in[:, o_g:]),
        q_norm=p["q_norm"][l].reshape(1, -1), kv_norm=p["kv_norm"][l].reshape(1, -1),
        wq=bf(wq), wuk_t=bf(wuk_t), wuv=bf(wuv),
        conv_w=p["conv_w"][l], conv_b=p["conv_b"][l].reshape(1, -1),
        dt_bias=pad_heads(p["dt_bias"][l], 0.0),
        a_neg=pad_heads(-jnp.exp(p["a_log"][l].astype(F32)), 0.0),
        d_skip=jnp.repeat(p["d_skip"][l].astype(F32), SSM_HEAD_DIM).reshape(1, -1),
        ssm_norm=p["ssm_norm"][l].reshape(1, -1),
        w_mem_kv=bf(p["w_mem_kv"][l]),
        wa=bf(p["w_proj_a"][l]), wb=bf(p["w_proj_b"][l]), wc=bf(p["w_proj_c"][l]), wo=bf(p["w_out"][l]),
        ln1_g=p["ln1_g"][l].reshape(1, -1), ln1_b=p["ln1_b"][l].reshape(1, -1),
        w13=bf(jnp.concatenate([p["exp_w1"][l], p["exp_w3"][l]], axis=-1)), w2=bf(p["exp_w2"][l]),
        ln2_g=p["ln2_g"][l].reshape(1, -1), ln2_b=p["ln2_b"][l].reshape(1, -1),
    )


def _dispatch_plan(info, counts, t):
    e = info[:, 0:TOP_K].astype(I32)
    rank = info[:, 4:4 + TOP_K].astype(I32)
    cnt = counts[0, :N_EXPERTS].astype(I32)
    tiles = (cnt + MOE_TM - 1) // MOE_TM
    tile_end = jnp.cumsum(tiles)
    row_start = (tile_end - tiles) * MOE_TM
    pos = row_start[e] + rank
    n_tiles = (TOP_K * t) // MOE_TM + N_EXPERTS
    tile_expert = jnp.minimum(
        jnp.sum(jnp.arange(n_tiles, dtype=I32)[:, None] >= tile_end[None, :], axis=1), N_EXPERTS - 1)
    n_used = tile_end[-1:].astype(I32)
    tm = min(PERM_TM, t)
    pos3 = pos.reshape(t // tm, 1, TOP_K * tm)
    return pos3, tile_expert.astype(I32), n_used, n_tiles * MOE_TM


def kernel(x, mem, positions, w_in, q_norm, w_uq, kv_norm, w_ukv, w_proj_a, conv_w, conv_b, dt_bias, a_log,
           d_skip, ssm_norm, w_proj_b, w_mem_kv, w_proj_c, w_out, ln1_g, ln1_b, router_w, router_bias,
           exp_w1, exp_w3, exp_w2, ln2_g, ln2_b):
    params = dict(w_in=w_in, q_norm=q_norm, w_uq=w_uq, kv_norm=kv_norm, w_ukv=w_ukv, w_proj_a=w_proj_a,
                  conv_w=conv_w, conv_b=conv_b, dt_bias=dt_bias, a_log=a_log, d_skip=d_skip,
                  ssm_norm=ssm_norm, w_proj_b=w_proj_b, w_mem_kv=w_mem_kv, w_proj_c=w_proj_c, w_out=w_out,
                  ln1_g=ln1_g, ln1_b=ln1_b, exp_w1=exp_w1, exp_w3=exp_w3, exp_w2=exp_w2,
                  ln2_g=ln2_g, ln2_b=ln2_b)
    bsz, s, d = x.shape
    t = bsz * s
    depth = w_in.shape[0]
    alpha = float((2 * depth) ** 0.25)
    n_mem = mem.shape[1]

    inv = ROPE_THETA ** (-jnp.arange(0, ROPE_DIM, 2, dtype=F32) / ROPE_DIM)
    ang = positions.astype(F32)[..., None] * inv
    cos, sin = jnp.cos(ang), jnp.sin(ang)
    cosq = jnp.tile(jnp.concatenate([cos, cos], axis=-1), (1, 1, MLA_HEADS)).reshape(t, -1)
    sinq = jnp.tile(jnp.concatenate([-sin, sin], axis=-1), (1, 1, MLA_HEADS)).reshape(t, -1)

    rw = jnp.concatenate([router_w.astype(F32), jnp.zeros((d, LANES - N_EXPERTS), F32)], axis=1)
    rb = jnp.concatenate([router_bias.astype(F32), jnp.zeros((LANES - N_EXPERTS,), F32)]).reshape(1, LANES)
    mem_b = mem.reshape(bsz * n_mem, d).astype(BF16)

    xf = x.reshape(t, d).astype(F32)
    xb = xf.astype(BF16)
    for l in range(depth):
        w = _prep_layer(l, params)
        ha = matmul(xb, w["w_small"], F32, "proj_small")
        z = matmul(xb, w["w_z"], BF16, "proj_z")
        xbc = matmul(xb, w["w_xbc"], BF16, "proj_xbc")
        qm = matmul(xb, w["w_qm"], BF16, "proj_qmem")
        g = matmul(xb, w["w_g"], BF16, "proj_gate")

        ql, qr, ck, kr = mla_prep(ha, cosq, sinq, w["q_norm"], w["kv_norm"], w["wq"], w["wuk_t"])
        o_lat = mla_attention(ql.reshape(bsz, s * MLA_HEADS, KV_RANK),
                              qr.reshape(bsz, s * MLA_HEADS, ROPE_DIM),
                              ck.reshape(bsz, s, KV_RANK), kr.reshape(bsz, s, ROPE_DIM))
        o_lat = o_lat.reshape(t, MLA_HEADS * KV_RANK)

        xbc_c = conv_silu(xbc.reshape(bsz, s, -1), w["conv_w"], w["conv_b"])
        yn = ssd(xbc_c, z.reshape(bsz, s, -1), ha.reshape(bsz, s, HA_W), w["dt_bias"], w["a_neg"],
                 w["d_skip"], w["ssm_norm"]).reshape(t, -1)

        kv = matmul(mem_b, w["w_mem_kv"], BF16, "proj_memkv").reshape(bsz, n_mem, -1)
        cm = mem_attention(qm.reshape(bsz, s, -1), kv).reshape(t, -1)

        x1 = merge_ln(o_lat, yn, cm, g, xf, w["wuv"], w["wa"], w["wb"], w["wc"], w["wo"],
                           w["ln1_g"], w["ln1_b"], alpha)

        info, counts = route(x1, rw, rb)
        pos3, tile_expert, n_used, n_rows = _dispatch_plan(info, counts, t)
        xs = scatter_rows(x1, pos3, n_rows)
        ys = expert_ffn(xs, w["w13"], w["w2"], tile_expert, n_used)
        xf, xb = combine_ln(pos3, info, x1, ys, w["ln2_g"], w["ln2_b"], alpha)
    return xf.reshape(bsz, s, d)
```

```python
import functools

import jax
import jax.numpy as jnp
from jax import lax
from jax.experimental import pallas as pl
from jax.experimental.pallas import tpu as pltpu

F32 = jnp.float32
BF16 = jnp.bfloat16
I32 = jnp.int32

MLA_HEADS = 8
Q_RANK = 384
KV_RANK = 256
NOPE_DIM = 128
ROPE_DIM = 64
V_DIM = 128
ROPE_THETA = 10000.0
SSM_HEADS = 32
SSM_HEAD_DIM = 64
SSM_GROUPS = 8
D_STATE = 128
CONV_K = 4
XA_HEADS = 4
XA_HEAD_DIM = 256
N_EXPERTS = 16
N_EXPERT_GROUPS = 4
EXPERTS_PER_GROUP = 4
TOP_K = 2
NORM_EPS = 1e-5
RMS_EPS = 1e-6

LANES = 128
V7X_VMEM_LIMIT = 56 * 1024 * 1024

MM_TM = 1024
MM_TN = 1024
PREP_TM = 512
ATT_TQ = 128
ATT_TK = 512
ATT_GROUP_ROWS = 512
LOG2E = 1.4426950408889634
CONV_TM = 512
CONV_TC = 1024
CONV_HALO = 16
SSD_CHUNK = 256
XA_TQ = 512
MERGE_TM = 512
ROUTE_TM = 512
MOE_TM = 256
PERM_TM = 256
DMA_UNROLL = 8

HA_W = 1024
HA_DQ = 0
HA_C = 384
HA_KR = 640
HA_KRS = 768
HA_DT = 896


def _cp(*sem):
    return pltpu.CompilerParams(dimension_semantics=sem, vmem_limit_bytes=V7X_VMEM_LIMIT)


def _sigmoid(x):
    return 1.0 / (1.0 + jnp.exp(-x))


def _mm_kernel(x_ref, w_ref, o_ref):
    o_ref[...] = jnp.dot(x_ref[...], w_ref[...], preferred_element_type=F32).astype(o_ref.dtype)


def matmul(x, w, out_dtype, name):
    m, k = x.shape
    n = w.shape[1]
    tm = min(MM_TM, m)
    tn = min(MM_TN, n)
    return pl.pallas_call(
        _mm_kernel,
        grid=(n // tn, m // tm),
        in_specs=[pl.BlockSpec((tm, k), lambda j, i: (i, 0)),
                  pl.BlockSpec((k, tn), lambda j, i: (0, j))],
        out_specs=pl.BlockSpec((tm, tn), lambda j, i: (i, j)),
        out_shape=jax.ShapeDtypeStruct((m, n), out_dtype),
        compiler_params=_cp("parallel", "parallel"),
        name=name,
    )(x, w)


def _mla_prep_kernel(ha_ref, cos_ref, sin_ref, qn_ref, kvn_ref, wq_ref, wuk_ref,
                     ql_ref, qr_ref, ck_ref, kr_ref, *, scale):
    nb, _, tq, _ = ql_ref.shape
    ha = ha_ref[...]
    dq = ha[:, HA_DQ:HA_DQ + Q_RANK]
    c_q = dq * lax.rsqrt(jnp.mean(dq * dq, axis=-1, keepdims=True) + RMS_EPS) * qn_ref[...]
    q = jnp.dot(c_q.astype(BF16), wq_ref[...], preferred_element_type=F32)
    n_nope = MLA_HEADS * NOPE_DIM
    n_rope = MLA_HEADS * ROPE_DIM
    cosq = cos_ref[...]
    sinq = sin_ref[...]
    q_rope = ((q[:, n_nope:n_nope + n_rope] * cosq + q[:, n_nope + n_rope:] * sinq) * scale).astype(BF16)
    for h in range(MLA_HEADS):
        qh = q[:, h * NOPE_DIM:(h + 1) * NOPE_DIM].astype(BF16)
        ql = jnp.dot(qh, wuk_ref[h], preferred_element_type=F32)
        ql_ref[:, h] = (ql * scale).astype(BF16).reshape(nb, tq, KV_RANK)
        qr_ref[:, h] = q_rope[:, h * ROPE_DIM:(h + 1) * ROPE_DIM].reshape(nb, tq, ROPE_DIM)
    c = ha[:, HA_C:HA_C + KV_RANK]
    c_kv = c * lax.rsqrt(jnp.mean(c * c, axis=-1, keepdims=True) + RMS_EPS) * kvn_ref[...]
    ck_ref[...] = c_kv.astype(BF16)
    k_rope = (ha[:, HA_KR:HA_KR + LANES] * cosq[:, :LANES]
              + ha[:, HA_KRS:HA_KRS + LANES] * sinq[:, :LANES])
    kr_ref[...] = k_rope[:, :ROPE_DIM].astype(BF16)


def mla_prep(ha, cosq, sinq, q_norm, kv_norm, wq, wuk_t, tq):
    t = ha.shape[0]
    tm = min(PREP_TM, t)
    nb = tm // tq
    scale = float((NOPE_DIM + ROPE_DIM) ** -0.5 * LOG2E)
    n_rope = MLA_HEADS * ROPE_DIM
    full = lambda shape: pl.BlockSpec(shape, lambda i: (0,) * len(shape))
    return pl.pallas_call(
        functools.partial(_mla_prep_kernel, scale=scale),
        grid=(t // tm,),
        in_specs=[pl.BlockSpec((tm, HA_W), lambda i: (i, 0)),
                  pl.BlockSpec((tm, n_rope), lambda i: (i, 0)),
                  pl.BlockSpec((tm, n_rope), lambda i: (i, 0)),
                  full((1, Q_RANK)), full((1, KV_RANK)),
                  full(wq.shape), full(wuk_t.shape)],
        out_specs=[pl.BlockSpec((nb, MLA_HEADS, tq, KV_RANK), lambda i: (i, 0, 0, 0)),
                   pl.BlockSpec((nb, MLA_HEADS, tq, ROPE_DIM), lambda i: (i, 0, 0, 0)),
                   pl.BlockSpec((tm, KV_RANK), lambda i: (i, 0)),
                   pl.BlockSpec((tm, ROPE_DIM), lambda i: (i, 0))],
        out_shape=[jax.ShapeDtypeStruct((t // tq, MLA_HEADS, tq, KV_RANK), BF16),
                   jax.ShapeDtypeStruct((t // tq, MLA_HEADS, tq, ROPE_DIM), BF16),
                   jax.ShapeDtypeStruct((t, KV_RANK), BF16),
                   jax.ShapeDtypeStruct((t, ROPE_DIM), BF16)],
        compiler_params=_cp("parallel"),
        name="mla_prep",
    )(ha, cosq, sinq, q_norm, kv_norm, wq, wuk_t)


def _mla_attn_kernel(ql_ref, qr_ref, ck_ref, kr_ref, o_ref, m_scr, l_scr, acc_scr, *, tq, tk, hpc):
    rc = hpc * tq
    n_groups = MLA_HEADS // hpc
    q_start = pl.program_id(1) * tq
    n_full = q_start // tk
    m_scr[...] = jnp.full(m_scr.shape, -jnp.inf, F32)
    l_scr[...] = jnp.zeros(l_scr.shape, F32)
    acc_scr[...] = jnp.zeros(acc_scr.shape, F32)
    nt = (((1,), (1,)), ((), ()))

    def step(j, masked):
        ks = pl.multiple_of(j * tk, tk)
        ck = ck_ref[0, pl.ds(ks, tk), :]
        kr = kr_ref[0, pl.ds(ks, tk), :]
        if masked:
            q_pos = q_start + (lax.broadcasted_iota(I32, (rc, tk), 0) & (tq - 1))
            k_pos = ks + lax.broadcasted_iota(I32, (rc, tk), 1)
            visible = k_pos <= q_pos
        for c in range(n_groups):
            rs = slice(c * rc, (c + 1) * rc)
            ql = ql_ref[0, c * hpc:(c + 1) * hpc].reshape(rc, KV_RANK)
            qr = qr_ref[0, c * hpc:(c + 1) * hpc].reshape(rc, ROPE_DIM)
            s = (lax.dot_general(ql, ck, nt, preferred_element_type=F32)
                 + lax.dot_general(qr, kr, nt, preferred_element_type=F32))
            if masked:
                s = jnp.where(visible, s, -jnp.inf)
            m_prev = m_scr[rs]
            m_new = jnp.maximum(m_prev, jnp.max(s, axis=-1, keepdims=True))
            alpha = jnp.exp2(m_prev - m_new)
            p = jnp.exp2(s - jnp.tile(m_new, (1, tk // LANES)))
            l_scr[rs] = alpha * l_scr[rs] + jnp.sum(p, axis=-1, keepdims=True)
            acc_scr[rs] = (jnp.tile(alpha, (1, KV_RANK // LANES)) * acc_scr[rs]
                           + jnp.dot(p.astype(BF16), ck, preferred_element_type=F32))
            m_scr[rs] = m_new

    def body(j, carry):
        step(j, False)
        return carry

    lax.fori_loop(0, n_full, body, 0)
    step(n_full, True)
    out = acc_scr[...] / jnp.tile(l_scr[...], (1, KV_RANK // LANES))
    o_ref[0] = out.astype(o_ref.dtype).reshape(MLA_HEADS, tq, KV_RANK)


def mla_attention(ql, qr, ck, kr):
    _, _, tq, _ = ql.shape
    b, s, _ = ck.shape
    tk = min(ATT_TK, s)
    assert tk % tq == 0 and s % tk == 0 and tq & (tq - 1) == 0
    nq = s // tq
    rows = tq * MLA_HEADS
    hpc = max(1, min(MLA_HEADS, ATT_GROUP_ROWS // tq))
    qspec = lambda dim: pl.BlockSpec((1, MLA_HEADS, tq, dim), lambda bi, i: (bi * nq + i, 0, 0, 0))
    return pl.pallas_call(
        functools.partial(_mla_attn_kernel, tq=tq, tk=tk, hpc=hpc),
        grid=(b, nq),
        in_specs=[qspec(KV_RANK), qspec(ROPE_DIM),
                  pl.BlockSpec((1, s, KV_RANK), lambda bi, i: (bi, 0, 0)),
                  pl.BlockSpec((1, s, ROPE_DIM), lambda bi, i: (bi, 0, 0))],
        out_specs=qspec(KV_RANK),
        out_shape=jax.ShapeDtypeStruct(ql.shape, BF16),
        scratch_shapes=[pltpu.VMEM((rows, LANES), F32), pltpu.VMEM((rows, LANES), F32),
                        pltpu.VMEM((rows, KV_RANK), F32)],
        compiler_params=_cp("parallel", "parallel"),
        name="mla_attention",
    )(ql, qr, ck, kr)


def _conv_kernel(u_ref, halo_ref, w_ref, b_ref, o_ref):
    i = pl.program_id(1)
    u = u_ref[0].astype(F32)
    halo = halo_ref[0].astype(F32)
    halo = jnp.where(i > 0, halo, jnp.zeros_like(halo))
    ext = jnp.concatenate([halo, u], axis=0)
    w = w_ref[...]
    acc = b_ref[...] + w[CONV_K - 1:CONV_K, :] * u
    for d in range(1, CONV_K):
        shifted = pltpu.roll(ext, d, 0)[CONV_HALO:, :]
        acc = acc + w[CONV_K - 1 - d:CONV_K - d, :] * shifted
    o_ref[0] = (acc * _sigmoid(acc)).astype(o_ref.dtype)


def conv_silu(u, w, b):
    bsz, s, c = u.shape
    tm = min(CONV_TM, s)
    tc = min(CONV_TC, c)
    hb = tm // CONV_HALO
    return pl.pallas_call(
        _conv_kernel,
        grid=(bsz, s // tm, c // tc),
        in_specs=[pl.BlockSpec((1, tm, tc), lambda bi, i, j: (bi, i, j)),
                  pl.BlockSpec((1, CONV_HALO, tc), lambda bi, i, j: (bi, jnp.maximum(i * hb - 1, 0), j)),
                  pl.BlockSpec((CONV_K, tc), lambda bi, i, j: (0, j)),
                  pl.BlockSpec((1, tc), lambda bi, i, j: (0, j))],
        out_specs=pl.BlockSpec((1, tm, tc), lambda bi, i, j: (bi, i, j)),
        out_shape=jax.ShapeDtypeStruct((bsz, s, c), BF16),
        compiler_params=_cp("parallel", "parallel", "parallel"),
        name="conv_silu",
    )(u, u, w, b)


def _ssd_kernel(xbc_ref, z_ref, dt_ref, dtb_ref, a_ref, dsk_ref, ng_ref, exp_ref, o_ref,
                state_scr, *, chunk):
    d_inner = SSM_HEADS * SSM_HEAD_DIM
    gn = SSM_GROUPS * D_STATE
    rep = SSM_HEADS // SSM_GROUPS
    gw = rep * SSM_HEAD_DIM

    @pl.when(pl.program_id(1) == 0)
    def _():
        state_scr[...] = jnp.zeros(state_scr.shape, F32)

    x_raw = dt_ref[0] + dtb_ref[...]
    dt = jnp.maximum(x_raw, 0.0) + jnp.log(1.0 + jnp.exp(-jnp.abs(x_raw)))
    da = dt * a_ref[...]
    row = lax.broadcasted_iota(I32, (chunk, chunk), 0)
    col = lax.broadcasted_iota(I32, (chunk, chunk), 1)
    causal = row >= col
    tri = jnp.where(causal, 1.0, 0.0).astype(BF16)
    acum = jnp.zeros(da.shape, F32)
    rem = da
    for _ in range(3):
        part = rem.astype(BF16)
        acum = acum + jnp.dot(tri, part, preferred_element_type=F32)
        rem = rem - part.astype(F32)
    a2 = acum * LOG2E
    a2_t = a2.T
    src_t = a2_t - jnp.log2(dt.T)
    w_t = jnp.exp2(a2_t[:, chunk - 1:chunk] - src_t)
    e_end = jnp.exp2(a2[chunk - 1:chunk, :])
    e_all = jnp.dot(jnp.exp2(a2).astype(BF16), exp_ref[...], preferred_element_type=F32)
    head_of_lane = lax.broadcasted_iota(I32, (1, gw), 1) // SSM_HEAD_DIM

    for g in range(SSM_GROUPS):
        bg = xbc_ref[0, :, d_inner + g * D_STATE:d_inner + (g + 1) * D_STATE]
        cg = xbc_ref[0, :, d_inner + gn + g * D_STATE:d_inner + gn + (g + 1) * D_STATE]
        bt = bg.astype(F32).T
        cb = jnp.dot(cg, bt.astype(BF16), preferred_element_type=F32)
        st = state_scr[g]
        ys = jnp.dot(cg, st.astype(BF16), preferred_element_type=F32)
        xg = xbc_ref[0, :, g * gw:(g + 1) * gw].astype(F32)
        y = jnp.zeros((chunk, gw), F32)
        upd = jnp.zeros((D_STATE, gw), F32)
        sc = jnp.zeros((1, gw), F32)
        for r in range(rep):
            h = g * rep + r
            own = head_of_lane == r
            seg = a2[:, h:h + 1] - src_t[h:h + 1, :]
            m = (cb * jnp.exp2(jnp.where(causal, seg, -jnp.inf))).astype(BF16)
            xm = jnp.where(own, xg, 0.0).astype(BF16)
            y = y + jnp.dot(m, xm, preferred_element_type=F32)
            upd = upd + jnp.dot((bt * w_t[h:h + 1, :]).astype(BF16), xm, preferred_element_type=F32)
            sc = jnp.where(own, e_end[:, h:h + 1], sc)
        state_scr[g] = st * sc + upd
        gs = slice(g * gw, (g + 1) * gw)
        y = y + ys * e_all[:, gs] + xg * dsk_ref[:, gs]
        zg = z_ref[0, :, gs].astype(F32)
        y = y * (zg * _sigmoid(zg))
        y = y * lax.rsqrt(jnp.mean(y * y, axis=-1, keepdims=True) + RMS_EPS) * ng_ref[:, gs]
        o_ref[0, :, gs] = y.astype(o_ref.dtype)


def ssd(xbc, z, ha3, dt_bias, a_neg, d_skip, norm_g):
    bsz, s, c = xbc.shape
    d_inner = SSM_HEADS * SSM_HEAD_DIM
    chunk = min(SSD_CHUNK, s)
    gw = d_inner // SSM_GROUPS
    vec = lambda n: pl.BlockSpec((1, n), lambda bi, ci: (0, 0))
    expand = (jnp.arange(d_inner, dtype=I32)[None, :] // SSM_HEAD_DIM
              == jnp.arange(LANES, dtype=I32)[:, None]).astype(BF16)
    return pl.pallas_call(
        functools.partial(_ssd_kernel, chunk=chunk),
        grid=(bsz, s // chunk),
        in_specs=[pl.BlockSpec((1, chunk, c), lambda bi, ci: (bi, ci, 0)),
                  pl.BlockSpec((1, chunk, d_inner), lambda bi, ci: (bi, ci, 0)),
                  pl.BlockSpec((1, chunk, LANES), lambda bi, ci: (bi, ci, HA_DT // LANES)),
                  vec(LANES), vec(LANES), vec(d_inner), vec(d_inner),
                  pl.BlockSpec((LANES, d_inner), lambda bi, ci: (0, 0))],
        out_specs=pl.BlockSpec((1, chunk, d_inner), lambda bi, ci: (bi, ci, 0)),
        out_shape=jax.ShapeDtypeStruct((bsz, s, d_inner), BF16),
        scratch_shapes=[pltpu.VMEM((SSM_GROUPS, D_STATE, gw), F32)],
        compiler_params=_cp("parallel", "arbitrary"),
        name="ssd_scan",
    )(xbc, z, ha3, dt_bias, a_neg, d_skip, norm_g, expand)


def _xattn_kernel(q_ref, kv_ref, o_ref, *, scale):
    hd = XA_HEADS * XA_HEAD_DIM
    nt = (((1,), (1,)), ((), ()))
    for h in range(XA_HEADS):
        sl = slice(h * XA_HEAD_DIM, (h + 1) * XA_HEAD_DIM)
        q = q_ref[0, :, sl]
        k = kv_ref[0, :, sl]
        v = kv_ref[0, :, hd + h * XA_HEAD_DIM:hd + (h + 1) * XA_HEAD_DIM]
        s = lax.dot_general(q, k, nt, preferred_element_type=F32) * scale
        p = jnp.exp(s - jnp.max(s, axis=-1, keepdims=True))
        l = jnp.sum(p, axis=-1, keepdims=True)
        o = jnp.dot(p.astype(BF16), v, preferred_element_type=F32) / l
        o_ref[0, :, sl] = o.astype(o_ref.dtype)


def mem_attention(q, kv):
    bsz, s, hd = q.shape
    m = kv.shape[1]
    tq = min(XA_TQ, s)
    return pl.pallas_call(
        functools.partial(_xattn_kernel, scale=float(XA_HEAD_DIM ** -0.5)),
        grid=(bsz, s // tq),
        in_specs=[pl.BlockSpec((1, tq, hd), lambda bi, i: (bi, i, 0)),
                  pl.BlockSpec((1, m, 2 * hd), lambda bi, i: (bi, 0, 0))],
        out_specs=pl.BlockSpec((1, tq, hd), lambda bi, i: (bi, i, 0)),
        out_shape=jax.ShapeDtypeStruct((bsz, s, hd), BF16),
        compiler_params=_cp("parallel", "parallel"),
        name="mem_attention",
    )(q, kv)


def _layer_norm(v, g, b):
    mu = jnp.mean(v, axis=-1, keepdims=True)
    d = v - mu
    var = jnp.mean(d * d, axis=-1, keepdims=True)
    return d * lax.rsqrt(var + NORM_EPS) * g + b


def _merge_kernel(ol_ref, yn_ref, cm_ref, g_ref, x_ref, wuv_ref, wa_ref, wb_ref, wc_ref, wo_ref,
                  lg_ref, lb_ref, xo_ref, a_scr, *, alpha):
    tm, d = x_ref.shape
    for h in range(MLA_HEADS):
        a_scr[:, h * V_DIM:(h + 1) * V_DIM] = jnp.dot(
            ol_ref[:, h].reshape(tm, KV_RANK), wuv_ref[h],
            preferred_element_type=F32).astype(BF16)
    o_a = jnp.dot(a_scr[...], wa_ref[...], preferred_element_type=F32)
    o_b = jnp.dot(yn_ref[...], wb_ref[...], preferred_element_type=F32)
    o_c = jnp.dot(cm_ref[...], wc_ref[...], preferred_element_type=F32)
    g = g_ref[...].astype(F32)
    merged = (_sigmoid(g[:, :d]) * o_a + _sigmoid(g[:, d:2 * d]) * o_b + _sigmoid(g[:, 2 * d:]) * o_c)
    y = alpha * x_ref[...] + jnp.dot(merged.astype(BF16), wo_ref[...], preferred_element_type=F32)
    xo_ref[...] = _layer_norm(y, lg_ref[...], lb_ref[...])


def merge_ln(ol, yn, cm, g, x, wuv, wa, wb, wc, wo, ln_g, ln_b, alpha):
    t, d = x.shape
    tm = min(MERGE_TM, t)
    tq = ol.shape[2]
    row = lambda n: pl.BlockSpec((tm, n), lambda i: (i, 0))
    full = lambda a: pl.BlockSpec(a.shape, lambda i: (0,) * a.ndim, pipeline_mode=pl.Buffered(1))
    return pl.pallas_call(
        functools.partial(_merge_kernel, alpha=alpha),
        grid=(t // tm,),
        in_specs=[pl.BlockSpec((tm // tq, MLA_HEADS, tq, KV_RANK), lambda i: (i, 0, 0, 0)),
                  row(yn.shape[1]), row(cm.shape[1]), row(g.shape[1]), row(d),
                  full(wuv), full(wa), full(wb), full(wc), full(wo), full(ln_g), full(ln_b)],
        out_specs=row(d),
        out_shape=jax.ShapeDtypeStruct((t, d), F32),
        scratch_shapes=[pltpu.VMEM((tm, MLA_HEADS * V_DIM), BF16)],
        compiler_params=_cp("parallel"),
        name="merge_ln",
    )(ol, yn, cm, g, x, wuv, wa, wb, wc, wo, ln_g, ln_b)


def _first_max(v, lane):
    m = jnp.max(v, axis=-1, keepdims=True)
    idx = jnp.min(jnp.where(v == m, lane, LANES), axis=-1, keepdims=True)
    return m, idx


def _route_kernel(x_ref, rwh_ref, rwl_ref, rb_ref, info_ref, cnt_ref, carry_scr):
    tm = x_ref.shape[0]

    @pl.when(pl.program_id(0) == 0)
    def _():
        carry_scr[...] = jnp.zeros(carry_scr.shape, F32)

    x = x_ref[...]
    x_hi = x.astype(BF16)
    x_lo = (x - x_hi.astype(F32)).astype(BF16)
    logits = (jnp.dot(x_hi, rwh_ref[...], preferred_element_type=F32)
              + jnp.dot(x_lo, rwh_ref[...], preferred_element_type=F32)
              + jnp.dot(x_hi, rwl_ref[...], preferred_element_type=F32))
    scores = _sigmoid(logits)
    sel = scores + rb_ref[...]
    lane = lax.broadcasted_iota(I32, (tm, LANES), 1)
    neg = -jnp.inf
    best_score = None
    best_group = None
    for j in range(N_EXPERT_GROUPS):
        in_j = (lane >= j * EXPERTS_PER_GROUP) & (lane < (j + 1) * EXPERTS_PER_GROUP)
        v = jnp.where(in_j, sel, neg)
        m1, i1 = _first_max(v, lane)
        m2, _ = _first_max(jnp.where(lane == i1, neg, v), lane)
        gs = m1 + m2
        if j == 0:
            best_score, best_group = gs, jnp.zeros_like(i1)
        else:
            better = gs > best_score
            best_score = jnp.where(better, gs, best_score)
            best_group = jnp.where(better, j, best_group)
    lo = best_group * EXPERTS_PER_GROUP
    v = jnp.where((lane >= lo) & (lane < lo + EXPERTS_PER_GROUP), sel, neg)
    _, e1 = _first_max(v, lane)
    _, e2 = _first_max(jnp.where(lane == e1, neg, v), lane)
    w1 = jnp.sum(jnp.where(lane == e1, scores, 0.0), axis=-1, keepdims=True)
    w2 = jnp.sum(jnp.where(lane == e2, scores, 0.0), axis=-1, keepdims=True)
    wsum = w1 + w2
    member = jnp.where((lane == e1) | (lane == e2), 1.0, 0.0)
    row = lax.broadcasted_iota(I32, (tm, tm), 0)
    col = lax.broadcasted_iota(I32, (tm, tm), 1)
    before = jnp.where(row > col, 1.0, 0.0).astype(BF16)
    carry = carry_scr[0:1, :]
    rank = jnp.dot(before, member.astype(BF16), preferred_element_type=F32) + carry
    r1 = jnp.sum(jnp.where(lane == e1, rank, 0.0), axis=-1, keepdims=True)
    r2 = jnp.sum(jnp.where(lane == e2, rank, 0.0), axis=-1, keepdims=True)
    new_carry = carry + jnp.sum(member, axis=0, keepdims=True)
    carry_scr[...] = jnp.broadcast_to(new_carry, carry_scr.shape)
    cnt_ref[...] = jnp.broadcast_to(new_carry, cnt_ref.shape)
    info = jnp.where(lane == 0, e1.astype(F32),
           jnp.where(lane == 1, e2.astype(F32),
           jnp.where(lane == 2, w1 / wsum,
           jnp.where(lane == 3, w2 / wsum,
           jnp.where(lane == 4, r1,
           jnp.where(lane == 5, r2, 0.0))))))
    info_ref[...] = info


def route(x, rw_hi, rw_lo, rb):
    t, d = x.shape
    tm = min(ROUTE_TM, t)
    return pl.pallas_call(
        _route_kernel,
        grid=(t // tm,),
        in_specs=[pl.BlockSpec((tm, d), lambda i: (i, 0)),
                  pl.BlockSpec((d, LANES), lambda i: (0, 0)),
                  pl.BlockSpec((d, LANES), lambda i: (0, 0)),
                  pl.BlockSpec((1, LANES), lambda i: (0, 0))],
        out_specs=[pl.BlockSpec((tm, LANES), lambda i: (i, 0)),
                   pl.BlockSpec((8, LANES), lambda i: (0, 0))],
        out_shape=[jax.ShapeDtypeStruct((t, LANES), F32), jax.ShapeDtypeStruct((8, LANES), F32)],
        scratch_shapes=[pltpu.VMEM((8, LANES), F32)],
        compiler_params=_cp("arbitrary"),
        name="route",
    )(x, rw_hi, rw_lo, rb)


def _scatter_kernel(pos_ref, x_ref, init_ref, o_ref, sem):
    del init_ref
    tm = x_ref.shape[0]

    def copy(r, k):
        p = pos_ref[0, 0, TOP_K * r + k]
        return pltpu.make_async_copy(x_ref.at[pl.ds(r, 1)], o_ref.at[pl.ds(p, 1)], sem)

    def start(r, c):
        for k in range(TOP_K):
            copy(r, k).start(priority=k)
        return c

    def wait(r, c):
        for k in range(TOP_K):
            copy(r, k).wait()
        return c

    lax.fori_loop(0, tm, start, 0, unroll=DMA_UNROLL)
    lax.fori_loop(0, tm, wait, 0, unroll=DMA_UNROLL)


def scatter_rows(x, pos3, n_rows):
    t, d = x.shape
    tm = pos3.shape[2] // TOP_K
    init = jnp.zeros((n_rows, d), x.dtype)
    return pl.pallas_call(
        _scatter_kernel,
        grid=(t // tm,),
        in_specs=[pl.BlockSpec((1, 1, TOP_K * tm), lambda i: (i, 0, 0), memory_space=pltpu.SMEM),
                  pl.BlockSpec((tm, d), lambda i: (i, 0)),
                  pl.BlockSpec(memory_space=pl.ANY)],
        out_specs=pl.BlockSpec(memory_space=pl.ANY),
        out_shape=jax.ShapeDtypeStruct((n_rows, d), x.dtype),
        scratch_shapes=[pltpu.SemaphoreType.DMA(())],
        input_output_aliases={2: 0},
        compiler_params=_cp("arbitrary"),
        name="moe_scatter",
    )(pos3, x, init)


def _expert_kernel(te_ref, nu_ref, xs_ref, w13_ref, w2_ref, o_ref):
    del te_ref
    i = pl.program_id(0)
    f = w2_ref.shape[1]

    @pl.when(i < nu_ref[0])
    def _():
        xb = xs_ref[...].astype(BF16)
        h = jnp.dot(xb, w13_ref[0], preferred_element_type=F32)
        h1 = h[:, :f]
        act = (h1 * _sigmoid(h1) * h[:, f:]).astype(BF16)
        o_ref[...] = jnp.dot(act, w2_ref[0], preferred_element_type=F32)

    @pl.when(i >= nu_ref[0])
    def _():
        o_ref[...] = jnp.zeros(o_ref.shape, o_ref.dtype)


def expert_ffn(xs, w13, w2, tile_expert, n_used):
    p, d = xs.shape
    f = w2.shape[1]
    n_tiles = p // MOE_TM
    grid_spec = pltpu.PrefetchScalarGridSpec(
        num_scalar_prefetch=2,
        grid=(n_tiles,),
        in_specs=[pl.BlockSpec((MOE_TM, d), lambda i, te, nu: (i, 0)),
                  pl.BlockSpec((1, d, 2 * f), lambda i, te, nu: (te[i], 0, 0)),
                  pl.BlockSpec((1, f, d), lambda i, te, nu: (te[i], 0, 0))],
        out_specs=pl.BlockSpec((MOE_TM, d), lambda i, te, nu: (i, 0)),
    )
    return pl.pallas_call(
        _expert_kernel,
        grid_spec=grid_spec,
        out_shape=jax.ShapeDtypeStruct((p, d), F32),
        compiler_params=_cp("arbitrary"),
        name="expert_ffn",
    )(tile_expert, n_used, xs, w13, w2)


def _combine_kernel(pos_ref, info_ref, x_ref, ys_ref, lg_ref, lb_ref, xo_ref, xb_ref, ybuf, sem, *, alpha):
    tm = x_ref.shape[0]

    def copy(r, k):
        p = pos_ref[0, 0, TOP_K * r + k]
        return pltpu.make_async_copy(ys_ref.at[pl.ds(p, 1)], ybuf.at[k, pl.ds(r, 1)], sem)

    def start(r, c):
        for k in range(TOP_K):
            copy(r, k).start(priority=k)
        return c

    def wait(r, c):
        for k in range(TOP_K):
            copy(r, k).wait()
        return c

    lax.fori_loop(0, tm, start, 0, unroll=DMA_UNROLL)
    lax.fori_loop(0, tm, wait, 0, unroll=DMA_UNROLL)
    info = info_ref[...]
    moe = info[:, 2:3] * ybuf[0] + info[:, 3:4] * ybuf[1]
    out = _layer_norm(alpha * x_ref[...] + moe, lg_ref[...], lb_ref[...])
    xo_ref[...] = out
    xb_ref[...] = out.astype(BF16)


def combine_ln(pos3, info, x, ys, ln_g, ln_b, alpha):
    t, d = x.shape
    tm = pos3.shape[2] // TOP_K
    return pl.pallas_call(
        functools.partial(_combine_kernel, alpha=alpha),
        grid=(t // tm,),
        in_specs=[pl.BlockSpec((1, 1, TOP_K * tm), lambda i: (i, 0, 0), memory_space=pltpu.SMEM),
                  pl.BlockSpec((tm, LANES), lambda i: (i, 0)),
                  pl.BlockSpec((tm, d), lambda i: (i, 0)),
                  pl.BlockSpec(memory_space=pl.ANY),
                  pl.BlockSpec((1, d), lambda i: (0, 0)),
                  pl.BlockSpec((1, d), lambda i: (0, 0))],
        out_specs=[pl.BlockSpec((tm, d), lambda i: (i, 0)), pl.BlockSpec((tm, d), lambda i: (i, 0))],
        out_shape=[jax.ShapeDtypeStruct((t, d), F32), jax.ShapeDtypeStruct((t, d), BF16)],
        scratch_shapes=[pltpu.VMEM((TOP_K, tm, d), F32), pltpu.SemaphoreType.DMA(())],
        compiler_params=_cp("arbitrary"),
        name="moe_combine_ln",
    )(pos3, info, x, ys, ln_g, ln_b)


def _prep_layer(l, p):
    w_in = p["w_in"][l]
    d = w_in.shape[0]
    o_dq = 0
    o_dkv = o_dq + Q_RANK
    o_z = o_dkv + KV_RANK + ROPE_DIM
    d_inner = SSM_HEADS * SSM_HEAD_DIM
    conv_ch = d_inner + 2 * SSM_GROUPS * D_STATE
    o_xbc = o_z + d_inner
    o_dt = o_xbc + conv_ch
    o_qm = o_dt + SSM_HEADS
    o_g = o_qm + XA_HEADS * XA_HEAD_DIM
    half = ROPE_DIM // 2
    kr0 = o_dkv + KV_RANK
    zeros = lambda n: jnp.zeros((d, n), F32)
    w_small = jnp.concatenate([
        w_in[:, o_dq:o_dq + Q_RANK],
        w_in[:, o_dkv:o_dkv + KV_RANK],
        w_in[:, kr0:kr0 + ROPE_DIM], zeros(LANES - ROPE_DIM),
        w_in[:, kr0 + half:kr0 + ROPE_DIM], w_in[:, kr0:kr0 + half], zeros(LANES - ROPE_DIM),
        w_in[:, o_dt:o_dt + SSM_HEADS], zeros(LANES - SSM_HEADS)], axis=1)
    w_uq = p["w_uq"][l].reshape(Q_RANK, MLA_HEADS, NOPE_DIM + ROPE_DIM)
    wq_nope = w_uq[:, :, :NOPE_DIM].reshape(Q_RANK, -1)
    wq_rope = w_uq[:, :, NOPE_DIM:]
    wq_rope_sw = jnp.concatenate([wq_rope[:, :, half:], wq_rope[:, :, :half]], axis=-1)
    wq = jnp.concatenate([wq_nope, wq_rope.reshape(Q_RANK, -1), wq_rope_sw.reshape(Q_RANK, -1)], axis=1)
    w_ukv = p["w_ukv"][l]
    wuk_t = jnp.transpose(w_ukv[:, :, :NOPE_DIM], (1, 2, 0))
    wuv = jnp.transpose(w_ukv[:, :, NOPE_DIM:], (1, 0, 2))
    pad_heads = lambda v, fill: jnp.concatenate(
        [v.astype(F32), jnp.full((LANES - SSM_HEADS,), fill, F32)]).reshape(1, LANES)
    bf = lambda a: a.astype(BF16)
    return dict(
        w_small=bf(w_small), w_z=bf(w_in[:, o_z:o_z + d_inner]), w_xbc=bf(w_in[:, o_xbc:o_xbc + conv_ch]),
        w_qm=bf(w_in[:, o_qm:o_g]), w_g=bf(w_in[:, o_g:]),
        q_norm=p["q_norm"][l].reshape(1, -1), kv_norm=p["kv_norm"][l].reshape(1, -1),
        wq=bf(wq), wuk_t=bf(wuk_t), wuv=bf(wuv),
        conv_w=p["conv_w"][l], conv_b=p["conv_b"][l].reshape(1, -1),
        dt_bias=pad_heads(p["dt_bias"][l], 0.0),
        a_neg=pad_heads(-jnp.exp(p["a_log"][l].astype(F32)), 0.0),
        d_skip=jnp.repeat(p["d_skip"][l].astype(F32), SSM_HEAD_DIM).reshape(1, -1),
        ssm_norm=p["ssm_norm"][l].reshape(1, -1),
        w_mem_kv=bf(p["w_mem_kv"][l]),
        wa=bf(p["w_proj_a"][l]), wb=bf(p["w_proj_b"][l]), wc=bf(p["w_proj_c"][l]), wo=bf(p["w_out"][l]),
        ln1_g=p["ln1_g"][l].reshape(1, -1), ln1_b=p["ln1_b"][l].reshape(1, -1),
        w13=bf(jnp.concatenate([p["exp_w1"][l], p["exp_w3"][l]], axis=-1)), w2=bf(p["exp_w2"][l]),
        ln2_g=p["ln2_g"][l].reshape(1, -1), ln2_b=p["ln2_b"][l].reshape(1, -1),
    )


def _dispatch_plan(info, counts, t):
    e = info[:, 0:TOP_K].astype(I32)
    rank = info[:, 4:4 + TOP_K].astype(I32)
    cnt = counts[0, :N_EXPERTS].astype(I32)
    tiles = (cnt + MOE_TM - 1) // MOE_TM
    tile_end = jnp.cumsum(tiles)
    row_start = (tile_end - tiles) * MOE_TM
    pos = row_start[e] + rank
    n_tiles = (TOP_K * t) // MOE_TM + N_EXPERTS
    tile_expert = jnp.minimum(
        jnp.sum(jnp.arange(n_tiles, dtype=I32)[:, None] >= tile_end[None, :], axis=1), N_EXPERTS - 1)
    n_used = tile_end[-1:].astype(I32)
    tm = min(PERM_TM, t)
    pos3 = pos.reshape(t // tm, 1, TOP_K * tm)
    return pos3, tile_expert.astype(I32), n_used, n_tiles * MOE_TM


def kernel(x, mem, positions, w_in, q_norm, w_uq, kv_norm, w_ukv, w_proj_a, conv_w, conv_b, dt_bias, a_log,
           d_skip, ssm_norm, w_proj_b, w_mem_kv, w_proj_c, w_out, ln1_g, ln1_b, router_w, router_bias,
           exp_w1, exp_w3, exp_w2, ln2_g, ln2_b):
    params = dict(w_in=w_in, q_norm=q_norm, w_uq=w_uq, kv_norm=kv_norm, w_ukv=w_ukv, w_proj_a=w_proj_a,
                  conv_w=conv_w, conv_b=conv_b, dt_bias=dt_bias, a_log=a_log, d_skip=d_skip,
                  ssm_norm=ssm_norm, w_proj_b=w_proj_b, w_mem_kv=w_mem_kv, w_proj_c=w_proj_c, w_out=w_out,
                  ln1_g=ln1_g, ln1_b=ln1_b, exp_w1=exp_w1, exp_w3=exp_w3, exp_w2=exp_w2,
                  ln2_g=ln2_g, ln2_b=ln2_b)
    bsz, s, d = x.shape
    t = bsz * s
    depth = w_in.shape[0]
    alpha = float((2 * depth) ** 0.25)
    n_mem = mem.shape[1]

    inv = ROPE_THETA ** (-jnp.arange(0, ROPE_DIM, 2, dtype=F32) / ROPE_DIM)
    ang = positions.astype(F32)[..., None] * inv
    cos, sin = jnp.cos(ang), jnp.sin(ang)
    cosq = jnp.tile(jnp.concatenate([cos, cos], axis=-1), (1, 1, MLA_HEADS)).reshape(t, -1)
    sinq = jnp.tile(jnp.concatenate([-sin, sin], axis=-1), (1, 1, MLA_HEADS)).reshape(t, -1)

    rw = jnp.concatenate([router_w.astype(F32), jnp.zeros((d, LANES - N_EXPERTS), F32)], axis=1)
    rw_hi = rw.astype(BF16)
    rw_lo = (rw - rw_hi.astype(F32)).astype(BF16)
    rb = jnp.concatenate([router_bias.astype(F32), jnp.zeros((LANES - N_EXPERTS,), F32)]).reshape(1, LANES)
    mem_b = mem.reshape(bsz * n_mem, d).astype(BF16)

    xf = x.reshape(t, d).astype(F32)
    xb = xf.astype(BF16)
    for l in range(depth):
        w = _prep_layer(l, params)
        ha = matmul(xb, w["w_small"], F32, "proj_small")
        z = matmul(xb, w["w_z"], BF16, "proj_z")
        xbc = matmul(xb, w["w_xbc"], BF16, "proj_xbc")
        qm = matmul(xb, w["w_qm"], BF16, "proj_qmem")
        g = matmul(xb, w["w_g"], BF16, "proj_gate")

        ql, qr, ck, kr = mla_prep(ha, cosq, sinq, w["q_norm"], w["kv_norm"], w["wq"], w["wuk_t"],
                                  min(ATT_TQ, s))
        o_lat = mla_attention(ql, qr, ck.reshape(bsz, s, KV_RANK), kr.reshape(bsz, s, ROPE_DIM))

        xbc_c = conv_silu(xbc.reshape(bsz, s, -1), w["conv_w"], w["conv_b"])
        yn = ssd(xbc_c, z.reshape(bsz, s, -1), ha.reshape(bsz, s, HA_W), w["dt_bias"], w["a_neg"],
                 w["d_skip"], w["ssm_norm"]).reshape(t, -1)

        kv = matmul(mem_b, w["w_mem_kv"], BF16, "proj_memkv").reshape(bsz, n_mem, -1)
        cm = mem_attention(qm.reshape(bsz, s, -1), kv).reshape(t, -1)

        x1 = merge_ln(o_lat, yn, cm, g, xf, w["wuv"], w["wa"], w["wb"], w["wc"], w["wo"],
                           w["ln1_g"], w["ln1_b"], alpha)

        info, counts = route(x1, rw_hi, rw_lo, rb)
        pos3, tile_expert, n_used, n_rows = _dispatch_plan(info, counts, t)
        xs = scatter_rows(x1, pos3, n_rows)
        ys = expert_ffn(xs, w["w13"], w["w2"], tile_expert, n_used)
        xf, xb = combine_ln(pos3, info, x1, ys, w["ln2_g"], w["ln2_b"], alpha)
    return xf.reshape(bsz, s, d)
```

```python
import functools

import jax
import jax.numpy as jnp
from jax import lax
from jax.experimental import pallas as pl
from jax.experimental.pallas import tpu as pltpu

F32 = jnp.float32
BF16 = jnp.bfloat16
I32 = jnp.int32

MLA_HEADS = 8
Q_RANK = 384
KV_RANK = 256
NOPE_DIM = 128
ROPE_DIM = 64
V_DIM = 128
ROPE_THETA = 10000.0
SSM_HEADS = 32
SSM_HEAD_DIM = 64
SSM_GROUPS = 8
D_STATE = 128
CONV_K = 4
XA_HEADS = 4
XA_HEAD_DIM = 256
N_EXPERTS = 16
N_EXPERT_GROUPS = 4
EXPERTS_PER_GROUP = 4
TOP_K = 2
NORM_EPS = 1e-5
RMS_EPS = 1e-6

LANES = 128
V7X_VMEM_LIMIT = 56 * 1024 * 1024

MM_TM = 1024
MM_TN = 1024
PREP_TM = 512
ATT_TQ = 128
ATT_TK = 512
ATT_GROUP_ROWS = 512
LOG2E = 1.4426950408889634
PROJ_CONV_TN = 512
PROJ_CONV_SUB = 1024
CONV_HALO = 16
SSD_CHUNK = 256
XA_TQ = 512
MERGE_TM = 512
MOE_TM = 256
DISPATCH_TM = 256
ROW_ALIGN = 8
LOCAL_ROWS = 640

HA_W = 1024
HA_DQ = 0
HA_C = 384
HA_KR = 640
HA_KRS = 768
HA_DT = 896


def _cp(*sem):
    return pltpu.CompilerParams(dimension_semantics=sem, vmem_limit_bytes=V7X_VMEM_LIMIT)


def _sigmoid(x):
    return 1.0 / (1.0 + jnp.exp(-x))


def _mm_kernel(x_ref, w_ref, o_ref):
    o_ref[...] = jnp.dot(x_ref[...], w_ref[...], preferred_element_type=F32).astype(o_ref.dtype)


def matmul(x, w, out_dtype, name):
    m, k = x.shape
    n = w.shape[1]
    tm = min(MM_TM, m)
    tn = min(MM_TN, n)
    return pl.pallas_call(
        _mm_kernel,
        grid=(n // tn, m // tm),
        in_specs=[pl.BlockSpec((tm, k), lambda j, i: (i, 0)),
                  pl.BlockSpec((k, tn), lambda j, i: (0, j))],
        out_specs=pl.BlockSpec((tm, tn), lambda j, i: (i, j)),
        out_shape=jax.ShapeDtypeStruct((m, n), out_dtype),
        compiler_params=_cp("parallel", "parallel"),
        name=name,
    )(x, w)


def _proj_conv_kernel(x_ref, xh_ref, w_ref, cw_ref, cb_ref, o_ref, *, tiles_per_seq):
    first = (pl.program_id(1) % tiles_per_seq) == 0
    w = w_ref[...]
    cw = cw_ref[...]
    halo = jnp.dot(xh_ref[...], w, preferred_element_type=F32)
    halo = jnp.where(first, jnp.zeros_like(halo), halo)
    tm = x_ref.shape[0]
    sub = min(PROJ_CONV_SUB, tm)
    for c in range(tm // sub):
        rs = slice(c * sub, (c + 1) * sub)
        u = jnp.dot(x_ref[rs, :], w, preferred_element_type=F32)
        ext = jnp.concatenate([halo, u], axis=0)
        halo = u[sub - CONV_HALO:, :]
        prev = pltpu.roll(ext, 1, 0)
        near = cw[3:4, :] * ext + cw[2:3, :] * prev
        far = cw[1:2, :] * ext + cw[0:1, :] * prev
        acc = (near + pltpu.roll(far, 2, 0))[CONV_HALO:, :] + cb_ref[...]
        half = 0.5 * acc
        o_ref[rs, :] = (half * jnp.tanh(half) + half).astype(o_ref.dtype)


def proj_conv_silu(x, w, conv_w, conv_b, seq_len):
    m, k = x.shape
    n = w.shape[1]
    tm = min(MM_TM, seq_len)
    tn = min(PROJ_CONV_TN, n)
    hb = tm // CONV_HALO
    return pl.pallas_call(
        functools.partial(_proj_conv_kernel, tiles_per_seq=seq_len // tm),
        grid=(n // tn, m // tm),
        in_specs=[pl.BlockSpec((tm, k), lambda j, i: (i, 0)),
                  pl.BlockSpec((CONV_HALO, k), lambda j, i: (jnp.maximum(i * hb - 1, 0), 0)),
                  pl.BlockSpec((k, tn), lambda j, i: (0, j)),
                  pl.BlockSpec((CONV_K, tn), lambda j, i: (0, j)),
                  pl.BlockSpec((1, tn), lambda j, i: (0, j))],
        out_specs=pl.BlockSpec((tm, tn), lambda j, i: (i, j)),
        out_shape=jax.ShapeDtypeStruct((m, n), BF16),
        compiler_params=_cp("parallel", "parallel"),
        name="proj_xbc_conv",
    )(x, x, w, conv_w, conv_b)


def _mla_prep_kernel(ha_ref, cos_ref, sin_ref, qn_ref, kvn_ref, wq_ref, wuk_ref,
                     ql_ref, qr_ref, ck_ref, kr_ref, *, scale):
    nb, _, tq, _ = ql_ref.shape
    ha = ha_ref[...]
    dq = ha[:, HA_DQ:HA_DQ + Q_RANK]
    c_q = dq * lax.rsqrt(jnp.mean(dq * dq, axis=-1, keepdims=True) + RMS_EPS) * qn_ref[...]
    q = jnp.dot(c_q.astype(BF16), wq_ref[...], preferred_element_type=F32)
    n_nope = MLA_HEADS * NOPE_DIM
    n_rope = MLA_HEADS * ROPE_DIM
    cosq = cos_ref[...]
    sinq = sin_ref[...]
    q_rope = ((q[:, n_nope:n_nope + n_rope] * cosq + q[:, n_nope + n_rope:] * sinq) * scale).astype(BF16)
    for h in range(MLA_HEADS):
        qh = q[:, h * NOPE_DIM:(h + 1) * NOPE_DIM].astype(BF16)
        ql = jnp.dot(qh, wuk_ref[h], preferred_element_type=F32)
        ql_ref[:, h] = (ql * scale).astype(BF16).reshape(nb, tq, KV_RANK)
        qr_ref[:, h] = q_rope[:, h * ROPE_DIM:(h + 1) * ROPE_DIM].reshape(nb, tq, ROPE_DIM)
    c = ha[:, HA_C:HA_C + KV_RANK]
    c_kv = c * lax.rsqrt(jnp.mean(c * c, axis=-1, keepdims=True) + RMS_EPS) * kvn_ref[...]
    ck_ref[...] = c_kv.astype(BF16)
    k_rope = (ha[:, HA_KR:HA_KR + LANES] * cosq[:, :LANES]
              + ha[:, HA_KRS:HA_KRS + LANES] * sinq[:, :LANES])
    kr_ref[...] = k_rope[:, :ROPE_DIM].astype(BF16)


def mla_prep(ha, cosq, sinq, q_norm, kv_norm, wq, wuk_t, tq):
    t = ha.shape[0]
    tm = min(PREP_TM, t)
    nb = tm // tq
    scale = float((NOPE_DIM + ROPE_DIM) ** -0.5 * LOG2E)
    n_rope = MLA_HEADS * ROPE_DIM
    full = lambda shape: pl.BlockSpec(shape, lambda i: (0,) * len(shape))
    return pl.pallas_call(
        functools.partial(_mla_prep_kernel, scale=scale),
        grid=(t // tm,),
        in_specs=[pl.BlockSpec((tm, HA_W), lambda i: (i, 0)),
                  pl.BlockSpec((tm, n_rope), lambda i: (i, 0)),
                  pl.BlockSpec((tm, n_rope), lambda i: (i, 0)),
                  full((1, Q_RANK)), full((1, KV_RANK)),
                  full(wq.shape), full(wuk_t.shape)],
        out_specs=[pl.BlockSpec((nb, MLA_HEADS, tq, KV_RANK), lambda i: (i, 0, 0, 0)),
                   pl.BlockSpec((nb, MLA_HEADS, tq, ROPE_DIM), lambda i: (i, 0, 0, 0)),
                   pl.BlockSpec((tm, KV_RANK), lambda i: (i, 0)),
                   pl.BlockSpec((tm, ROPE_DIM), lambda i: (i, 0))],
        out_shape=[jax.ShapeDtypeStruct((t // tq, MLA_HEADS, tq, KV_RANK), BF16),
                   jax.ShapeDtypeStruct((t // tq, MLA_HEADS, tq, ROPE_DIM), BF16),
                   jax.ShapeDtypeStruct((t, KV_RANK), BF16),
                   jax.ShapeDtypeStruct((t, ROPE_DIM), BF16)],
        compiler_params=_cp("parallel"),
        name="mla_prep",
    )(ha, cosq, sinq, q_norm, kv_norm, wq, wuk_t)


def _mla_attn_kernel(ql_ref, qr_ref, ck_ref, kr_ref, o_ref, m_scr, l_scr, acc_scr, *, tq, tk, hpc):
    rc = hpc * tq
    n_groups = MLA_HEADS // hpc
    q_start = pl.program_id(1) * tq
    n_full = q_start // tk
    m_scr[...] = jnp.full(m_scr.shape, -jnp.inf, F32)
    l_scr[...] = jnp.zeros(l_scr.shape, F32)
    acc_scr[...] = jnp.zeros(acc_scr.shape, F32)
    nt = (((1,), (1,)), ((), ()))

    def step(j, masked):
        ks = pl.multiple_of(j * tk, tk)
        ck = ck_ref[0, pl.ds(ks, tk), :]
        kr = kr_ref[0, pl.ds(ks, tk), :]
        if masked:
            q_pos = q_start + (lax.broadcasted_iota(I32, (rc, tk), 0) & (tq - 1))
            k_pos = ks + lax.broadcasted_iota(I32, (rc, tk), 1)
            visible = k_pos <= q_pos
        for c in range(n_groups):
            rs = slice(c * rc, (c + 1) * rc)
            ql = ql_ref[0, c * hpc:(c + 1) * hpc].reshape(rc, KV_RANK)
            qr = qr_ref[0, c * hpc:(c + 1) * hpc].reshape(rc, ROPE_DIM)
            s = (lax.dot_general(ql, ck, nt, preferred_element_type=F32)
                 + lax.dot_general(qr, kr, nt, preferred_element_type=F32))
            if masked:
                s = jnp.where(visible, s, -jnp.inf)
            m_prev = m_scr[rs]
            m_new = jnp.maximum(m_prev, jnp.max(s, axis=-1, keepdims=True))
            alpha = jnp.exp2(m_prev - m_new)
            p = jnp.exp2(s - jnp.tile(m_new, (1, tk // LANES)))
            l_scr[rs] = alpha * l_scr[rs] + jnp.sum(p, axis=-1, keepdims=True)
            acc_scr[rs] = (jnp.tile(alpha, (1, KV_RANK // LANES)) * acc_scr[rs]
                           + jnp.dot(p.astype(BF16), ck, preferred_element_type=F32))
            m_scr[rs] = m_new

    def body(j, carry):
        step(j, False)
        return carry

    lax.fori_loop(0, n_full, body, 0)
    step(n_full, True)
    out = acc_scr[...] / jnp.tile(l_scr[...], (1, KV_RANK // LANES))
    o_ref[0] = out.astype(o_ref.dtype).reshape(MLA_HEADS, tq, KV_RANK)


def mla_attention(ql, qr, ck, kr):
    _, _, tq, _ = ql.shape
    b, s, _ = ck.shape
    tk = min(ATT_TK, s)
    assert tk % tq == 0 and s % tk == 0 and tq & (tq - 1) == 0
    nq = s // tq
    rows = tq * MLA_HEADS
    hpc = max(1, min(MLA_HEADS, ATT_GROUP_ROWS // tq))
    qspec = lambda dim: pl.BlockSpec((1, MLA_HEADS, tq, dim), lambda bi, i: (bi * nq + i, 0, 0, 0))
    return pl.pallas_call(
        functools.partial(_mla_attn_kernel, tq=tq, tk=tk, hpc=hpc),
        grid=(b, nq),
        in_specs=[qspec(KV_RANK), qspec(ROPE_DIM),
                  pl.BlockSpec((1, s, KV_RANK), lambda bi, i: (bi, 0, 0)),
                  pl.BlockSpec((1, s, ROPE_DIM), lambda bi, i: (bi, 0, 0))],
        out_specs=qspec(KV_RANK),
        out_shape=jax.ShapeDtypeStruct(ql.shape, BF16),
        scratch_shapes=[pltpu.VMEM((rows, LANES), F32), pltpu.VMEM((rows, LANES), F32),
                        pltpu.VMEM((rows, KV_RANK), F32)],
        compiler_params=_cp("parallel", "parallel"),
        name="mla_attention",
    )(ql, qr, ck, kr)


def _ssd_kernel(xbc_ref, z_ref, dt_ref, dtb_ref, a_ref, dsk_ref, ng_ref, exp_ref, o_ref,
                state_scr, *, chunk):
    d_inner = SSM_HEADS * SSM_HEAD_DIM
    gn = SSM_GROUPS * D_STATE
    rep = SSM_HEADS // SSM_GROUPS
    gw = rep * SSM_HEAD_DIM

    @pl.when(pl.program_id(1) == 0)
    def _():
        state_scr[...] = jnp.zeros(state_scr.shape, F32)

    x_raw = dt_ref[0] + dtb_ref[...]
    dt = jnp.maximum(x_raw, 0.0) + jnp.log(1.0 + jnp.exp(-jnp.abs(x_raw)))
    da = dt * a_ref[...]
    row = lax.broadcasted_iota(I32, (chunk, chunk), 0)
    col = lax.broadcasted_iota(I32, (chunk, chunk), 1)
    causal = row >= col
    tri = jnp.where(causal, 1.0, 0.0).astype(BF16)
    acum = jnp.zeros(da.shape, F32)
    rem = da
    for _ in range(3):
        part = rem.astype(BF16)
        acum = acum + jnp.dot(tri, part, preferred_element_type=F32)
        rem = rem - part.astype(F32)
    a2 = acum * LOG2E
    a2_t = a2.T
    src_t = a2_t - jnp.log2(dt.T)
    w_t = jnp.exp2(a2_t[:, chunk - 1:chunk] - src_t)
    e_end = jnp.exp2(a2[chunk - 1:chunk, :])
    e_all = jnp.dot(jnp.exp2(a2).astype(BF16), exp_ref[...], preferred_element_type=F32)
    head_of_lane = lax.broadcasted_iota(I32, (1, gw), 1) // SSM_HEAD_DIM

    for g in range(SSM_GROUPS):
        bg = xbc_ref[0, :, d_inner + g * D_STATE:d_inner + (g + 1) * D_STATE]
        cg = xbc_ref[0, :, d_inner + gn + g * D_STATE:d_inner + gn + (g + 1) * D_STATE]
        bt = bg.astype(F32).T
        cb = jnp.dot(cg, bt.astype(BF16), preferred_element_type=F32)
        st = state_scr[g]
        ys = jnp.dot(cg, st.astype(BF16), preferred_element_type=F32)
        xg = xbc_ref[0, :, g * gw:(g + 1) * gw].astype(F32)
        y = jnp.zeros((chunk, gw), F32)
        upd = jnp.zeros((D_STATE, gw), F32)
        sc = jnp.zeros((1, gw), F32)
        for r in range(rep):
            h = g * rep + r
            own = head_of_lane == r
            seg = a2[:, h:h + 1] - src_t[h:h + 1, :]
            m = (cb * jnp.exp2(jnp.where(causal, seg, -jnp.inf))).astype(BF16)
            xm = jnp.where(own, xg, 0.0).astype(BF16)
            y = y + jnp.dot(m, xm, preferred_element_type=F32)
            upd = upd + jnp.dot((bt * w_t[h:h + 1, :]).astype(BF16), xm, preferred_element_type=F32)
            sc = jnp.where(own, e_end[:, h:h + 1], sc)
        state_scr[g] = st * sc + upd
        gs = slice(g * gw, (g + 1) * gw)
        y = y + ys * e_all[:, gs] + xg * dsk_ref[:, gs]
        zg = z_ref[0, :, gs].astype(F32)
        y = y * (zg * _sigmoid(zg))
        y = y * lax.rsqrt(jnp.mean(y * y, axis=-1, keepdims=True) + RMS_EPS) * ng_ref[:, gs]
        o_ref[0, :, gs] = y.astype(o_ref.dtype)


def ssd(xbc, z, ha3, dt_bias, a_neg, d_skip, norm_g):
    bsz, s, c = xbc.shape
    d_inner = SSM_HEADS * SSM_HEAD_DIM
    chunk = min(SSD_CHUNK, s)
    gw = d_inner // SSM_GROUPS
    vec = lambda n: pl.BlockSpec((1, n), lambda bi, ci: (0, 0))
    expand = (jnp.arange(d_inner, dtype=I32)[None, :] // SSM_HEAD_DIM
              == jnp.arange(LANES, dtype=I32)[:, None]).astype(BF16)
    return pl.pallas_call(
        functools.partial(_ssd_kernel, chunk=chunk),
        grid=(bsz, s // chunk),
        in_specs=[pl.BlockSpec((1, chunk, c), lambda bi, ci: (bi, ci, 0)),
                  pl.BlockSpec((1, chunk, d_inner), lambda bi, ci: (bi, ci, 0)),
                  pl.BlockSpec((1, chunk, LANES), lambda bi, ci: (bi, ci, HA_DT // LANES)),
                  vec(LANES), vec(LANES), vec(d_inner), vec(d_inner),
                  pl.BlockSpec((LANES, d_inner), lambda bi, ci: (0, 0))],
        out_specs=pl.BlockSpec((1, chunk, d_inner), lambda bi, ci: (bi, ci, 0)),
        out_shape=jax.ShapeDtypeStruct((bsz, s, d_inner), BF16),
        scratch_shapes=[pltpu.VMEM((SSM_GROUPS, D_STATE, gw), F32)],
        compiler_params=_cp("parallel", "arbitrary"),
        name="ssd_scan",
    )(xbc, z, ha3, dt_bias, a_neg, d_skip, norm_g, expand)


def _xattn_kernel(q_ref, kv_ref, o_ref, *, scale):
    hd = XA_HEADS * XA_HEAD_DIM
    nt = (((1,), (1,)), ((), ()))
    for h in range(XA_HEADS):
        sl = slice(h * XA_HEAD_DIM, (h + 1) * XA_HEAD_DIM)
        q = q_ref[0, :, sl]
        k = kv_ref[0, :, sl]
        v = kv_ref[0, :, hd + h * XA_HEAD_DIM:hd + (h + 1) * XA_HEAD_DIM]
        s = lax.dot_general(q, k, nt, preferred_element_type=F32) * scale
        p = jnp.exp(s - jnp.max(s, axis=-1, keepdims=True))
        l = jnp.sum(p, axis=-1, keepdims=True)
        o = jnp.dot(p.astype(BF16), v, preferred_element_type=F32) / l
        o_ref[0, :, sl] = o.astype(o_ref.dtype)


def mem_attention(q, kv):
    bsz, s, hd = q.shape
    m = kv.shape[1]
    tq = min(XA_TQ, s)
    return pl.pallas_call(
        functools.partial(_xattn_kernel, scale=float(XA_HEAD_DIM ** -0.5)),
        grid=(bsz, s // tq),
        in_specs=[pl.BlockSpec((1, tq, hd), lambda bi, i: (bi, i, 0)),
                  pl.BlockSpec((1, m, 2 * hd), lambda bi, i: (bi, 0, 0))],
        out_specs=pl.BlockSpec((1, tq, hd), lambda bi, i: (bi, i, 0)),
        out_shape=jax.ShapeDtypeStruct((bsz, s, hd), BF16),
        compiler_params=_cp("parallel", "parallel"),
        name="mem_attention",
    )(q, kv)


def _layer_norm(v, g, b):
    mu = jnp.mean(v, axis=-1, keepdims=True)
    d = v - mu
    var = jnp.mean(d * d, axis=-1, keepdims=True)
    return d * lax.rsqrt(var + NORM_EPS) * g + b


def _merge_kernel(ol_ref, yn_ref, cm_ref, g_ref, x_ref, wuv_ref, wa_ref, wb_ref, wc_ref, wo_ref,
                  lg_ref, lb_ref, xo_ref, a_scr, *, alpha):
    tm, d = x_ref.shape
    for h in range(MLA_HEADS):
        a_scr[:, h * V_DIM:(h + 1) * V_DIM] = jnp.dot(
            ol_ref[:, h].reshape(tm, KV_RANK), wuv_ref[h],
            preferred_element_type=F32).astype(BF16)
    o_a = jnp.dot(a_scr[...], wa_ref[...], preferred_element_type=F32)
    o_b = jnp.dot(yn_ref[...], wb_ref[...], preferred_element_type=F32)
    o_c = jnp.dot(cm_ref[...], wc_ref[...], preferred_element_type=F32)
    g = g_ref[...].astype(F32)
    merged = (_sigmoid(g[:, :d]) * o_a + _sigmoid(g[:, d:2 * d]) * o_b + _sigmoid(g[:, 2 * d:]) * o_c)
    y = alpha * x_ref[...] + jnp.dot(merged.astype(BF16), wo_ref[...], preferred_element_type=F32)
    xo_ref[...] = _layer_norm(y, lg_ref[...], lb_ref[...])


def merge_ln(ol, yn, cm, g, x, wuv, wa, wb, wc, wo, ln_g, ln_b, alpha):
    t, d = x.shape
    tm = min(MERGE_TM, t)
    tq = ol.shape[2]
    row = lambda n: pl.BlockSpec((tm, n), lambda i: (i, 0))
    full = lambda a: pl.BlockSpec(a.shape, lambda i: (0,) * a.ndim, pipeline_mode=pl.Buffered(1))
    return pl.pallas_call(
        functools.partial(_merge_kernel, alpha=alpha),
        grid=(t // tm,),
        in_specs=[pl.BlockSpec((tm // tq, MLA_HEADS, tq, KV_RANK), lambda i: (i, 0, 0, 0)),
                  row(yn.shape[1]), row(cm.shape[1]), row(g.shape[1]), row(d),
                  full(wuv), full(wa), full(wb), full(wc), full(wo), full(ln_g), full(ln_b)],
        out_specs=row(d),
        out_shape=jax.ShapeDtypeStruct((t, d), F32),
        scratch_shapes=[pltpu.VMEM((tm, MLA_HEADS * V_DIM), BF16)],
        compiler_params=_cp("parallel"),
        name="merge_ln",
    )(ol, yn, cm, g, x, wuv, wa, wb, wc, wo, ln_g, ln_b)


def _first_max(v, lane):
    m = jnp.max(v, axis=-1, keepdims=True)
    idx = jnp.min(jnp.where(v == m, lane, LANES), axis=-1, keepdims=True)
    return m, idx


def _route_kernel(x_ref, rwh_ref, rwl_ref, rb_ref, info_ref, tinfo_ref, cnt_ref, carry_scr):
    tm = x_ref.shape[0]

    @pl.when(pl.program_id(0) == 0)
    def _():
        carry_scr[...] = jnp.zeros(carry_scr.shape, F32)

    x = x_ref[...]
    x_hi = x.astype(BF16)
    x_lo = (x - x_hi.astype(F32)).astype(BF16)
    logits = (jnp.dot(x_hi, rwh_ref[...], preferred_element_type=F32)
              + jnp.dot(x_lo, rwh_ref[...], preferred_element_type=F32)
              + jnp.dot(x_hi, rwl_ref[...], preferred_element_type=F32))
    scores = _sigmoid(logits)
    sel = scores + rb_ref[...]
    lane = lax.broadcasted_iota(I32, (tm, LANES), 1)
    neg = -jnp.inf
    best_score = None
    best_group = None
    for j in range(N_EXPERT_GROUPS):
        in_j = (lane >= j * EXPERTS_PER_GROUP) & (lane < (j + 1) * EXPERTS_PER_GROUP)
        v = jnp.where(in_j, sel, neg)
        m1, i1 = _first_max(v, lane)
        m2, _ = _first_max(jnp.where(lane == i1, neg, v), lane)
        gs = m1 + m2
        if j == 0:
            best_score, best_group = gs, jnp.zeros_like(i1)
        else:
            better = gs > best_score
            best_score = jnp.where(better, gs, best_score)
            best_group = jnp.where(better, j, best_group)
    lo = best_group * EXPERTS_PER_GROUP
    v = jnp.where((lane >= lo) & (lane < lo + EXPERTS_PER_GROUP), sel, neg)
    _, e1 = _first_max(v, lane)
    _, e2 = _first_max(jnp.where(lane == e1, neg, v), lane)
    w1 = jnp.sum(jnp.where(lane == e1, scores, 0.0), axis=-1, keepdims=True)
    w2 = jnp.sum(jnp.where(lane == e2, scores, 0.0), axis=-1, keepdims=True)
    wsum = w1 + w2
    member = jnp.where(lane == e1, 1.0, jnp.where(lane == e2, 1.0, 0.0))
    row = lax.broadcasted_iota(I32, (tm, tm), 0)
    col = lax.broadcasted_iota(I32, (tm, tm), 1)
    before = jnp.where(row > col, 1.0, 0.0).astype(BF16)
    lrank = jnp.dot(before, member.astype(BF16), preferred_element_type=F32)
    n = jnp.sum(member, axis=0, keepdims=True)
    n8 = jnp.floor((n + (ROW_ALIGN - 1)) * (1.0 / ROW_ALIGN)) * ROW_ALIGN
    ua = lax.broadcasted_iota(I32, (LANES, LANES), 0)
    ub = lax.broadcasted_iota(I32, (LANES, LANES), 1)
    lower_experts = jnp.where(ua < ub, 1.0, 0.0).astype(BF16)
    n8_rows = jnp.broadcast_to(n8, (8, LANES))
    block_start = jnp.dot(n8_rows.astype(BF16), lower_experts, preferred_element_type=F32)
    local_row = block_start[0:1, :] + lrank
    j1 = jnp.sum(jnp.where(lane == e1, local_row, 0.0), axis=-1, keepdims=True)
    j2 = jnp.sum(jnp.where(lane == e2, local_row, 0.0), axis=-1, keepdims=True)
    carry = carry_scr[...]
    sub = lax.broadcasted_iota(I32, (8, LANES), 0)
    tinfo_ref[0] = jnp.where(sub == 0, n8_rows, jnp.where(sub == 1, block_start, jnp.where(sub == 2, carry, 0.0)))
    new_carry = carry + n8_rows
    carry_scr[...] = new_carry
    cnt_ref[...] = new_carry
    info = jnp.where(lane == 0, e1.astype(F32),
           jnp.where(lane == 1, e2.astype(F32),
           jnp.where(lane == 2, w1 / wsum,
           jnp.where(lane == 3, w2 / wsum,
           jnp.where(lane == 4, j1,
           jnp.where(lane == 5, j2, 0.0))))))
    info_ref[...] = info


def route(x, rw_hi, rw_lo, rb):
    t, d = x.shape
    tm = min(DISPATCH_TM, t)
    return pl.pallas_call(
        _route_kernel,
        grid=(t // tm,),
        in_specs=[pl.BlockSpec((tm, d), lambda i: (i, 0)),
                  pl.BlockSpec((d, LANES), lambda i: (0, 0)),
                  pl.BlockSpec((d, LANES), lambda i: (0, 0)),
                  pl.BlockSpec((1, LANES), lambda i: (0, 0))],
        out_specs=[pl.BlockSpec((tm, LANES), lambda i: (i, 0)),
                   pl.BlockSpec((1, 8, LANES), lambda i: (i, 0, 0)),
                   pl.BlockSpec((8, LANES), lambda i: (0, 0))],
        out_shape=[jax.ShapeDtypeStruct((t, LANES), F32),
                   jax.ShapeDtypeStruct((t // tm, 8, LANES), F32),
                   jax.ShapeDtypeStruct((8, LANES), F32)],
        scratch_shapes=[pltpu.VMEM((8, LANES), F32)],
        compiler_params=_cp("arbitrary"),
        name="route",
    )(x, rw_hi, rw_lo, rb)


def _block_copies(plan_ref, local_ref, sorted_ref, sem, to_sorted, start):
    for e in range(N_EXPERTS):
        count = plan_ref[0, 0, e]
        loc0 = plan_ref[0, 0, N_EXPERTS + e]
        dst0 = plan_ref[0, 0, 2 * N_EXPERTS + e]

        def body(k, carry, loc0=loc0, dst0=dst0):
            loc = local_ref.at[pl.ds(pl.multiple_of(loc0 + k * ROW_ALIGN, ROW_ALIGN), ROW_ALIGN)]
            srt = sorted_ref.at[pl.ds(pl.multiple_of(dst0 + k * ROW_ALIGN, ROW_ALIGN), ROW_ALIGN)]
            cp = pltpu.make_async_copy(loc, srt, sem) if to_sorted else pltpu.make_async_copy(srt, loc, sem)
            if start:
                cp.start()
            else:
                cp.wait()
            return carry

        lax.fori_loop(0, count, body, 0)


def _dispatch_kernel(plan_ref, info_ref, x_ref, init_ref, o_ref, local_scr, sem):
    del init_ref
    tm = x_ref.shape[0]
    info_t = info_ref[...].T
    j1 = info_t[4:5, :]
    j2 = info_t[5:6, :]
    rowid = lax.broadcasted_iota(I32, (LOCAL_ROWS, tm), 0).astype(F32)
    sel = jnp.where(rowid == j1, 1.0, jnp.where(rowid == j2, 1.0, 0.0)).astype(BF16)
    local_scr[...] = jnp.dot(sel, x_ref[...].astype(BF16), preferred_element_type=F32)
    _block_copies(plan_ref, local_scr, o_ref, sem, True, True)
    _block_copies(plan_ref, local_scr, o_ref, sem, True, False)


def dispatch_rows(x, info, plan, n_rows):
    t, d = x.shape
    tm = min(DISPATCH_TM, t)
    init = jnp.zeros((n_rows, d), F32)
    return pl.pallas_call(
        _dispatch_kernel,
        grid=(t // tm,),
        in_specs=[pl.BlockSpec((1, 1, 3 * N_EXPERTS), lambda i: (i, 0, 0), memory_space=pltpu.SMEM),
                  pl.BlockSpec((tm, LANES), lambda i: (i, 0)),
                  pl.BlockSpec((tm, d), lambda i: (i, 0)),
                  pl.BlockSpec(memory_space=pl.ANY)],
        out_specs=pl.BlockSpec(memory_space=pl.ANY),
        out_shape=jax.ShapeDtypeStruct((n_rows, d), F32),
        scratch_shapes=[pltpu.VMEM((LOCAL_ROWS, d), F32), pltpu.SemaphoreType.DMA(())],
        input_output_aliases={3: 0},
        compiler_params=_cp("arbitrary"),
        name="moe_dispatch",
    )(plan, info, x, init)


def _expert_kernel(te_ref, nu_ref, xs_ref, w13_ref, w2_ref, o_ref):
    del te_ref
    i = pl.program_id(0)
    f = w2_ref.shape[1]

    @pl.when(i < nu_ref[0])
    def _():
        xb = xs_ref[...].astype(BF16)
        h = jnp.dot(xb, w13_ref[0], preferred_element_type=F32)
        h1 = h[:, :f]
        act = (h1 * _sigmoid(h1) * h[:, f:]).astype(BF16)
        o_ref[...] = jnp.dot(act, w2_ref[0], preferred_element_type=F32)

    @pl.when(i >= nu_ref[0])
    def _():
        o_ref[...] = jnp.zeros(o_ref.shape, o_ref.dtype)


def expert_ffn(xs, w13, w2, tile_expert, n_used):
    p, d = xs.shape
    f = w2.shape[1]
    n_tiles = p // MOE_TM
    grid_spec = pltpu.PrefetchScalarGridSpec(
        num_scalar_prefetch=2,
        grid=(n_tiles,),
        in_specs=[pl.BlockSpec((MOE_TM, d), lambda i, te, nu: (i, 0)),
                  pl.BlockSpec((1, d, 2 * f), lambda i, te, nu: (te[i], 0, 0)),
                  pl.BlockSpec((1, f, d), lambda i, te, nu: (te[i], 0, 0))],
        out_specs=pl.BlockSpec((MOE_TM, d), lambda i, te, nu: (i, 0)),
    )
    return pl.pallas_call(
        _expert_kernel,
        grid_spec=grid_spec,
        out_shape=jax.ShapeDtypeStruct((p, d), F32),
        compiler_params=_cp("arbitrary"),
        name="expert_ffn",
    )(tile_expert, n_used, xs, w13, w2)


def _combine_kernel(plan_ref, info_ref, x_ref, ys_ref, lg_ref, lb_ref, xo_ref, xb_ref, local_scr, sem, *, alpha):
    tm = x_ref.shape[0]
    _block_copies(plan_ref, local_scr, ys_ref, sem, False, True)
    _block_copies(plan_ref, local_scr, ys_ref, sem, False, False)
    last = N_EXPERTS - 1
    used = plan_ref[0, 0, N_EXPERTS + last] + plan_ref[0, 0, last] * ROW_ALIGN
    rowid = lax.broadcasted_iota(I32, (LOCAL_ROWS, 1), 0)
    y = jnp.where(rowid < used, local_scr[...], 0.0).astype(BF16)
    info = info_ref[...]
    col = lax.broadcasted_iota(I32, (tm, LOCAL_ROWS), 1).astype(F32)
    gate = jnp.where(col == info[:, 4:5], info[:, 2:3],
                     jnp.where(col == info[:, 5:6], info[:, 3:4], 0.0)).astype(BF16)
    moe = jnp.dot(gate, y, preferred_element_type=F32)
    out = _layer_norm(alpha * x_ref[...] + moe, lg_ref[...], lb_ref[...])
    xo_ref[...] = out
    xb_ref[...] = out.astype(BF16)


def combine_ln(plan, info, x, ys, ln_g, ln_b, alpha):
    t, d = x.shape
    tm = min(DISPATCH_TM, t)
    return pl.pallas_call(
        functools.partial(_combine_kernel, alpha=alpha),
        grid=(t // tm,),
        in_specs=[pl.BlockSpec((1, 1, 3 * N_EXPERTS), lambda i: (i, 0, 0), memory_space=pltpu.SMEM),
                  pl.BlockSpec((tm, LANES), lambda i: (i, 0)),
                  pl.BlockSpec((tm, d), lambda i: (i, 0)),
                  pl.BlockSpec(memory_space=pl.ANY),
                  pl.BlockSpec((1, d), lambda i: (0, 0)),
                  pl.BlockSpec((1, d), lambda i: (0, 0))],
        out_specs=[pl.BlockSpec((tm, d), lambda i: (i, 0)), pl.BlockSpec((tm, d), lambda i: (i, 0))],
        out_shape=[jax.ShapeDtypeStruct((t, d), F32), jax.ShapeDtypeStruct((t, d), BF16)],
        scratch_shapes=[pltpu.VMEM((LOCAL_ROWS, d), F32), pltpu.SemaphoreType.DMA(())],
        compiler_params=_cp("arbitrary"),
        name="moe_combine_ln",
    )(plan, info, x, ys, ln_g, ln_b)


def _prep_layer(l, p):
    w_in = p["w_in"][l]
    d = w_in.shape[0]
    o_dq = 0
    o_dkv = o_dq + Q_RANK
    o_z = o_dkv + KV_RANK + ROPE_DIM
    d_inner = SSM_HEADS * SSM_HEAD_DIM
    conv_ch = d_inner + 2 * SSM_GROUPS * D_STATE
    o_xbc = o_z + d_inner
    o_dt = o_xbc + conv_ch
    o_qm = o_dt + SSM_HEADS
    o_g = o_qm + XA_HEADS * XA_HEAD_DIM
    half = ROPE_DIM // 2
    kr0 = o_dkv + KV_RANK
    zeros = lambda n: jnp.zeros((d, n), F32)
    w_small = jnp.concatenate([
        w_in[:, o_dq:o_dq + Q_RANK],
        w_in[:, o_dkv:o_dkv + KV_RANK],
        w_in[:, kr0:kr0 + ROPE_DIM], zeros(LANES - ROPE_DIM),
        w_in[:, kr0 + half:kr0 + ROPE_DIM], w_in[:, kr0:kr0 + half], zeros(LANES - ROPE_DIM),
        w_in[:, o_dt:o_dt + SSM_HEADS], zeros(LANES - SSM_HEADS)], axis=1)
    w_uq = p["w_uq"][l].reshape(Q_RANK, MLA_HEADS, NOPE_DIM + ROPE_DIM)
    wq_nope = w_uq[:, :, :NOPE_DIM].reshape(Q_RANK, -1)
    wq_rope = w_uq[:, :, NOPE_DIM:]
    wq_rope_sw = jnp.concatenate([wq_rope[:, :, half:], wq_rope[:, :, :half]], axis=-1)
    wq = jnp.concatenate([wq_nope, wq_rope.reshape(Q_RANK, -1), wq_rope_sw.reshape(Q_RANK, -1)], axis=1)
    w_ukv = p["w_ukv"][l]
    wuk_t = jnp.transpose(w_ukv[:, :, :NOPE_DIM], (1, 2, 0))
    wuv = jnp.transpose(w_ukv[:, :, NOPE_DIM:], (1, 0, 2))
    pad_heads = lambda v, fill: jnp.concatenate(
        [v.astype(F32), jnp.full((LANES - SSM_HEADS,), fill, F32)]).reshape(1, LANES)
    bf = lambda a: a.astype(BF16)
    return dict(
        w_small=bf(w_small), w_z=bf(w_in[:, o_z:o_z + d_inner]), w_xbc=bf(w_in[:, o_xbc:o_xbc + conv_ch]),
        w_qm=bf(w_in[:, o_qm:o_g]), w_g=bf(w_in[:, o_g:]),
        q_norm=p["q_norm"][l].reshape(1, -1), kv_norm=p["kv_norm"][l].reshape(1, -1),
        wq=bf(wq), wuk_t=bf(wuk_t), wuv=bf(wuv),
        conv_w=p["conv_w"][l], conv_b=p["conv_b"][l].reshape(1, -1),
        dt_bias=pad_heads(p["dt_bias"][l], 0.0),
        a_neg=pad_heads(-jnp.exp(p["a_log"][l].astype(F32)), 0.0),
        d_skip=jnp.repeat(p["d_skip"][l].astype(F32), SSM_HEAD_DIM).reshape(1, -1),
        ssm_norm=p["ssm_norm"][l].reshape(1, -1),
        w_mem_kv=bf(p["w_mem_kv"][l]),
        wa=bf(p["w_proj_a"][l]), wb=bf(p["w_proj_b"][l]), wc=bf(p["w_proj_c"][l]), wo=bf(p["w_out"][l]),
        ln1_g=p["ln1_g"][l].reshape(1, -1), ln1_b=p["ln1_b"][l].reshape(1, -1),
        w13=bf(jnp.concatenate([p["exp_w1"][l], p["exp_w3"][l]], axis=-1)), w2=bf(p["exp_w2"][l]),
        ln2_g=p["ln2_g"][l].reshape(1, -1), ln2_b=p["ln2_b"][l].reshape(1, -1),
    )


def _dispatch_plan(tinfo, counts, t):
    n_tok_tiles = tinfo.shape[0]
    n8 = tinfo[:, 0, :N_EXPERTS].astype(I32)
    block_start = tinfo[:, 1, :N_EXPERTS].astype(I32)
    carry = tinfo[:, 2, :N_EXPERTS].astype(I32)
    rows = counts[0, :N_EXPERTS].astype(I32)
    tiles = (rows + MOE_TM - 1) // MOE_TM
    tile_end = jnp.cumsum(tiles)
    row_start = (tile_end - tiles) * MOE_TM
    plan = jnp.concatenate([n8 // ROW_ALIGN, block_start, row_start[None, :] + carry], axis=1)
    max_rows = TOP_K * t + N_EXPERTS * (ROW_ALIGN - 1) * n_tok_tiles
    n_tiles = -(-max_rows // MOE_TM) + N_EXPERTS
    tile_expert = jnp.minimum(
        jnp.sum(jnp.arange(n_tiles, dtype=I32)[:, None] >= tile_end[None, :], axis=1), N_EXPERTS - 1)
    n_used = tile_end[-1:].astype(I32)
    return plan.reshape(n_tok_tiles, 1, 3 * N_EXPERTS), tile_expert.astype(I32), n_used, n_tiles * MOE_TM


def kernel(x, mem, positions, w_in, q_norm, w_uq, kv_norm, w_ukv, w_proj_a, conv_w, conv_b, dt_bias, a_log,
           d_skip, ssm_norm, w_proj_b, w_mem_kv, w_proj_c, w_out, ln1_g, ln1_b, router_w, router_bias,
           exp_w1, exp_w3, exp_w2, ln2_g, ln2_b):
    params = dict(w_in=w_in, q_norm=q_norm, w_uq=w_uq, kv_norm=kv_norm, w_ukv=w_ukv, w_proj_a=w_proj_a,
                  conv_w=conv_w, conv_b=conv_b, dt_bias=dt_bias, a_log=a_log, d_skip=d_skip,
                  ssm_norm=ssm_norm, w_proj_b=w_proj_b, w_mem_kv=w_mem_kv, w_proj_c=w_proj_c, w_out=w_out,
                  ln1_g=ln1_g, ln1_b=ln1_b, exp_w1=exp_w1, exp_w3=exp_w3, exp_w2=exp_w2,
                  ln2_g=ln2_g, ln2_b=ln2_b)
    bsz, s, d = x.shape
    t = bsz * s
    depth = w_in.shape[0]
    alpha = float((2 * depth) ** 0.25)
    n_mem = mem.shape[1]

    inv = ROPE_THETA ** (-jnp.arange(0, ROPE_DIM, 2, dtype=F32) / ROPE_DIM)
    ang = positions.astype(F32)[..., None] * inv
    cos, sin = jnp.cos(ang), jnp.sin(ang)
    cosq = jnp.tile(jnp.concatenate([cos, cos], axis=-1), (1, 1, MLA_HEADS)).reshape(t, -1)
    sinq = jnp.tile(jnp.concatenate([-sin, sin], axis=-1), (1, 1, MLA_HEADS)).reshape(t, -1)

    rw = jnp.concatenate([router_w.astype(F32), jnp.zeros((d, LANES - N_EXPERTS), F32)], axis=1)
    rw_hi = rw.astype(BF16)
    rw_lo = (rw - rw_hi.astype(F32)).astype(BF16)
    rb = jnp.concatenate([router_bias.astype(F32), jnp.zeros((LANES - N_EXPERTS,), F32)]).reshape(1, LANES)
    mem_b = mem.reshape(bsz * n_mem, d).astype(BF16)

    xf = x.reshape(t, d).astype(F32)
    xb = xf.astype(BF16)
    for l in range(depth):
        w = _prep_layer(l, params)
        ha = matmul(xb, w["w_small"], F32, "proj_small")
        z = matmul(xb, w["w_z"], BF16, "proj_z")
        xbc_c = proj_conv_silu(xb, w["w_xbc"], w["conv_w"], w["conv_b"], s).reshape(bsz, s, -1)
        qm = matmul(xb, w["w_qm"], BF16, "proj_qmem")
        g = matmul(xb, w["w_g"], BF16, "proj_gate")

        ql, qr, ck, kr = mla_prep(ha, cosq, sinq, w["q_norm"], w["kv_norm"], w["wq"], w["wuk_t"],
                                  min(ATT_TQ, s))
        o_lat = mla_attention(ql, qr, ck.reshape(bsz, s, KV_RANK), kr.reshape(bsz, s, ROPE_DIM))

        yn = ssd(xbc_c, z.reshape(bsz, s, -1), ha.reshape(bsz, s, HA_W), w["dt_bias"], w["a_neg"],
                 w["d_skip"], w["ssm_norm"]).reshape(t, -1)

        kv = matmul(mem_b, w["w_mem_kv"], BF16, "proj_memkv").reshape(bsz, n_mem, -1)
        cm = mem_attention(qm.reshape(bsz, s, -1), kv).reshape(t, -1)

        x1 = merge_ln(o_lat, yn, cm, g, xf, w["wuv"], w["wa"], w["wb"], w["wc"], w["wo"],
                           w["ln1_g"], w["ln1_b"], alpha)

        info, tinfo, counts = route(x1, rw_hi, rw_lo, rb)
        plan, tile_expert, n_used, n_rows = _dispatch_plan(tinfo, counts, t)
        xs = dispatch_rows(x1, info, plan, n_rows)
        ys = expert_ffn(xs, w["w13"], w["w2"], tile_expert, n_used)
        xf, xb = combine_ln(plan, info, x1, ys, w["ln2_g"], w["ln2_b"], alpha)
    return xf.reshape(bsz, s, d)
```

```python
import functools

import jax
import jax.numpy as jnp
from jax import lax
from jax.experimental import pallas as pl
from jax.experimental.pallas import tpu as pltpu

F32 = jnp.float32
BF16 = jnp.bfloat16
I32 = jnp.int32

MLA_HEADS = 8
Q_RANK = 384
KV_RANK = 256
NOPE_DIM = 128
ROPE_DIM = 64
V_DIM = 128
ROPE_THETA = 10000.0
SSM_HEADS = 32
SSM_HEAD_DIM = 64
SSM_GROUPS = 8
D_STATE = 128
CONV_K = 4
XA_HEADS = 4
XA_HEAD_DIM = 256
N_EXPERTS = 16
N_EXPERT_GROUPS = 4
EXPERTS_PER_GROUP = 4
TOP_K = 2
NORM_EPS = 1e-5
RMS_EPS = 1e-6

LANES = 128
V7X_VMEM_LIMIT = 56 * 1024 * 1024

MM_TM = 1024
MM_TN = 1024
PREP_TM = 512
ATT_TQ = 128
ATT_TK = 512
ATT_GROUP_ROWS = 512
LOG2E = 1.4426950408889634
PROJ_CONV_TN = 512
PROJ_CONV_SUB = 1024
CONV_HALO = 16
SSD_CHUNK = 256
XA_TQ = 512
MERGE_TM = 512
MOE_TM = 512
DISPATCH_TM = 256
ROW_ALIGN = 8
LOCAL_ROWS = 640

HA_W = 1024
HA_DQ = 0
HA_C = 384
HA_KR = 640
HA_KRS = 768
HA_DT = 896


def _cp(*sem):
    return pltpu.CompilerParams(dimension_semantics=sem, vmem_limit_bytes=V7X_VMEM_LIMIT)


def _sigmoid(x):
    return 1.0 / (1.0 + jnp.exp(-x))


def _mm_kernel(x_ref, w_ref, o_ref):
    o_ref[...] = jnp.dot(x_ref[...], w_ref[...], preferred_element_type=F32).astype(o_ref.dtype)


def matmul(x, w, out_dtype, name):
    m, k = x.shape
    n = w.shape[1]
    tm = min(MM_TM, m)
    tn = min(MM_TN, n)
    return pl.pallas_call(
        _mm_kernel,
        grid=(n // tn, m // tm),
        in_specs=[pl.BlockSpec((tm, k), lambda j, i: (i, 0)),
                  pl.BlockSpec((k, tn), lambda j, i: (0, j))],
        out_specs=pl.BlockSpec((tm, tn), lambda j, i: (i, j)),
        out_shape=jax.ShapeDtypeStruct((m, n), out_dtype),
        compiler_params=_cp("parallel", "parallel"),
        name=name,
    )(x, w)


def _proj_conv_kernel(x_ref, xh_ref, w_ref, cw_ref, cb_ref, o_ref, *, tiles_per_seq):
    first = (pl.program_id(1) % tiles_per_seq) == 0
    w = w_ref[...]
    cw = cw_ref[...]
    halo = jnp.dot(xh_ref[...], w, preferred_element_type=F32)
    halo = jnp.where(first, jnp.zeros_like(halo), halo)
    tm = x_ref.shape[0]
    sub = min(PROJ_CONV_SUB, tm)
    for c in range(tm // sub):
        rs = slice(c * sub, (c + 1) * sub)
        u = jnp.dot(x_ref[rs, :], w, preferred_element_type=F32)
        ext = jnp.concatenate([halo, u], axis=0)
        halo = u[sub - CONV_HALO:, :]
        prev = pltpu.roll(ext, 1, 0)
        near = cw[3:4, :] * ext + cw[2:3, :] * prev
        far = cw[1:2, :] * ext + cw[0:1, :] * prev
        acc = (near + pltpu.roll(far, 2, 0))[CONV_HALO:, :] + cb_ref[...]
        half = 0.5 * acc
        o_ref[rs, :] = (half * jnp.tanh(half) + half).astype(o_ref.dtype)


def proj_conv_silu(x, w, conv_w, conv_b, seq_len):
    m, k = x.shape
    n = w.shape[1]
    tm = min(MM_TM, seq_len)
    tn = min(PROJ_CONV_TN, n)
    hb = tm // CONV_HALO
    return pl.pallas_call(
        functools.partial(_proj_conv_kernel, tiles_per_seq=seq_len // tm),
        grid=(n // tn, m // tm),
        in_specs=[pl.BlockSpec((tm, k), lambda j, i: (i, 0)),
                  pl.BlockSpec((CONV_HALO, k), lambda j, i: (jnp.maximum(i * hb - 1, 0), 0)),
                  pl.BlockSpec((k, tn), lambda j, i: (0, j)),
                  pl.BlockSpec((CONV_K, tn), lambda j, i: (0, j)),
                  pl.BlockSpec((1, tn), lambda j, i: (0, j))],
        out_specs=pl.BlockSpec((tm, tn), lambda j, i: (i, j)),
        out_shape=jax.ShapeDtypeStruct((m, n), BF16),
        compiler_params=_cp("parallel", "parallel"),
        name="proj_xbc_conv",
    )(x, x, w, conv_w, conv_b)


def _mla_prep_kernel(ha_ref, cos_ref, sin_ref, qn_ref, kvn_ref, wq_ref, wuk_ref,
                     ql_ref, qr_ref, ck_ref, kr_ref, *, scale):
    nb, _, tq, _ = ql_ref.shape
    ha = ha_ref[...]
    dq = ha[:, HA_DQ:HA_DQ + Q_RANK]
    c_q = dq * lax.rsqrt(jnp.mean(dq * dq, axis=-1, keepdims=True) + RMS_EPS) * qn_ref[...]
    q = jnp.dot(c_q.astype(BF16), wq_ref[...], preferred_element_type=F32)
    n_nope = MLA_HEADS * NOPE_DIM
    n_rope = MLA_HEADS * ROPE_DIM
    cosq = cos_ref[...]
    sinq = sin_ref[...]
    q_rope = ((q[:, n_nope:n_nope + n_rope] * cosq + q[:, n_nope + n_rope:] * sinq) * scale).astype(BF16)
    for h in range(MLA_HEADS):
        qh = q[:, h * NOPE_DIM:(h + 1) * NOPE_DIM].astype(BF16)
        ql = jnp.dot(qh, wuk_ref[h], preferred_element_type=F32)
        ql_ref[:, h] = (ql * scale).astype(BF16).reshape(nb, tq, KV_RANK)
        qr_ref[:, h] = q_rope[:, h * ROPE_DIM:(h + 1) * ROPE_DIM].reshape(nb, tq, ROPE_DIM)
    c = ha[:, HA_C:HA_C + KV_RANK]
    c_kv = c * lax.rsqrt(jnp.mean(c * c, axis=-1, keepdims=True) + RMS_EPS) * kvn_ref[...]
    ck_ref[...] = c_kv.astype(BF16)
    k_rope = (ha[:, HA_KR:HA_KR + LANES] * cosq[:, :LANES]
              + ha[:, HA_KRS:HA_KRS + LANES] * sinq[:, :LANES])
    kr_ref[...] = k_rope[:, :ROPE_DIM].astype(BF16)


def mla_prep(ha, cosq, sinq, q_norm, kv_norm, wq, wuk_t, tq):
    t = ha.shape[0]
    tm = min(PREP_TM, t)
    nb = tm // tq
    scale = float((NOPE_DIM + ROPE_DIM) ** -0.5 * LOG2E)
    n_rope = MLA_HEADS * ROPE_DIM
    full = lambda shape: pl.BlockSpec(shape, lambda i: (0,) * len(shape))
    return pl.pallas_call(
        functools.partial(_mla_prep_kernel, scale=scale),
        grid=(t // tm,),
        in_specs=[pl.BlockSpec((tm, HA_W), lambda i: (i, 0)),
                  pl.BlockSpec((tm, n_rope), lambda i: (i, 0)),
                  pl.BlockSpec((tm, n_rope), lambda i: (i, 0)),
                  full((1, Q_RANK)), full((1, KV_RANK)),
                  full(wq.shape), full(wuk_t.shape)],
        out_specs=[pl.BlockSpec((nb, MLA_HEADS, tq, KV_RANK), lambda i: (i, 0, 0, 0)),
                   pl.BlockSpec((nb, MLA_HEADS, tq, ROPE_DIM), lambda i: (i, 0, 0, 0)),
                   pl.BlockSpec((tm, KV_RANK), lambda i: (i, 0)),
                   pl.BlockSpec((tm, ROPE_DIM), lambda i: (i, 0))],
        out_shape=[jax.ShapeDtypeStruct((t // tq, MLA_HEADS, tq, KV_RANK), BF16),
                   jax.ShapeDtypeStruct((t // tq, MLA_HEADS, tq, ROPE_DIM), BF16),
                   jax.ShapeDtypeStruct((t, KV_RANK), BF16),
                   jax.ShapeDtypeStruct((t, ROPE_DIM), BF16)],
        compiler_params=_cp("parallel"),
        name="mla_prep",
    )(ha, cosq, sinq, q_norm, kv_norm, wq, wuk_t)


def _mla_attn_kernel(ql_ref, qr_ref, ck_ref, kr_ref, o_ref, m_scr, l_scr, acc_scr, *, tq, tk, hpc):
    rc = hpc * tq
    n_groups = MLA_HEADS // hpc
    q_start = pl.program_id(1) * tq
    n_full = q_start // tk
    m_scr[...] = jnp.full(m_scr.shape, -jnp.inf, F32)
    l_scr[...] = jnp.zeros(l_scr.shape, F32)
    acc_scr[...] = jnp.zeros(acc_scr.shape, F32)
    nt = (((1,), (1,)), ((), ()))

    def step(j, masked, width=tk):
        ks = pl.multiple_of(j * tk, tk)
        ck = ck_ref[0, pl.ds(ks, width), :]
        kr = kr_ref[0, pl.ds(ks, width), :]
        if masked:
            q_pos = q_start + (lax.broadcasted_iota(I32, (rc, width), 0) & (tq - 1))
            k_pos = ks + lax.broadcasted_iota(I32, (rc, width), 1)
            visible = k_pos <= q_pos
        for c in range(n_groups):
            rs = slice(c * rc, (c + 1) * rc)
            ql = ql_ref[0, c * hpc:(c + 1) * hpc].reshape(rc, KV_RANK)
            qr = qr_ref[0, c * hpc:(c + 1) * hpc].reshape(rc, ROPE_DIM)
            s = (lax.dot_general(ql, ck, nt, preferred_element_type=F32)
                 + lax.dot_general(qr, kr, nt, preferred_element_type=F32))
            if masked:
                s = jnp.where(visible, s, -jnp.inf)
            m_prev = m_scr[rs]
            m_new = jnp.maximum(m_prev, jnp.max(s, axis=-1, keepdims=True))
            alpha = jnp.exp2(m_prev - m_new)
            p = jnp.exp2(s - jnp.tile(m_new, (1, width // LANES)))
            l_scr[rs] = alpha * l_scr[rs] + jnp.sum(p, axis=-1, keepdims=True)
            acc_scr[rs] = (jnp.tile(alpha, (1, KV_RANK // LANES)) * acc_scr[rs]
                           + jnp.dot(p.astype(BF16), ck, preferred_element_type=F32))
            m_scr[rs] = m_new

    def pair(jj, carry):
        step(2 * jj, False)
        step(2 * jj + 1, False)
        return carry

    lax.fori_loop(0, n_full // 2, pair, 0)

    @pl.when(n_full % 2 == 1)
    def _():
        step(n_full - 1, False)

    sub = (q_start - n_full * tk) // tq
    for v in range(tk // tq):
        @pl.when(sub == v)
        def _(v=v):
            step(n_full, True, (v + 1) * tq)
    out = acc_scr[...] / jnp.tile(l_scr[...], (1, KV_RANK // LANES))
    o_ref[0] = out.astype(o_ref.dtype).reshape(MLA_HEADS, tq, KV_RANK)


def mla_attention(ql, qr, ck, kr):
    _, _, tq, _ = ql.shape
    b, s, _ = ck.shape
    tk = min(ATT_TK, s)
    assert tk % tq == 0 and s % tk == 0 and tq & (tq - 1) == 0
    nq = s // tq
    rows = tq * MLA_HEADS
    hpc = max(1, min(MLA_HEADS, ATT_GROUP_ROWS // tq))
    qspec = lambda dim: pl.BlockSpec((1, MLA_HEADS, tq, dim), lambda bi, i: (bi * nq + i, 0, 0, 0))
    return pl.pallas_call(
        functools.partial(_mla_attn_kernel, tq=tq, tk=tk, hpc=hpc),
        grid=(b, nq),
        in_specs=[qspec(KV_RANK), qspec(ROPE_DIM),
                  pl.BlockSpec((1, s, KV_RANK), lambda bi, i: (bi, 0, 0)),
                  pl.BlockSpec((1, s, ROPE_DIM), lambda bi, i: (bi, 0, 0))],
        out_specs=qspec(KV_RANK),
        out_shape=jax.ShapeDtypeStruct(ql.shape, BF16),
        scratch_shapes=[pltpu.VMEM((rows, LANES), F32), pltpu.VMEM((rows, LANES), F32),
                        pltpu.VMEM((rows, KV_RANK), F32)],
        compiler_params=_cp("parallel", "parallel"),
        name="mla_attention",
    )(ql, qr, ck, kr)


def _ssd_kernel(xbc_ref, z_ref, dt_ref, dtb_ref, a_ref, dsk_ref, ng_ref, exp_ref, o_ref,
                state_scr, *, chunk):
    d_inner = SSM_HEADS * SSM_HEAD_DIM
    gn = SSM_GROUPS * D_STATE
    rep = SSM_HEADS // SSM_GROUPS
    gw = rep * SSM_HEAD_DIM

    @pl.when(pl.program_id(1) == 0)
    def _():
        state_scr[...] = jnp.zeros(state_scr.shape, F32)

    x_raw = dt_ref[0] + dtb_ref[...]
    dt = jnp.maximum(x_raw, 0.0) + jnp.log(1.0 + jnp.exp(-jnp.abs(x_raw)))
    da = dt * a_ref[...]
    row = lax.broadcasted_iota(I32, (chunk, chunk), 0)
    col = lax.broadcasted_iota(I32, (chunk, chunk), 1)
    causal = row >= col
    tri = jnp.where(causal, 1.0, 0.0).astype(BF16)
    acum = jnp.zeros(da.shape, F32)
    rem = da
    for _ in range(3):
        part = rem.astype(BF16)
        acum = acum + jnp.dot(tri, part, preferred_element_type=F32)
        rem = rem - part.astype(F32)
    a2 = acum * LOG2E
    a2_t = a2.T
    src_t = a2_t - jnp.log2(dt.T)
    w_t = jnp.exp2(a2_t[:, chunk - 1:chunk] - src_t)
    e_end = jnp.exp2(a2[chunk - 1:chunk, :])
    e_all = jnp.dot(jnp.exp2(a2).astype(BF16), exp_ref[...], preferred_element_type=F32)
    head_of_lane = lax.broadcasted_iota(I32, (1, gw), 1) // SSM_HEAD_DIM

    for g in range(SSM_GROUPS):
        bg = xbc_ref[0, :, d_inner + g * D_STATE:d_inner + (g + 1) * D_STATE]
        cg = xbc_ref[0, :, d_inner + gn + g * D_STATE:d_inner + gn + (g + 1) * D_STATE]
        bt = bg.astype(F32).T
        cb = jnp.dot(cg, bt.astype(BF16), preferred_element_type=F32)
        st = state_scr[g]
        ys = jnp.dot(cg, st.astype(BF16), preferred_element_type=F32)
        xg = xbc_ref[0, :, g * gw:(g + 1) * gw].astype(F32)
        y = jnp.zeros((chunk, gw), F32)
        upd = jnp.zeros((D_STATE, gw), F32)
        sc = jnp.zeros((1, gw), F32)
        for r in range(rep):
            h = g * rep + r
            own = head_of_lane == r
            seg = a2[:, h:h + 1] - src_t[h:h + 1, :]
            m = (cb * jnp.exp2(jnp.where(causal, seg, -jnp.inf))).astype(BF16)
            xm = jnp.where(own, xg, 0.0).astype(BF16)
            y = y + jnp.dot(m, xm, preferred_element_type=F32)
            upd = upd + jnp.dot((bt * w_t[h:h + 1, :]).astype(BF16), xm, preferred_element_type=F32)
            sc = jnp.where(own, e_end[:, h:h + 1], sc)
        state_scr[g] = st * sc + upd
        gs = slice(g * gw, (g + 1) * gw)
        y = y + ys * e_all[:, gs] + xg * dsk_ref[:, gs]
        zg = z_ref[0, :, gs].astype(F32)
        y = y * (zg * _sigmoid(zg))
        y = y * lax.rsqrt(jnp.mean(y * y, axis=-1, keepdims=True) + RMS_EPS) * ng_ref[:, gs]
        o_ref[0, :, gs] = y.astype(o_ref.dtype)


def ssd(xbc, z, ha3, dt_bias, a_neg, d_skip, norm_g):
    bsz, s, c = xbc.shape
    d_inner = SSM_HEADS * SSM_HEAD_DIM
    chunk = min(SSD_CHUNK, s)
    gw = d_inner // SSM_GROUPS
    vec = lambda n: pl.BlockSpec((1, n), lambda bi, ci: (0, 0))
    expand = (jnp.arange(d_inner, dtype=I32)[None, :] // SSM_HEAD_DIM
              == jnp.arange(LANES, dtype=I32)[:, None]).astype(BF16)
    return pl.pallas_call(
        functools.partial(_ssd_kernel, chunk=chunk),
        grid=(bsz, s // chunk),
        in_specs=[pl.BlockSpec((1, chunk, c), lambda bi, ci: (bi, ci, 0)),
                  pl.BlockSpec((1, chunk, d_inner), lambda bi, ci: (bi, ci, 0)),
                  pl.BlockSpec((1, chunk, LANES), lambda bi, ci: (bi, ci, HA_DT // LANES)),
                  vec(LANES), vec(LANES), vec(d_inner), vec(d_inner),
                  pl.BlockSpec((LANES, d_inner), lambda bi, ci: (0, 0))],
        out_specs=pl.BlockSpec((1, chunk, d_inner), lambda bi, ci: (bi, ci, 0)),
        out_shape=jax.ShapeDtypeStruct((bsz, s, d_inner), BF16),
        scratch_shapes=[pltpu.VMEM((SSM_GROUPS, D_STATE, gw), F32)],
        compiler_params=_cp("parallel", "arbitrary"),
        name="ssd_scan",
    )(xbc, z, ha3, dt_bias, a_neg, d_skip, norm_g, expand)


def _xattn_kernel(q_ref, kv_ref, o_ref, *, scale):
    hd = XA_HEADS * XA_HEAD_DIM
    nt = (((1,), (1,)), ((), ()))
    for h in range(XA_HEADS):
        sl = slice(h * XA_HEAD_DIM, (h + 1) * XA_HEAD_DIM)
        q = q_ref[0, :, sl]
        k = kv_ref[0, :, sl]
        v = kv_ref[0, :, hd + h * XA_HEAD_DIM:hd + (h + 1) * XA_HEAD_DIM]
        s = lax.dot_general(q, k, nt, preferred_element_type=F32) * scale
        p = jnp.exp(s - jnp.max(s, axis=-1, keepdims=True))
        l = jnp.sum(p, axis=-1, keepdims=True)
        o = jnp.dot(p.astype(BF16), v, preferred_element_type=F32) / l
        o_ref[0, :, sl] = o.astype(o_ref.dtype)


def mem_attention(q, kv):
    bsz, s, hd = q.shape
    m = kv.shape[1]
    tq = min(XA_TQ, s)
    return pl.pallas_call(
        functools.partial(_xattn_kernel, scale=float(XA_HEAD_DIM ** -0.5)),
        grid=(bsz, s // tq),
        in_specs=[pl.BlockSpec((1, tq, hd), lambda bi, i: (bi, i, 0)),
                  pl.BlockSpec((1, m, 2 * hd), lambda bi, i: (bi, 0, 0))],
        out_specs=pl.BlockSpec((1, tq, hd), lambda bi, i: (bi, i, 0)),
        out_shape=jax.ShapeDtypeStruct((bsz, s, hd), BF16),
        compiler_params=_cp("parallel", "parallel"),
        name="mem_attention",
    )(q, kv)


def _layer_norm(v, g, b):
    mu = jnp.mean(v, axis=-1, keepdims=True)
    d = v - mu
    var = jnp.mean(d * d, axis=-1, keepdims=True)
    return d * lax.rsqrt(var + NORM_EPS) * g + b


def _merge_kernel(ol_ref, yn_ref, cm_ref, g_ref, x_ref, wuv_ref, wa_ref, wb_ref, wc_ref, wo_ref,
                  lg_ref, lb_ref, xo_ref, a_scr, *, alpha):
    tm, d = x_ref.shape
    for h in range(MLA_HEADS):
        a_scr[:, h * V_DIM:(h + 1) * V_DIM] = jnp.dot(
            ol_ref[:, h].reshape(tm, KV_RANK), wuv_ref[h],
            preferred_element_type=F32).astype(BF16)
    o_a = jnp.dot(a_scr[...], wa_ref[...], preferred_element_type=F32)
    o_b = jnp.dot(yn_ref[...], wb_ref[...], preferred_element_type=F32)
    o_c = jnp.dot(cm_ref[...], wc_ref[...], preferred_element_type=F32)
    g = g_ref[...].astype(F32)
    merged = (_sigmoid(g[:, :d]) * o_a + _sigmoid(g[:, d:2 * d]) * o_b + _sigmoid(g[:, 2 * d:]) * o_c)
    y = alpha * x_ref[...] + jnp.dot(merged.astype(BF16), wo_ref[...], preferred_element_type=F32)
    xo_ref[...] = _layer_norm(y, lg_ref[...], lb_ref[...])


def merge_ln(ol, yn, cm, g, x, wuv, wa, wb, wc, wo, ln_g, ln_b, alpha):
    t, d = x.shape
    tm = min(MERGE_TM, t)
    tq = ol.shape[2]
    row = lambda n: pl.BlockSpec((tm, n), lambda i: (i, 0))
    full = lambda a: pl.BlockSpec(a.shape, lambda i: (0,) * a.ndim, pipeline_mode=pl.Buffered(1))
    return pl.pallas_call(
        functools.partial(_merge_kernel, alpha=alpha),
        grid=(t // tm,),
        in_specs=[pl.BlockSpec((tm // tq, MLA_HEADS, tq, KV_RANK), lambda i: (i, 0, 0, 0)),
                  row(yn.shape[1]), row(cm.shape[1]), row(g.shape[1]), row(d),
                  full(wuv), full(wa), full(wb), full(wc), full(wo), full(ln_g), full(ln_b)],
        out_specs=row(d),
        out_shape=jax.ShapeDtypeStruct((t, d), F32),
        scratch_shapes=[pltpu.VMEM((tm, MLA_HEADS * V_DIM), BF16)],
        compiler_params=_cp("parallel"),
        name="merge_ln",
    )(ol, yn, cm, g, x, wuv, wa, wb, wc, wo, ln_g, ln_b)


def _first_max(v, lane):
    m = jnp.max(v, axis=-1, keepdims=True)
    idx = jnp.min(jnp.where(v == m, lane, LANES), axis=-1, keepdims=True)
    return m, idx


def _route_kernel(x_ref, rwh_ref, rwl_ref, rb_ref, info_ref, tinfo_ref, cnt_ref, carry_scr):
    tm = x_ref.shape[0]

    @pl.when(pl.program_id(0) == 0)
    def _():
        carry_scr[...] = jnp.zeros(carry_scr.shape, F32)

    x = x_ref[...]
    x_hi = x.astype(BF16)
    x_lo = (x - x_hi.astype(F32)).astype(BF16)
    logits = (jnp.dot(x_hi, rwh_ref[...], preferred_element_type=F32)
              + jnp.dot(x_lo, rwh_ref[...], preferred_element_type=F32)
              + jnp.dot(x_hi, rwl_ref[...], preferred_element_type=F32))
    scores = _sigmoid(logits)
    sel = scores + rb_ref[...]
    lane = lax.broadcasted_iota(I32, (tm, LANES), 1)
    neg = -jnp.inf
    best_score = None
    best_group = None
    for j in range(N_EXPERT_GROUPS):
        in_j = (lane >= j * EXPERTS_PER_GROUP) & (lane < (j + 1) * EXPERTS_PER_GROUP)
        v = jnp.where(in_j, sel, neg)
        m1, i1 = _first_max(v, lane)
        m2, _ = _first_max(jnp.where(lane == i1, neg, v), lane)
        gs = m1 + m2
        if j == 0:
            best_score, best_group = gs, jnp.zeros_like(i1)
        else:
            better = gs > best_score
            best_score = jnp.where(better, gs, best_score)
            best_group = jnp.where(better, j, best_group)
    lo = best_group * EXPERTS_PER_GROUP
    v = jnp.where((lane >= lo) & (lane < lo + EXPERTS_PER_GROUP), sel, neg)
    _, e1 = _first_max(v, lane)
    _, e2 = _first_max(jnp.where(lane == e1, neg, v), lane)
    w1 = jnp.sum(jnp.where(lane == e1, scores, 0.0), axis=-1, keepdims=True)
    w2 = jnp.sum(jnp.where(lane == e2, scores, 0.0), axis=-1, keepdims=True)
    wsum = w1 + w2
    member = jnp.where(lane == e1, 1.0, jnp.where(lane == e2, 1.0, 0.0))
    row = lax.broadcasted_iota(I32, (tm, tm), 0)
    col = lax.broadcasted_iota(I32, (tm, tm), 1)
    before = jnp.where(row > col, 1.0, 0.0).astype(BF16)
    lrank = jnp.dot(before, member.astype(BF16), preferred_element_type=F32)
    n = jnp.sum(member, axis=0, keepdims=True)
    n8 = jnp.floor((n + (ROW_ALIGN - 1)) * (1.0 / ROW_ALIGN)) * ROW_ALIGN
    ua = lax.broadcasted_iota(I32, (LANES, LANES), 0)
    ub = lax.broadcasted_iota(I32, (LANES, LANES), 1)
    lower_experts = jnp.where(ua < ub, 1.0, 0.0).astype(BF16)
    n8_rows = jnp.broadcast_to(n8, (8, LANES))
    block_start = jnp.dot(n8_rows.astype(BF16), lower_experts, preferred_element_type=F32)
    local_row = block_start[0:1, :] + lrank
    j1 = jnp.sum(jnp.where(lane == e1, local_row, 0.0), axis=-1, keepdims=True)
    j2 = jnp.sum(jnp.where(lane == e2, local_row, 0.0), axis=-1, keepdims=True)
    carry = carry_scr[...]
    sub = lax.broadcasted_iota(I32, (8, LANES), 0)
    tinfo_ref[0] = jnp.where(sub == 0, n8_rows, jnp.where(sub == 1, block_start, jnp.where(sub == 2, carry, 0.0)))
    new_carry = carry + n8_rows
    carry_scr[...] = new_carry
    cnt_ref[...] = new_carry
    info = jnp.where(lane == 0, e1.astype(F32),
           jnp.where(lane == 1, e2.astype(F32),
           jnp.where(lane == 2, w1 / wsum,
           jnp.where(lane == 3, w2 / wsum,
           jnp.where(lane == 4, j1,
           jnp.where(lane == 5, j2, 0.0))))))
    info_ref[...] = info


def route(x, rw_hi, rw_lo, rb):
    t, d = x.shape
    tm = min(DISPATCH_TM, t)
    return pl.pallas_call(
        _route_kernel,
        grid=(t // tm,),
        in_specs=[pl.BlockSpec((tm, d), lambda i: (i, 0)),
                  pl.BlockSpec((d, LANES), lambda i: (0, 0)),
                  pl.BlockSpec((d, LANES), lambda i: (0, 0)),
                  pl.BlockSpec((1, LANES), lambda i: (0, 0))],
        out_specs=[pl.BlockSpec((tm, LANES), lambda i: (i, 0)),
                   pl.BlockSpec((1, 8, LANES), lambda i: (i, 0, 0)),
                   pl.BlockSpec((8, LANES), lambda i: (0, 0))],
        out_shape=[jax.ShapeDtypeStruct((t, LANES), F32),
                   jax.ShapeDtypeStruct((t // tm, 8, LANES), F32),
                   jax.ShapeDtypeStruct((8, LANES), F32)],
        scratch_shapes=[pltpu.VMEM((8, LANES), F32)],
        compiler_params=_cp("arbitrary"),
        name="route",
    )(x, rw_hi, rw_lo, rb)


def _block_copies(plan_ref, local_ref, sorted_ref, sem, to_sorted, start):
    for e in range(N_EXPERTS):
        count = plan_ref[0, 0, e]
        loc0 = plan_ref[0, 0, N_EXPERTS + e]
        dst0 = plan_ref[0, 0, 2 * N_EXPERTS + e]

        def body(k, carry, loc0=loc0, dst0=dst0):
            loc = local_ref.at[pl.ds(pl.multiple_of(loc0 + k * ROW_ALIGN, ROW_ALIGN), ROW_ALIGN)]
            srt = sorted_ref.at[pl.ds(pl.multiple_of(dst0 + k * ROW_ALIGN, ROW_ALIGN), ROW_ALIGN)]
            cp = pltpu.make_async_copy(loc, srt, sem) if to_sorted else pltpu.make_async_copy(srt, loc, sem)
            if start:
                cp.start()
            else:
                cp.wait()
            return carry

        lax.fori_loop(0, count, body, 0)


def _dispatch_kernel(plan_ref, prev_plan_ref, info_ref, x_ref, init_ref, o_ref, local_scr, sem):
    del init_ref
    tm = x_ref.shape[0]
    i = pl.program_id(0)
    slot = i % 2
    info_t = info_ref[...].T
    j1 = info_t[4:5, :]
    j2 = info_t[5:6, :]
    rowid = lax.broadcasted_iota(I32, (LOCAL_ROWS, tm), 0).astype(F32)
    sel = jnp.where(rowid == j1, 1.0, jnp.where(rowid == j2, 1.0, 0.0)).astype(BF16)
    local_scr[slot] = jnp.dot(sel, x_ref[...].astype(BF16), preferred_element_type=F32)

    @pl.when(i > 0)
    def _():
        _block_copies(prev_plan_ref, local_scr.at[1 - slot], o_ref, sem.at[1 - slot], True, False)

    _block_copies(plan_ref, local_scr.at[slot], o_ref, sem.at[slot], True, True)

    @pl.when(i == pl.num_programs(0) - 1)
    def _():
        _block_copies(plan_ref, local_scr.at[slot], o_ref, sem.at[slot], True, False)


def dispatch_rows(x, info, plan, n_rows):
    t, d = x.shape
    tm = min(DISPATCH_TM, t)
    init = jnp.zeros((n_rows, d), F32)
    return pl.pallas_call(
        _dispatch_kernel,
        grid=(t // tm,),
        in_specs=[pl.BlockSpec((1, 1, 3 * N_EXPERTS), lambda i: (i, 0, 0), memory_space=pltpu.SMEM),
                  pl.BlockSpec((1, 1, 3 * N_EXPERTS), lambda i: (jnp.maximum(i - 1, 0), 0, 0),
                               memory_space=pltpu.SMEM),
                  pl.BlockSpec((tm, LANES), lambda i: (i, 0)),
                  pl.BlockSpec((tm, d), lambda i: (i, 0)),
                  pl.BlockSpec(memory_space=pl.ANY)],
        out_specs=pl.BlockSpec(memory_space=pl.ANY),
        out_shape=jax.ShapeDtypeStruct((n_rows, d), F32),
        scratch_shapes=[pltpu.VMEM((2, LOCAL_ROWS, d), F32), pltpu.SemaphoreType.DMA((2,))],
        input_output_aliases={4: 0},
        compiler_params=_cp("arbitrary"),
        name="moe_dispatch",
    )(plan, plan, info, x, init)


def _expert_kernel(te_ref, nu_ref, xs_ref, w13_ref, w2_ref, o_ref):
    del te_ref
    i = pl.program_id(0)
    f = w2_ref.shape[1]

    @pl.when(i < nu_ref[0])
    def _():
        xb = xs_ref[...].astype(BF16)
        h = jnp.dot(xb, w13_ref[0], preferred_element_type=F32)
        h1 = h[:, :f]
        act = (h1 * _sigmoid(h1) * h[:, f:]).astype(BF16)
        o_ref[...] = jnp.dot(act, w2_ref[0], preferred_element_type=F32)

    @pl.when(i >= nu_ref[0])
    def _():
        o_ref[...] = jnp.zeros(o_ref.shape, o_ref.dtype)


def expert_ffn(xs, w13, w2, tile_expert, n_used):
    p, d = xs.shape
    f = w2.shape[1]
    n_tiles = p // MOE_TM
    grid_spec = pltpu.PrefetchScalarGridSpec(
        num_scalar_prefetch=2,
        grid=(n_tiles,),
        in_specs=[pl.BlockSpec((MOE_TM, d), lambda i, te, nu: (i, 0)),
                  pl.BlockSpec((1, d, 2 * f), lambda i, te, nu: (te[i], 0, 0)),
                  pl.BlockSpec((1, f, d), lambda i, te, nu: (te[i], 0, 0))],
        out_specs=pl.BlockSpec((MOE_TM, d), lambda i, te, nu: (i, 0)),
    )
    return pl.pallas_call(
        _expert_kernel,
        grid_spec=grid_spec,
        out_shape=jax.ShapeDtypeStruct((p, d), F32),
        compiler_params=_cp("arbitrary"),
        name="expert_ffn",
    )(tile_expert, n_used, xs, w13, w2)


def _combine_kernel(plan_ref, next_plan_ref, info_ref, x_ref, ys_ref, lg_ref, lb_ref, xo_ref, xb_ref,
                    local_scr, sem, *, alpha):
    tm = x_ref.shape[0]
    i = pl.program_id(0)
    slot = i % 2

    @pl.when(i == 0)
    def _():
        _block_copies(plan_ref, local_scr.at[slot], ys_ref, sem.at[slot], False, True)

    @pl.when(i + 1 < pl.num_programs(0))
    def _():
        _block_copies(next_plan_ref, local_scr.at[1 - slot], ys_ref, sem.at[1 - slot], False, True)

    _block_copies(plan_ref, local_scr.at[slot], ys_ref, sem.at[slot], False, False)
    last = N_EXPERTS - 1
    used = plan_ref[0, 0, N_EXPERTS + last] + plan_ref[0, 0, last] * ROW_ALIGN
    rowid = lax.broadcasted_iota(I32, (LOCAL_ROWS, 1), 0)
    y = jnp.where(rowid < used, local_scr[slot], 0.0).astype(BF16)
    info = info_ref[...]
    col = lax.broadcasted_iota(I32, (tm, LOCAL_ROWS), 1).astype(F32)
    gate = jnp.where(col == info[:, 4:5], info[:, 2:3],
                     jnp.where(col == info[:, 5:6], info[:, 3:4], 0.0)).astype(BF16)
    moe = jnp.dot(gate, y, preferred_element_type=F32)
    out = _layer_norm(alpha * x_ref[...] + moe, lg_ref[...], lb_ref[...])
    xo_ref[...] = out
    xb_ref[...] = out.astype(BF16)


def combine_ln(plan, info, x, ys, ln_g, ln_b, alpha):
    t, d = x.shape
    tm = min(DISPATCH_TM, t)
    return pl.pallas_call(
        functools.partial(_combine_kernel, alpha=alpha),
        grid=(t // tm,),
        in_specs=[pl.BlockSpec((1, 1, 3 * N_EXPERTS), lambda i: (i, 0, 0), memory_space=pltpu.SMEM),
                  pl.BlockSpec((1, 1, 3 * N_EXPERTS), lambda i: (jnp.minimum(i + 1, t // tm - 1), 0, 0),
                               memory_space=pltpu.SMEM),
                  pl.BlockSpec((tm, LANES), lambda i: (i, 0)),
                  pl.BlockSpec((tm, d), lambda i: (i, 0)),
                  pl.BlockSpec(memory_space=pl.ANY),
                  pl.BlockSpec((1, d), lambda i: (0, 0)),
                  pl.BlockSpec((1, d), lambda i: (0, 0))],
        out_specs=[pl.BlockSpec((tm, d), lambda i: (i, 0)), pl.BlockSpec((tm, d), lambda i: (i, 0))],
        out_shape=[jax.ShapeDtypeStruct((t, d), F32), jax.ShapeDtypeStruct((t, d), BF16)],
        scratch_shapes=[pltpu.VMEM((2, LOCAL_ROWS, d), F32), pltpu.SemaphoreType.DMA((2,))],
        compiler_params=_cp("arbitrary"),
        name="moe_combine_ln",
    )(plan, plan, info, x, ys, ln_g, ln_b)


def _prep_layer(l, p):
    w_in = p["w_in"][l]
    d = w_in.shape[0]
    o_dq = 0
    o_dkv = o_dq + Q_RANK
    o_z = o_dkv + KV_RANK + ROPE_DIM
    d_inner = SSM_HEADS * SSM_HEAD_DIM
    conv_ch = d_inner + 2 * SSM_GROUPS * D_STATE
    o_xbc = o_z + d_inner
    o_dt = o_xbc + conv_ch
    o_qm = o_dt + SSM_HEADS
    o_g = o_qm + XA_HEADS * XA_HEAD_DIM
    half = ROPE_DIM // 2
    kr0 = o_dkv + KV_RANK
    zeros = lambda n: jnp.zeros((d, n), F32)
    w_small = jnp.concatenate([
        w_in[:, o_dq:o_dq + Q_RANK],
        w_in[:, o_dkv:o_dkv + KV_RANK],
        w_in[:, kr0:kr0 + ROPE_DIM], zeros(LANES - ROPE_DIM),
        w_in[:, kr0 + half:kr0 + ROPE_DIM], w_in[:, kr0:kr0 + half], zeros(LANES - ROPE_DIM),
        w_in[:, o_dt:o_dt + SSM_HEADS], zeros(LANES - SSM_HEADS)], axis=1)
    w_uq = p["w_uq"][l].reshape(Q_RANK, MLA_HEADS, NOPE_DIM + ROPE_DIM)
    wq_nope = w_uq[:, :, :NOPE_DIM].reshape(Q_RANK, -1)
    wq_rope = w_uq[:, :, NOPE_DIM:]
    wq_rope_sw = jnp.concatenate([wq_rope[:, :, half:], wq_rope[:, :, :half]], axis=-1)
    wq = jnp.concatenate([wq_nope, wq_rope.reshape(Q_RANK, -1), wq_rope_sw.reshape(Q_RANK, -1)], axis=1)
    w_ukv = p["w_ukv"][l]
    wuk_t = jnp.transpose(w_ukv[:, :, :NOPE_DIM], (1, 2, 0))
    wuv = jnp.transpose(w_ukv[:, :, NOPE_DIM:], (1, 0, 2))
    pad_heads = lambda v, fill: jnp.concatenate(
        [v.astype(F32), jnp.full((LANES - SSM_HEADS,), fill, F32)]).reshape(1, LANES)
    bf = lambda a: a.astype(BF16)
    return dict(
        w_small=bf(w_small), w_z=bf(w_in[:, o_z:o_z + d_inner]), w_xbc=bf(w_in[:, o_xbc:o_xbc + conv_ch]),
        w_qm=bf(w_in[:, o_qm:o_g]), w_g=bf(w_in[:, o_g:]),
        q_norm=p["q_norm"][l].reshape(1, -1), kv_norm=p["kv_norm"][l].reshape(1, -1),
        wq=bf(wq), wuk_t=bf(wuk_t), wuv=bf(wuv),
        conv_w=p["conv_w"][l], conv_b=p["conv_b"][l].reshape(1, -1),
        dt_bias=pad_heads(p["dt_bias"][l], 0.0),
        a_neg=pad_heads(-jnp.exp(p["a_log"][l].astype(F32)), 0.0),
        d_skip=jnp.repeat(p["d_skip"][l].astype(F32), SSM_HEAD_DIM).reshape(1, -1),
        ssm_norm=p["ssm_norm"][l].reshape(1, -1),
        w_mem_kv=bf(p["w_mem_kv"][l]),
        wa=bf(p["w_proj_a"][l]), wb=bf(p["w_proj_b"][l]), wc=bf(p["w_proj_c"][l]), wo=bf(p["w_out"][l]),
        ln1_g=p["ln1_g"][l].reshape(1, -1), ln1_b=p["ln1_b"][l].reshape(1, -1),
        w13=bf(jnp.concatenate([p["exp_w1"][l], p["exp_w3"][l]], axis=-1)), w2=bf(p["exp_w2"][l]),
        ln2_g=p["ln2_g"][l].reshape(1, -1), ln2_b=p["ln2_b"][l].reshape(1, -1),
    )


def _dispatch_plan(tinfo, counts, t):
    n_tok_tiles = tinfo.shape[0]
    n8 = tinfo[:, 0, :N_EXPERTS].astype(I32)
    block_start = tinfo[:, 1, :N_EXPERTS].astype(I32)
    carry = tinfo[:, 2, :N_EXPERTS].astype(I32)
    rows = counts[0, :N_EXPERTS].astype(I32)
    tiles = (rows + MOE_TM - 1) // MOE_TM
    tile_end = jnp.cumsum(tiles)
    row_start = (tile_end - tiles) * MOE_TM
    plan = jnp.concatenate([n8 // ROW_ALIGN, block_start, row_start[None, :] + carry], axis=1)
    max_rows = TOP_K * t + N_EXPERTS * (ROW_ALIGN - 1) * n_tok_tiles
    n_tiles = -(-max_rows // MOE_TM) + N_EXPERTS
    tile_expert = jnp.minimum(
        jnp.sum(jnp.arange(n_tiles, dtype=I32)[:, None] >= tile_end[None, :], axis=1), N_EXPERTS - 1)
    n_used = tile_end[-1:].astype(I32)
    return plan.reshape(n_tok_tiles, 1, 3 * N_EXPERTS), tile_expert.astype(I32), n_used, n_tiles * MOE_TM


def kernel(x, mem, positions, w_in, q_norm, w_uq, kv_norm, w_ukv, w_proj_a, conv_w, conv_b, dt_bias, a_log,
           d_skip, ssm_norm, w_proj_b, w_mem_kv, w_proj_c, w_out, ln1_g, ln1_b, router_w, router_bias,
           exp_w1, exp_w3, exp_w2, ln2_g, ln2_b):
    params = dict(w_in=w_in, q_norm=q_norm, w_uq=w_uq, kv_norm=kv_norm, w_ukv=w_ukv, w_proj_a=w_proj_a,
                  conv_w=conv_w, conv_b=conv_b, dt_bias=dt_bias, a_log=a_log, d_skip=d_skip,
                  ssm_norm=ssm_norm, w_proj_b=w_proj_b, w_mem_kv=w_mem_kv, w_proj_c=w_proj_c, w_out=w_out,
                  ln1_g=ln1_g, ln1_b=ln1_b, exp_w1=exp_w1, exp_w3=exp_w3, exp_w2=exp_w2,
                  ln2_g=ln2_g, ln2_b=ln2_b)
    bsz, s, d = x.shape
    t = bsz * s
    depth = w_in.shape[0]
    alpha = float((2 * depth) ** 0.25)
    n_mem = mem.shape[1]

    inv = ROPE_THETA ** (-jnp.arange(0, ROPE_DIM, 2, dtype=F32) / ROPE_DIM)
    ang = positions.astype(F32)[..., None] * inv
    cos, sin = jnp.cos(ang), jnp.sin(ang)
    cosq = jnp.tile(jnp.concatenate([cos, cos], axis=-1), (1, 1, MLA_HEADS)).reshape(t, -1)
    sinq = jnp.tile(jnp.concatenate([-sin, sin], axis=-1), (1, 1, MLA_HEADS)).reshape(t, -1)

    rw = jnp.concatenate([router_w.astype(F32), jnp.zeros((d, LANES - N_EXPERTS), F32)], axis=1)
    rw_hi = rw.astype(BF16)
    rw_lo = (rw - rw_hi.astype(F32)).astype(BF16)
    rb = jnp.concatenate([router_bias.astype(F32), jnp.zeros((LANES - N_EXPERTS,), F32)]).reshape(1, LANES)
    mem_b = mem.reshape(bsz * n_mem, d).astype(BF16)

    xf = x.reshape(t, d).astype(F32)
    xb = xf.astype(BF16)
    for l in range(depth):
        w = _prep_layer(l, params)
        ha = matmul(xb, w["w_small"], F32, "proj_small")
        z = matmul(xb, w["w_z"], BF16, "proj_z")
        xbc_c = proj_conv_silu(xb, w["w_xbc"], w["conv_w"], w["conv_b"], s).reshape(bsz, s, -1)
        qm = matmul(xb, w["w_qm"], BF16, "proj_qmem")
        g = matmul(xb, w["w_g"], BF16, "proj_gate")

        ql, qr, ck, kr = mla_prep(ha, cosq, sinq, w["q_norm"], w["kv_norm"], w["wq"], w["wuk_t"],
                                  min(ATT_TQ, s))
        o_lat = mla_attention(ql, qr, ck.reshape(bsz, s, KV_RANK), kr.reshape(bsz, s, ROPE_DIM))

        yn = ssd(xbc_c, z.reshape(bsz, s, -1), ha.reshape(bsz, s, HA_W), w["dt_bias"], w["a_neg"],
                 w["d_skip"], w["ssm_norm"]).reshape(t, -1)

        kv = matmul(mem_b, w["w_mem_kv"], BF16, "proj_memkv").reshape(bsz, n_mem, -1)
        cm = mem_attention(qm.reshape(bsz, s, -1), kv).reshape(t, -1)

        x1 = merge_ln(o_lat, yn, cm, g, xf, w["wuv"], w["wa"], w["wb"], w["wc"], w["wo"],
                           w["ln1_g"], w["ln1_b"], alpha)

        info, tinfo, counts = route(x1, rw_hi, rw_lo, rb)
        plan, tile_expert, n_used, n_rows = _dispatch_plan(tinfo, counts, t)
        xs = dispatch_rows(x1, info, plan, n_rows)
        ys = expert_ffn(xs, w["w13"], w["w2"], tile_expert, n_used)
        xf, xb = combine_ln(plan, info, x1, ys, w["ln2_g"], w["ln2_b"], alpha)
    return xf.reshape(bsz, s, d)
```

```python
import functools

import jax
import jax.numpy as jnp
from jax import lax
from jax.experimental import pallas as pl
from jax.experimental.pallas import tpu as pltpu

F32 = jnp.float32
BF16 = jnp.bfloat16
I32 = jnp.int32

MLA_HEADS = 8
Q_RANK = 384
KV_RANK = 256
NOPE_DIM = 128
ROPE_DIM = 64
V_DIM = 128
ROPE_THETA = 10000.0
SSM_HEADS = 32
SSM_HEAD_DIM = 64
SSM_GROUPS = 8
D_STATE = 128
CONV_K = 4
XA_HEADS = 4
XA_HEAD_DIM = 256
N_EXPERTS = 16
N_EXPERT_GROUPS = 4
EXPERTS_PER_GROUP = 4
TOP_K = 2
NORM_EPS = 1e-5
RMS_EPS = 1e-6

LANES = 128
V7X_VMEM_LIMIT = 56 * 1024 * 1024

MM_TM = 1024
MM_TN = 1024
PREP_TM = 512
ATT_TQ = 128
ATT_TK = 512
ATT_GROUP_ROWS = 512
ATT_UNROLL = 2
LOG2E = 1.4426950408889634
PROJ_CONV_TN = 512
PROJ_CONV_SUB = 256
CONV_HALO = 16
SSD_CHUNK = 256
XA_TQ = 512
MERGE_TM = 512
MOE_TM = 512
DISPATCH_TM = 256
ROW_ALIGN = 8
LOCAL_ROWS = 640

HA_W = 1024
HA_DQ = 0
HA_C = 384
HA_KR = 640
HA_KRS = 768
HA_DT = 896


def _cp(*sem):
    return pltpu.CompilerParams(dimension_semantics=sem, vmem_limit_bytes=V7X_VMEM_LIMIT)


def _sigmoid(x):
    return 1.0 / (1.0 + jnp.exp(-x))


def _mm_kernel(x_ref, w_ref, o_ref):
    o_ref[...] = jnp.dot(x_ref[...], w_ref[...], preferred_element_type=F32).astype(o_ref.dtype)


def matmul(x, w, out_dtype, name):
    m, k = x.shape
    n = w.shape[1]
    tm = min(MM_TM, m)
    tn = min(MM_TN, n)
    return pl.pallas_call(
        _mm_kernel,
        grid=(n // tn, m // tm),
        in_specs=[pl.BlockSpec((tm, k), lambda j, i: (i, 0)),
                  pl.BlockSpec((k, tn), lambda j, i: (0, j))],
        out_specs=pl.BlockSpec((tm, tn), lambda j, i: (i, j)),
        out_shape=jax.ShapeDtypeStruct((m, n), out_dtype),
        compiler_params=_cp("parallel", "parallel"),
        name=name,
    )(x, w)


def _proj_conv_kernel(x_ref, xh_ref, w_ref, cw_ref, cb_ref, o_ref, *, tiles_per_seq):
    first = (pl.program_id(1) % tiles_per_seq) == 0
    tn = o_ref.shape[1]
    sub = min(PROJ_CONV_SUB, tn)
    for c in range(tn // sub):
        cs = slice(c * sub, (c + 1) * sub)
        w = w_ref[:, cs]
        cw = cw_ref[:, cs]
        u = jnp.dot(x_ref[...], w, preferred_element_type=F32)
        halo = jnp.dot(xh_ref[...], w, preferred_element_type=F32)
        halo = jnp.where(first, jnp.zeros_like(halo), halo)
        ext = jnp.concatenate([halo, u], axis=0)
        prev = pltpu.roll(ext, 1, 0)
        near = cw[3:4, :] * ext + cw[2:3, :] * prev
        far = cw[1:2, :] * ext + cw[0:1, :] * prev
        acc = (near + pltpu.roll(far, 2, 0))[CONV_HALO:, :] + cb_ref[:, cs]
        half = 0.5 * acc
        o_ref[:, cs] = (half * jnp.tanh(half) + half).astype(o_ref.dtype)


def proj_conv_silu(x, w, conv_w, conv_b, seq_len):
    m, k = x.shape
    n = w.shape[1]
    tm = min(MM_TM, seq_len)
    tn = min(PROJ_CONV_TN, n)
    hb = tm // CONV_HALO
    return pl.pallas_call(
        functools.partial(_proj_conv_kernel, tiles_per_seq=seq_len // tm),
        grid=(n // tn, m // tm),
        in_specs=[pl.BlockSpec((tm, k), lambda j, i: (i, 0)),
                  pl.BlockSpec((CONV_HALO, k), lambda j, i: (jnp.maximum(i * hb - 1, 0), 0)),
                  pl.BlockSpec((k, tn), lambda j, i: (0, j)),
                  pl.BlockSpec((CONV_K, tn), lambda j, i: (0, j)),
                  pl.BlockSpec((1, tn), lambda j, i: (0, j))],
        out_specs=pl.BlockSpec((tm, tn), lambda j, i: (i, j)),
        out_shape=jax.ShapeDtypeStruct((m, n), BF16),
        compiler_params=_cp("parallel", "parallel"),
        name="proj_xbc_conv",
    )(x, x, w, conv_w, conv_b)


def _mla_prep_kernel(ha_ref, cos_ref, sin_ref, qn_ref, kvn_ref, wq_ref, wuk_ref,
                     ql_ref, qr_ref, ck_ref, kr_ref, *, scale):
    nb, _, tq, _ = ql_ref.shape
    ha = ha_ref[...]
    dq = ha[:, HA_DQ:HA_DQ + Q_RANK]
    c_q = dq * lax.rsqrt(jnp.mean(dq * dq, axis=-1, keepdims=True) + RMS_EPS) * qn_ref[...]
    q = jnp.dot(c_q.astype(BF16), wq_ref[...], preferred_element_type=F32)
    n_nope = MLA_HEADS * NOPE_DIM
    n_rope = MLA_HEADS * ROPE_DIM
    cosq = cos_ref[...]
    sinq = sin_ref[...]
    q_rope = ((q[:, n_nope:n_nope + n_rope] * cosq + q[:, n_nope + n_rope:] * sinq) * scale).astype(BF16)
    for h in range(MLA_HEADS):
        qh = q[:, h * NOPE_DIM:(h + 1) * NOPE_DIM].astype(BF16)
        ql = jnp.dot(qh, wuk_ref[h], preferred_element_type=F32)
        ql_ref[:, h] = (ql * scale).astype(BF16).reshape(nb, tq, KV_RANK)
        qr_ref[:, h] = q_rope[:, h * ROPE_DIM:(h + 1) * ROPE_DIM].reshape(nb, tq, ROPE_DIM)
    c = ha[:, HA_C:HA_C + KV_RANK]
    c_kv = c * lax.rsqrt(jnp.mean(c * c, axis=-1, keepdims=True) + RMS_EPS) * kvn_ref[...]
    ck_ref[...] = c_kv.astype(BF16)
    k_rope = (ha[:, HA_KR:HA_KR + LANES] * cosq[:, :LANES]
              + ha[:, HA_KRS:HA_KRS + LANES] * sinq[:, :LANES])
    kr_ref[...] = k_rope[:, :ROPE_DIM].astype(BF16)


def mla_prep(ha, cosq, sinq, q_norm, kv_norm, wq, wuk_t, tq):
    t = ha.shape[0]
    tm = min(PREP_TM, t)
    nb = tm // tq
    scale = float((NOPE_DIM + ROPE_DIM) ** -0.5 * LOG2E)
    n_rope = MLA_HEADS * ROPE_DIM
    full = lambda shape: pl.BlockSpec(shape, lambda i: (0,) * len(shape))
    return pl.pallas_call(
        functools.partial(_mla_prep_kernel, scale=scale),
        grid=(t // tm,),
        in_specs=[pl.BlockSpec((tm, HA_W), lambda i: (i, 0)),
                  pl.BlockSpec((tm, n_rope), lambda i: (i, 0)),
                  pl.BlockSpec((tm, n_rope), lambda i: (i, 0)),
                  full((1, Q_RANK)), full((1, KV_RANK)),
                  full(wq.shape), full(wuk_t.shape)],
        out_specs=[pl.BlockSpec((nb, MLA_HEADS, tq, KV_RANK), lambda i: (i, 0, 0, 0)),
                   pl.BlockSpec((nb, MLA_HEADS, tq, ROPE_DIM), lambda i: (i, 0, 0, 0)),
                   pl.BlockSpec((tm, KV_RANK), lambda i: (i, 0)),
                   pl.BlockSpec((tm, ROPE_DIM), lambda i: (i, 0))],
        out_shape=[jax.ShapeDtypeStruct((t // tq, MLA_HEADS, tq, KV_RANK), BF16),
                   jax.ShapeDtypeStruct((t // tq, MLA_HEADS, tq, ROPE_DIM), BF16),
                   jax.ShapeDtypeStruct((t, KV_RANK), BF16),
                   jax.ShapeDtypeStruct((t, ROPE_DIM), BF16)],
        compiler_params=_cp("parallel"),
        name="mla_prep",
    )(ha, cosq, sinq, q_norm, kv_norm, wq, wuk_t)


def _mla_attn_kernel(ql_ref, qr_ref, ck_ref, kr_ref, o_ref, m_scr, l_scr, acc_scr, *, tq, tk, hpc):
    rc = hpc * tq
    n_groups = MLA_HEADS // hpc
    q_start = pl.program_id(1) * tq
    n_full = q_start // tk
    m_scr[...] = jnp.full(m_scr.shape, -jnp.inf, F32)
    l_scr[...] = jnp.zeros(l_scr.shape, F32)
    acc_scr[...] = jnp.zeros(acc_scr.shape, F32)
    nt = (((1,), (1,)), ((), ()))

    def step(j, masked, width=tk):
        ks = pl.multiple_of(j * tk, tk)
        ck = ck_ref[0, pl.ds(ks, width), :]
        kr = kr_ref[0, pl.ds(ks, width), :]
        if masked:
            q_pos = q_start + (lax.broadcasted_iota(I32, (rc, width), 0) & (tq - 1))
            k_pos = ks + lax.broadcasted_iota(I32, (rc, width), 1)
            visible = k_pos <= q_pos
        for c in range(n_groups):
            rs = slice(c * rc, (c + 1) * rc)
            ql = ql_ref[0, c * hpc:(c + 1) * hpc].reshape(rc, KV_RANK)
            qr = qr_ref[0, c * hpc:(c + 1) * hpc].reshape(rc, ROPE_DIM)
            s = (lax.dot_general(ql, ck, nt, preferred_element_type=F32)
                 + lax.dot_general(qr, kr, nt, preferred_element_type=F32))
            if masked:
                s = jnp.where(visible, s, -jnp.inf)
            m_prev = m_scr[rs]
            m_new = jnp.maximum(m_prev, jnp.max(s, axis=-1, keepdims=True))
            alpha = jnp.exp2(m_prev - m_new)
            p = jnp.exp2(s - jnp.tile(m_new, (1, width // LANES)))
            l_scr[rs] = alpha * l_scr[rs] + jnp.sum(p, axis=-1, keepdims=True)
            acc_scr[rs] = (jnp.tile(alpha, (1, KV_RANK // LANES)) * acc_scr[rs]
                           + jnp.dot(p.astype(BF16), ck, preferred_element_type=F32))
            m_scr[rs] = m_new

    def trip(jj, carry):
        for u in range(ATT_UNROLL):
            step(ATT_UNROLL * jj + u, False)
        return carry

    n_trips = n_full // ATT_UNROLL
    lax.fori_loop(0, n_trips, trip, 0)

    def single(j, carry):
        step(j, False)
        return carry

    lax.fori_loop(n_trips * ATT_UNROLL, n_full, single, 0)

    sub = (q_start - n_full * tk) // tq
    for v in range(tk // tq):
        @pl.when(sub == v)
        def _(v=v):
            step(n_full, True, (v + 1) * tq)
    out = acc_scr[...] / jnp.tile(l_scr[...], (1, KV_RANK // LANES))
    o_ref[0] = out.astype(o_ref.dtype).reshape(MLA_HEADS, tq, KV_RANK)


def mla_attention(ql, qr, ck, kr):
    _, _, tq, _ = ql.shape
    b, s, _ = ck.shape
    tk = min(ATT_TK, s)
    assert tk % tq == 0 and s % tk == 0 and tq & (tq - 1) == 0
    nq = s // tq
    rows = tq * MLA_HEADS
    hpc = max(1, min(MLA_HEADS, ATT_GROUP_ROWS // tq))
    qspec = lambda dim: pl.BlockSpec((1, MLA_HEADS, tq, dim), lambda bi, i: (bi * nq + i, 0, 0, 0))
    return pl.pallas_call(
        functools.partial(_mla_attn_kernel, tq=tq, tk=tk, hpc=hpc),
        grid=(b, nq),
        in_specs=[qspec(KV_RANK), qspec(ROPE_DIM),
                  pl.BlockSpec((1, s, KV_RANK), lambda bi, i: (bi, 0, 0)),
                  pl.BlockSpec((1, s, ROPE_DIM), lambda bi, i: (bi, 0, 0))],
        out_specs=qspec(KV_RANK),
        out_shape=jax.ShapeDtypeStruct(ql.shape, BF16),
        scratch_shapes=[pltpu.VMEM((rows, LANES), F32), pltpu.VMEM((rows, LANES), F32),
                        pltpu.VMEM((rows, KV_RANK), F32)],
        compiler_params=_cp("parallel", "parallel"),
        name="mla_attention",
    )(ql, qr, ck, kr)


def _ssd_kernel(xbc_ref, z_ref, dt_ref, dtb_ref, a_ref, dsk_ref, ng_ref, exp_ref, o_ref,
                state_scr, *, chunk):
    d_inner = SSM_HEADS * SSM_HEAD_DIM
    gn = SSM_GROUPS * D_STATE
    rep = SSM_HEADS // SSM_GROUPS
    gw = rep * SSM_HEAD_DIM

    @pl.when(pl.program_id(1) == 0)
    def _():
        state_scr[...] = jnp.zeros(state_scr.shape, F32)

    x_raw = dt_ref[0] + dtb_ref[...]
    dt = jnp.maximum(x_raw, 0.0) + jnp.log(1.0 + jnp.exp(-jnp.abs(x_raw)))
    da = dt * a_ref[...]
    row = lax.broadcasted_iota(I32, (chunk, chunk), 0)
    col = lax.broadcasted_iota(I32, (chunk, chunk), 1)
    causal = row >= col
    tri = jnp.where(causal, 1.0, 0.0).astype(BF16)
    acum = jnp.zeros(da.shape, F32)
    rem = da
    for _ in range(3):
        part = rem.astype(BF16)
        acum = acum + jnp.dot(tri, part, preferred_element_type=F32)
        rem = rem - part.astype(F32)
    a2 = acum * LOG2E
    a2_t = a2.T
    src_t = a2_t - jnp.log2(dt.T)
    w_t = jnp.exp2(a2_t[:, chunk - 1:chunk] - src_t)
    e_end = jnp.exp2(a2[chunk - 1:chunk, :])
    e_all = jnp.dot(jnp.exp2(a2).astype(BF16), exp_ref[...], preferred_element_type=F32)
    head_of_lane = lax.broadcasted_iota(I32, (1, gw), 1) // SSM_HEAD_DIM

    for g in range(SSM_GROUPS):
        bg = xbc_ref[0, :, d_inner + g * D_STATE:d_inner + (g + 1) * D_STATE]
        cg = xbc_ref[0, :, d_inner + gn + g * D_STATE:d_inner + gn + (g + 1) * D_STATE]
        bt = bg.astype(F32).T
        cb = jnp.dot(cg, bt.astype(BF16), preferred_element_type=F32)
        st = state_scr[g]
        ys = jnp.dot(cg, st.astype(BF16), preferred_element_type=F32)
        xg = xbc_ref[0, :, g * gw:(g + 1) * gw].astype(F32)
        y = jnp.zeros((chunk, gw), F32)
        upd = jnp.zeros((D_STATE, gw), F32)
        sc = jnp.zeros((1, gw), F32)
        for r in range(rep):
            h = g * rep + r
            own = head_of_lane == r
            seg = a2[:, h:h + 1] - src_t[h:h + 1, :]
            m = (cb * jnp.exp2(jnp.where(causal, seg, -jnp.inf))).astype(BF16)
            xm = jnp.where(own, xg, 0.0).astype(BF16)
            y = y + jnp.dot(m, xm, preferred_element_type=F32)
            upd = upd + jnp.dot((bt * w_t[h:h + 1, :]).astype(BF16), xm, preferred_element_type=F32)
            sc = jnp.where(own, e_end[:, h:h + 1], sc)
        state_scr[g] = st * sc + upd
        gs = slice(g * gw, (g + 1) * gw)
        y = y + ys * e_all[:, gs] + xg * dsk_ref[:, gs]
        zg = z_ref[0, :, gs].astype(F32)
        y = y * (zg * _sigmoid(zg))
        y = y * lax.rsqrt(jnp.mean(y * y, axis=-1, keepdims=True) + RMS_EPS) * ng_ref[:, gs]
        o_ref[0, :, gs] = y.astype(o_ref.dtype)


def ssd(xbc, z, ha3, dt_bias, a_neg, d_skip, norm_g):
    bsz, s, c = xbc.shape
    d_inner = SSM_HEADS * SSM_HEAD_DIM
    chunk = min(SSD_CHUNK, s)
    gw = d_inner // SSM_GROUPS
    vec = lambda n: pl.BlockSpec((1, n), lambda bi, ci: (0, 0))
    expand = (jnp.arange(d_inner, dtype=I32)[None, :] // SSM_HEAD_DIM
              == jnp.arange(LANES, dtype=I32)[:, None]).astype(BF16)
    return pl.pallas_call(
        functools.partial(_ssd_kernel, chunk=chunk),
        grid=(bsz, s // chunk),
        in_specs=[pl.BlockSpec((1, chunk, c), lambda bi, ci: (bi, ci, 0)),
                  pl.BlockSpec((1, chunk, d_inner), lambda bi, ci: (bi, ci, 0)),
                  pl.BlockSpec((1, chunk, LANES), lambda bi, ci: (bi, ci, HA_DT // LANES)),
                  vec(LANES), vec(LANES), vec(d_inner), vec(d_inner),
                  pl.BlockSpec((LANES, d_inner), lambda bi, ci: (0, 0))],
        out_specs=pl.BlockSpec((1, chunk, d_inner), lambda bi, ci: (bi, ci, 0)),
        out_shape=jax.ShapeDtypeStruct((bsz, s, d_inner), BF16),
        scratch_shapes=[pltpu.VMEM((SSM_GROUPS, D_STATE, gw), F32)],
        compiler_params=_cp("parallel", "arbitrary"),
        name="ssd_scan",
    )(xbc, z, ha3, dt_bias, a_neg, d_skip, norm_g, expand)


def _xattn_kernel(q_ref, kv_ref, o_ref, *, scale):
    hd = XA_HEADS * XA_HEAD_DIM
    nt = (((1,), (1,)), ((), ()))
    for h in range(XA_HEADS):
        sl = slice(h * XA_HEAD_DIM, (h + 1) * XA_HEAD_DIM)
        q = q_ref[0, :, sl]
        k = kv_ref[0, :, sl]
        v = kv_ref[0, :, hd + h * XA_HEAD_DIM:hd + (h + 1) * XA_HEAD_DIM]
        s = lax.dot_general(q, k, nt, preferred_element_type=F32) * scale
        p = jnp.exp(s - jnp.max(s, axis=-1, keepdims=True))
        l = jnp.sum(p, axis=-1, keepdims=True)
        o = jnp.dot(p.astype(BF16), v, preferred_element_type=F32) / l
        o_ref[0, :, sl] = o.astype(o_ref.dtype)


def mem_attention(q, kv):
    bsz, s, hd = q.shape
    m = kv.shape[1]
    tq = min(XA_TQ, s)
    return pl.pallas_call(
        functools.partial(_xattn_kernel, scale=float(XA_HEAD_DIM ** -0.5)),
        grid=(bsz, s // tq),
        in_specs=[pl.BlockSpec((1, tq, hd), lambda bi, i: (bi, i, 0)),
                  pl.BlockSpec((1, m, 2 * hd), lambda bi, i: (bi, 0, 0))],
        out_specs=pl.BlockSpec((1, tq, hd), lambda bi, i: (bi, i, 0)),
        out_shape=jax.ShapeDtypeStruct((bsz, s, hd), BF16),
        compiler_params=_cp("parallel", "parallel"),
        name="mem_attention",
    )(q, kv)


def _layer_norm(v, g, b):
    mu = jnp.mean(v, axis=-1, keepdims=True)
    d = v - mu
    var = jnp.mean(d * d, axis=-1, keepdims=True)
    return d * lax.rsqrt(var + NORM_EPS) * g + b


def _merge_kernel(ol_ref, yn_ref, cm_ref, g_ref, x_ref, wuv_ref, wa_ref, wb_ref, wc_ref, wo_ref,
                  lg_ref, lb_ref, xo_ref, a_scr, *, alpha):
    tm, d = x_ref.shape
    for h in range(MLA_HEADS):
        a_scr[:, h * V_DIM:(h + 1) * V_DIM] = jnp.dot(
            ol_ref[:, h].reshape(tm, KV_RANK), wuv_ref[h],
            preferred_element_type=F32).astype(BF16)
    o_a = jnp.dot(a_scr[...], wa_ref[...], preferred_element_type=F32)
    o_b = jnp.dot(yn_ref[...], wb_ref[...], preferred_element_type=F32)
    o_c = jnp.dot(cm_ref[...], wc_ref[...], preferred_element_type=F32)
    g = g_ref[...].astype(F32)
    merged = (_sigmoid(g[:, :d]) * o_a + _sigmoid(g[:, d:2 * d]) * o_b + _sigmoid(g[:, 2 * d:]) * o_c)
    y = alpha * x_ref[...] + jnp.dot(merged.astype(BF16), wo_ref[...], preferred_element_type=F32)
    xo_ref[...] = _layer_norm(y, lg_ref[...], lb_ref[...])


def merge_ln(ol, yn, cm, g, x, wuv, wa, wb, wc, wo, ln_g, ln_b, alpha):
    t, d = x.shape
    tm = min(MERGE_TM, t)
    tq = ol.shape[2]
    row = lambda n: pl.BlockSpec((tm, n), lambda i: (i, 0))
    full = lambda a: pl.BlockSpec(a.shape, lambda i: (0,) * a.ndim, pipeline_mode=pl.Buffered(1))
    return pl.pallas_call(
        functools.partial(_merge_kernel, alpha=alpha),
        grid=(t // tm,),
        in_specs=[pl.BlockSpec((tm // tq, MLA_HEADS, tq, KV_RANK), lambda i: (i, 0, 0, 0)),
                  row(yn.shape[1]), row(cm.shape[1]), row(g.shape[1]), row(d),
                  full(wuv), full(wa), full(wb), full(wc), full(wo), full(ln_g), full(ln_b)],
        out_specs=row(d),
        out_shape=jax.ShapeDtypeStruct((t, d), F32),
        scratch_shapes=[pltpu.VMEM((tm, MLA_HEADS * V_DIM), BF16)],
        compiler_params=_cp("parallel"),
        name="merge_ln",
    )(ol, yn, cm, g, x, wuv, wa, wb, wc, wo, ln_g, ln_b)


def _first_max(v, expert):
    m = jnp.max(v, axis=0, keepdims=True)
    idx = jnp.min(jnp.where(v == m, expert, N_EXPERTS), axis=0, keepdims=True)
    return m, idx


def _route_kernel(x_ref, rwh_ref, rwl_ref, rbt_ref, info_ref, infot_ref, tinfo_ref, cnt_ref, carry_scr):
    tm = x_ref.shape[0]

    @pl.when(pl.program_id(0) == 0)
    def _():
        carry_scr[...] = jnp.zeros(carry_scr.shape, F32)

    x = x_ref[...]
    x_hi = x.astype(BF16)
    x_lo = (x - x_hi.astype(F32)).astype(BF16)
    logits = (jnp.dot(x_hi, rwh_ref[...], preferred_element_type=F32)
              + jnp.dot(x_lo, rwh_ref[...], preferred_element_type=F32)
              + jnp.dot(x_hi, rwl_ref[...], preferred_element_type=F32))
    scores = _sigmoid(logits.T[:N_EXPERTS, :])
    sel = scores + jnp.tile(rbt_ref[...], (1, tm // LANES))
    expert = lax.broadcasted_iota(I32, (N_EXPERTS, tm), 0)
    neg = -jnp.inf
    best_score = None
    best_group = None
    for j in range(N_EXPERT_GROUPS):
        in_j = (expert >= j * EXPERTS_PER_GROUP) & (expert < (j + 1) * EXPERTS_PER_GROUP)
        v = jnp.where(in_j, sel, neg)
        m1, i1 = _first_max(v, expert)
        m2, _ = _first_max(jnp.where(expert == i1, neg, v), expert)
        gs = m1 + m2
        if j == 0:
            best_score, best_group = gs, jnp.zeros_like(i1)
        else:
            better = gs > best_score
            best_score = jnp.where(better, gs, best_score)
            best_group = jnp.where(better, j, best_group)
    lo = best_group * EXPERTS_PER_GROUP
    v = jnp.where((expert >= lo) & (expert < lo + EXPERTS_PER_GROUP), sel, neg)
    _, e1 = _first_max(v, expert)
    _, e2 = _first_max(jnp.where(expert == e1, neg, v), expert)
    w1 = jnp.sum(jnp.where(expert == e1, scores, 0.0), axis=0, keepdims=True)
    w2 = jnp.sum(jnp.where(expert == e2, scores, 0.0), axis=0, keepdims=True)
    wsum = w1 + w2
    member = jnp.where(expert == e1, 1.0, jnp.where(expert == e2, 1.0, 0.0))
    row = lax.broadcasted_iota(I32, (tm, tm), 0)
    col = lax.broadcasted_iota(I32, (tm, tm), 1)
    earlier = jnp.where(row < col, 1.0, 0.0).astype(BF16)
    lrank = jnp.dot(member.astype(BF16), earlier, preferred_element_type=F32)
    n_col = jnp.sum(member, axis=1, keepdims=True)
    diag = (lax.broadcasted_iota(I32, (N_EXPERTS, LANES), 0)
            == lax.broadcasted_iota(I32, (N_EXPERTS, LANES), 1))
    n = jnp.sum(jnp.where(diag, n_col, 0.0), axis=0, keepdims=True)
    n8 = jnp.floor((n + (ROW_ALIGN - 1)) * (1.0 / ROW_ALIGN)) * ROW_ALIGN
    ua = lax.broadcasted_iota(I32, (LANES, LANES), 0)
    ub = lax.broadcasted_iota(I32, (LANES, LANES), 1)
    lower_experts = jnp.where(ua < ub, 1.0, 0.0).astype(BF16)
    n8_rows = jnp.broadcast_to(n8, (8, LANES))
    block_start = jnp.dot(n8_rows.astype(BF16), lower_experts, preferred_element_type=F32)
    start_col = jnp.sum(jnp.where(diag, block_start[0:1, :], 0.0), axis=1, keepdims=True)
    local_row = start_col + lrank
    j1 = jnp.sum(jnp.where(expert == e1, local_row, 0.0), axis=0, keepdims=True)
    j2 = jnp.sum(jnp.where(expert == e2, local_row, 0.0), axis=0, keepdims=True)
    carry = carry_scr[...]
    sub = lax.broadcasted_iota(I32, (8, LANES), 0)
    tinfo_ref[0] = jnp.where(sub == 0, n8_rows, jnp.where(sub == 1, block_start, jnp.where(sub == 2, carry, 0.0)))
    new_carry = carry + n8_rows
    carry_scr[...] = new_carry
    cnt_ref[...] = new_carry
    field = lax.broadcasted_iota(I32, (8, tm), 0)
    info_t = jnp.where(field == 0, e1.astype(F32),
             jnp.where(field == 1, e2.astype(F32),
             jnp.where(field == 2, w1 / wsum,
             jnp.where(field == 3, w2 / wsum,
             jnp.where(field == 4, j1,
             jnp.where(field == 5, j2, 0.0))))))
    infot_ref[0] = info_t
    info_ref[...] = jnp.concatenate([info_t, jnp.zeros((LANES - 8, tm), F32)], axis=0).T


def route(x, rw_hi, rw_lo, rb):
    t, d = x.shape
    tm = min(DISPATCH_TM, t)
    return pl.pallas_call(
        _route_kernel,
        grid=(t // tm,),
        in_specs=[pl.BlockSpec((tm, d), lambda i: (i, 0)),
                  pl.BlockSpec((d, LANES), lambda i: (0, 0)),
                  pl.BlockSpec((d, LANES), lambda i: (0, 0)),
                  pl.BlockSpec((N_EXPERTS, LANES), lambda i: (0, 0))],
        out_specs=[pl.BlockSpec((tm, LANES), lambda i: (i, 0)),
                   pl.BlockSpec((1, 8, tm), lambda i: (i, 0, 0)),
                   pl.BlockSpec((1, 8, LANES), lambda i: (i, 0, 0)),
                   pl.BlockSpec((8, LANES), lambda i: (0, 0))],
        out_shape=[jax.ShapeDtypeStruct((t, LANES), F32),
                   jax.ShapeDtypeStruct((t // tm, 8, tm), F32),
                   jax.ShapeDtypeStruct((t // tm, 8, LANES), F32),
                   jax.ShapeDtypeStruct((8, LANES), F32)],
        scratch_shapes=[pltpu.VMEM((8, LANES), F32)],
        compiler_params=_cp("arbitrary"),
        name="route",
    )(x, rw_hi, rw_lo, rb)


def _block_copies(plan_ref, local_ref, sorted_ref, sem, to_sorted, start):
    for e in range(N_EXPERTS):
        count = plan_ref[0, 0, e]
        loc0 = plan_ref[0, 0, N_EXPERTS + e]
        dst0 = plan_ref[0, 0, 2 * N_EXPERTS + e]

        def body(k, carry, loc0=loc0, dst0=dst0):
            loc = local_ref.at[pl.ds(pl.multiple_of(loc0 + k * ROW_ALIGN, ROW_ALIGN), ROW_ALIGN)]
            srt = sorted_ref.at[pl.ds(pl.multiple_of(dst0 + k * ROW_ALIGN, ROW_ALIGN), ROW_ALIGN)]
            cp = pltpu.make_async_copy(loc, srt, sem) if to_sorted else pltpu.make_async_copy(srt, loc, sem)
            if start:
                cp.start()
            else:
                cp.wait()
            return carry

        lax.fori_loop(0, count, body, 0)


def _dispatch_kernel(plan_ref, prev_plan_ref, infot_ref, x_ref, init_ref, o_ref, local_scr, sem):
    del init_ref
    tm = x_ref.shape[0]
    i = pl.program_id(0)
    slot = i % 2
    j1 = infot_ref[0, 4:5, :]
    j2 = infot_ref[0, 5:6, :]
    rowid = lax.broadcasted_iota(I32, (LOCAL_ROWS, tm), 0).astype(F32)
    sel = jnp.where(rowid == j1, 1.0, jnp.where(rowid == j2, 1.0, 0.0)).astype(BF16)
    local_scr[slot] = jnp.dot(sel, x_ref[...].astype(BF16), preferred_element_type=F32)

    @pl.when(i > 0)
    def _():
        _block_copies(prev_plan_ref, local_scr.at[1 - slot], o_ref, sem.at[1 - slot], True, False)

    _block_copies(plan_ref, local_scr.at[slot], o_ref, sem.at[slot], True, True)

    @pl.when(i == pl.num_programs(0) - 1)
    def _():
        _block_copies(plan_ref, local_scr.at[slot], o_ref, sem.at[slot], True, False)


def dispatch_rows(x, info_t, plan, n_rows):
    t, d = x.shape
    tm = min(DISPATCH_TM, t)
    init = jnp.zeros((n_rows, d), F32)
    return pl.pallas_call(
        _dispatch_kernel,
        grid=(t // tm,),
        in_specs=[pl.BlockSpec((1, 1, 3 * N_EXPERTS), lambda i: (i, 0, 0), memory_space=pltpu.SMEM),
                  pl.BlockSpec((1, 1, 3 * N_EXPERTS), lambda i: (jnp.maximum(i - 1, 0), 0, 0),
                               memory_space=pltpu.SMEM),
                  pl.BlockSpec((1, 8, tm), lambda i: (i, 0, 0)),
                  pl.BlockSpec((tm, d), lambda i: (i, 0)),
                  pl.BlockSpec(memory_space=pl.ANY)],
        out_specs=pl.BlockSpec(memory_space=pl.ANY),
        out_shape=jax.ShapeDtypeStruct((n_rows, d), F32),
        scratch_shapes=[pltpu.VMEM((2, LOCAL_ROWS, d), F32), pltpu.SemaphoreType.DMA((2,))],
        input_output_aliases={4: 0},
        compiler_params=_cp("arbitrary"),
        name="moe_dispatch",
    )(plan, plan, info_t, x, init)


def _expert_kernel(te_ref, nu_ref, xs_ref, w13_ref, w2_ref, o_ref):
    del te_ref
    i = pl.program_id(0)
    f = w2_ref.shape[1]

    @pl.when(i < nu_ref[0])
    def _():
        xb = xs_ref[...].astype(BF16)
        h = jnp.dot(xb, w13_ref[0], preferred_element_type=F32)
        h1 = h[:, :f]
        act = (h1 * _sigmoid(h1) * h[:, f:]).astype(BF16)
        o_ref[...] = jnp.dot(act, w2_ref[0], preferred_element_type=F32)

    @pl.when(i >= nu_ref[0])
    def _():
        o_ref[...] = jnp.zeros(o_ref.shape, o_ref.dtype)


def expert_ffn(xs, w13, w2, tile_expert, n_used):
    p, d = xs.shape
    f = w2.shape[1]
    n_tiles = p // MOE_TM
    grid_spec = pltpu.PrefetchScalarGridSpec(
        num_scalar_prefetch=2,
        grid=(n_tiles,),
        in_specs=[pl.BlockSpec((MOE_TM, d), lambda i, te, nu: (i, 0)),
                  pl.BlockSpec((1, d, 2 * f), lambda i, te, nu: (te[i], 0, 0)),
                  pl.BlockSpec((1, f, d), lambda i, te, nu: (te[i], 0, 0))],
        out_specs=pl.BlockSpec((MOE_TM, d), lambda i, te, nu: (i, 0)),
    )
    return pl.pallas_call(
        _expert_kernel,
        grid_spec=grid_spec,
        out_shape=jax.ShapeDtypeStruct((p, d), F32),
        compiler_params=_cp("arbitrary"),
        name="expert_ffn",
    )(tile_expert, n_used, xs, w13, w2)


def _combine_kernel(plan_ref, next_plan_ref, info_ref, x_ref, ys_ref, lg_ref, lb_ref, xo_ref, xb_ref,
                    local_scr, sem, *, alpha):
    tm = x_ref.shape[0]
    i = pl.program_id(0)
    slot = i % 2

    @pl.when(i == 0)
    def _():
        _block_copies(plan_ref, local_scr.at[slot], ys_ref, sem.at[slot], False, True)

    @pl.when(i + 1 < pl.num_programs(0))
    def _():
        _block_copies(next_plan_ref, local_scr.at[1 - slot], ys_ref, sem.at[1 - slot], False, True)

    _block_copies(plan_ref, local_scr.at[slot], ys_ref, sem.at[slot], False, False)
    last = N_EXPERTS - 1
    used = plan_ref[0, 0, N_EXPERTS + last] + plan_ref[0, 0, last] * ROW_ALIGN
    rowid = lax.broadcasted_iota(I32, (LOCAL_ROWS, 1), 0)
    y = jnp.where(rowid < used, local_scr[slot], 0.0).astype(BF16)
    info = info_ref[...]
    col = lax.broadcasted_iota(I32, (tm, LOCAL_ROWS), 1).astype(F32)
    gate = jnp.where(col == info[:, 4:5], info[:, 2:3],
                     jnp.where(col == info[:, 5:6], info[:, 3:4], 0.0)).astype(BF16)
    moe = jnp.dot(gate, y, preferred_element_type=F32)
    out = _layer_norm(alpha * x_ref[...] + moe, lg_ref[...], lb_ref[...])
    xo_ref[...] = out
    xb_ref[...] = out.astype(BF16)


def combine_ln(plan, info, x, ys, ln_g, ln_b, alpha):
    t, d = x.shape
    tm = min(DISPATCH_TM, t)
    return pl.pallas_call(
        functools.partial(_combine_kernel, alpha=alpha),
        grid=(t // tm,),
        in_specs=[pl.BlockSpec((1, 1, 3 * N_EXPERTS), lambda i: (i, 0, 0), memory_space=pltpu.SMEM),
                  pl.BlockSpec((1, 1, 3 * N_EXPERTS), lambda i: (jnp.minimum(i + 1, t // tm - 1), 0, 0),
                               memory_space=pltpu.SMEM),
                  pl.BlockSpec((tm, LANES), lambda i: (i, 0)),
                  pl.BlockSpec((tm, d), lambda i: (i, 0)),
                  pl.BlockSpec(memory_space=pl.ANY),
                  pl.BlockSpec((1, d), lambda i: (0, 0)),
                  pl.BlockSpec((1, d), lambda i: (0, 0))],
        out_specs=[pl.BlockSpec((tm, d), lambda i: (i, 0)), pl.BlockSpec((tm, d), lambda i: (i, 0))],
        out_shape=[jax.ShapeDtypeStruct((t, d), F32), jax.ShapeDtypeStruct((t, d), BF16)],
        scratch_shapes=[pltpu.VMEM((2, LOCAL_ROWS, d), F32), pltpu.SemaphoreType.DMA((2,))],
        compiler_params=_cp("arbitrary"),
        name="moe_combine_ln",
    )(plan, plan, info, x, ys, ln_g, ln_b)


def _prep_layer(l, p):
    w_in = p["w_in"][l]
    d = w_in.shape[0]
    o_dq = 0
    o_dkv = o_dq + Q_RANK
    o_z = o_dkv + KV_RANK + ROPE_DIM
    d_inner = SSM_HEADS * SSM_HEAD_DIM
    conv_ch = d_inner + 2 * SSM_GROUPS * D_STATE
    o_xbc = o_z + d_inner
    o_dt = o_xbc + conv_ch
    o_qm = o_dt + SSM_HEADS
    o_g = o_qm + XA_HEADS * XA_HEAD_DIM
    half = ROPE_DIM // 2
    kr0 = o_dkv + KV_RANK
    zeros = lambda n: jnp.zeros((d, n), F32)
    w_small = jnp.concatenate([
        w_in[:, o_dq:o_dq + Q_RANK],
        w_in[:, o_dkv:o_dkv + KV_RANK],
        w_in[:, kr0:kr0 + ROPE_DIM], zeros(LANES - ROPE_DIM),
        w_in[:, kr0 + half:kr0 + ROPE_DIM], w_in[:, kr0:kr0 + half], zeros(LANES - ROPE_DIM),
        w_in[:, o_dt:o_dt + SSM_HEADS], zeros(LANES - SSM_HEADS)], axis=1)
    w_uq = p["w_uq"][l].reshape(Q_RANK, MLA_HEADS, NOPE_DIM + ROPE_DIM)
    wq_nope = w_uq[:, :, :NOPE_DIM].reshape(Q_RANK, -1)
    wq_rope = w_uq[:, :, NOPE_DIM:]
    wq_rope_sw = jnp.concatenate([wq_rope[:, :, half:], wq_rope[:, :, :half]], axis=-1)
    wq = jnp.concatenate([wq_nope, wq_rope.reshape(Q_RANK, -1), wq_rope_sw.reshape(Q_RANK, -1)], axis=1)
    w_ukv = p["w_ukv"][l]
    wuk_t = jnp.transpose(w_ukv[:, :, :NOPE_DIM], (1, 2, 0))
    wuv = jnp.transpose(w_ukv[:, :, NOPE_DIM:], (1, 0, 2))
    pad_heads = lambda v, fill: jnp.concatenate(
        [v.astype(F32), jnp.full((LANES - SSM_HEADS,), fill, F32)]).reshape(1, LANES)
    bf = lambda a: a.astype(BF16)
    return dict(
        w_small=bf(w_small), w_z=bf(w_in[:, o_z:o_z + d_inner]), w_xbc=bf(w_in[:, o_xbc:o_xbc + conv_ch]),
        w_qm=bf(w_in[:, o_qm:o_g]), w_g=bf(w_in[:, o_g:]),
        q_norm=p["q_norm"][l].reshape(1, -1), kv_norm=p["kv_norm"][l].reshape(1, -1),
        wq=bf(wq), wuk_t=bf(wuk_t), wuv=bf(wuv),
        conv_w=p["conv_w"][l], conv_b=p["conv_b"][l].reshape(1, -1),
        dt_bias=pad_heads(p["dt_bias"][l], 0.0),
        a_neg=pad_heads(-jnp.exp(p["a_log"][l].astype(F32)), 0.0),
        d_skip=jnp.repeat(p["d_skip"][l].astype(F32), SSM_HEAD_DIM).reshape(1, -1),
        ssm_norm=p["ssm_norm"][l].reshape(1, -1),
        w_mem_kv=bf(p["w_mem_kv"][l]),
        wa=bf(p["w_proj_a"][l]), wb=bf(p["w_proj_b"][l]), wc=bf(p["w_proj_c"][l]), wo=bf(p["w_out"][l]),
        ln1_g=p["ln1_g"][l].reshape(1, -1), ln1_b=p["ln1_b"][l].reshape(1, -1),
        w13=bf(jnp.concatenate([p["exp_w1"][l], p["exp_w3"][l]], axis=-1)), w2=bf(p["exp_w2"][l]),
        ln2_g=p["ln2_g"][l].reshape(1, -1), ln2_b=p["ln2_b"][l].reshape(1, -1),
    )


def _dispatch_plan(tinfo, counts, t):
    n_tok_tiles = tinfo.shape[0]
    n8 = tinfo[:, 0, :N_EXPERTS].astype(I32)
    block_start = tinfo[:, 1, :N_EXPERTS].astype(I32)
    carry = tinfo[:, 2, :N_EXPERTS].astype(I32)
    rows = counts[0, :N_EXPERTS].astype(I32)
    tiles = (rows + MOE_TM - 1) // MOE_TM
    tile_end = jnp.cumsum(tiles)
    row_start = (tile_end - tiles) * MOE_TM
    plan = jnp.concatenate([n8 // ROW_ALIGN, block_start, row_start[None, :] + carry], axis=1)
    max_rows = TOP_K * t + N_EXPERTS * (ROW_ALIGN - 1) * n_tok_tiles
    n_tiles = -(-max_rows // MOE_TM) + N_EXPERTS
    tile_ids = jnp.arange(n_tiles, dtype=I32)
    tile_expert = jnp.minimum(jnp.sum(tile_ids[:, None] >= tile_end[None, :], axis=1), N_EXPERTS - 1)
    n_used = tile_end[-1:].astype(I32)
    return plan.reshape(n_tok_tiles, 1, 3 * N_EXPERTS), tile_expert.astype(I32), n_used, n_tiles * MOE_TM


def kernel(x, mem, positions, w_in, q_norm, w_uq, kv_norm, w_ukv, w_proj_a, conv_w, conv_b, dt_bias, a_log,
           d_skip, ssm_norm, w_proj_b, w_mem_kv, w_proj_c, w_out, ln1_g, ln1_b, router_w, router_bias,
           exp_w1, exp_w3, exp_w2, ln2_g, ln2_b):
    params = dict(w_in=w_in, q_norm=q_norm, w_uq=w_uq, kv_norm=kv_norm, w_ukv=w_ukv, w_proj_a=w_proj_a,
                  conv_w=conv_w, conv_b=conv_b, dt_bias=dt_bias, a_log=a_log, d_skip=d_skip,
                  ssm_norm=ssm_norm, w_proj_b=w_proj_b, w_mem_kv=w_mem_kv, w_proj_c=w_proj_c, w_out=w_out,
                  ln1_g=ln1_g, ln1_b=ln1_b, exp_w1=exp_w1, exp_w3=exp_w3, exp_w2=exp_w2,
                  ln2_g=ln2_g, ln2_b=ln2_b)
    bsz, s, d = x.shape
    t = bsz * s
    depth = w_in.shape[0]
    alpha = float((2 * depth) ** 0.25)
    n_mem = mem.shape[1]

    inv = ROPE_THETA ** (-jnp.arange(0, ROPE_DIM, 2, dtype=F32) / ROPE_DIM)
    ang = positions.astype(F32)[..., None] * inv
    cos, sin = jnp.cos(ang), jnp.sin(ang)
    cosq = jnp.tile(jnp.concatenate([cos, cos], axis=-1), (1, 1, MLA_HEADS)).reshape(t, -1)
    sinq = jnp.tile(jnp.concatenate([-sin, sin], axis=-1), (1, 1, MLA_HEADS)).reshape(t, -1)

    rw = jnp.concatenate([router_w.astype(F32), jnp.zeros((d, LANES - N_EXPERTS), F32)], axis=1)
    rw_hi = rw.astype(BF16)
    rw_lo = (rw - rw_hi.astype(F32)).astype(BF16)
    rbt = jnp.broadcast_to(router_bias.astype(F32)[:, None], (N_EXPERTS, LANES))
    mem_b = mem.reshape(bsz * n_mem, d).astype(BF16)

    xf = x.reshape(t, d).astype(F32)
    xb = xf.astype(BF16)
    for l in range(depth):
        w = _prep_layer(l, params)
        ha = matmul(xb, w["w_small"], F32, "proj_small")
        z = matmul(xb, w["w_z"], BF16, "proj_z")
        xbc_c = proj_conv_silu(xb, w["w_xbc"], w["conv_w"], w["conv_b"], s).reshape(bsz, s, -1)
        qm = matmul(xb, w["w_qm"], BF16, "proj_qmem")
        g = matmul(xb, w["w_g"], BF16, "proj_gate")

        ql, qr, ck, kr = mla_prep(ha, cosq, sinq, w["q_norm"], w["kv_norm"], w["wq"], w["wuk_t"],
                                  min(ATT_TQ, s))
        o_lat = mla_attention(ql, qr, ck.reshape(bsz, s, KV_RANK), kr.reshape(bsz, s, ROPE_DIM))

        yn = ssd(xbc_c, z.reshape(bsz, s, -1), ha.reshape(bsz, s, HA_W), w["dt_bias"], w["a_neg"],
                 w["d_skip"], w["ssm_norm"]).reshape(t, -1)

        kv = matmul(mem_b, w["w_mem_kv"], BF16, "proj_memkv").reshape(bsz, n_mem, -1)
        cm = mem_attention(qm.reshape(bsz, s, -1), kv).reshape(t, -1)

        x1 = merge_ln(o_lat, yn, cm, g, xf, w["wuv"], w["wa"], w["wb"], w["wc"], w["wo"],
                           w["ln1_g"], w["ln1_b"], alpha)

        info, info_t, tinfo, counts = route(x1, rw_hi, rw_lo, rbt)
        plan, tile_expert, n_used, n_rows = _dispatch_plan(tinfo, counts, t)
        xs = dispatch_rows(x1, info_t, plan, n_rows)
        ys = expert_ffn(xs, w["w13"], w["w2"], tile_expert, n_used)
        xf, xb = combine_ln(plan, info, x1, ys, w["ln2_g"], w["ln2_b"], alpha)
    return xf.reshape(bsz, s, d)
```

```python
import functools

import jax
import jax.numpy as jnp
from jax import lax
from jax.experimental import pallas as pl
from jax.experimental.pallas import tpu as pltpu

F32 = jnp.float32
BF16 = jnp.bfloat16
I32 = jnp.int32

MLA_HEADS = 8
Q_RANK = 384
KV_RANK = 256
NOPE_DIM = 128
ROPE_DIM = 64
V_DIM = 128
ROPE_THETA = 10000.0
SSM_HEADS = 32
SSM_HEAD_DIM = 64
SSM_GROUPS = 8
D_STATE = 128
CONV_K = 4
XA_HEADS = 4
XA_HEAD_DIM = 256
N_EXPERTS = 16
N_EXPERT_GROUPS = 4
EXPERTS_PER_GROUP = 4
TOP_K = 2
NORM_EPS = 1e-5
RMS_EPS = 1e-6

LANES = 128
V7X_VMEM_LIMIT = 56 * 1024 * 1024

MM_TM = 1024
MM_TN = 1024
PREP_TM = 512
ATT_TQ = 128
ATT_TK = 512
ATT_GROUP_ROWS = 512
ATT_UNROLL = 2
LOG2E = 1.4426950408889634
PROJ_CONV_TN = 512
PROJ_CONV_SUB = 256
CONV_HALO = 16
SSD_CHUNK = 256
XA_TQ = 512
MERGE_TM = 512
MOE_TM = 512
DISPATCH_TM = 512
ROW_ALIGN = 16
LOCAL_ROWS = 1280

HA_W = 1024
HA_DQ = 0
HA_C = 384
HA_KR = 640
HA_KRS = 768
HA_DT = 896


def _cp(*sem):
    return pltpu.CompilerParams(dimension_semantics=sem, vmem_limit_bytes=V7X_VMEM_LIMIT)


def _sigmoid(x):
    return 1.0 / (1.0 + jnp.exp(-x))


def _mm_kernel(x_ref, w_ref, o_ref):
    o_ref[...] = jnp.dot(x_ref[...], w_ref[...], preferred_element_type=F32).astype(o_ref.dtype)


def matmul(x, w, out_dtype, name):
    m, k = x.shape
    n = w.shape[1]
    tm = min(MM_TM, m)
    tn = min(MM_TN, n)
    return pl.pallas_call(
        _mm_kernel,
        grid=(n // tn, m // tm),
        in_specs=[pl.BlockSpec((tm, k), lambda j, i: (i, 0)),
                  pl.BlockSpec((k, tn), lambda j, i: (0, j))],
        out_specs=pl.BlockSpec((tm, tn), lambda j, i: (i, j)),
        out_shape=jax.ShapeDtypeStruct((m, n), out_dtype),
        compiler_params=_cp("parallel", "parallel"),
        name=name,
    )(x, w)


def _proj_conv_kernel(x_ref, xh_ref, w_ref, cw_ref, cb_ref, o_ref, *, tiles_per_seq):
    first = (pl.program_id(1) % tiles_per_seq) == 0
    tn = o_ref.shape[1]
    sub = min(PROJ_CONV_SUB, tn)
    for c in range(tn // sub):
        cs = slice(c * sub, (c + 1) * sub)
        w = w_ref[:, cs]
        cw = cw_ref[:, cs]
        u = jnp.dot(x_ref[...], w, preferred_element_type=F32)
        halo = jnp.dot(xh_ref[...], w, preferred_element_type=F32)
        halo = jnp.where(first, jnp.zeros_like(halo), halo)
        ext = jnp.concatenate([halo, u], axis=0)
        prev = pltpu.roll(ext, 1, 0)
        near = cw[3:4, :] * ext + cw[2:3, :] * prev
        far = cw[1:2, :] * ext + cw[0:1, :] * prev
        acc = (near + pltpu.roll(far, 2, 0))[CONV_HALO:, :] + cb_ref[:, cs]
        half = 0.5 * acc
        o_ref[:, cs] = (half * jnp.tanh(half) + half).astype(o_ref.dtype)


def proj_conv_silu(x, w, conv_w, conv_b, seq_len):
    m, k = x.shape
    n = w.shape[1]
    tm = min(MM_TM, seq_len)
    tn = min(PROJ_CONV_TN, n)
    hb = tm // CONV_HALO
    return pl.pallas_call(
        functools.partial(_proj_conv_kernel, tiles_per_seq=seq_len // tm),
        grid=(n // tn, m // tm),
        in_specs=[pl.BlockSpec((tm, k), lambda j, i: (i, 0)),
                  pl.BlockSpec((CONV_HALO, k), lambda j, i: (jnp.maximum(i * hb - 1, 0), 0)),
                  pl.BlockSpec((k, tn), lambda j, i: (0, j)),
                  pl.BlockSpec((CONV_K, tn), lambda j, i: (0, j)),
                  pl.BlockSpec((1, tn), lambda j, i: (0, j))],
        out_specs=pl.BlockSpec((tm, tn), lambda j, i: (i, j)),
        out_shape=jax.ShapeDtypeStruct((m, n), BF16),
        compiler_params=_cp("parallel", "parallel"),
        name="proj_xbc_conv",
    )(x, x, w, conv_w, conv_b)


def _mla_prep_kernel(ha_ref, cos_ref, sin_ref, qn_ref, kvn_ref, wq_ref, wuk_ref,
                     ql_ref, qr_ref, ck_ref, kr_ref, *, scale):
    nb, _, tq, _ = ql_ref.shape
    ha = ha_ref[...]
    dq = ha[:, HA_DQ:HA_DQ + Q_RANK]
    c_q = dq * lax.rsqrt(jnp.mean(dq * dq, axis=-1, keepdims=True) + RMS_EPS) * qn_ref[...]
    q = jnp.dot(c_q.astype(BF16), wq_ref[...], preferred_element_type=F32)
    n_nope = MLA_HEADS * NOPE_DIM
    n_rope = MLA_HEADS * ROPE_DIM
    cosq = cos_ref[...]
    sinq = sin_ref[...]
    q_rope = ((q[:, n_nope:n_nope + n_rope] * cosq + q[:, n_nope + n_rope:] * sinq) * scale).astype(BF16)
    for h in range(MLA_HEADS):
        qh = q[:, h * NOPE_DIM:(h + 1) * NOPE_DIM].astype(BF16)
        ql = jnp.dot(qh, wuk_ref[h], preferred_element_type=F32)
        ql_ref[:, h] = (ql * scale).astype(BF16).reshape(nb, tq, KV_RANK)
        qr_ref[:, h] = q_rope[:, h * ROPE_DIM:(h + 1) * ROPE_DIM].reshape(nb, tq, ROPE_DIM)
    c = ha[:, HA_C:HA_C + KV_RANK]
    c_kv = c * lax.rsqrt(jnp.mean(c * c, axis=-1, keepdims=True) + RMS_EPS) * kvn_ref[...]
    ck_ref[...] = c_kv.astype(BF16)
    k_rope = (ha[:, HA_KR:HA_KR + LANES] * cosq[:, :LANES]
              + ha[:, HA_KRS:HA_KRS + LANES] * sinq[:, :LANES])
    kr_ref[...] = k_rope[:, :ROPE_DIM].astype(BF16)


def mla_prep(ha, cosq, sinq, q_norm, kv_norm, wq, wuk_t, tq):
    t = ha.shape[0]
    tm = min(PREP_TM, t)
    nb = tm // tq
    scale = float((NOPE_DIM + ROPE_DIM) ** -0.5 * LOG2E)
    n_rope = MLA_HEADS * ROPE_DIM
    full = lambda shape: pl.BlockSpec(shape, lambda i: (0,) * len(shape))
    return pl.pallas_call(
        functools.partial(_mla_prep_kernel, scale=scale),
        grid=(t // tm,),
        in_specs=[pl.BlockSpec((tm, HA_W), lambda i: (i, 0)),
                  pl.BlockSpec((tm, n_rope), lambda i: (i, 0)),
                  pl.BlockSpec((tm, n_rope), lambda i: (i, 0)),
                  full((1, Q_RANK)), full((1, KV_RANK)),
                  full(wq.shape), full(wuk_t.shape)],
        out_specs=[pl.BlockSpec((nb, MLA_HEADS, tq, KV_RANK), lambda i: (i, 0, 0, 0)),
                   pl.BlockSpec((nb, MLA_HEADS, tq, ROPE_DIM), lambda i: (i, 0, 0, 0)),
                   pl.BlockSpec((tm, KV_RANK), lambda i: (i, 0)),
                   pl.BlockSpec((tm, ROPE_DIM), lambda i: (i, 0))],
        out_shape=[jax.ShapeDtypeStruct((t // tq, MLA_HEADS, tq, KV_RANK), BF16),
                   jax.ShapeDtypeStruct((t // tq, MLA_HEADS, tq, ROPE_DIM), BF16),
                   jax.ShapeDtypeStruct((t, KV_RANK), BF16),
                   jax.ShapeDtypeStruct((t, ROPE_DIM), BF16)],
        compiler_params=_cp("parallel"),
        name="mla_prep",
    )(ha, cosq, sinq, q_norm, kv_norm, wq, wuk_t)


def _mla_attn_kernel(ql_ref, qr_ref, ck_ref, kr_ref, o_ref, m_scr, l_scr, acc_scr, *, tq, tk, hpc):
    rc = hpc * tq
    n_groups = MLA_HEADS // hpc
    q_start = pl.program_id(1) * tq
    n_full = q_start // tk
    m_scr[...] = jnp.full(m_scr.shape, -jnp.inf, F32)
    l_scr[...] = jnp.zeros(l_scr.shape, F32)
    acc_scr[...] = jnp.zeros(acc_scr.shape, F32)
    nt = (((1,), (1,)), ((), ()))

    def step(j, masked, width=tk):
        ks = pl.multiple_of(j * tk, tk)
        ck = ck_ref[0, pl.ds(ks, width), :]
        kr = kr_ref[0, pl.ds(ks, width), :]
        if masked:
            q_pos = q_start + (lax.broadcasted_iota(I32, (rc, width), 0) & (tq - 1))
            k_pos = ks + lax.broadcasted_iota(I32, (rc, width), 1)
            visible = k_pos <= q_pos
        for c in range(n_groups):
            rs = slice(c * rc, (c + 1) * rc)
            ql = ql_ref[0, c * hpc:(c + 1) * hpc].reshape(rc, KV_RANK)
            qr = qr_ref[0, c * hpc:(c + 1) * hpc].reshape(rc, ROPE_DIM)
            s = (lax.dot_general(ql, ck, nt, preferred_element_type=F32)
                 + lax.dot_general(qr, kr, nt, preferred_element_type=F32))
            if masked:
                s = jnp.where(visible, s, -jnp.inf)
            m_prev = m_scr[rs]
            m_new = jnp.maximum(m_prev, jnp.max(s, axis=-1, keepdims=True))
            alpha = jnp.exp2(m_prev - m_new)
            p = jnp.exp2(s - jnp.tile(m_new, (1, width // LANES)))
            l_scr[rs] = alpha * l_scr[rs] + jnp.sum(p, axis=-1, keepdims=True)
            acc_scr[rs] = (jnp.tile(alpha, (1, KV_RANK // LANES)) * acc_scr[rs]
                           + jnp.dot(p.astype(BF16), ck, preferred_element_type=F32))
            m_scr[rs] = m_new

    def trip(jj, carry):
        for u in range(ATT_UNROLL):
            step(ATT_UNROLL * jj + u, False)
        return carry

    n_trips = n_full // ATT_UNROLL
    lax.fori_loop(0, n_trips, trip, 0)

    def single(j, carry):
        step(j, False)
        return carry

    lax.fori_loop(n_trips * ATT_UNROLL, n_full, single, 0)

    sub = (q_start - n_full * tk) // tq
    for v in range(tk // tq):
        @pl.when(sub == v)
        def _(v=v):
            step(n_full, True, (v + 1) * tq)
    out = acc_scr[...] / jnp.tile(l_scr[...], (1, KV_RANK // LANES))
    o_ref[0] = out.astype(o_ref.dtype).reshape(MLA_HEADS, tq, KV_RANK)


def mla_attention(ql, qr, ck, kr):
    _, _, tq, _ = ql.shape
    b, s, _ = ck.shape
    tk = min(ATT_TK, s)
    assert tk % tq == 0 and s % tk == 0 and tq & (tq - 1) == 0
    nq = s // tq
    rows = tq * MLA_HEADS
    hpc = max(1, min(MLA_HEADS, ATT_GROUP_ROWS // tq))
    qspec = lambda dim: pl.BlockSpec((1, MLA_HEADS, tq, dim), lambda bi, i: (bi * nq + i, 0, 0, 0))
    return pl.pallas_call(
        functools.partial(_mla_attn_kernel, tq=tq, tk=tk, hpc=hpc),
        grid=(b, nq),
        in_specs=[qspec(KV_RANK), qspec(ROPE_DIM),
                  pl.BlockSpec((1, s, KV_RANK), lambda bi, i: (bi, 0, 0)),
                  pl.BlockSpec((1, s, ROPE_DIM), lambda bi, i: (bi, 0, 0))],
        out_specs=qspec(KV_RANK),
        out_shape=jax.ShapeDtypeStruct(ql.shape, BF16),
        scratch_shapes=[pltpu.VMEM((rows, LANES), F32), pltpu.VMEM((rows, LANES), F32),
                        pltpu.VMEM((rows, KV_RANK), F32)],
        compiler_params=_cp("parallel", "parallel"),
        name="mla_attention",
    )(ql, qr, ck, kr)


def _ssd_kernel(xbc_ref, z_ref, dt_ref, dtb_ref, a_ref, dsk_ref, ng_ref, exp_ref, o_ref,
                state_scr, *, chunk):
    d_inner = SSM_HEADS * SSM_HEAD_DIM
    gn = SSM_GROUPS * D_STATE
    rep = SSM_HEADS // SSM_GROUPS
    gw = rep * SSM_HEAD_DIM

    @pl.when(pl.program_id(1) == 0)
    def _():
        state_scr[...] = jnp.zeros(state_scr.shape, F32)

    x_raw = dt_ref[0] + dtb_ref[...]
    dt = jnp.maximum(x_raw, 0.0) + jnp.log(1.0 + jnp.exp(-jnp.abs(x_raw)))
    da = dt * a_ref[...]
    row = lax.broadcasted_iota(I32, (chunk, chunk), 0)
    col = lax.broadcasted_iota(I32, (chunk, chunk), 1)
    causal = row >= col
    tri = jnp.where(causal, 1.0, 0.0).astype(BF16)
    acum = jnp.zeros(da.shape, F32)
    rem = da
    for _ in range(3):
        part = rem.astype(BF16)
        acum = acum + jnp.dot(tri, part, preferred_element_type=F32)
        rem = rem - part.astype(F32)
    a2 = acum * LOG2E
    a2_t = a2.T
    src_t = a2_t - jnp.log2(dt.T)
    w_t = jnp.exp2(a2_t[:, chunk - 1:chunk] - src_t)
    e_end = jnp.exp2(a2[chunk - 1:chunk, :])
    e_all = jnp.dot(jnp.exp2(a2).astype(BF16), exp_ref[...], preferred_element_type=F32)
    head_of_lane = lax.broadcasted_iota(I32, (1, gw), 1) // SSM_HEAD_DIM

    for g in range(SSM_GROUPS):
        bg = xbc_ref[0, :, d_inner + g * D_STATE:d_inner + (g + 1) * D_STATE]
        cg = xbc_ref[0, :, d_inner + gn + g * D_STATE:d_inner + gn + (g + 1) * D_STATE]
        bt = bg.astype(F32).T
        cb = jnp.dot(cg, bt.astype(BF16), preferred_element_type=F32)
        st = state_scr[g]
        ys = jnp.dot(cg, st.astype(BF16), preferred_element_type=F32)
        xg = xbc_ref[0, :, g * gw:(g + 1) * gw].astype(F32)
        y = jnp.zeros((chunk, gw), F32)
        upd = jnp.zeros((D_STATE, gw), F32)
        sc = jnp.zeros((1, gw), F32)
        for r in range(rep):
            h = g * rep + r
            own = head_of_lane == r
            seg = a2[:, h:h + 1] - src_t[h:h + 1, :]
            m = (cb * jnp.exp2(jnp.where(causal, seg, -jnp.inf))).astype(BF16)
            xm = jnp.where(own, xg, 0.0).astype(BF16)
            y = y + jnp.dot(m, xm, preferred_element_type=F32)
            upd = upd + jnp.dot((bt * w_t[h:h + 1, :]).astype(BF16), xm, preferred_element_type=F32)
            sc = jnp.where(own, e_end[:, h:h + 1], sc)
        state_scr[g] = st * sc + upd
        gs = slice(g * gw, (g + 1) * gw)
        y = y + ys * e_all[:, gs] + xg * dsk_ref[:, gs]
        zg = z_ref[0, :, gs].astype(F32)
        y = y * (zg * _sigmoid(zg))
        y = y * lax.rsqrt(jnp.mean(y * y, axis=-1, keepdims=True) + RMS_EPS) * ng_ref[:, gs]
        o_ref[0, :, gs] = y.astype(o_ref.dtype)


def ssd(xbc, z, ha3, dt_bias, a_neg, d_skip, norm_g):
    bsz, s, c = xbc.shape
    d_inner = SSM_HEADS * SSM_HEAD_DIM
    chunk = min(SSD_CHUNK, s)
    gw = d_inner // SSM_GROUPS
    vec = lambda n: pl.BlockSpec((1, n), lambda bi, ci: (0, 0))
    expand = (jnp.arange(d_inner, dtype=I32)[None, :] // SSM_HEAD_DIM
              == jnp.arange(LANES, dtype=I32)[:, None]).astype(BF16)
    return pl.pallas_call(
        functools.partial(_ssd_kernel, chunk=chunk),
        grid=(bsz, s // chunk),
        in_specs=[pl.BlockSpec((1, chunk, c), lambda bi, ci: (bi, ci, 0)),
                  pl.BlockSpec((1, chunk, d_inner), lambda bi, ci: (bi, ci, 0)),
                  pl.BlockSpec((1, chunk, LANES), lambda bi, ci: (bi, ci, HA_DT // LANES)),
                  vec(LANES), vec(LANES), vec(d_inner), vec(d_inner),
                  pl.BlockSpec((LANES, d_inner), lambda bi, ci: (0, 0))],
        out_specs=pl.BlockSpec((1, chunk, d_inner), lambda bi, ci: (bi, ci, 0)),
        out_shape=jax.ShapeDtypeStruct((bsz, s, d_inner), BF16),
        scratch_shapes=[pltpu.VMEM((SSM_GROUPS, D_STATE, gw), F32)],
        compiler_params=_cp("parallel", "arbitrary"),
        name="ssd_scan",
    )(xbc, z, ha3, dt_bias, a_neg, d_skip, norm_g, expand)


def _xattn_kernel(q_ref, kv_ref, o_ref, *, scale):
    hd = XA_HEADS * XA_HEAD_DIM
    nt = (((1,), (1,)), ((), ()))
    for h in range(XA_HEADS):
        sl = slice(h * XA_HEAD_DIM, (h + 1) * XA_HEAD_DIM)
        q = q_ref[0, :, sl]
        k = kv_ref[0, :, sl]
        v = kv_ref[0, :, hd + h * XA_HEAD_DIM:hd + (h + 1) * XA_HEAD_DIM]
        s = lax.dot_general(q, k, nt, preferred_element_type=F32) * scale
        p = jnp.exp(s - jnp.max(s, axis=-1, keepdims=True))
        l = jnp.sum(p, axis=-1, keepdims=True)
        o = jnp.dot(p.astype(BF16), v, preferred_element_type=F32) / l
        o_ref[0, :, sl] = o.astype(o_ref.dtype)


def mem_attention(q, kv):
    bsz, s, hd = q.shape
    m = kv.shape[1]
    tq = min(XA_TQ, s)
    return pl.pallas_call(
        functools.partial(_xattn_kernel, scale=float(XA_HEAD_DIM ** -0.5)),
        grid=(bsz, s // tq),
        in_specs=[pl.BlockSpec((1, tq, hd), lambda bi, i: (bi, i, 0)),
                  pl.BlockSpec((1, m, 2 * hd), lambda bi, i: (bi, 0, 0))],
        out_specs=pl.BlockSpec((1, tq, hd), lambda bi, i: (bi, i, 0)),
        out_shape=jax.ShapeDtypeStruct((bsz, s, hd), BF16),
        compiler_params=_cp("parallel", "parallel"),
        name="mem_attention",
    )(q, kv)


def _layer_norm(v, g, b):
    mu = jnp.mean(v, axis=-1, keepdims=True)
    d = v - mu
    var = jnp.mean(d * d, axis=-1, keepdims=True)
    return d * lax.rsqrt(var + NORM_EPS) * g + b


def _merge_kernel(ol_ref, yn_ref, cm_ref, g_ref, x_ref, wuv_ref, wa_ref, wb_ref, wc_ref, wo_ref,
                  lg_ref, lb_ref, xo_ref, a_scr, *, alpha):
    tm, d = x_ref.shape
    for h in range(MLA_HEADS):
        a_scr[:, h * V_DIM:(h + 1) * V_DIM] = jnp.dot(
            ol_ref[:, h].reshape(tm, KV_RANK), wuv_ref[h],
            preferred_element_type=F32).astype(BF16)
    o_a = jnp.dot(a_scr[...], wa_ref[...], preferred_element_type=F32)
    o_b = jnp.dot(yn_ref[...], wb_ref[...], preferred_element_type=F32)
    o_c = jnp.dot(cm_ref[...], wc_ref[...], preferred_element_type=F32)
    g = g_ref[...].astype(F32)
    merged = (_sigmoid(g[:, :d]) * o_a + _sigmoid(g[:, d:2 * d]) * o_b + _sigmoid(g[:, 2 * d:]) * o_c)
    y = alpha * x_ref[...] + jnp.dot(merged.astype(BF16), wo_ref[...], preferred_element_type=F32)
    xo_ref[...] = _layer_norm(y, lg_ref[...], lb_ref[...])


def merge_ln(ol, yn, cm, g, x, wuv, wa, wb, wc, wo, ln_g, ln_b, alpha):
    t, d = x.shape
    tm = min(MERGE_TM, t)
    tq = ol.shape[2]
    row = lambda n: pl.BlockSpec((tm, n), lambda i: (i, 0))
    full = lambda a: pl.BlockSpec(a.shape, lambda i: (0,) * a.ndim, pipeline_mode=pl.Buffered(1))
    return pl.pallas_call(
        functools.partial(_merge_kernel, alpha=alpha),
        grid=(t // tm,),
        in_specs=[pl.BlockSpec((tm // tq, MLA_HEADS, tq, KV_RANK), lambda i: (i, 0, 0, 0)),
                  row(yn.shape[1]), row(cm.shape[1]), row(g.shape[1]), row(d),
                  full(wuv), full(wa), full(wb), full(wc), full(wo), full(ln_g), full(ln_b)],
        out_specs=row(d),
        out_shape=jax.ShapeDtypeStruct((t, d), F32),
        scratch_shapes=[pltpu.VMEM((tm, MLA_HEADS * V_DIM), BF16)],
        compiler_params=_cp("parallel"),
        name="merge_ln",
    )(ol, yn, cm, g, x, wuv, wa, wb, wc, wo, ln_g, ln_b)


def _first_max(v, expert):
    m = jnp.max(v, axis=0, keepdims=True)
    idx = jnp.min(jnp.where(v == m, expert, N_EXPERTS), axis=0, keepdims=True)
    return m, idx


def _route_kernel(x_ref, rwh_ref, rwl_ref, rbt_ref, info_ref, infot_ref, tinfo_ref, cnt_ref, carry_scr):
    tm = x_ref.shape[0]

    @pl.when(pl.program_id(0) == 0)
    def _():
        carry_scr[...] = jnp.zeros(carry_scr.shape, F32)

    x = x_ref[...]
    x_hi = x.astype(BF16)
    x_lo = (x - x_hi.astype(F32)).astype(BF16)
    logits = (jnp.dot(x_hi, rwh_ref[...], preferred_element_type=F32)
              + jnp.dot(x_lo, rwh_ref[...], preferred_element_type=F32)
              + jnp.dot(x_hi, rwl_ref[...], preferred_element_type=F32))
    scores = _sigmoid(logits.T[:N_EXPERTS, :])
    sel = scores + jnp.tile(rbt_ref[...], (1, tm // LANES))
    expert = lax.broadcasted_iota(I32, (N_EXPERTS, tm), 0)
    neg = -jnp.inf
    best_score = None
    best_group = None
    for j in range(N_EXPERT_GROUPS):
        in_j = (expert >= j * EXPERTS_PER_GROUP) & (expert < (j + 1) * EXPERTS_PER_GROUP)
        v = jnp.where(in_j, sel, neg)
        m1, i1 = _first_max(v, expert)
        m2, _ = _first_max(jnp.where(expert == i1, neg, v), expert)
        gs = m1 + m2
        if j == 0:
            best_score, best_group = gs, jnp.zeros_like(i1)
        else:
            better = gs > best_score
            best_score = jnp.where(better, gs, best_score)
            best_group = jnp.where(better, j, best_group)
    lo = best_group * EXPERTS_PER_GROUP
    v = jnp.where((expert >= lo) & (expert < lo + EXPERTS_PER_GROUP), sel, neg)
    _, e1 = _first_max(v, expert)
    _, e2 = _first_max(jnp.where(expert == e1, neg, v), expert)
    w1 = jnp.sum(jnp.where(expert == e1, scores, 0.0), axis=0, keepdims=True)
    w2 = jnp.sum(jnp.where(expert == e2, scores, 0.0), axis=0, keepdims=True)
    wsum = w1 + w2
    member = jnp.where(expert == e1, 1.0, jnp.where(expert == e2, 1.0, 0.0))
    row = lax.broadcasted_iota(I32, (tm, tm), 0)
    col = lax.broadcasted_iota(I32, (tm, tm), 1)
    earlier = jnp.where(row < col, 1.0, 0.0).astype(BF16)
    lrank = jnp.dot(member.astype(BF16), earlier, preferred_element_type=F32)
    n_col = jnp.sum(member, axis=1, keepdims=True)
    diag = (lax.broadcasted_iota(I32, (N_EXPERTS, LANES), 0)
            == lax.broadcasted_iota(I32, (N_EXPERTS, LANES), 1))
    n = jnp.sum(jnp.where(diag, n_col, 0.0), axis=0, keepdims=True)
    n8 = jnp.floor((n + (ROW_ALIGN - 1)) * (1.0 / ROW_ALIGN)) * ROW_ALIGN
    ua = lax.broadcasted_iota(I32, (LANES, LANES), 0)
    ub = lax.broadcasted_iota(I32, (LANES, LANES), 1)
    lower_experts = jnp.where(ua < ub, 1.0, 0.0).astype(BF16)
    n8_rows = jnp.broadcast_to(n8, (8, LANES))
    block_start = jnp.dot(n8_rows.astype(BF16), lower_experts, preferred_element_type=F32)
    start_col = jnp.sum(jnp.where(diag, block_start[0:1, :], 0.0), axis=1, keepdims=True)
    local_row = start_col + lrank
    j1 = jnp.sum(jnp.where(expert == e1, local_row, 0.0), axis=0, keepdims=True)
    j2 = jnp.sum(jnp.where(expert == e2, local_row, 0.0), axis=0, keepdims=True)
    carry = carry_scr[...]
    sub = lax.broadcasted_iota(I32, (8, LANES), 0)
    tinfo_ref[0] = jnp.where(sub == 0, n8_rows, jnp.where(sub == 1, block_start, jnp.where(sub == 2, carry, 0.0)))
    new_carry = carry + n8_rows
    carry_scr[...] = new_carry
    cnt_ref[...] = new_carry
    field = lax.broadcasted_iota(I32, (8, tm), 0)
    info_t = jnp.where(field == 0, e1.astype(F32),
             jnp.where(field == 1, e2.astype(F32),
             jnp.where(field == 2, w1 / wsum,
             jnp.where(field == 3, w2 / wsum,
             jnp.where(field == 4, j1,
             jnp.where(field == 5, j2, 0.0))))))
    infot_ref[0] = info_t
    info_ref[...] = jnp.concatenate([info_t, jnp.zeros((LANES - 8, tm), F32)], axis=0).T


def route(x, rw_hi, rw_lo, rb):
    t, d = x.shape
    tm = min(DISPATCH_TM, t)
    return pl.pallas_call(
        _route_kernel,
        grid=(t // tm,),
        in_specs=[pl.BlockSpec((tm, d), lambda i: (i, 0)),
                  pl.BlockSpec((d, LANES), lambda i: (0, 0)),
                  pl.BlockSpec((d, LANES), lambda i: (0, 0)),
                  pl.BlockSpec((N_EXPERTS, LANES), lambda i: (0, 0))],
        out_specs=[pl.BlockSpec((tm, LANES), lambda i: (i, 0)),
                   pl.BlockSpec((1, 8, tm), lambda i: (i, 0, 0)),
                   pl.BlockSpec((1, 8, LANES), lambda i: (i, 0, 0)),
                   pl.BlockSpec((8, LANES), lambda i: (0, 0))],
        out_shape=[jax.ShapeDtypeStruct((t, LANES), F32),
                   jax.ShapeDtypeStruct((t // tm, 8, tm), F32),
                   jax.ShapeDtypeStruct((t // tm, 8, LANES), F32),
                   jax.ShapeDtypeStruct((8, LANES), F32)],
        scratch_shapes=[pltpu.VMEM((8, LANES), F32)],
        compiler_params=_cp("arbitrary"),
        name="route",
    )(x, rw_hi, rw_lo, rb)


def _block_copies(plan_ref, local_ref, sorted_ref, sem, to_sorted, start):
    for e in range(N_EXPERTS):
        count = plan_ref[0, 0, e]
        loc0 = plan_ref[0, 0, N_EXPERTS + e]
        dst0 = plan_ref[0, 0, 2 * N_EXPERTS + e]

        def body(k, carry, loc0=loc0, dst0=dst0):
            loc = local_ref.at[pl.ds(pl.multiple_of(loc0 + k * ROW_ALIGN, ROW_ALIGN), ROW_ALIGN)]
            srt = sorted_ref.at[pl.ds(pl.multiple_of(dst0 + k * ROW_ALIGN, ROW_ALIGN), ROW_ALIGN)]
            cp = pltpu.make_async_copy(loc, srt, sem) if to_sorted else pltpu.make_async_copy(srt, loc, sem)
            if start:
                cp.start()
            else:
                cp.wait()
            return carry

        lax.fori_loop(0, count, body, 0)


def _dispatch_kernel(plan_ref, prev_plan_ref, infot_ref, x_ref, init_ref, o_ref, local_scr, sem):
    del init_ref
    tm = x_ref.shape[0]
    i = pl.program_id(0)
    slot = i % 2
    j1 = infot_ref[0, 4:5, :]
    j2 = infot_ref[0, 5:6, :]
    rowid = lax.broadcasted_iota(I32, (LOCAL_ROWS, tm), 0).astype(F32)
    sel = jnp.where(rowid == j1, 1.0, jnp.where(rowid == j2, 1.0, 0.0)).astype(BF16)
    local_scr[slot] = jnp.dot(sel, x_ref[...].astype(BF16), preferred_element_type=F32).astype(BF16)

    @pl.when(i > 0)
    def _():
        _block_copies(prev_plan_ref, local_scr.at[1 - slot], o_ref, sem.at[1 - slot], True, False)

    _block_copies(plan_ref, local_scr.at[slot], o_ref, sem.at[slot], True, True)

    @pl.when(i == pl.num_programs(0) - 1)
    def _():
        _block_copies(plan_ref, local_scr.at[slot], o_ref, sem.at[slot], True, False)


def dispatch_rows(x, info_t, plan, n_rows):
    t, d = x.shape
    tm = min(DISPATCH_TM, t)
    init = jnp.zeros((n_rows, d), BF16)
    return pl.pallas_call(
        _dispatch_kernel,
        grid=(t // tm,),
        in_specs=[pl.BlockSpec((1, 1, 3 * N_EXPERTS), lambda i: (i, 0, 0), memory_space=pltpu.SMEM),
                  pl.BlockSpec((1, 1, 3 * N_EXPERTS), lambda i: (jnp.maximum(i - 1, 0), 0, 0),
                               memory_space=pltpu.SMEM),
                  pl.BlockSpec((1, 8, tm), lambda i: (i, 0, 0)),
                  pl.BlockSpec((tm, d), lambda i: (i, 0)),
                  pl.BlockSpec(memory_space=pl.ANY)],
        out_specs=pl.BlockSpec(memory_space=pl.ANY),
        out_shape=jax.ShapeDtypeStruct((n_rows, d), BF16),
        scratch_shapes=[pltpu.VMEM((2, LOCAL_ROWS, d), BF16), pltpu.SemaphoreType.DMA((2,))],
        input_output_aliases={4: 0},
        compiler_params=_cp("arbitrary"),
        name="moe_dispatch",
    )(plan, plan, info_t, x, init)


def _expert_kernel(te_ref, nu_ref, xs_ref, w13_ref, w2_ref, o_ref):
    del te_ref
    i = pl.program_id(0)
    f = w2_ref.shape[1]

    @pl.when(i < nu_ref[0])
    def _():
        h = jnp.dot(xs_ref[...], w13_ref[0], preferred_element_type=F32)
        h1 = h[:, :f]
        act = (h1 * _sigmoid(h1) * h[:, f:]).astype(BF16)
        o_ref[...] = jnp.dot(act, w2_ref[0], preferred_element_type=F32).astype(o_ref.dtype)

    @pl.when(i >= nu_ref[0])
    def _():
        o_ref[...] = jnp.zeros(o_ref.shape, o_ref.dtype)


def expert_ffn(xs, w13, w2, tile_expert, n_used):
    p, d = xs.shape
    f = w2.shape[1]
    n_tiles = p // MOE_TM
    grid_spec = pltpu.PrefetchScalarGridSpec(
        num_scalar_prefetch=2,
        grid=(n_tiles,),
        in_specs=[pl.BlockSpec((MOE_TM, d), lambda i, te, nu: (i, 0)),
                  pl.BlockSpec((1, d, 2 * f), lambda i, te, nu: (te[i], 0, 0)),
                  pl.BlockSpec((1, f, d), lambda i, te, nu: (te[i], 0, 0))],
        out_specs=pl.BlockSpec((MOE_TM, d), lambda i, te, nu: (i, 0)),
    )
    return pl.pallas_call(
        _expert_kernel,
        grid_spec=grid_spec,
        out_shape=jax.ShapeDtypeStruct((p, d), BF16),
        compiler_params=_cp("arbitrary"),
        name="expert_ffn",
    )(tile_expert, n_used, xs, w13, w2)


def _combine_kernel(plan_ref, next_plan_ref, info_ref, x_ref, ys_ref, lg_ref, lb_ref, xo_ref, xb_ref,
                    local_scr, sem, *, alpha):
    tm = x_ref.shape[0]
    i = pl.program_id(0)
    slot = i % 2

    @pl.when(i == 0)
    def _():
        _block_copies(plan_ref, local_scr.at[slot], ys_ref, sem.at[slot], False, True)

    @pl.when(i + 1 < pl.num_programs(0))
    def _():
        _block_copies(next_plan_ref, local_scr.at[1 - slot], ys_ref, sem.at[1 - slot], False, True)

    _block_copies(plan_ref, local_scr.at[slot], ys_ref, sem.at[slot], False, False)
    last = N_EXPERTS - 1
    used = plan_ref[0, 0, N_EXPERTS + last] + plan_ref[0, 0, last] * ROW_ALIGN
    rowid = lax.broadcasted_iota(I32, (LOCAL_ROWS, 1), 0)
    y = local_scr[slot]
    y = jnp.where(rowid < used, y, jnp.zeros_like(y))
    info = info_ref[...]
    col = lax.broadcasted_iota(I32, (tm, LOCAL_ROWS), 1).astype(F32)
    gate = jnp.where(col == info[:, 4:5], info[:, 2:3],
                     jnp.where(col == info[:, 5:6], info[:, 3:4], 0.0)).astype(BF16)
    moe = jnp.dot(gate, y, preferred_element_type=F32)
    out = _layer_norm(alpha * x_ref[...] + moe, lg_ref[...], lb_ref[...])
    xo_ref[...] = out
    xb_ref[...] = out.astype(BF16)


def combine_ln(plan, info, x, ys, ln_g, ln_b, alpha):
    t, d = x.shape
    tm = min(DISPATCH_TM, t)
    return pl.pallas_call(
        functools.partial(_combine_kernel, alpha=alpha),
        grid=(t // tm,),
        in_specs=[pl.BlockSpec((1, 1, 3 * N_EXPERTS), lambda i: (i, 0, 0), memory_space=pltpu.SMEM),
                  pl.BlockSpec((1, 1, 3 * N_EXPERTS), lambda i: (jnp.minimum(i + 1, t // tm - 1), 0, 0),
                               memory_space=pltpu.SMEM),
                  pl.BlockSpec((tm, LANES), lambda i: (i, 0)),
                  pl.BlockSpec((tm, d), lambda i: (i, 0)),
                  pl.BlockSpec(memory_space=pl.ANY),
                  pl.BlockSpec((1, d), lambda i: (0, 0)),
                  pl.BlockSpec((1, d), lambda i: (0, 0))],
        out_specs=[pl.BlockSpec((tm, d), lambda i: (i, 0)), pl.BlockSpec((tm, d), lambda i: (i, 0))],
        out_shape=[jax.ShapeDtypeStruct((t, d), F32), jax.ShapeDtypeStruct((t, d), BF16)],
        scratch_shapes=[pltpu.VMEM((2, LOCAL_ROWS, d), BF16), pltpu.SemaphoreType.DMA((2,))],
        compiler_params=_cp("arbitrary"),
        name="moe_combine_ln",
    )(plan, plan, info, x, ys, ln_g, ln_b)


def _prep_layer(l, p):
    w_in = p["w_in"][l]
    d = w_in.shape[0]
    o_dq = 0
    o_dkv = o_dq + Q_RANK
    o_z = o_dkv + KV_RANK + ROPE_DIM
    d_inner = SSM_HEADS * SSM_HEAD_DIM
    conv_ch = d_inner + 2 * SSM_GROUPS * D_STATE
    o_xbc = o_z + d_inner
    o_dt = o_xbc + conv_ch
    o_qm = o_dt + SSM_HEADS
    o_g = o_qm + XA_HEADS * XA_HEAD_DIM
    half = ROPE_DIM // 2
    kr0 = o_dkv + KV_RANK
    zeros = lambda n: jnp.zeros((d, n), F32)
    w_small = jnp.concatenate([
        w_in[:, o_dq:o_dq + Q_RANK],
        w_in[:, o_dkv:o_dkv + KV_RANK],
        w_in[:, kr0:kr0 + ROPE_DIM], zeros(LANES - ROPE_DIM),
        w_in[:, kr0 + half:kr0 + ROPE_DIM], w_in[:, kr0:kr0 + half], zeros(LANES - ROPE_DIM),
        w_in[:, o_dt:o_dt + SSM_HEADS], zeros(LANES - SSM_HEADS)], axis=1)
    w_uq = p["w_uq"][l].reshape(Q_RANK, MLA_HEADS, NOPE_DIM + ROPE_DIM)
    wq_nope = w_uq[:, :, :NOPE_DIM].reshape(Q_RANK, -1)
    wq_rope = w_uq[:, :, NOPE_DIM:]
    wq_rope_sw = jnp.concatenate([wq_rope[:, :, half:], wq_rope[:, :, :half]], axis=-1)
    wq = jnp.concatenate([wq_nope, wq_rope.reshape(Q_RANK, -1), wq_rope_sw.reshape(Q_RANK, -1)], axis=1)
    w_ukv = p["w_ukv"][l]
    wuk_t = jnp.transpose(w_ukv[:, :, :NOPE_DIM], (1, 2, 0))
    wuv = jnp.transpose(w_ukv[:, :, NOPE_DIM:], (1, 0, 2))
    pad_heads = lambda v, fill: jnp.concatenate(
        [v.astype(F32), jnp.full((LANES - SSM_HEADS,), fill, F32)]).reshape(1, LANES)
    bf = lambda a: a.astype(BF16)
    return dict(
        w_small=bf(w_small), w_z=bf(w_in[:, o_z:o_z + d_inner]), w_xbc=bf(w_in[:, o_xbc:o_xbc + conv_ch]),
        w_qm=bf(w_in[:, o_qm:o_g]), w_g=bf(w_in[:, o_g:]),
        q_norm=p["q_norm"][l].reshape(1, -1), kv_norm=p["kv_norm"][l].reshape(1, -1),
        wq=bf(wq), wuk_t=bf(wuk_t), wuv=bf(wuv),
        conv_w=p["conv_w"][l], conv_b=p["conv_b"][l].reshape(1, -1),
        dt_bias=pad_heads(p["dt_bias"][l], 0.0),
        a_neg=pad_heads(-jnp.exp(p["a_log"][l].astype(F32)), 0.0),
        d_skip=jnp.repeat(p["d_skip"][l].astype(F32), SSM_HEAD_DIM).reshape(1, -1),
        ssm_norm=p["ssm_norm"][l].reshape(1, -1),
        w_mem_kv=bf(p["w_mem_kv"][l]),
        wa=bf(p["w_proj_a"][l]), wb=bf(p["w_proj_b"][l]), wc=bf(p["w_proj_c"][l]), wo=bf(p["w_out"][l]),
        ln1_g=p["ln1_g"][l].reshape(1, -1), ln1_b=p["ln1_b"][l].reshape(1, -1),
        w13=bf(jnp.concatenate([p["exp_w1"][l], p["exp_w3"][l]], axis=-1)), w2=bf(p["exp_w2"][l]),
        ln2_g=p["ln2_g"][l].reshape(1, -1), ln2_b=p["ln2_b"][l].reshape(1, -1),
    )


def _dispatch_plan(tinfo, counts, t):
    n_tok_tiles = tinfo.shape[0]
    n8 = tinfo[:, 0, :N_EXPERTS].astype(I32)
    block_start = tinfo[:, 1, :N_EXPERTS].astype(I32)
    carry = tinfo[:, 2, :N_EXPERTS].astype(I32)
    rows = counts[0, :N_EXPERTS].astype(I32)
    tiles = (rows + MOE_TM - 1) // MOE_TM
    tile_end = jnp.cumsum(tiles)
    row_start = (tile_end - tiles) * MOE_TM
    plan = jnp.concatenate([n8 // ROW_ALIGN, block_start, row_start[None, :] + carry], axis=1)
    max_rows = TOP_K * t + N_EXPERTS * (ROW_ALIGN - 1) * n_tok_tiles
    n_tiles = -(-max_rows // MOE_TM) + N_EXPERTS
    tile_ids = jnp.arange(n_tiles, dtype=I32)
    tile_expert = jnp.minimum(jnp.sum(tile_ids[:, None] >= tile_end[None, :], axis=1), N_EXPERTS - 1)
    n_used = tile_end[-1:].astype(I32)
    return plan.reshape(n_tok_tiles, 1, 3 * N_EXPERTS), tile_expert.astype(I32), n_used, n_tiles * MOE_TM


def kernel(x, mem, positions, w_in, q_norm, w_uq, kv_norm, w_ukv, w_proj_a, conv_w, conv_b, dt_bias, a_log,
           d_skip, ssm_norm, w_proj_b, w_mem_kv, w_proj_c, w_out, ln1_g, ln1_b, router_w, router_bias,
           exp_w1, exp_w3, exp_w2, ln2_g, ln2_b):
    params = dict(w_in=w_in, q_norm=q_norm, w_uq=w_uq, kv_norm=kv_norm, w_ukv=w_ukv, w_proj_a=w_proj_a,
                  conv_w=conv_w, conv_b=conv_b, dt_bias=dt_bias, a_log=a_log, d_skip=d_skip,
                  ssm_norm=ssm_norm, w_proj_b=w_proj_b, w_mem_kv=w_mem_kv, w_proj_c=w_proj_c, w_out=w_out,
                  ln1_g=ln1_g, ln1_b=ln1_b, exp_w1=exp_w1, exp_w3=exp_w3, exp_w2=exp_w2,
                  ln2_g=ln2_g, ln2_b=ln2_b)
    bsz, s, d = x.shape
    t = bsz * s
    depth = w_in.shape[0]
    alpha = float((2 * depth) ** 0.25)
    n_mem = mem.shape[1]

    inv = ROPE_THETA ** (-jnp.arange(0, ROPE_DIM, 2, dtype=F32) / ROPE_DIM)
    ang = positions.astype(F32)[..., None] * inv
    cos, sin = jnp.cos(ang), jnp.sin(ang)
    cosq = jnp.tile(jnp.concatenate([cos, cos], axis=-1), (1, 1, MLA_HEADS)).reshape(t, -1)
    sinq = jnp.tile(jnp.concatenate([-sin, sin], axis=-1), (1, 1, MLA_HEADS)).reshape(t, -1)

    rw = jnp.concatenate([router_w.astype(F32), jnp.zeros((d, LANES - N_EXPERTS), F32)], axis=1)
    rw_hi = rw.astype(BF16)
    rw_lo = (rw - rw_hi.astype(F32)).astype(BF16)
    rbt = jnp.broadcast_to(router_bias.astype(F32)[:, None], (N_EXPERTS, LANES))
    mem_b = mem.reshape(bsz * n_mem, d).astype(BF16)

    xf = x.reshape(t, d).astype(F32)
    xb = xf.astype(BF16)
    for l in range(depth):
        w = _prep_layer(l, params)
        ha = matmul(xb, w["w_small"], F32, "proj_small")
        z = matmul(xb, w["w_z"], BF16, "proj_z")
        xbc_c = proj_conv_silu(xb, w["w_xbc"], w["conv_w"], w["conv_b"], s).reshape(bsz, s, -1)
        qm = matmul(xb, w["w_qm"], BF16, "proj_qmem")
        g = matmul(xb, w["w_g"], BF16, "proj_gate")

        ql, qr, ck, kr = mla_prep(ha, cosq, sinq, w["q_norm"], w["kv_norm"], w["wq"], w["wuk_t"],
                                  min(ATT_TQ, s))
        o_lat = mla_attention(ql, qr, ck.reshape(bsz, s, KV_RANK), kr.reshape(bsz, s, ROPE_DIM))

        yn = ssd(xbc_c, z.reshape(bsz, s, -1), ha.reshape(bsz, s, HA_W), w["dt_bias"], w["a_neg"],
                 w["d_skip"], w["ssm_norm"]).reshape(t, -1)

        kv = matmul(mem_b, w["w_mem_kv"], BF16, "proj_memkv").reshape(bsz, n_mem, -1)
        cm = mem_attention(qm.reshape(bsz, s, -1), kv).reshape(t, -1)

        x1 = merge_ln(o_lat, yn, cm, g, xf, w["wuv"], w["wa"], w["wb"], w["wc"], w["wo"],
                           w["ln1_g"], w["ln1_b"], alpha)

        info, info_t, tinfo, counts = route(x1, rw_hi, rw_lo, rbt)
        plan, tile_expert, n_used, n_rows = _dispatch_plan(tinfo, counts, t)
        xs = dispatch_rows(x1, info_t, plan, n_rows)
        ys = expert_ffn(xs, w["w13"], w["w2"], tile_expert, n_used)
        xf, xb = combine_ln(plan, info, x1, ys, w["ln2_g"], w["ln2_b"], alpha)
    return xf.reshape(bsz, s, d)
```

```python
import functools

import jax
import jax.numpy as jnp
from jax import lax
from jax.experimental import pallas as pl
from jax.experimental.pallas import tpu as pltpu

F32 = jnp.float32
BF16 = jnp.bfloat16
I32 = jnp.int32

MLA_HEADS = 8
Q_RANK = 384
KV_RANK = 256
NOPE_DIM = 128
ROPE_DIM = 64
V_DIM = 128
ROPE_THETA = 10000.0
SSM_HEADS = 32
SSM_HEAD_DIM = 64
SSM_GROUPS = 8
D_STATE = 128
CONV_K = 4
XA_HEADS = 4
XA_HEAD_DIM = 256
N_EXPERTS = 16
N_EXPERT_GROUPS = 4
EXPERTS_PER_GROUP = 4
TOP_K = 2
NORM_EPS = 1e-5
RMS_EPS = 1e-6

LANES = 128
V7X_VMEM_LIMIT = 56 * 1024 * 1024

MM_TM = 1024
MM_TN = 1024
PREP_TM = 512
ATT_TQ = 128
ATT_TK = 512
ATT_GROUP_ROWS = 512
ATT_UNROLL = 2
LOG2E = 1.4426950408889634
PROJ_CONV_TM = 1024
PROJ_CONV_TN = 1024
PROJ_CONV_SUB = 256
CONV_HALO = 16
SSD_CHUNK = 256
XA_TQ = 512
MERGE_TM = 512
MOE_TM = 512
DISPATCH_TM = 512
ROW_ALIGN = 16
LOCAL_ROWS = 1280

HA_W = 1024
HA_DQ = 0
HA_C = 384
HA_KR = 640
HA_KRS = 768
HA_DT = 896


def _cp(*sem):
    return pltpu.CompilerParams(dimension_semantics=sem, vmem_limit_bytes=V7X_VMEM_LIMIT)


def _sigmoid(x):
    return 1.0 / (1.0 + jnp.exp(-x))


def _mm_kernel(x_ref, w_ref, o_ref):
    o_ref[...] = jnp.dot(x_ref[...], w_ref[...], preferred_element_type=F32).astype(o_ref.dtype)


def matmul(x, w, out_dtype, name):
    m, k = x.shape
    n = w.shape[1]
    tm = min(MM_TM, m)
    tn = min(MM_TN, n)
    return pl.pallas_call(
        _mm_kernel,
        grid=(n // tn, m // tm),
        in_specs=[pl.BlockSpec((tm, k), lambda j, i: (i, 0)),
                  pl.BlockSpec((k, tn), lambda j, i: (0, j))],
        out_specs=pl.BlockSpec((tm, tn), lambda j, i: (i, j)),
        out_shape=jax.ShapeDtypeStruct((m, n), out_dtype),
        compiler_params=_cp("parallel", "parallel"),
        name=name,
    )(x, w)


def _proj_conv_kernel(x_ref, xh_ref, w_ref, cw_ref, cb_ref, o_ref, *, tiles_per_seq):
    first = (pl.program_id(1) % tiles_per_seq) == 0
    tn = o_ref.shape[1]
    sub = min(PROJ_CONV_SUB, tn)
    def project(c):
        w = w_ref[:, c * sub:(c + 1) * sub]
        u = jnp.dot(x_ref[...], w, preferred_element_type=F32)
        halo = jnp.dot(xh_ref[...], w, preferred_element_type=F32)
        return jnp.concatenate([jnp.where(first, jnp.zeros_like(halo), halo), u], axis=0)

    n_sub = tn // sub
    nxt = project(0)
    for c in range(n_sub):
        cs = slice(c * sub, (c + 1) * sub)
        cw = cw_ref[:, cs]
        ext = nxt
        if c + 1 < n_sub:
            nxt = project(c + 1)
        prev = pltpu.roll(ext, 1, 0)
        near = cw[3:4, :] * ext + cw[2:3, :] * prev
        far = cw[1:2, :] * ext + cw[0:1, :] * prev
        acc = (near + pltpu.roll(far, 2, 0))[CONV_HALO:, :] + cb_ref[:, cs]
        half = 0.5 * acc
        o_ref[:, cs] = (half * jnp.tanh(half) + half).astype(o_ref.dtype)


def proj_conv_silu(x, w, conv_w, conv_b, seq_len):
    m, k = x.shape
    n = w.shape[1]
    tm = min(PROJ_CONV_TM, seq_len)
    tn = min(PROJ_CONV_TN, n)
    hb = tm // CONV_HALO
    return pl.pallas_call(
        functools.partial(_proj_conv_kernel, tiles_per_seq=seq_len // tm),
        grid=(n // tn, m // tm),
        in_specs=[pl.BlockSpec((tm, k), lambda j, i: (i, 0)),
                  pl.BlockSpec((CONV_HALO, k), lambda j, i: (jnp.maximum(i * hb - 1, 0), 0)),
                  pl.BlockSpec((k, tn), lambda j, i: (0, j)),
                  pl.BlockSpec((CONV_K, tn), lambda j, i: (0, j)),
                  pl.BlockSpec((1, tn), lambda j, i: (0, j))],
        out_specs=pl.BlockSpec((tm, tn), lambda j, i: (i, j)),
        out_shape=jax.ShapeDtypeStruct((m, n), BF16),
        compiler_params=_cp("parallel", "parallel"),
        name="proj_xbc_conv",
    )(x, x, w, conv_w, conv_b)


def _mla_prep_kernel(ha_ref, cos_ref, sin_ref, qn_ref, kvn_ref, wq_ref, wuk_ref,
                     ql_ref, qr_ref, ck_ref, kr_ref, *, scale):
    nb, _, tq, _ = ql_ref.shape
    ha = ha_ref[...]
    dq = ha[:, HA_DQ:HA_DQ + Q_RANK]
    c_q = dq * lax.rsqrt(jnp.mean(dq * dq, axis=-1, keepdims=True) + RMS_EPS) * qn_ref[...]
    q = jnp.dot(c_q.astype(BF16), wq_ref[...], preferred_element_type=F32)
    n_nope = MLA_HEADS * NOPE_DIM
    n_rope = MLA_HEADS * ROPE_DIM
    cosq = cos_ref[...]
    sinq = sin_ref[...]
    q_rope = ((q[:, n_nope:n_nope + n_rope] * cosq + q[:, n_nope + n_rope:] * sinq) * scale).astype(BF16)
    for h in range(MLA_HEADS):
        qh = q[:, h * NOPE_DIM:(h + 1) * NOPE_DIM].astype(BF16)
        ql = jnp.dot(qh, wuk_ref[h], preferred_element_type=F32)
        ql_ref[:, h] = (ql * scale).astype(BF16).reshape(nb, tq, KV_RANK)
        qr_ref[:, h] = q_rope[:, h * ROPE_DIM:(h + 1) * ROPE_DIM].reshape(nb, tq, ROPE_DIM)
    c = ha[:, HA_C:HA_C + KV_RANK]
    c_kv = c * lax.rsqrt(jnp.mean(c * c, axis=-1, keepdims=True) + RMS_EPS) * kvn_ref[...]
    ck_ref[...] = c_kv.astype(BF16)
    k_rope = (ha[:, HA_KR:HA_KR + LANES] * cosq[:, :LANES]
              + ha[:, HA_KRS:HA_KRS + LANES] * sinq[:, :LANES])
    kr_ref[...] = k_rope[:, :ROPE_DIM].astype(BF16)


def mla_prep(ha, cosq, sinq, q_norm, kv_norm, wq, wuk_t, tq):
    t = ha.shape[0]
    tm = min(PREP_TM, t)
    nb = tm // tq
    scale = float((NOPE_DIM + ROPE_DIM) ** -0.5 * LOG2E)
    n_rope = MLA_HEADS * ROPE_DIM
    full = lambda shape: pl.BlockSpec(shape, lambda i: (0,) * len(shape))
    return pl.pallas_call(
        functools.partial(_mla_prep_kernel, scale=scale),
        grid=(t // tm,),
        in_specs=[pl.BlockSpec((tm, HA_W), lambda i: (i, 0)),
                  pl.BlockSpec((tm, n_rope), lambda i: (i, 0)),
                  pl.BlockSpec((tm, n_rope), lambda i: (i, 0)),
                  full((1, Q_RANK)), full((1, KV_RANK)),
                  full(wq.shape), full(wuk_t.shape)],
        out_specs=[pl.BlockSpec((nb, MLA_HEADS, tq, KV_RANK), lambda i: (i, 0, 0, 0)),
                   pl.BlockSpec((nb, MLA_HEADS, tq, ROPE_DIM), lambda i: (i, 0, 0, 0)),
                   pl.BlockSpec((tm, KV_RANK), lambda i: (i, 0)),
                   pl.BlockSpec((tm, ROPE_DIM), lambda i: (i, 0))],
        out_shape=[jax.ShapeDtypeStruct((t // tq, MLA_HEADS, tq, KV_RANK), BF16),
                   jax.ShapeDtypeStruct((t // tq, MLA_HEADS, tq, ROPE_DIM), BF16),
                   jax.ShapeDtypeStruct((t, KV_RANK), BF16),
                   jax.ShapeDtypeStruct((t, ROPE_DIM), BF16)],
        compiler_params=_cp("parallel"),
        name="mla_prep",
    )(ha, cosq, sinq, q_norm, kv_norm, wq, wuk_t)


def _mla_attn_kernel(ql_ref, qr_ref, ck_ref, kr_ref, o_ref, m_scr, l_scr, acc_scr, *, tq, tk, hpc):
    rc = hpc * tq
    n_groups = MLA_HEADS // hpc
    q_start = pl.program_id(1) * tq
    n_full = q_start // tk
    m_scr[...] = jnp.full(m_scr.shape, -jnp.inf, F32)
    l_scr[...] = jnp.zeros(l_scr.shape, F32)
    acc_scr[...] = jnp.zeros(acc_scr.shape, F32)
    nt = (((1,), (1,)), ((), ()))

    def step(j, masked, width=tk):
        ks = pl.multiple_of(j * tk, tk)
        ck = ck_ref[0, pl.ds(ks, width), :]
        kr = kr_ref[0, pl.ds(ks, width), :]
        if masked:
            q_pos = q_start + (lax.broadcasted_iota(I32, (rc, width), 0) & (tq - 1))
            k_pos = ks + lax.broadcasted_iota(I32, (rc, width), 1)
            visible = k_pos <= q_pos
        for c in range(n_groups):
            rs = slice(c * rc, (c + 1) * rc)
            ql = ql_ref[0, c * hpc:(c + 1) * hpc].reshape(rc, KV_RANK)
            qr = qr_ref[0, c * hpc:(c + 1) * hpc].reshape(rc, ROPE_DIM)
            s = (lax.dot_general(ql, ck, nt, preferred_element_type=F32)
                 + lax.dot_general(qr, kr, nt, preferred_element_type=F32))
            if masked:
                s = jnp.where(visible, s, -jnp.inf)
            m_prev = m_scr[rs]
            m_new = jnp.maximum(m_prev, jnp.max(s, axis=-1, keepdims=True))
            alpha = jnp.exp2(m_prev - m_new)
            p = jnp.exp2(s - jnp.tile(m_new, (1, width // LANES)))
            l_scr[rs] = alpha * l_scr[rs] + jnp.sum(p, axis=-1, keepdims=True)
            acc_scr[rs] = (jnp.tile(alpha, (1, KV_RANK // LANES)) * acc_scr[rs]
                           + jnp.dot(p.astype(BF16), ck, preferred_element_type=F32))
            m_scr[rs] = m_new

    def trip(jj, carry):
        for u in range(ATT_UNROLL):
            step(ATT_UNROLL * jj + u, False)
        return carry

    n_trips = n_full // ATT_UNROLL
    lax.fori_loop(0, n_trips, trip, 0)

    def single(j, carry):
        step(j, False)
        return carry

    lax.fori_loop(n_trips * ATT_UNROLL, n_full, single, 0)

    sub = (q_start - n_full * tk) // tq
    for v in range(tk // tq):
        @pl.when(sub == v)
        def _(v=v):
            step(n_full, True, (v + 1) * tq)
    out = acc_scr[...] / jnp.tile(l_scr[...], (1, KV_RANK // LANES))
    o_ref[0] = out.astype(o_ref.dtype).reshape(MLA_HEADS, tq, KV_RANK)


def mla_attention(ql, qr, ck, kr):
    _, _, tq, _ = ql.shape
    b, s, _ = ck.shape
    tk = min(ATT_TK, s)
    assert tk % tq == 0 and s % tk == 0 and tq & (tq - 1) == 0
    nq = s // tq
    rows = tq * MLA_HEADS
    hpc = max(1, min(MLA_HEADS, ATT_GROUP_ROWS // tq))
    qspec = lambda dim: pl.BlockSpec((1, MLA_HEADS, tq, dim), lambda bi, i: (bi * nq + i, 0, 0, 0))
    return pl.pallas_call(
        functools.partial(_mla_attn_kernel, tq=tq, tk=tk, hpc=hpc),
        grid=(b, nq),
        in_specs=[qspec(KV_RANK), qspec(ROPE_DIM),
                  pl.BlockSpec((1, s, KV_RANK), lambda bi, i: (bi, 0, 0)),
                  pl.BlockSpec((1, s, ROPE_DIM), lambda bi, i: (bi, 0, 0))],
        out_specs=qspec(KV_RANK),
        out_shape=jax.ShapeDtypeStruct(ql.shape, BF16),
        scratch_shapes=[pltpu.VMEM((rows, LANES), F32), pltpu.VMEM((rows, LANES), F32),
                        pltpu.VMEM((rows, KV_RANK), F32)],
        compiler_params=_cp("parallel", "parallel"),
        name="mla_attention",
    )(ql, qr, ck, kr)


def _ssd_kernel(xbc_ref, z_ref, dt_ref, dtb_ref, a_ref, dsk_ref, ng_ref, exp_ref, o_ref,
                state_scr, *, chunk):
    d_inner = SSM_HEADS * SSM_HEAD_DIM
    gn = SSM_GROUPS * D_STATE
    rep = SSM_HEADS // SSM_GROUPS
    gw = rep * SSM_HEAD_DIM

    @pl.when(pl.program_id(1) == 0)
    def _():
        state_scr[...] = jnp.zeros(state_scr.shape, F32)

    x_raw = dt_ref[0] + dtb_ref[...]
    dt = jnp.maximum(x_raw, 0.0) + jnp.log(1.0 + jnp.exp(-jnp.abs(x_raw)))
    da = dt * a_ref[...]
    row = lax.broadcasted_iota(I32, (chunk, chunk), 0)
    col = lax.broadcasted_iota(I32, (chunk, chunk), 1)
    causal = row >= col
    tri = jnp.where(causal, 1.0, 0.0).astype(BF16)
    acum = jnp.zeros(da.shape, F32)
    rem = da
    for _ in range(3):
        part = rem.astype(BF16)
        acum = acum + jnp.dot(tri, part, preferred_element_type=F32)
        rem = rem - part.astype(F32)
    a2 = acum * LOG2E
    a2_t = a2.T
    src_t = a2_t - jnp.log2(dt.T)
    w_t = jnp.exp2(a2_t[:, chunk - 1:chunk] - src_t)
    e_end = jnp.exp2(a2[chunk - 1:chunk, :])
    e_all = jnp.dot(jnp.exp2(a2).astype(BF16), exp_ref[...], preferred_element_type=F32)
    head_of_lane = lax.broadcasted_iota(I32, (1, gw), 1) // SSM_HEAD_DIM

    for g in range(SSM_GROUPS):
        bg = xbc_ref[0, :, d_inner + g * D_STATE:d_inner + (g + 1) * D_STATE]
        cg = xbc_ref[0, :, d_inner + gn + g * D_STATE:d_inner + gn + (g + 1) * D_STATE]
        bt = bg.astype(F32).T
        cb = jnp.dot(cg, bt.astype(BF16), preferred_element_type=F32)
        st = state_scr[g]
        ys = jnp.dot(cg, st.astype(BF16), preferred_element_type=F32)
        xg = xbc_ref[0, :, g * gw:(g + 1) * gw].astype(F32)
        y = jnp.zeros((chunk, gw), F32)
        upd = jnp.zeros((D_STATE, gw), F32)
        sc = jnp.zeros((1, gw), F32)
        for r in range(rep):
            h = g * rep + r
            own = head_of_lane == r
            seg = a2[:, h:h + 1] - src_t[h:h + 1, :]
            m = (cb * jnp.exp2(jnp.where(causal, seg, -jnp.inf))).astype(BF16)
            xm = jnp.where(own, xg, 0.0).astype(BF16)
            y = y + jnp.dot(m, xm, preferred_element_type=F32)
            upd = upd + jnp.dot((bt * w_t[h:h + 1, :]).astype(BF16), xm, preferred_element_type=F32)
            sc = jnp.where(own, e_end[:, h:h + 1], sc)
        state_scr[g] = st * sc + upd
        gs = slice(g * gw, (g + 1) * gw)
        y = y + ys * e_all[:, gs] + xg * dsk_ref[:, gs]
        zg = z_ref[0, :, gs].astype(F32)
        y = y * (zg * _sigmoid(zg))
        y = y * lax.rsqrt(jnp.mean(y * y, axis=-1, keepdims=True) + RMS_EPS) * ng_ref[:, gs]
        o_ref[0, :, gs] = y.astype(o_ref.dtype)


def ssd(xbc, z, ha3, dt_bias, a_neg, d_skip, norm_g):
    bsz, s, c = xbc.shape
    d_inner = SSM_HEADS * SSM_HEAD_DIM
    chunk = min(SSD_CHUNK, s)
    gw = d_inner // SSM_GROUPS
    vec = lambda n: pl.BlockSpec((1, n), lambda bi, ci: (0, 0))
    expand = (jnp.arange(d_inner, dtype=I32)[None, :] // SSM_HEAD_DIM
              == jnp.arange(LANES, dtype=I32)[:, None]).astype(BF16)
    return pl.pallas_call(
        functools.partial(_ssd_kernel, chunk=chunk),
        grid=(bsz, s // chunk),
        in_specs=[pl.BlockSpec((1, chunk, c), lambda bi, ci: (bi, ci, 0)),
                  pl.BlockSpec((1, chunk, d_inner), lambda bi, ci: (bi, ci, 0)),
                  pl.BlockSpec((1, chunk, LANES), lambda bi, ci: (bi, ci, HA_DT // LANES)),
                  vec(LANES), vec(LANES), vec(d_inner), vec(d_inner),
                  pl.BlockSpec((LANES, d_inner), lambda bi, ci: (0, 0))],
        out_specs=pl.BlockSpec((1, chunk, d_inner), lambda bi, ci: (bi, ci, 0)),
        out_shape=jax.ShapeDtypeStruct((bsz, s, d_inner), BF16),
        scratch_shapes=[pltpu.VMEM((SSM_GROUPS, D_STATE, gw), F32)],
        compiler_params=_cp("parallel", "arbitrary"),
        name="ssd_scan",
    )(xbc, z, ha3, dt_bias, a_neg, d_skip, norm_g, expand)


def _xattn_kernel(q_ref, kv_ref, o_ref, *, scale):
    hd = XA_HEADS * XA_HEAD_DIM
    nt = (((1,), (1,)), ((), ()))
    for h in range(XA_HEADS):
        sl = slice(h * XA_HEAD_DIM, (h + 1) * XA_HEAD_DIM)
        q = q_ref[0, :, sl]
        k = kv_ref[0, :, sl]
        v = kv_ref[0, :, hd + h * XA_HEAD_DIM:hd + (h + 1) * XA_HEAD_DIM]
        s = lax.dot_general(q, k, nt, preferred_element_type=F32) * scale
        p = jnp.exp(s - jnp.max(s, axis=-1, keepdims=True))
        l = jnp.sum(p, axis=-1, keepdims=True)
        o = jnp.dot(p.astype(BF16), v, preferred_element_type=F32) / l
        o_ref[0, :, sl] = o.astype(o_ref.dtype)


def mem_attention(q, kv):
    bsz, s, hd = q.shape
    m = kv.shape[1]
    tq = min(XA_TQ, s)
    return pl.pallas_call(
        functools.partial(_xattn_kernel, scale=float(XA_HEAD_DIM ** -0.5)),
        grid=(bsz, s // tq),
        in_specs=[pl.BlockSpec((1, tq, hd), lambda bi, i: (bi, i, 0)),
                  pl.BlockSpec((1, m, 2 * hd), lambda bi, i: (bi, 0, 0))],
        out_specs=pl.BlockSpec((1, tq, hd), lambda bi, i: (bi, i, 0)),
        out_shape=jax.ShapeDtypeStruct((bsz, s, hd), BF16),
        compiler_params=_cp("parallel", "parallel"),
        name="mem_attention",
    )(q, kv)


def _layer_norm(v, g, b):
    mu = jnp.mean(v, axis=-1, keepdims=True)
    d = v - mu
    var = jnp.mean(d * d, axis=-1, keepdims=True)
    return d * lax.rsqrt(var + NORM_EPS) * g + b


def _merge_kernel(ol_ref, yn_ref, cm_ref, g_ref, x_ref, wuv_ref, wa_ref, wb_ref, wc_ref, wo_ref,
                  lg_ref, lb_ref, xo_ref, a_scr, *, alpha):
    tm, d = x_ref.shape
    for h in range(MLA_HEADS):
        a_scr[:, h * V_DIM:(h + 1) * V_DIM] = jnp.dot(
            ol_ref[:, h].reshape(tm, KV_RANK), wuv_ref[h],
            preferred_element_type=F32).astype(BF16)
    o_a = jnp.dot(a_scr[...], wa_ref[...], preferred_element_type=F32)
    o_b = jnp.dot(yn_ref[...], wb_ref[...], preferred_element_type=F32)
    o_c = jnp.dot(cm_ref[...], wc_ref[...], preferred_element_type=F32)
    g = g_ref[...].astype(F32)
    merged = (_sigmoid(g[:, :d]) * o_a + _sigmoid(g[:, d:2 * d]) * o_b + _sigmoid(g[:, 2 * d:]) * o_c)
    y = alpha * x_ref[...] + jnp.dot(merged.astype(BF16), wo_ref[...], preferred_element_type=F32)
    xo_ref[...] = _layer_norm(y, lg_ref[...], lb_ref[...])


def merge_ln(ol, yn, cm, g, x, wuv, wa, wb, wc, wo, ln_g, ln_b, alpha):
    t, d = x.shape
    tm = min(MERGE_TM, t)
    tq = ol.shape[2]
    row = lambda n: pl.BlockSpec((tm, n), lambda i: (i, 0))
    full = lambda a: pl.BlockSpec(a.shape, lambda i: (0,) * a.ndim, pipeline_mode=pl.Buffered(1))
    return pl.pallas_call(
        functools.partial(_merge_kernel, alpha=alpha),
        grid=(t // tm,),
        in_specs=[pl.BlockSpec((tm // tq, MLA_HEADS, tq, KV_RANK), lambda i: (i, 0, 0, 0)),
                  row(yn.shape[1]), row(cm.shape[1]), row(g.shape[1]), row(d),
                  full(wuv), full(wa), full(wb), full(wc), full(wo), full(ln_g), full(ln_b)],
        out_specs=row(d),
        out_shape=jax.ShapeDtypeStruct((t, d), F32),
        scratch_shapes=[pltpu.VMEM((tm, MLA_HEADS * V_DIM), BF16)],
        compiler_params=_cp("parallel"),
        name="merge_ln",
    )(ol, yn, cm, g, x, wuv, wa, wb, wc, wo, ln_g, ln_b)


def _first_max(v, expert):
    m = jnp.max(v, axis=0, keepdims=True)
    idx = jnp.min(jnp.where(v == m, expert, N_EXPERTS), axis=0, keepdims=True)
    return m, idx


def _route_kernel(x_ref, rwh_ref, rwl_ref, rbt_ref, info_ref, infot_ref, tinfo_ref, cnt_ref, carry_scr):
    tm = x_ref.shape[0]

    @pl.when(pl.program_id(0) == 0)
    def _():
        carry_scr[...] = jnp.zeros(carry_scr.shape, F32)

    x = x_ref[...]
    x_hi = x.astype(BF16)
    x_lo = (x - x_hi.astype(F32)).astype(BF16)
    logits = (jnp.dot(x_hi, rwh_ref[...], preferred_element_type=F32)
              + jnp.dot(x_lo, rwh_ref[...], preferred_element_type=F32)
              + jnp.dot(x_hi, rwl_ref[...], preferred_element_type=F32))
    scores = _sigmoid(logits.T[:N_EXPERTS, :])
    sel = scores + jnp.tile(rbt_ref[...], (1, tm // LANES))
    expert = lax.broadcasted_iota(I32, (N_EXPERTS, tm), 0)
    neg = -jnp.inf
    best_score = None
    best_group = None
    for j in range(N_EXPERT_GROUPS):
        in_j = (expert >= j * EXPERTS_PER_GROUP) & (expert < (j + 1) * EXPERTS_PER_GROUP)
        v = jnp.where(in_j, sel, neg)
        m1, i1 = _first_max(v, expert)
        m2, _ = _first_max(jnp.where(expert == i1, neg, v), expert)
        gs = m1 + m2
        if j == 0:
            best_score, best_group = gs, jnp.zeros_like(i1)
        else:
            better = gs > best_score
            best_score = jnp.where(better, gs, best_score)
            best_group = jnp.where(better, j, best_group)
    lo = best_group * EXPERTS_PER_GROUP
    v = jnp.where((expert >= lo) & (expert < lo + EXPERTS_PER_GROUP), sel, neg)
    _, e1 = _first_max(v, expert)
    _, e2 = _first_max(jnp.where(expert == e1, neg, v), expert)
    w1 = jnp.sum(jnp.where(expert == e1, scores, 0.0), axis=0, keepdims=True)
    w2 = jnp.sum(jnp.where(expert == e2, scores, 0.0), axis=0, keepdims=True)
    wsum = w1 + w2
    member = jnp.where(expert == e1, 1.0, jnp.where(expert == e2, 1.0, 0.0))
    row = lax.broadcasted_iota(I32, (tm, tm), 0)
    col = lax.broadcasted_iota(I32, (tm, tm), 1)
    earlier = jnp.where(row < col, 1.0, 0.0).astype(BF16)
    lrank = jnp.dot(member.astype(BF16), earlier, preferred_element_type=F32)
    n_col = jnp.sum(member, axis=1, keepdims=True)
    diag = (lax.broadcasted_iota(I32, (N_EXPERTS, LANES), 0)
            == lax.broadcasted_iota(I32, (N_EXPERTS, LANES), 1))
    n = jnp.sum(jnp.where(diag, n_col, 0.0), axis=0, keepdims=True)
    n8 = jnp.floor((n + (ROW_ALIGN - 1)) * (1.0 / ROW_ALIGN)) * ROW_ALIGN
    ua = lax.broadcasted_iota(I32, (LANES, LANES), 0)
    ub = lax.broadcasted_iota(I32, (LANES, LANES), 1)
    lower_experts = jnp.where(ua < ub, 1.0, 0.0).astype(BF16)
    n8_rows = jnp.broadcast_to(n8, (8, LANES))
    block_start = jnp.dot(n8_rows.astype(BF16), lower_experts, preferred_element_type=F32)
    start_col = jnp.sum(jnp.where(diag, block_start[0:1, :], 0.0), axis=1, keepdims=True)
    local_row = start_col + lrank
    j1 = jnp.sum(jnp.where(expert == e1, local_row, 0.0), axis=0, keepdims=True)
    j2 = jnp.sum(jnp.where(expert == e2, local_row, 0.0), axis=0, keepdims=True)
    carry = carry_scr[...]
    sub = lax.broadcasted_iota(I32, (8, LANES), 0)
    tinfo_ref[0] = jnp.where(sub == 0, n8_rows, jnp.where(sub == 1, block_start, jnp.where(sub == 2, carry, 0.0)))
    new_carry = carry + n8_rows
    carry_scr[...] = new_carry
    cnt_ref[...] = new_carry
    field = lax.broadcasted_iota(I32, (8, tm), 0)
    info_t = jnp.where(field == 0, e1.astype(F32),
             jnp.where(field == 1, e2.astype(F32),
             jnp.where(field == 2, w1 / wsum,
             jnp.where(field == 3, w2 / wsum,
             jnp.where(field == 4, j1,
             jnp.where(field == 5, j2, 0.0))))))
    infot_ref[0] = info_t
    info_ref[...] = jnp.concatenate([info_t, jnp.zeros((LANES - 8, tm), F32)], axis=0).T


def route(x, rw_hi, rw_lo, rb):
    t, d = x.shape
    tm = min(DISPATCH_TM, t)
    return pl.pallas_call(
        _route_kernel,
        grid=(t // tm,),
        in_specs=[pl.BlockSpec((tm, d), lambda i: (i, 0)),
                  pl.BlockSpec((d, LANES), lambda i: (0, 0)),
                  pl.BlockSpec((d, LANES), lambda i: (0, 0)),
                  pl.BlockSpec((N_EXPERTS, LANES), lambda i: (0, 0))],
        out_specs=[pl.BlockSpec((tm, LANES), lambda i: (i, 0)),
                   pl.BlockSpec((1, 8, tm), lambda i: (i, 0, 0)),
                   pl.BlockSpec((1, 8, LANES), lambda i: (i, 0, 0)),
                   pl.BlockSpec((8, LANES), lambda i: (0, 0))],
        out_shape=[jax.ShapeDtypeStruct((t, LANES), F32),
                   jax.ShapeDtypeStruct((t // tm, 8, tm), F32),
                   jax.ShapeDtypeStruct((t // tm, 8, LANES), F32),
                   jax.ShapeDtypeStruct((8, LANES), F32)],
        scratch_shapes=[pltpu.VMEM((8, LANES), F32)],
        compiler_params=_cp("arbitrary"),
        name="route",
    )(x, rw_hi, rw_lo, rb)


def _block_copies(plan_ref, local_ref, sorted_ref, sem, to_sorted, start):
    for e in range(N_EXPERTS):
        count = plan_ref[0, 0, e]
        loc0 = plan_ref[0, 0, N_EXPERTS + e]
        dst0 = plan_ref[0, 0, 2 * N_EXPERTS + e]

        def body(k, carry, loc0=loc0, dst0=dst0):
            loc = local_ref.at[pl.ds(pl.multiple_of(loc0 + k * ROW_ALIGN, ROW_ALIGN), ROW_ALIGN)]
            srt = sorted_ref.at[pl.ds(pl.multiple_of(dst0 + k * ROW_ALIGN, ROW_ALIGN), ROW_ALIGN)]
            cp = pltpu.make_async_copy(loc, srt, sem) if to_sorted else pltpu.make_async_copy(srt, loc, sem)
            if start:
                cp.start()
            else:
                cp.wait()
            return carry

        lax.fori_loop(0, count, body, 0)


def _dispatch_kernel(plan_ref, prev_plan_ref, infot_ref, x_ref, init_ref, o_ref, local_scr, sem):
    del init_ref
    tm = x_ref.shape[0]
    i = pl.program_id(0)
    slot = i % 2
    j1 = infot_ref[0, 4:5, :]
    j2 = infot_ref[0, 5:6, :]
    rowid = lax.broadcasted_iota(I32, (LOCAL_ROWS, tm), 0).astype(F32)
    sel = jnp.where(rowid == j1, 1.0, jnp.where(rowid == j2, 1.0, 0.0)).astype(BF16)
    local_scr[slot] = jnp.dot(sel, x_ref[...].astype(BF16), preferred_element_type=F32).astype(BF16)

    @pl.when(i > 0)
    def _():
        _block_copies(prev_plan_ref, local_scr.at[1 - slot], o_ref, sem.at[1 - slot], True, False)

    _block_copies(plan_ref, local_scr.at[slot], o_ref, sem.at[slot], True, True)

    @pl.when(i == pl.num_programs(0) - 1)
    def _():
        _block_copies(plan_ref, local_scr.at[slot], o_ref, sem.at[slot], True, False)


def dispatch_rows(x, info_t, plan, n_rows):
    t, d = x.shape
    tm = min(DISPATCH_TM, t)
    init = jnp.zeros((n_rows, d), BF16)
    return pl.pallas_call(
        _dispatch_kernel,
        grid=(t // tm,),
        in_specs=[pl.BlockSpec((1, 1, 3 * N_EXPERTS), lambda i: (i, 0, 0), memory_space=pltpu.SMEM),
                  pl.BlockSpec((1, 1, 3 * N_EXPERTS), lambda i: (jnp.maximum(i - 1, 0), 0, 0),
                               memory_space=pltpu.SMEM),
                  pl.BlockSpec((1, 8, tm), lambda i: (i, 0, 0)),
                  pl.BlockSpec((tm, d), lambda i: (i, 0)),
                  pl.BlockSpec(memory_space=pl.ANY)],
        out_specs=pl.BlockSpec(memory_space=pl.ANY),
        out_shape=jax.ShapeDtypeStruct((n_rows, d), BF16),
        scratch_shapes=[pltpu.VMEM((2, LOCAL_ROWS, d), BF16), pltpu.SemaphoreType.DMA((2,))],
        input_output_aliases={4: 0},
        compiler_params=_cp("arbitrary"),
        name="moe_dispatch",
    )(plan, plan, info_t, x, init)


def _expert_kernel(te_ref, nu_ref, xs_ref, w1_ref, w3_ref, w2_ref, o_ref, w13_scr, w2_scr):
    i = pl.program_id(0)
    f = w2_ref.shape[2]
    new_expert = (i == 0) | (te_ref[i] != te_ref[jnp.maximum(i - 1, 0)])

    @pl.when((i < nu_ref[0]) & new_expert)
    def _():
        w13_scr[:, :f] = w1_ref[0, 0].astype(BF16)
        w13_scr[:, f:] = w3_ref[0, 0].astype(BF16)
        w2_scr[...] = w2_ref[0, 0].astype(BF16)

    @pl.when(i < nu_ref[0])
    def _():
        h = jnp.dot(xs_ref[...], w13_scr[...], preferred_element_type=F32)
        h1 = h[:, :f]
        act = (h1 * _sigmoid(h1) * h[:, f:]).astype(BF16)
        o_ref[...] = jnp.dot(act, w2_scr[...], preferred_element_type=F32).astype(o_ref.dtype)

    @pl.when(i >= nu_ref[0])
    def _():
        o_ref[...] = jnp.zeros(o_ref.shape, o_ref.dtype)


def expert_ffn(xs, w1, w3, w2, layer, tile_expert, n_used):
    p, d = xs.shape
    f = w2.shape[2]
    n_tiles = p // MOE_TM
    grid_spec = pltpu.PrefetchScalarGridSpec(
        num_scalar_prefetch=2,
        grid=(n_tiles,),
        in_specs=[pl.BlockSpec((MOE_TM, d), lambda i, te, nu: (i, 0)),
                  pl.BlockSpec((1, 1, d, f), lambda i, te, nu: (layer, te[i], 0, 0)),
                  pl.BlockSpec((1, 1, d, f), lambda i, te, nu: (layer, te[i], 0, 0)),
                  pl.BlockSpec((1, 1, f, d), lambda i, te, nu: (layer, te[i], 0, 0))],
        out_specs=pl.BlockSpec((MOE_TM, d), lambda i, te, nu: (i, 0)),
        scratch_shapes=[pltpu.VMEM((d, 2 * f), BF16), pltpu.VMEM((f, d), BF16)],
    )
    return pl.pallas_call(
        _expert_kernel,
        grid_spec=grid_spec,
        out_shape=jax.ShapeDtypeStruct((p, d), BF16),
        compiler_params=_cp("arbitrary"),
        name="expert_ffn",
    )(tile_expert, n_used, xs, w1, w3, w2)


def _combine_kernel(plan_ref, next_plan_ref, info_ref, x_ref, ys_ref, lg_ref, lb_ref, xo_ref, xb_ref,
                    local_scr, sem, *, alpha):
    tm = x_ref.shape[0]
    i = pl.program_id(0)
    slot = i % 2

    @pl.when(i == 0)
    def _():
        _block_copies(plan_ref, local_scr.at[slot], ys_ref, sem.at[slot], False, True)

    @pl.when(i + 1 < pl.num_programs(0))
    def _():
        _block_copies(next_plan_ref, local_scr.at[1 - slot], ys_ref, sem.at[1 - slot], False, True)

    _block_copies(plan_ref, local_scr.at[slot], ys_ref, sem.at[slot], False, False)
    last = N_EXPERTS - 1
    used = plan_ref[0, 0, N_EXPERTS + last] + plan_ref[0, 0, last] * ROW_ALIGN
    rowid = lax.broadcasted_iota(I32, (LOCAL_ROWS, 1), 0)
    y = local_scr[slot]
    y = jnp.where(rowid < used, y, jnp.zeros_like(y))
    info = info_ref[...]
    col = lax.broadcasted_iota(I32, (tm, LOCAL_ROWS), 1).astype(F32)
    gate = jnp.where(col == info[:, 4:5], info[:, 2:3],
                     jnp.where(col == info[:, 5:6], info[:, 3:4], 0.0)).astype(BF16)
    moe = jnp.dot(gate, y, preferred_element_type=F32)
    out = _layer_norm(alpha * x_ref[...] + moe, lg_ref[...], lb_ref[...])
    xo_ref[...] = out
    xb_ref[...] = out.astype(BF16)


def combine_ln(plan, info, x, ys, ln_g, ln_b, alpha):
    t, d = x.shape
    tm = min(DISPATCH_TM, t)
    return pl.pallas_call(
        functools.partial(_combine_kernel, alpha=alpha),
        grid=(t // tm,),
        in_specs=[pl.BlockSpec((1, 1, 3 * N_EXPERTS), lambda i: (i, 0, 0), memory_space=pltpu.SMEM),
                  pl.BlockSpec((1, 1, 3 * N_EXPERTS), lambda i: (jnp.minimum(i + 1, t // tm - 1), 0, 0),
                               memory_space=pltpu.SMEM),
                  pl.BlockSpec((tm, LANES), lambda i: (i, 0)),
                  pl.BlockSpec((tm, d), lambda i: (i, 0)),
                  pl.BlockSpec(memory_space=pl.ANY),
                  pl.BlockSpec((1, d), lambda i: (0, 0)),
                  pl.BlockSpec((1, d), lambda i: (0, 0))],
        out_specs=[pl.BlockSpec((tm, d), lambda i: (i, 0)), pl.BlockSpec((tm, d), lambda i: (i, 0))],
        out_shape=[jax.ShapeDtypeStruct((t, d), F32), jax.ShapeDtypeStruct((t, d), BF16)],
        scratch_shapes=[pltpu.VMEM((2, LOCAL_ROWS, d), BF16), pltpu.SemaphoreType.DMA((2,))],
        compiler_params=_cp("arbitrary"),
        name="moe_combine_ln",
    )(plan, plan, info, x, ys, ln_g, ln_b)


def _prep_layer(l, p):
    w_in = p["w_in"][l]
    d = w_in.shape[0]
    o_dq = 0
    o_dkv = o_dq + Q_RANK
    o_z = o_dkv + KV_RANK + ROPE_DIM
    d_inner = SSM_HEADS * SSM_HEAD_DIM
    conv_ch = d_inner + 2 * SSM_GROUPS * D_STATE
    o_xbc = o_z + d_inner
    o_dt = o_xbc + conv_ch
    o_qm = o_dt + SSM_HEADS
    o_g = o_qm + XA_HEADS * XA_HEAD_DIM
    half = ROPE_DIM // 2
    kr0 = o_dkv + KV_RANK
    zeros = lambda n: jnp.zeros((d, n), F32)
    w_small = jnp.concatenate([
        w_in[:, o_dq:o_dq + Q_RANK],
        w_in[:, o_dkv:o_dkv + KV_RANK],
        w_in[:, kr0:kr0 + ROPE_DIM], zeros(LANES - ROPE_DIM),
        w_in[:, kr0 + half:kr0 + ROPE_DIM], w_in[:, kr0:kr0 + half], zeros(LANES - ROPE_DIM),
        w_in[:, o_dt:o_dt + SSM_HEADS], zeros(LANES - SSM_HEADS)], axis=1)
    w_uq = p["w_uq"][l].reshape(Q_RANK, MLA_HEADS, NOPE_DIM + ROPE_DIM)
    wq_nope = w_uq[:, :, :NOPE_DIM].reshape(Q_RANK, -1)
    wq_rope = w_uq[:, :, NOPE_DIM:]
    wq_rope_sw = jnp.concatenate([wq_rope[:, :, half:], wq_rope[:, :, :half]], axis=-1)
    wq = jnp.concatenate([wq_nope, wq_rope.reshape(Q_RANK, -1), wq_rope_sw.reshape(Q_RANK, -1)], axis=1)
    w_ukv = p["w_ukv"][l]
    wuk_t = jnp.transpose(w_ukv[:, :, :NOPE_DIM], (1, 2, 0))
    wuv = jnp.transpose(w_ukv[:, :, NOPE_DIM:], (1, 0, 2))
    pad_heads = lambda v, fill: jnp.concatenate(
        [v.astype(F32), jnp.full((LANES - SSM_HEADS,), fill, F32)]).reshape(1, LANES)
    bf = lambda a: a.astype(BF16)
    return dict(
        w_small=bf(w_small), w_z=bf(w_in[:, o_z:o_z + d_inner]), w_xbc=bf(w_in[:, o_xbc:o_xbc + conv_ch]),
        w_qm=bf(w_in[:, o_qm:o_g]), w_g=bf(w_in[:, o_g:]),
        q_norm=p["q_norm"][l].reshape(1, -1), kv_norm=p["kv_norm"][l].reshape(1, -1),
        wq=bf(wq), wuk_t=bf(wuk_t), wuv=bf(wuv),
        conv_w=p["conv_w"][l], conv_b=p["conv_b"][l].reshape(1, -1),
        dt_bias=pad_heads(p["dt_bias"][l], 0.0),
        a_neg=pad_heads(-jnp.exp(p["a_log"][l].astype(F32)), 0.0),
        d_skip=jnp.repeat(p["d_skip"][l].astype(F32), SSM_HEAD_DIM).reshape(1, -1),
        ssm_norm=p["ssm_norm"][l].reshape(1, -1),
        w_mem_kv=bf(p["w_mem_kv"][l]),
        wa=bf(p["w_proj_a"][l]), wb=bf(p["w_proj_b"][l]), wc=bf(p["w_proj_c"][l]), wo=bf(p["w_out"][l]),
        ln1_g=p["ln1_g"][l].reshape(1, -1), ln1_b=p["ln1_b"][l].reshape(1, -1),
        ln2_g=p["ln2_g"][l].reshape(1, -1), ln2_b=p["ln2_b"][l].reshape(1, -1),
    )


def _dispatch_plan(tinfo, counts, t):
    n_tok_tiles = tinfo.shape[0]
    n8 = tinfo[:, 0, :N_EXPERTS].astype(I32)
    block_start = tinfo[:, 1, :N_EXPERTS].astype(I32)
    carry = tinfo[:, 2, :N_EXPERTS].astype(I32)
    rows = counts[0, :N_EXPERTS].astype(I32)
    tiles = (rows + MOE_TM - 1) // MOE_TM
    tile_end = jnp.cumsum(tiles)
    row_start = (tile_end - tiles) * MOE_TM
    plan = jnp.concatenate([n8 // ROW_ALIGN, block_start, row_start[None, :] + carry], axis=1)
    max_rows = TOP_K * t + N_EXPERTS * (ROW_ALIGN - 1) * n_tok_tiles
    n_tiles = -(-max_rows // MOE_TM) + N_EXPERTS
    tile_ids = jnp.arange(n_tiles, dtype=I32)
    tile_expert = jnp.minimum(jnp.sum(tile_ids[:, None] >= tile_end[None, :], axis=1), N_EXPERTS - 1)
    n_used = tile_end[-1:].astype(I32)
    return plan.reshape(n_tok_tiles, 1, 3 * N_EXPERTS), tile_expert.astype(I32), n_used, n_tiles * MOE_TM


def kernel(x, mem, positions, w_in, q_norm, w_uq, kv_norm, w_ukv, w_proj_a, conv_w, conv_b, dt_bias, a_log,
           d_skip, ssm_norm, w_proj_b, w_mem_kv, w_proj_c, w_out, ln1_g, ln1_b, router_w, router_bias,
           exp_w1, exp_w3, exp_w2, ln2_g, ln2_b):
    params = dict(w_in=w_in, q_norm=q_norm, w_uq=w_uq, kv_norm=kv_norm, w_ukv=w_ukv, w_proj_a=w_proj_a,
                  conv_w=conv_w, conv_b=conv_b, dt_bias=dt_bias, a_log=a_log, d_skip=d_skip,
                  ssm_norm=ssm_norm, w_proj_b=w_proj_b, w_mem_kv=w_mem_kv, w_proj_c=w_proj_c, w_out=w_out,
                  ln1_g=ln1_g, ln1_b=ln1_b, exp_w1=exp_w1, exp_w3=exp_w3, exp_w2=exp_w2,
                  ln2_g=ln2_g, ln2_b=ln2_b)
    bsz, s, d = x.shape
    t = bsz * s
    depth = w_in.shape[0]
    alpha = float((2 * depth) ** 0.25)
    n_mem = mem.shape[1]

    inv = ROPE_THETA ** (-jnp.arange(0, ROPE_DIM, 2, dtype=F32) / ROPE_DIM)
    ang = positions.astype(F32)[..., None] * inv
    cos, sin = jnp.cos(ang), jnp.sin(ang)
    cosq = jnp.tile(jnp.concatenate([cos, cos], axis=-1), (1, 1, MLA_HEADS)).reshape(t, -1)
    sinq = jnp.tile(jnp.concatenate([-sin, sin], axis=-1), (1, 1, MLA_HEADS)).reshape(t, -1)

    rw = jnp.concatenate([router_w.astype(F32), jnp.zeros((d, LANES - N_EXPERTS), F32)], axis=1)
    rw_hi = rw.astype(BF16)
    rw_lo = (rw - rw_hi.astype(F32)).astype(BF16)
    rbt = jnp.broadcast_to(router_bias.astype(F32)[:, None], (N_EXPERTS, LANES))
    mem_b = mem.reshape(bsz * n_mem, d).astype(BF16)

    xf = x.reshape(t, d).astype(F32)
    xb = xf.astype(BF16)
    for l in range(depth):
        w = _prep_layer(l, params)
        ha = matmul(xb, w["w_small"], F32, "proj_small")
        z = matmul(xb, w["w_z"], BF16, "proj_z")
        xbc_c = proj_conv_silu(xb, w["w_xbc"], w["conv_w"], w["conv_b"], s).reshape(bsz, s, -1)
        qm = matmul(xb, w["w_qm"], BF16, "proj_qmem")
        g = matmul(xb, w["w_g"], BF16, "proj_gate")

        ql, qr, ck, kr = mla_prep(ha, cosq, sinq, w["q_norm"], w["kv_norm"], w["wq"], w["wuk_t"],
                                  min(ATT_TQ, s))
        o_lat = mla_attention(ql, qr, ck.reshape(bsz, s, KV_RANK), kr.reshape(bsz, s, ROPE_DIM))

        yn = ssd(xbc_c, z.reshape(bsz, s, -1), ha.reshape(bsz, s, HA_W), w["dt_bias"], w["a_neg"],
                 w["d_skip"], w["ssm_norm"]).reshape(t, -1)

        kv = matmul(mem_b, w["w_mem_kv"], BF16, "proj_memkv").reshape(bsz, n_mem, -1)
        cm = mem_attention(qm.reshape(bsz, s, -1), kv).reshape(t, -1)

        x1 = merge_ln(o_lat, yn, cm, g, xf, w["wuv"], w["wa"], w["wb"], w["wc"], w["wo"],
                           w["ln1_g"], w["ln1_b"], alpha)

        info, info_t, tinfo, counts = route(x1, rw_hi, rw_lo, rbt)
        plan, tile_expert, n_used, n_rows = _dispatch_plan(tinfo, counts, t)
        xs = dispatch_rows(x1, info_t, plan, n_rows)
        ys = expert_ffn(xs, exp_w1, exp_w3, exp_w2, l, tile_expert, n_used)
        xf, xb = combine_ln(plan, info, x1, ys, w["ln2_g"], w["ln2_b"], alpha)
    return xf.reshape(bsz, s, d)
```

```python
import functools

import jax
import jax.numpy as jnp
from jax import lax
from jax.experimental import pallas as pl
from jax.experimental.pallas import tpu as pltpu

F32 = jnp.float32
BF16 = jnp.bfloat16
I32 = jnp.int32

MLA_HEADS = 8
Q_RANK = 384
KV_RANK = 256
NOPE_DIM = 128
ROPE_DIM = 64
V_DIM = 128
ROPE_THETA = 10000.0
SSM_HEADS = 32
SSM_HEAD_DIM = 64
SSM_GROUPS = 8
D_STATE = 128
CONV_K = 4
XA_HEADS = 4
XA_HEAD_DIM = 256
N_EXPERTS = 16
N_EXPERT_GROUPS = 4
EXPERTS_PER_GROUP = 4
TOP_K = 2
NORM_EPS = 1e-5
RMS_EPS = 1e-6

LANES = 128
V7X_VMEM_LIMIT = 56 * 1024 * 1024

MM_TM = 1024
MM_TN = 1024
PREP_TM = 512
ATT_TQ = 128
ATT_TK = 512
ATT_GROUP_ROWS = 512
ATT_UNROLL = 2
LOG2E = 1.4426950408889634
PROJ_CONV_TM = 1024
PROJ_CONV_TN = 1024
PROJ_CONV_SUB = 256
CONV_HALO = 16
SSD_CHUNK = 256
XA_TQ = 512
MERGE_TM = 512
MOE_TM = 512
DISPATCH_TM = 512
ROW_ALIGN = 16
LOCAL_ROWS = 1280

HA_W = 1024
HA_DQ = 0
HA_C = 384
HA_KR = 640
HA_KRS = 768
HA_DT = 896


def _cp(*sem):
    return pltpu.CompilerParams(dimension_semantics=sem, vmem_limit_bytes=V7X_VMEM_LIMIT)


def _sigmoid(x):
    return 1.0 / (1.0 + jnp.exp(-x))


def _mm_kernel(x_ref, w_ref, o_ref):
    o_ref[...] = jnp.dot(x_ref[...], w_ref[...], preferred_element_type=F32).astype(o_ref.dtype)


def matmul(x, w, out_dtype, name):
    m, k = x.shape
    n = w.shape[1]
    tm = min(MM_TM, m)
    tn = min(MM_TN, n)
    return pl.pallas_call(
        _mm_kernel,
        grid=(n // tn, m // tm),
        in_specs=[pl.BlockSpec((tm, k), lambda j, i: (i, 0)),
                  pl.BlockSpec((k, tn), lambda j, i: (0, j))],
        out_specs=pl.BlockSpec((tm, tn), lambda j, i: (i, j)),
        out_shape=jax.ShapeDtypeStruct((m, n), out_dtype),
        compiler_params=_cp("parallel", "parallel"),
        name=name,
    )(x, w)


def _proj_conv_kernel(x_ref, xh_ref, w_ref, cw_ref, cb_ref, o_ref, *, tiles_per_seq):
    first = (pl.program_id(1) % tiles_per_seq) == 0
    tn = o_ref.shape[1]
    sub = min(PROJ_CONV_SUB, tn)
    def project(c):
        w = w_ref[:, c * sub:(c + 1) * sub]
        u = jnp.dot(x_ref[...], w, preferred_element_type=F32)
        halo = jnp.dot(xh_ref[...], w, preferred_element_type=F32)
        return jnp.concatenate([jnp.where(first, jnp.zeros_like(halo), halo), u], axis=0)

    n_sub = tn // sub
    nxt = project(0)
    for c in range(n_sub):
        cs = slice(c * sub, (c + 1) * sub)
        cw = cw_ref[:, cs]
        ext = nxt
        if c + 1 < n_sub:
            nxt = project(c + 1)
        prev = pltpu.roll(ext, 1, 0)
        near = cw[3:4, :] * ext + cw[2:3, :] * prev
        far = cw[1:2, :] * ext + cw[0:1, :] * prev
        acc = (near + pltpu.roll(far, 2, 0))[CONV_HALO:, :] + cb_ref[:, cs]
        half = 0.5 * acc
        o_ref[:, cs] = (half * jnp.tanh(half) + half).astype(o_ref.dtype)


def proj_conv_silu(x, w, conv_w, conv_b, seq_len):
    m, k = x.shape
    n = w.shape[1]
    tm = min(PROJ_CONV_TM, seq_len)
    tn = min(PROJ_CONV_TN, n)
    hb = tm // CONV_HALO
    return pl.pallas_call(
        functools.partial(_proj_conv_kernel, tiles_per_seq=seq_len // tm),
        grid=(n // tn, m // tm),
        in_specs=[pl.BlockSpec((tm, k), lambda j, i: (i, 0)),
                  pl.BlockSpec((CONV_HALO, k), lambda j, i: (jnp.maximum(i * hb - 1, 0), 0)),
                  pl.BlockSpec((k, tn), lambda j, i: (0, j)),
                  pl.BlockSpec((CONV_K, tn), lambda j, i: (0, j)),
                  pl.BlockSpec((1, tn), lambda j, i: (0, j))],
        out_specs=pl.BlockSpec((tm, tn), lambda j, i: (i, j)),
        out_shape=jax.ShapeDtypeStruct((m, n), BF16),
        compiler_params=_cp("parallel", "parallel"),
        name="proj_xbc_conv",
    )(x, x, w, conv_w, conv_b)


def _mla_prep_kernel(ha_ref, cos_ref, sin_ref, qn_ref, kvn_ref, wq_ref, wuk_ref,
                     ql_ref, qr_ref, ck_ref, kr_ref, *, scale):
    nb, _, tq, _ = ql_ref.shape
    ha = ha_ref[...]
    dq = ha[:, HA_DQ:HA_DQ + Q_RANK]
    c_q = dq * lax.rsqrt(jnp.mean(dq * dq, axis=-1, keepdims=True) + RMS_EPS) * qn_ref[...]
    q = jnp.dot(c_q.astype(BF16), wq_ref[...], preferred_element_type=F32)
    n_nope = MLA_HEADS * NOPE_DIM
    n_rope = MLA_HEADS * ROPE_DIM
    cosq = jnp.tile(cos_ref[...], (1, n_rope // LANES))
    sinq = jnp.tile(sin_ref[...], (1, n_rope // LANES))
    q_rope = ((q[:, n_nope:n_nope + n_rope] * cosq + q[:, n_nope + n_rope:] * sinq) * scale).astype(BF16)
    for h in range(MLA_HEADS):
        qh = q[:, h * NOPE_DIM:(h + 1) * NOPE_DIM].astype(BF16)
        ql = jnp.dot(qh, wuk_ref[h], preferred_element_type=F32)
        ql_ref[:, h] = (ql * scale).astype(BF16).reshape(nb, tq, KV_RANK)
        qr_ref[:, h] = q_rope[:, h * ROPE_DIM:(h + 1) * ROPE_DIM].reshape(nb, tq, ROPE_DIM)
    c = ha[:, HA_C:HA_C + KV_RANK]
    c_kv = c * lax.rsqrt(jnp.mean(c * c, axis=-1, keepdims=True) + RMS_EPS) * kvn_ref[...]
    ck_ref[...] = c_kv.astype(BF16)
    k_rope = (ha[:, HA_KR:HA_KR + LANES] * cos_ref[...]
              + ha[:, HA_KRS:HA_KRS + LANES] * sin_ref[...])
    kr_ref[...] = k_rope[:, :ROPE_DIM].astype(BF16)


def mla_prep(ha, cosq, sinq, q_norm, kv_norm, wq, wuk_t, tq):
    t = ha.shape[0]
    tm = min(PREP_TM, t)
    nb = tm // tq
    scale = float((NOPE_DIM + ROPE_DIM) ** -0.5 * LOG2E)
    n_rope = MLA_HEADS * ROPE_DIM
    full = lambda shape: pl.BlockSpec(shape, lambda i: (0,) * len(shape))
    return pl.pallas_call(
        functools.partial(_mla_prep_kernel, scale=scale),
        grid=(t // tm,),
        in_specs=[pl.BlockSpec((tm, HA_W), lambda i: (i, 0)),
                  pl.BlockSpec((tm, LANES), lambda i: (i, 0)),
                  pl.BlockSpec((tm, LANES), lambda i: (i, 0)),
                  full((1, Q_RANK)), full((1, KV_RANK)),
                  full(wq.shape), full(wuk_t.shape)],
        out_specs=[pl.BlockSpec((nb, MLA_HEADS, tq, KV_RANK), lambda i: (i, 0, 0, 0)),
                   pl.BlockSpec((nb, MLA_HEADS, tq, ROPE_DIM), lambda i: (i, 0, 0, 0)),
                   pl.BlockSpec((tm, KV_RANK), lambda i: (i, 0)),
                   pl.BlockSpec((tm, ROPE_DIM), lambda i: (i, 0))],
        out_shape=[jax.ShapeDtypeStruct((t // tq, MLA_HEADS, tq, KV_RANK), BF16),
                   jax.ShapeDtypeStruct((t // tq, MLA_HEADS, tq, ROPE_DIM), BF16),
                   jax.ShapeDtypeStruct((t, KV_RANK), BF16),
                   jax.ShapeDtypeStruct((t, ROPE_DIM), BF16)],
        compiler_params=_cp("parallel"),
        name="mla_prep",
    )(ha, cosq, sinq, q_norm, kv_norm, wq, wuk_t)


def _mla_attn_kernel(ql_ref, qr_ref, ck_ref, kr_ref, o_ref, m_scr, l_scr, acc_scr, *, tq, tk, hpc):
    rc = hpc * tq
    n_groups = MLA_HEADS // hpc
    q_start = pl.program_id(1) * tq
    n_full = q_start // tk
    m_scr[...] = jnp.full(m_scr.shape, -jnp.inf, F32)
    l_scr[...] = jnp.zeros(l_scr.shape, F32)
    acc_scr[...] = jnp.zeros(acc_scr.shape, F32)
    nt = (((1,), (1,)), ((), ()))

    def step(j, masked, width=tk):
        ks = pl.multiple_of(j * tk, tk)
        ck = ck_ref[0, pl.ds(ks, width), :]
        kr = kr_ref[0, pl.ds(ks, width), :]
        if masked:
            q_pos = q_start + (lax.broadcasted_iota(I32, (rc, width), 0) & (tq - 1))
            k_pos = ks + lax.broadcasted_iota(I32, (rc, width), 1)
            visible = k_pos <= q_pos
        for c in range(n_groups):
            rs = slice(c * rc, (c + 1) * rc)
            ql = ql_ref[0, c * hpc:(c + 1) * hpc].reshape(rc, KV_RANK)
            qr = qr_ref[0, c * hpc:(c + 1) * hpc].reshape(rc, ROPE_DIM)
            s = (lax.dot_general(ql, ck, nt, preferred_element_type=F32)
                 + lax.dot_general(qr, kr, nt, preferred_element_type=F32))
            if masked:
                s = jnp.where(visible, s, -jnp.inf)
            m_prev = m_scr[rs]
            m_new = jnp.maximum(m_prev, jnp.max(s, axis=-1, keepdims=True))
            alpha = jnp.exp2(m_prev - m_new)
            p = jnp.exp2(s - jnp.tile(m_new, (1, width // LANES)))
            l_scr[rs] = alpha * l_scr[rs] + jnp.sum(p, axis=-1, keepdims=True)
            acc_scr[rs] = (jnp.tile(alpha, (1, KV_RANK // LANES)) * acc_scr[rs]
                           + jnp.dot(p.astype(BF16), ck, preferred_element_type=F32))
            m_scr[rs] = m_new

    def trip(jj, carry):
        for u in range(ATT_UNROLL):
            step(ATT_UNROLL * jj + u, False)
        return carry

    n_trips = n_full // ATT_UNROLL
    lax.fori_loop(0, n_trips, trip, 0)

    def single(j, carry):
        step(j, False)
        return carry

    lax.fori_loop(n_trips * ATT_UNROLL, n_full, single, 0)

    sub = (q_start - n_full * tk) // tq
    for v in range(tk // tq):
        @pl.when(sub == v)
        def _(v=v):
            step(n_full, True, (v + 1) * tq)
    out = acc_scr[...] / jnp.tile(l_scr[...], (1, KV_RANK // LANES))
    o_ref[0] = out.astype(o_ref.dtype).reshape(MLA_HEADS, tq, KV_RANK)


def mla_attention(ql, qr, ck, kr):
    _, _, tq, _ = ql.shape
    b, s, _ = ck.shape
    tk = min(ATT_TK, s)
    assert tk % tq == 0 and s % tk == 0 and tq & (tq - 1) == 0
    nq = s // tq
    rows = tq * MLA_HEADS
    hpc = max(1, min(MLA_HEADS, ATT_GROUP_ROWS // tq))
    qspec = lambda dim: pl.BlockSpec((1, MLA_HEADS, tq, dim), lambda bi, i: (bi * nq + i, 0, 0, 0))
    return pl.pallas_call(
        functools.partial(_mla_attn_kernel, tq=tq, tk=tk, hpc=hpc),
        grid=(b, nq),
        in_specs=[qspec(KV_RANK), qspec(ROPE_DIM),
                  pl.BlockSpec((1, s, KV_RANK), lambda bi, i: (bi, 0, 0)),
                  pl.BlockSpec((1, s, ROPE_DIM), lambda bi, i: (bi, 0, 0))],
        out_specs=qspec(KV_RANK),
        out_shape=jax.ShapeDtypeStruct(ql.shape, BF16),
        scratch_shapes=[pltpu.VMEM((rows, LANES), F32), pltpu.VMEM((rows, LANES), F32),
                        pltpu.VMEM((rows, KV_RANK), F32)],
        compiler_params=_cp("parallel", "parallel"),
        name="mla_attention",
    )(ql, qr, ck, kr)


def _ssd_kernel(xbc_ref, z_ref, dt_ref, dtb_ref, a_ref, dsk_ref, ng_ref, exp_ref, o_ref,
                state_scr, *, chunk):
    d_inner = SSM_HEADS * SSM_HEAD_DIM
    gn = SSM_GROUPS * D_STATE
    rep = SSM_HEADS // SSM_GROUPS
    gw = rep * SSM_HEAD_DIM

    @pl.when(pl.program_id(1) == 0)
    def _():
        state_scr[...] = jnp.zeros(state_scr.shape, F32)

    x_raw = dt_ref[0] + dtb_ref[...]
    dt = jnp.maximum(x_raw, 0.0) + jnp.log(1.0 + jnp.exp(-jnp.abs(x_raw)))
    da = dt * a_ref[...]
    row = lax.broadcasted_iota(I32, (chunk, chunk), 0)
    col = lax.broadcasted_iota(I32, (chunk, chunk), 1)
    causal = row >= col
    tri = jnp.where(causal, 1.0, 0.0).astype(BF16)
    acum = jnp.zeros(da.shape, F32)
    rem = da
    for _ in range(3):
        part = rem.astype(BF16)
        acum = acum + jnp.dot(tri, part, preferred_element_type=F32)
        rem = rem - part.astype(F32)
    a2 = acum * LOG2E
    a2_t = a2.T
    src_t = a2_t - jnp.log2(dt.T)
    w_t = jnp.exp2(a2_t[:, chunk - 1:chunk] - src_t)
    e_end = jnp.exp2(a2[chunk - 1:chunk, :])
    e_all = jnp.dot(jnp.exp2(a2).astype(BF16), exp_ref[...], preferred_element_type=F32)
    head_of_lane = lax.broadcasted_iota(I32, (1, gw), 1) // SSM_HEAD_DIM

    for g in range(SSM_GROUPS):
        bg = xbc_ref[0, :, d_inner + g * D_STATE:d_inner + (g + 1) * D_STATE]
        cg = xbc_ref[0, :, d_inner + gn + g * D_STATE:d_inner + gn + (g + 1) * D_STATE]
        bt = bg.astype(F32).T
        cb = jnp.dot(cg, bt.astype(BF16), preferred_element_type=F32)
        st = state_scr[g]
        ys = jnp.dot(cg, st.astype(BF16), preferred_element_type=F32)
        xg = xbc_ref[0, :, g * gw:(g + 1) * gw].astype(F32)
        y = jnp.zeros((chunk, gw), F32)
        upd = jnp.zeros((D_STATE, gw), F32)
        sc = jnp.zeros((1, gw), F32)
        for r in range(rep):
            h = g * rep + r
            own = head_of_lane == r
            seg = a2[:, h:h + 1] - src_t[h:h + 1, :]
            m = (cb * jnp.exp2(jnp.where(causal, seg, -jnp.inf))).astype(BF16)
            xm = jnp.where(own, xg, 0.0).astype(BF16)
            y = y + jnp.dot(m, xm, preferred_element_type=F32)
            upd = upd + jnp.dot((bt * w_t[h:h + 1, :]).astype(BF16), xm, preferred_element_type=F32)
            sc = jnp.where(own, e_end[:, h:h + 1], sc)
        state_scr[g] = st * sc + upd
        gs = slice(g * gw, (g + 1) * gw)
        y = y + ys * e_all[:, gs] + xg * dsk_ref[:, gs]
        zg = z_ref[0, :, gs].astype(F32)
        y = y * (zg * _sigmoid(zg))
        y = y * lax.rsqrt(jnp.mean(y * y, axis=-1, keepdims=True) + RMS_EPS) * ng_ref[:, gs]
        o_ref[0, :, gs] = y.astype(o_ref.dtype)


def ssd(xbc, z, ha3, dt_bias, a_neg, d_skip, norm_g):
    bsz, s, c = xbc.shape
    d_inner = SSM_HEADS * SSM_HEAD_DIM
    chunk = min(SSD_CHUNK, s)
    gw = d_inner // SSM_GROUPS
    vec = lambda n: pl.BlockSpec((1, n), lambda bi, ci: (0, 0))
    expand = (jnp.arange(d_inner, dtype=I32)[None, :] // SSM_HEAD_DIM
              == jnp.arange(LANES, dtype=I32)[:, None]).astype(BF16)
    return pl.pallas_call(
        functools.partial(_ssd_kernel, chunk=chunk),
        grid=(bsz, s // chunk),
        in_specs=[pl.BlockSpec((1, chunk, c), lambda bi, ci: (bi, ci, 0)),
                  pl.BlockSpec((1, chunk, d_inner), lambda bi, ci: (bi, ci, 0)),
                  pl.BlockSpec((1, chunk, LANES), lambda bi, ci: (bi, ci, HA_DT // LANES)),
                  vec(LANES), vec(LANES), vec(d_inner), vec(d_inner),
                  pl.BlockSpec((LANES, d_inner), lambda bi, ci: (0, 0))],
        out_specs=pl.BlockSpec((1, chunk, d_inner), lambda bi, ci: (bi, ci, 0)),
        out_shape=jax.ShapeDtypeStruct((bsz, s, d_inner), BF16),
        scratch_shapes=[pltpu.VMEM((SSM_GROUPS, D_STATE, gw), F32)],
        compiler_params=_cp("parallel", "arbitrary"),
        name="ssd_scan",
    )(xbc, z, ha3, dt_bias, a_neg, d_skip, norm_g, expand)


def _xattn_kernel(q_ref, kv_ref, o_ref, *, scale):
    hd = XA_HEADS * XA_HEAD_DIM
    nt = (((1,), (1,)), ((), ()))
    for h in range(XA_HEADS):
        sl = slice(h * XA_HEAD_DIM, (h + 1) * XA_HEAD_DIM)
        q = q_ref[0, :, sl]
        k = kv_ref[0, :, sl]
        v = kv_ref[0, :, hd + h * XA_HEAD_DIM:hd + (h + 1) * XA_HEAD_DIM]
        s = lax.dot_general(q, k, nt, preferred_element_type=F32) * scale
        p = jnp.exp(s - jnp.max(s, axis=-1, keepdims=True))
        l = jnp.sum(p, axis=-1, keepdims=True)
        o = jnp.dot(p.astype(BF16), v, preferred_element_type=F32) / l
        o_ref[0, :, sl] = o.astype(o_ref.dtype)


def mem_attention(q, kv):
    bsz, s, hd = q.shape
    m = kv.shape[1]
    tq = min(XA_TQ, s)
    return pl.pallas_call(
        functools.partial(_xattn_kernel, scale=float(XA_HEAD_DIM ** -0.5)),
        grid=(bsz, s // tq),
        in_specs=[pl.BlockSpec((1, tq, hd), lambda bi, i: (bi, i, 0)),
                  pl.BlockSpec((1, m, 2 * hd), lambda bi, i: (bi, 0, 0))],
        out_specs=pl.BlockSpec((1, tq, hd), lambda bi, i: (bi, i, 0)),
        out_shape=jax.ShapeDtypeStruct((bsz, s, hd), BF16),
        compiler_params=_cp("parallel", "parallel"),
        name="mem_attention",
    )(q, kv)


def _layer_norm(v, g, b):
    mu = jnp.mean(v, axis=-1, keepdims=True)
    d = v - mu
    var = jnp.mean(d * d, axis=-1, keepdims=True)
    return d * lax.rsqrt(var + NORM_EPS) * g + b


def _merge_kernel(ol_ref, yn_ref, cm_ref, g_ref, x_ref, wuv_ref, wa_ref, wb_ref, wc_ref, wo_ref,
                  lg_ref, lb_ref, xo_ref, a_scr, *, alpha):
    tm, d = x_ref.shape
    for h in range(MLA_HEADS):
        a_scr[:, h * V_DIM:(h + 1) * V_DIM] = jnp.dot(
            ol_ref[:, h].reshape(tm, KV_RANK), wuv_ref[h],
            preferred_element_type=F32).astype(BF16)
    o_a = jnp.dot(a_scr[...], wa_ref[...], preferred_element_type=F32)
    o_b = jnp.dot(yn_ref[...], wb_ref[...], preferred_element_type=F32)
    o_c = jnp.dot(cm_ref[...], wc_ref[...], preferred_element_type=F32)
    g = g_ref[...].astype(F32)
    merged = (_sigmoid(g[:, :d]) * o_a + _sigmoid(g[:, d:2 * d]) * o_b + _sigmoid(g[:, 2 * d:]) * o_c)
    y = alpha * x_ref[...] + jnp.dot(merged.astype(BF16), wo_ref[...], preferred_element_type=F32)
    xo_ref[...] = _layer_norm(y, lg_ref[...], lb_ref[...])


def merge_ln(ol, yn, cm, g, x, wuv, wa, wb, wc, wo, ln_g, ln_b, alpha):
    t, d = x.shape
    tm = min(MERGE_TM, t)
    tq = ol.shape[2]
    row = lambda n: pl.BlockSpec((tm, n), lambda i: (i, 0))
    full = lambda a: pl.BlockSpec(a.shape, lambda i: (0,) * a.ndim, pipeline_mode=pl.Buffered(1))
    return pl.pallas_call(
        functools.partial(_merge_kernel, alpha=alpha),
        grid=(t // tm,),
        in_specs=[pl.BlockSpec((tm // tq, MLA_HEADS, tq, KV_RANK), lambda i: (i, 0, 0, 0)),
                  row(yn.shape[1]), row(cm.shape[1]), row(g.shape[1]), row(d),
                  full(wuv), full(wa), full(wb), full(wc), full(wo), full(ln_g), full(ln_b)],
        out_specs=row(d),
        out_shape=jax.ShapeDtypeStruct((t, d), F32),
        scratch_shapes=[pltpu.VMEM((tm, MLA_HEADS * V_DIM), BF16)],
        compiler_params=_cp("parallel"),
        name="merge_ln",
    )(ol, yn, cm, g, x, wuv, wa, wb, wc, wo, ln_g, ln_b)


def _first_max(v, expert):
    m = jnp.max(v, axis=0, keepdims=True)
    idx = jnp.min(jnp.where(v == m, expert, N_EXPERTS), axis=0, keepdims=True)
    return m, idx


def _route_kernel(x_ref, rwh_ref, rwl_ref, rbt_ref, info_ref, infot_ref, tinfo_ref, cnt_ref, carry_scr):
    tm = x_ref.shape[0]

    @pl.when(pl.program_id(0) == 0)
    def _():
        carry_scr[...] = jnp.zeros(carry_scr.shape, F32)

    x = x_ref[...]
    x_hi = x.astype(BF16)
    x_lo = (x - x_hi.astype(F32)).astype(BF16)
    logits = (jnp.dot(x_hi, rwh_ref[...], preferred_element_type=F32)
              + jnp.dot(x_lo, rwh_ref[...], preferred_element_type=F32)
              + jnp.dot(x_hi, rwl_ref[...], preferred_element_type=F32))
    scores = _sigmoid(logits.T[:N_EXPERTS, :])
    sel = scores + jnp.tile(rbt_ref[...], (1, tm // LANES))
    expert = lax.broadcasted_iota(I32, (N_EXPERTS, tm), 0)
    neg = -jnp.inf
    best_score = None
    best_group = None
    for j in range(N_EXPERT_GROUPS):
        in_j = (expert >= j * EXPERTS_PER_GROUP) & (expert < (j + 1) * EXPERTS_PER_GROUP)
        v = jnp.where(in_j, sel, neg)
        m1, i1 = _first_max(v, expert)
        m2, _ = _first_max(jnp.where(expert == i1, neg, v), expert)
        gs = m1 + m2
        if j == 0:
            best_score, best_group = gs, jnp.zeros_like(i1)
        else:
            better = gs > best_score
            best_score = jnp.where(better, gs, best_score)
            best_group = jnp.where(better, j, best_group)
    lo = best_group * EXPERTS_PER_GROUP
    v = jnp.where((expert >= lo) & (expert < lo + EXPERTS_PER_GROUP), sel, neg)
    _, e1 = _first_max(v, expert)
    _, e2 = _first_max(jnp.where(expert == e1, neg, v), expert)
    w1 = jnp.sum(jnp.where(expert == e1, scores, 0.0), axis=0, keepdims=True)
    w2 = jnp.sum(jnp.where(expert == e2, scores, 0.0), axis=0, keepdims=True)
    wsum = w1 + w2
    member = jnp.where(expert == e1, 1.0, jnp.where(expert == e2, 1.0, 0.0))
    row = lax.broadcasted_iota(I32, (tm, tm), 0)
    col = lax.broadcasted_iota(I32, (tm, tm), 1)
    earlier = jnp.where(row < col, 1.0, 0.0).astype(BF16)
    lrank = jnp.dot(member.astype(BF16), earlier, preferred_element_type=F32)
    n_col = jnp.sum(member, axis=1, keepdims=True)
    diag = (lax.broadcasted_iota(I32, (N_EXPERTS, LANES), 0)
            == lax.broadcasted_iota(I32, (N_EXPERTS, LANES), 1))
    n = jnp.sum(jnp.where(diag, n_col, 0.0), axis=0, keepdims=True)
    n8 = jnp.floor((n + (ROW_ALIGN - 1)) * (1.0 / ROW_ALIGN)) * ROW_ALIGN
    ua = lax.broadcasted_iota(I32, (LANES, LANES), 0)
    ub = lax.broadcasted_iota(I32, (LANES, LANES), 1)
    lower_experts = jnp.where(ua < ub, 1.0, 0.0).astype(BF16)
    n8_rows = jnp.broadcast_to(n8, (8, LANES))
    block_start = jnp.dot(n8_rows.astype(BF16), lower_experts, preferred_element_type=F32)
    start_col = jnp.sum(jnp.where(diag, block_start[0:1, :], 0.0), axis=1, keepdims=True)
    local_row = start_col + lrank
    j1 = jnp.sum(jnp.where(expert == e1, local_row, 0.0), axis=0, keepdims=True)
    j2 = jnp.sum(jnp.where(expert == e2, local_row, 0.0), axis=0, keepdims=True)
    carry = carry_scr[...]
    sub = lax.broadcasted_iota(I32, (8, LANES), 0)
    tinfo_ref[0] = jnp.where(sub == 0, n8_rows, jnp.where(sub == 1, block_start, jnp.where(sub == 2, carry, 0.0)))
    new_carry = carry + n8_rows
    carry_scr[...] = new_carry
    cnt_ref[...] = new_carry
    field = lax.broadcasted_iota(I32, (8, tm), 0)
    info_t = jnp.where(field == 0, e1.astype(F32),
             jnp.where(field == 1, e2.astype(F32),
             jnp.where(field == 2, w1 / wsum,
             jnp.where(field == 3, w2 / wsum,
             jnp.where(field == 4, j1,
             jnp.where(field == 5, j2, 0.0))))))
    infot_ref[0] = info_t
    info_ref[...] = jnp.concatenate([info_t, jnp.zeros((LANES - 8, tm), F32)], axis=0).T


def route(x, rw_hi, rw_lo, rb):
    t, d = x.shape
    tm = min(DISPATCH_TM, t)
    return pl.pallas_call(
        _route_kernel,
        grid=(t // tm,),
        in_specs=[pl.BlockSpec((tm, d), lambda i: (i, 0)),
                  pl.BlockSpec((d, LANES), lambda i: (0, 0)),
                  pl.BlockSpec((d, LANES), lambda i: (0, 0)),
                  pl.BlockSpec((N_EXPERTS, LANES), lambda i: (0, 0))],
        out_specs=[pl.BlockSpec((tm, LANES), lambda i: (i, 0)),
                   pl.BlockSpec((1, 8, tm), lambda i: (i, 0, 0)),
                   pl.BlockSpec((1, 8, LANES), lambda i: (i, 0, 0)),
                   pl.BlockSpec((8, LANES), lambda i: (0, 0))],
        out_shape=[jax.ShapeDtypeStruct((t, LANES), F32),
                   jax.ShapeDtypeStruct((t // tm, 8, tm), F32),
                   jax.ShapeDtypeStruct((t // tm, 8, LANES), F32),
                   jax.ShapeDtypeStruct((8, LANES), F32)],
        scratch_shapes=[pltpu.VMEM((8, LANES), F32)],
        compiler_params=_cp("arbitrary"),
        name="route",
    )(x, rw_hi, rw_lo, rb)


def _block_copies(plan_ref, local_ref, sorted_ref, sem, to_sorted, start):
    for e in range(N_EXPERTS):
        count = plan_ref[0, 0, e]
        loc0 = plan_ref[0, 0, N_EXPERTS + e]
        dst0 = plan_ref[0, 0, 2 * N_EXPERTS + e]

        def body(k, carry, loc0=loc0, dst0=dst0):
            loc = local_ref.at[pl.ds(pl.multiple_of(loc0 + k * ROW_ALIGN, ROW_ALIGN), ROW_ALIGN)]
            srt = sorted_ref.at[pl.ds(pl.multiple_of(dst0 + k * ROW_ALIGN, ROW_ALIGN), ROW_ALIGN)]
            cp = pltpu.make_async_copy(loc, srt, sem) if to_sorted else pltpu.make_async_copy(srt, loc, sem)
            if start:
                cp.start()
            else:
                cp.wait()
            return carry

        lax.fori_loop(0, count, body, 0)


def _dispatch_kernel(plan_ref, prev_plan_ref, infot_ref, x_ref, init_ref, o_ref, local_scr, sem):
    del init_ref
    tm = x_ref.shape[0]
    i = pl.program_id(0)
    slot = i % 2
    j1 = infot_ref[0, 4:5, :]
    j2 = infot_ref[0, 5:6, :]
    rowid = lax.broadcasted_iota(I32, (LOCAL_ROWS, tm), 0).astype(F32)
    sel = jnp.where(rowid == j1, 1.0, jnp.where(rowid == j2, 1.0, 0.0)).astype(BF16)
    local_scr[slot] = jnp.dot(sel, x_ref[...].astype(BF16), preferred_element_type=F32).astype(BF16)

    @pl.when(i > 0)
    def _():
        _block_copies(prev_plan_ref, local_scr.at[1 - slot], o_ref, sem.at[1 - slot], True, False)

    _block_copies(plan_ref, local_scr.at[slot], o_ref, sem.at[slot], True, True)

    @pl.when(i == pl.num_programs(0) - 1)
    def _():
        _block_copies(plan_ref, local_scr.at[slot], o_ref, sem.at[slot], True, False)


def dispatch_rows(x, info_t, plan, n_rows):
    t, d = x.shape
    tm = min(DISPATCH_TM, t)
    init = jnp.zeros((n_rows, d), BF16)
    return pl.pallas_call(
        _dispatch_kernel,
        grid=(t // tm,),
        in_specs=[pl.BlockSpec((1, 1, 3 * N_EXPERTS), lambda i: (i, 0, 0), memory_space=pltpu.SMEM),
                  pl.BlockSpec((1, 1, 3 * N_EXPERTS), lambda i: (jnp.maximum(i - 1, 0), 0, 0),
                               memory_space=pltpu.SMEM),
                  pl.BlockSpec((1, 8, tm), lambda i: (i, 0, 0)),
                  pl.BlockSpec((tm, d), lambda i: (i, 0)),
                  pl.BlockSpec(memory_space=pl.ANY)],
        out_specs=pl.BlockSpec(memory_space=pl.ANY),
        out_shape=jax.ShapeDtypeStruct((n_rows, d), BF16),
        scratch_shapes=[pltpu.VMEM((2, LOCAL_ROWS, d), BF16), pltpu.SemaphoreType.DMA((2,))],
        input_output_aliases={4: 0},
        compiler_params=_cp("arbitrary"),
        name="moe_dispatch",
    )(plan, plan, info_t, x, init)


def _expert_kernel(te_ref, nu_ref, xs_ref, w1_ref, w3_ref, w2_ref, o_ref, w13_scr, w2_scr):
    i = pl.program_id(0)
    f = w2_ref.shape[2]
    new_expert = (i == 0) | (te_ref[i] != te_ref[jnp.maximum(i - 1, 0)])

    @pl.when((i < nu_ref[0]) & new_expert)
    def _():
        w13_scr[:, :f] = w1_ref[0, 0].astype(BF16)
        w13_scr[:, f:] = w3_ref[0, 0].astype(BF16)
        w2_scr[...] = w2_ref[0, 0].astype(BF16)

    @pl.when(i < nu_ref[0])
    def _():
        h = jnp.dot(xs_ref[...], w13_scr[...], preferred_element_type=F32)
        h1 = h[:, :f]
        act = (h1 * _sigmoid(h1) * h[:, f:]).astype(BF16)
        o_ref[...] = jnp.dot(act, w2_scr[...], preferred_element_type=F32).astype(o_ref.dtype)

    @pl.when(i >= nu_ref[0])
    def _():
        o_ref[...] = jnp.zeros(o_ref.shape, o_ref.dtype)


def expert_ffn(xs, w1, w3, w2, layer, tile_expert, n_used):
    p, d = xs.shape
    f = w2.shape[2]
    n_tiles = p // MOE_TM
    grid_spec = pltpu.PrefetchScalarGridSpec(
        num_scalar_prefetch=2,
        grid=(n_tiles,),
        in_specs=[pl.BlockSpec((MOE_TM, d), lambda i, te, nu: (i, 0)),
                  pl.BlockSpec((1, 1, d, f), lambda i, te, nu: (layer, te[i], 0, 0)),
                  pl.BlockSpec((1, 1, d, f), lambda i, te, nu: (layer, te[i], 0, 0)),
                  pl.BlockSpec((1, 1, f, d), lambda i, te, nu: (layer, te[i], 0, 0))],
        out_specs=pl.BlockSpec((MOE_TM, d), lambda i, te, nu: (i, 0)),
        scratch_shapes=[pltpu.VMEM((d, 2 * f), BF16), pltpu.VMEM((f, d), BF16)],
    )
    return pl.pallas_call(
        _expert_kernel,
        grid_spec=grid_spec,
        out_shape=jax.ShapeDtypeStruct((p, d), BF16),
        compiler_params=_cp("arbitrary"),
        name="expert_ffn",
    )(tile_expert, n_used, xs, w1, w3, w2)


def _combine_kernel(plan_ref, next_plan_ref, info_ref, x_ref, ys_ref, lg_ref, lb_ref, xo_ref, xb_ref,
                    local_scr, sem, *, alpha):
    tm = x_ref.shape[0]
    i = pl.program_id(0)
    slot = i % 2

    @pl.when(i == 0)
    def _():
        _block_copies(plan_ref, local_scr.at[slot], ys_ref, sem.at[slot], False, True)

    @pl.when(i + 1 < pl.num_programs(0))
    def _():
        _block_copies(next_plan_ref, local_scr.at[1 - slot], ys_ref, sem.at[1 - slot], False, True)

    _block_copies(plan_ref, local_scr.at[slot], ys_ref, sem.at[slot], False, False)
    last = N_EXPERTS - 1
    used = plan_ref[0, 0, N_EXPERTS + last] + plan_ref[0, 0, last] * ROW_ALIGN
    rowid = lax.broadcasted_iota(I32, (LOCAL_ROWS, 1), 0)
    y = local_scr[slot]
    y = jnp.where(rowid < used, y, jnp.zeros_like(y))
    info = info_ref[...]
    col = lax.broadcasted_iota(I32, (tm, LOCAL_ROWS), 1).astype(F32)
    gate = jnp.where(col == info[:, 4:5], info[:, 2:3],
                     jnp.where(col == info[:, 5:6], info[:, 3:4], 0.0)).astype(BF16)
    moe = jnp.dot(gate, y, preferred_element_type=F32)
    out = _layer_norm(alpha * x_ref[...] + moe, lg_ref[...], lb_ref[...])
    xo_ref[...] = out
    xb_ref[...] = out.astype(BF16)


def combine_ln(plan, info, x, ys, ln_g, ln_b, alpha):
    t, d = x.shape
    tm = min(DISPATCH_TM, t)
    return pl.pallas_call(
        functools.partial(_combine_kernel, alpha=alpha),
        grid=(t // tm,),
        in_specs=[pl.BlockSpec((1, 1, 3 * N_EXPERTS), lambda i: (i, 0, 0), memory_space=pltpu.SMEM),
                  pl.BlockSpec((1, 1, 3 * N_EXPERTS), lambda i: (jnp.minimum(i + 1, t // tm - 1), 0, 0),
                               memory_space=pltpu.SMEM),
                  pl.BlockSpec((tm, LANES), lambda i: (i, 0)),
                  pl.BlockSpec((tm, d), lambda i: (i, 0)),
                  pl.BlockSpec(memory_space=pl.ANY),
                  pl.BlockSpec((1, d), lambda i: (0, 0)),
                  pl.BlockSpec((1, d), lambda i: (0, 0))],
        out_specs=[pl.BlockSpec((tm, d), lambda i: (i, 0)), pl.BlockSpec((tm, d), lambda i: (i, 0))],
        out_shape=[jax.ShapeDtypeStruct((t, d), F32), jax.ShapeDtypeStruct((t, d), BF16)],
        scratch_shapes=[pltpu.VMEM((2, LOCAL_ROWS, d), BF16), pltpu.SemaphoreType.DMA((2,))],
        compiler_params=_cp("arbitrary"),
        name="moe_combine_ln",
    )(plan, plan, info, x, ys, ln_g, ln_b)


def _prep_layer(l, p):
    w_in = p["w_in"][l]
    d = w_in.shape[0]
    o_dq = 0
    o_dkv = o_dq + Q_RANK
    o_z = o_dkv + KV_RANK + ROPE_DIM
    d_inner = SSM_HEADS * SSM_HEAD_DIM
    conv_ch = d_inner + 2 * SSM_GROUPS * D_STATE
    o_xbc = o_z + d_inner
    o_dt = o_xbc + conv_ch
    o_qm = o_dt + SSM_HEADS
    o_g = o_qm + XA_HEADS * XA_HEAD_DIM
    half = ROPE_DIM // 2
    kr0 = o_dkv + KV_RANK
    zeros = lambda n: jnp.zeros((d, n), F32)
    w_small = jnp.concatenate([
        w_in[:, o_dq:o_dq + Q_RANK],
        w_in[:, o_dkv:o_dkv + KV_RANK],
        w_in[:, kr0:kr0 + ROPE_DIM], zeros(LANES - ROPE_DIM),
        w_in[:, kr0 + half:kr0 + ROPE_DIM], w_in[:, kr0:kr0 + half], zeros(LANES - ROPE_DIM),
        w_in[:, o_dt:o_dt + SSM_HEADS], zeros(LANES - SSM_HEADS)], axis=1)
    w_uq = p["w_uq"][l].reshape(Q_RANK, MLA_HEADS, NOPE_DIM + ROPE_DIM)
    wq_nope = w_uq[:, :, :NOPE_DIM].reshape(Q_RANK, -1)
    wq_rope = w_uq[:, :, NOPE_DIM:]
    wq_rope_sw = jnp.concatenate([wq_rope[:, :, half:], wq_rope[:, :, :half]], axis=-1)
    wq = jnp.concatenate([wq_nope, wq_rope.reshape(Q_RANK, -1), wq_rope_sw.reshape(Q_RANK, -1)], axis=1)
    w_ukv = p["w_ukv"][l]
    wuk_t = jnp.transpose(w_ukv[:, :, :NOPE_DIM], (1, 2, 0))
    wuv = jnp.transpose(w_ukv[:, :, NOPE_DIM:], (1, 0, 2))
    pad_heads = lambda v, fill: jnp.concatenate(
        [v.astype(F32), jnp.full((LANES - SSM_HEADS,), fill, F32)]).reshape(1, LANES)
    bf = lambda a: a.astype(BF16)
    return dict(
        w_small=bf(w_small), w_z=bf(w_in[:, o_z:o_z + d_inner]), w_xbc=bf(w_in[:, o_xbc:o_xbc + conv_ch]),
        w_qm=bf(w_in[:, o_qm:o_g]), w_g=bf(w_in[:, o_g:]),
        q_norm=p["q_norm"][l].reshape(1, -1), kv_norm=p["kv_norm"][l].reshape(1, -1),
        wq=bf(wq), wuk_t=bf(wuk_t), wuv=bf(wuv),
        conv_w=p["conv_w"][l], conv_b=p["conv_b"][l].reshape(1, -1),
        dt_bias=pad_heads(p["dt_bias"][l], 0.0),
        a_neg=pad_heads(-jnp.exp(p["a_log"][l].astype(F32)), 0.0),
        d_skip=jnp.repeat(p["d_skip"][l].astype(F32), SSM_HEAD_DIM).reshape(1, -1),
        ssm_norm=p["ssm_norm"][l].reshape(1, -1),
        w_mem_kv=bf(p["w_mem_kv"][l]),
        wa=bf(p["w_proj_a"][l]), wb=bf(p["w_proj_b"][l]), wc=bf(p["w_proj_c"][l]), wo=bf(p["w_out"][l]),
        ln1_g=p["ln1_g"][l].reshape(1, -1), ln1_b=p["ln1_b"][l].reshape(1, -1),
        ln2_g=p["ln2_g"][l].reshape(1, -1), ln2_b=p["ln2_b"][l].reshape(1, -1),
    )


def _dispatch_plan(tinfo, counts, t):
    n_tok_tiles = tinfo.shape[0]
    n8 = tinfo[:, 0, :N_EXPERTS].astype(I32)
    block_start = tinfo[:, 1, :N_EXPERTS].astype(I32)
    carry = tinfo[:, 2, :N_EXPERTS].astype(I32)
    rows = counts[0, :N_EXPERTS].astype(I32)
    tiles = (rows + MOE_TM - 1) // MOE_TM
    tile_end = jnp.cumsum(tiles)
    row_start = (tile_end - tiles) * MOE_TM
    plan = jnp.concatenate([n8 // ROW_ALIGN, block_start, row_start[None, :] + carry], axis=1)
    max_rows = TOP_K * t + N_EXPERTS * (ROW_ALIGN - 1) * n_tok_tiles
    n_tiles = -(-max_rows // MOE_TM) + N_EXPERTS
    tile_ids = jnp.arange(n_tiles, dtype=I32)
    tile_expert = jnp.minimum(jnp.sum(tile_ids[:, None] >= tile_end[None, :], axis=1), N_EXPERTS - 1)
    n_used = tile_end[-1:].astype(I32)
    return plan.reshape(n_tok_tiles, 1, 3 * N_EXPERTS), tile_expert.astype(I32), n_used, n_tiles * MOE_TM


def kernel(x, mem, positions, w_in, q_norm, w_uq, kv_norm, w_ukv, w_proj_a, conv_w, conv_b, dt_bias, a_log,
           d_skip, ssm_norm, w_proj_b, w_mem_kv, w_proj_c, w_out, ln1_g, ln1_b, router_w, router_bias,
           exp_w1, exp_w3, exp_w2, ln2_g, ln2_b):
    params = dict(w_in=w_in, q_norm=q_norm, w_uq=w_uq, kv_norm=kv_norm, w_ukv=w_ukv, w_proj_a=w_proj_a,
                  conv_w=conv_w, conv_b=conv_b, dt_bias=dt_bias, a_log=a_log, d_skip=d_skip,
                  ssm_norm=ssm_norm, w_proj_b=w_proj_b, w_mem_kv=w_mem_kv, w_proj_c=w_proj_c, w_out=w_out,
                  ln1_g=ln1_g, ln1_b=ln1_b, exp_w1=exp_w1, exp_w3=exp_w3, exp_w2=exp_w2,
                  ln2_g=ln2_g, ln2_b=ln2_b)
    bsz, s, d = x.shape
    t = bsz * s
    depth = w_in.shape[0]
    alpha = float((2 * depth) ** 0.25)
    n_mem = mem.shape[1]

    inv = ROPE_THETA ** (-jnp.arange(0, ROPE_DIM, 2, dtype=F32) / ROPE_DIM)
    ang = positions.astype(F32)[..., None] * inv
    cos, sin = jnp.cos(ang), jnp.sin(ang)
    cosq = jnp.tile(jnp.concatenate([cos, cos], axis=-1), (1, 1, LANES // ROPE_DIM)).reshape(t, LANES)
    sinq = jnp.tile(jnp.concatenate([-sin, sin], axis=-1), (1, 1, LANES // ROPE_DIM)).reshape(t, LANES)

    rw = jnp.concatenate([router_w.astype(F32), jnp.zeros((d, LANES - N_EXPERTS), F32)], axis=1)
    rw_hi = rw.astype(BF16)
    rw_lo = (rw - rw_hi.astype(F32)).astype(BF16)
    rbt = jnp.broadcast_to(router_bias.astype(F32)[:, None], (N_EXPERTS, LANES))
    mem_b = mem.reshape(bsz * n_mem, d).astype(BF16)

    xf = x.reshape(t, d).astype(F32)
    xb = xf.astype(BF16)
    for l in range(depth):
        w = _prep_layer(l, params)
        ha = matmul(xb, w["w_small"], F32, "proj_small")
        z = matmul(xb, w["w_z"], BF16, "proj_z")
        xbc_c = proj_conv_silu(xb, w["w_xbc"], w["conv_w"], w["conv_b"], s).reshape(bsz, s, -1)
        qm = matmul(xb, w["w_qm"], BF16, "proj_qmem")
        g = matmul(xb, w["w_g"], BF16, "proj_gate")

        ql, qr, ck, kr = mla_prep(ha, cosq, sinq, w["q_norm"], w["kv_norm"], w["wq"], w["wuk_t"],
                                  min(ATT_TQ, s))
        o_lat = mla_attention(ql, qr, ck.reshape(bsz, s, KV_RANK), kr.reshape(bsz, s, ROPE_DIM))

        yn = ssd(xbc_c, z.reshape(bsz, s, -1), ha.reshape(bsz, s, HA_W), w["dt_bias"], w["a_neg"],
                 w["d_skip"], w["ssm_norm"]).reshape(t, -1)

        kv = matmul(mem_b, w["w_mem_kv"], BF16, "proj_memkv").reshape(bsz, n_mem, -1)
        cm = mem_attention(qm.reshape(bsz, s, -1), kv).reshape(t, -1)

        x1 = merge_ln(o_lat, yn, cm, g, xf, w["wuv"], w["wa"], w["wb"], w["wc"], w["wo"],
                           w["ln1_g"], w["ln1_b"], alpha)

        info, info_t, tinfo, counts = route(x1, rw_hi, rw_lo, rbt)
        plan, tile_expert, n_used, n_rows = _dispatch_plan(tinfo, counts, t)
        xs = dispatch_rows(x1, info_t, plan, n_rows)
        ys = expert_ffn(xs, exp_w1, exp_w3, exp_w2, l, tile_expert, n_used)
        xf, xb = combine_ln(plan, info, x1, ys, w["ln2_g"], w["ln2_b"], alpha)
    return xf.reshape(bsz, s, d)
```

```python
import functools

import jax
import jax.numpy as jnp
from jax import lax
from jax.experimental import pallas as pl
from jax.experimental.pallas import tpu as pltpu

F32 = jnp.float32
BF16 = jnp.bfloat16
I32 = jnp.int32

MLA_HEADS = 8
Q_RANK = 384
KV_RANK = 256
NOPE_DIM = 128
ROPE_DIM = 64
V_DIM = 128
ROPE_THETA = 10000.0
SSM_HEADS = 32
SSM_HEAD_DIM = 64
SSM_GROUPS = 8
D_STATE = 128
CONV_K = 4
XA_HEADS = 4
XA_HEAD_DIM = 256
N_EXPERTS = 16
N_EXPERT_GROUPS = 4
EXPERTS_PER_GROUP = 4
TOP_K = 2
NORM_EPS = 1e-5
RMS_EPS = 1e-6

LANES = 128
V7X_VMEM_LIMIT = 56 * 1024 * 1024

MM_TM = 2048
MM_TN = 1024
PREP_TM = 1024
ATT_TQ = 128
ATT_TK = 512
ATT_GROUP_ROWS = 512
ATT_UNROLL = 2
LOG2E = 1.4426950408889634
PROJ_CONV_TM = 1024
PROJ_CONV_TN = 1024
PROJ_CONV_SUB = 256
CONV_HALO = 16
SSD_CHUNK = 256
XA_TQ = 512
MERGE_TM = 512
MOE_TM = 512
DISPATCH_TM = 512
ROW_ALIGN = 16
LOCAL_ROWS = 1280

HA_W = 1024
HA_DQ = 0
HA_C = 384
HA_KR = 640
HA_KRS = 768
HA_DT = 896


def _cp(*sem):
    return pltpu.CompilerParams(dimension_semantics=sem, vmem_limit_bytes=V7X_VMEM_LIMIT)


def _sigmoid(x):
    return 1.0 / (1.0 + jnp.exp(-x))


def _mm_kernel(x_ref, w_ref, o_ref):
    o_ref[...] = jnp.dot(x_ref[...], w_ref[...], preferred_element_type=F32).astype(o_ref.dtype)


def matmul(x, w, out_dtype, name):
    m, k = x.shape
    n = w.shape[1]
    tm = min(MM_TM, m)
    tn = min(MM_TN, n)
    return pl.pallas_call(
        _mm_kernel,
        grid=(n // tn, m // tm),
        in_specs=[pl.BlockSpec((tm, k), lambda j, i: (i, 0)),
                  pl.BlockSpec((k, tn), lambda j, i: (0, j))],
        out_specs=pl.BlockSpec((tm, tn), lambda j, i: (i, j)),
        out_shape=jax.ShapeDtypeStruct((m, n), out_dtype),
        compiler_params=_cp("parallel", "parallel"),
        name=name,
    )(x, w)


def _proj_conv_kernel(x_ref, xh_ref, w_ref, cw_ref, cb_ref, o_ref, *, tiles_per_seq):
    first = (pl.program_id(1) % tiles_per_seq) == 0
    tn = o_ref.shape[1]
    sub = min(PROJ_CONV_SUB, tn)
    def project(c):
        w = w_ref[:, c * sub:(c + 1) * sub]
        u = jnp.dot(x_ref[...], w, preferred_element_type=F32)
        halo = jnp.dot(xh_ref[...], w, preferred_element_type=F32)
        return jnp.concatenate([jnp.where(first, jnp.zeros_like(halo), halo), u], axis=0)

    n_sub = tn // sub
    nxt = project(0)
    for c in range(n_sub):
        cs = slice(c * sub, (c + 1) * sub)
        cw = cw_ref[:, cs]
        ext = nxt
        if c + 1 < n_sub:
            nxt = project(c + 1)
        prev = pltpu.roll(ext, 1, 0)
        near = cw[3:4, :] * ext + cw[2:3, :] * prev
        far = cw[1:2, :] * ext + cw[0:1, :] * prev
        half = (near + pltpu.roll(far, 2, 0))[CONV_HALO:, :] + cb_ref[:, cs]
        o_ref[:, cs] = (half * jnp.tanh(half) + half).astype(o_ref.dtype)


def proj_conv_silu(x, w, conv_w, conv_b, seq_len):
    m, k = x.shape
    n = w.shape[1]
    tm = min(PROJ_CONV_TM, seq_len)
    tn = min(PROJ_CONV_TN, n)
    hb = tm // CONV_HALO
    return pl.pallas_call(
        functools.partial(_proj_conv_kernel, tiles_per_seq=seq_len // tm),
        grid=(n // tn, m // tm),
        in_specs=[pl.BlockSpec((tm, k), lambda j, i: (i, 0)),
                  pl.BlockSpec((CONV_HALO, k), lambda j, i: (jnp.maximum(i * hb - 1, 0), 0)),
                  pl.BlockSpec((k, tn), lambda j, i: (0, j)),
                  pl.BlockSpec((CONV_K, tn), lambda j, i: (0, j)),
                  pl.BlockSpec((1, tn), lambda j, i: (0, j))],
        out_specs=pl.BlockSpec((tm, tn), lambda j, i: (i, j)),
        out_shape=jax.ShapeDtypeStruct((m, n), BF16),
        compiler_params=_cp("parallel", "parallel"),
        name="proj_xbc_conv",
    )(x, x, w, conv_w, conv_b)


def _mla_prep_kernel(ha_ref, cos_ref, sin_ref, qn_ref, kvn_ref, wq_ref, wuk_ref,
                     ql_ref, qr_ref, ck_ref, kr_ref, *, scale):
    nb, _, tq, _ = ql_ref.shape
    ha = ha_ref[...]
    dq = ha[:, HA_DQ:HA_DQ + Q_RANK]
    c_q = dq * lax.rsqrt(jnp.mean(dq * dq, axis=-1, keepdims=True) + RMS_EPS) * qn_ref[...]
    q = jnp.dot(c_q.astype(BF16), wq_ref[...], preferred_element_type=F32)
    n_nope = MLA_HEADS * NOPE_DIM
    n_rope = MLA_HEADS * ROPE_DIM
    cosq = jnp.tile(cos_ref[...], (1, n_rope // LANES))
    sinq = jnp.tile(sin_ref[...], (1, n_rope // LANES))
    q_rope = ((q[:, n_nope:n_nope + n_rope] * cosq + q[:, n_nope + n_rope:] * sinq) * scale).astype(BF16)
    for h in range(MLA_HEADS):
        qh = q[:, h * NOPE_DIM:(h + 1) * NOPE_DIM].astype(BF16)
        ql = jnp.dot(qh, wuk_ref[h], preferred_element_type=F32)
        ql_ref[:, h] = (ql * scale).astype(BF16).reshape(nb, tq, KV_RANK)
        qr_ref[:, h] = q_rope[:, h * ROPE_DIM:(h + 1) * ROPE_DIM].reshape(nb, tq, ROPE_DIM)
    c = ha[:, HA_C:HA_C + KV_RANK]
    c_kv = c * lax.rsqrt(jnp.mean(c * c, axis=-1, keepdims=True) + RMS_EPS) * kvn_ref[...]
    ck_ref[...] = c_kv.astype(BF16)
    k_rope = (ha[:, HA_KR:HA_KR + LANES] * cos_ref[...]
              + ha[:, HA_KRS:HA_KRS + LANES] * sin_ref[...])
    kr_ref[...] = k_rope[:, :ROPE_DIM].astype(BF16)


def mla_prep(ha, cosq, sinq, q_norm, kv_norm, wq, wuk_t, tq):
    t = ha.shape[0]
    tm = min(PREP_TM, t)
    nb = tm // tq
    scale = float((NOPE_DIM + ROPE_DIM) ** -0.5 * LOG2E)
    n_rope = MLA_HEADS * ROPE_DIM
    full = lambda shape: pl.BlockSpec(shape, lambda i: (0,) * len(shape))
    return pl.pallas_call(
        functools.partial(_mla_prep_kernel, scale=scale),
        grid=(t // tm,),
        in_specs=[pl.BlockSpec((tm, HA_W), lambda i: (i, 0)),
                  pl.BlockSpec((tm, LANES), lambda i: (i, 0)),
                  pl.BlockSpec((tm, LANES), lambda i: (i, 0)),
                  full((1, Q_RANK)), full((1, KV_RANK)),
                  full(wq.shape), full(wuk_t.shape)],
        out_specs=[pl.BlockSpec((nb, MLA_HEADS, tq, KV_RANK), lambda i: (i, 0, 0, 0)),
                   pl.BlockSpec((nb, MLA_HEADS, tq, ROPE_DIM), lambda i: (i, 0, 0, 0)),
                   pl.BlockSpec((tm, KV_RANK), lambda i: (i, 0)),
                   pl.BlockSpec((tm, ROPE_DIM), lambda i: (i, 0))],
        out_shape=[jax.ShapeDtypeStruct((t // tq, MLA_HEADS, tq, KV_RANK), BF16),
                   jax.ShapeDtypeStruct((t // tq, MLA_HEADS, tq, ROPE_DIM), BF16),
                   jax.ShapeDtypeStruct((t, KV_RANK), BF16),
                   jax.ShapeDtypeStruct((t, ROPE_DIM), BF16)],
        compiler_params=_cp("parallel"),
        name="mla_prep",
    )(ha, cosq, sinq, q_norm, kv_norm, wq, wuk_t)


def _mla_attn_kernel(ql_ref, qr_ref, ck_ref, kr_ref, o_ref, m_scr, l_scr, acc_scr, *, tq, tk, hpc):
    rc = hpc * tq
    n_groups = MLA_HEADS // hpc
    q_start = pl.program_id(1) * tq
    n_full = q_start // tk
    m_scr[...] = jnp.full(m_scr.shape, -jnp.inf, F32)
    l_scr[...] = jnp.zeros(l_scr.shape, F32)
    acc_scr[...] = jnp.zeros(acc_scr.shape, F32)
    nt = (((1,), (1,)), ((), ()))

    def step(j, masked, width=tk):
        ks = pl.multiple_of(j * tk, tk)
        ck = ck_ref[0, pl.ds(ks, width), :]
        kr = kr_ref[0, pl.ds(ks, width), :]
        if masked:
            q_pos = q_start + (lax.broadcasted_iota(I32, (rc, width), 0) & (tq - 1))
            k_pos = ks + lax.broadcasted_iota(I32, (rc, width), 1)
            visible = k_pos <= q_pos
        for c in range(n_groups):
            rs = slice(c * rc, (c + 1) * rc)
            ql = ql_ref[0, c * hpc:(c + 1) * hpc].reshape(rc, KV_RANK)
            qr = qr_ref[0, c * hpc:(c + 1) * hpc].reshape(rc, ROPE_DIM)
            s = (lax.dot_general(ql, ck, nt, preferred_element_type=F32)
                 + lax.dot_general(qr, kr, nt, preferred_element_type=F32))
            if masked:
                s = jnp.where(visible, s, -jnp.inf)
            m_prev = m_scr[rs]
            m_new = jnp.maximum(m_prev, jnp.max(s, axis=-1, keepdims=True))
            alpha = jnp.exp2(m_prev - m_new)
            p = jnp.exp2(s - jnp.tile(m_new, (1, width // LANES)))
            l_scr[rs] = alpha * l_scr[rs] + jnp.sum(p, axis=-1, keepdims=True)
            acc_scr[rs] = (jnp.tile(alpha, (1, KV_RANK // LANES)) * acc_scr[rs]
                           + jnp.dot(p.astype(BF16), ck, preferred_element_type=F32))
            m_scr[rs] = m_new

    def trip(jj, carry):
        for u in range(ATT_UNROLL):
            step(ATT_UNROLL * jj + u, False)
        return carry

    n_trips = n_full // ATT_UNROLL
    lax.fori_loop(0, n_trips, trip, 0)

    def single(j, carry):
        step(j, False)
        return carry

    lax.fori_loop(n_trips * ATT_UNROLL, n_full, single, 0)

    sub = (q_start - n_full * tk) // tq
    for v in range(tk // tq):
        @pl.when(sub == v)
        def _(v=v):
            step(n_full, True, (v + 1) * tq)
    out = acc_scr[...] / jnp.tile(l_scr[...], (1, KV_RANK // LANES))
    o_ref[0] = out.astype(o_ref.dtype).reshape(MLA_HEADS, tq, KV_RANK)


def mla_attention(ql, qr, ck, kr):
    _, _, tq, _ = ql.shape
    b, s, _ = ck.shape
    tk = min(ATT_TK, s)
    assert tk % tq == 0 and s % tk == 0 and tq & (tq - 1) == 0
    nq = s // tq
    rows = tq * MLA_HEADS
    hpc = max(1, min(MLA_HEADS, ATT_GROUP_ROWS // tq))
    qspec = lambda dim: pl.BlockSpec((1, MLA_HEADS, tq, dim), lambda bi, i: (bi * nq + i, 0, 0, 0))
    return pl.pallas_call(
        functools.partial(_mla_attn_kernel, tq=tq, tk=tk, hpc=hpc),
        grid=(b, nq),
        in_specs=[qspec(KV_RANK), qspec(ROPE_DIM),
                  pl.BlockSpec((1, s, KV_RANK), lambda bi, i: (bi, 0, 0)),
                  pl.BlockSpec((1, s, ROPE_DIM), lambda bi, i: (bi, 0, 0))],
        out_specs=qspec(KV_RANK),
        out_shape=jax.ShapeDtypeStruct(ql.shape, BF16),
        scratch_shapes=[pltpu.VMEM((rows, LANES), F32), pltpu.VMEM((rows, LANES), F32),
                        pltpu.VMEM((rows, KV_RANK), F32)],
        compiler_params=_cp("parallel", "parallel"),
        name="mla_attention",
    )(ql, qr, ck, kr)


def _ssd_kernel(xbc_ref, z_ref, dt_ref, dtb_ref, a_ref, dsk_ref, ng_ref, exp_ref, o_ref,
                state_scr, *, chunk):
    d_inner = SSM_HEADS * SSM_HEAD_DIM
    gn = SSM_GROUPS * D_STATE
    rep = SSM_HEADS // SSM_GROUPS
    gw = rep * SSM_HEAD_DIM

    @pl.when(pl.program_id(1) == 0)
    def _():
        state_scr[...] = jnp.zeros(state_scr.shape, F32)

    x_raw = dt_ref[0] + dtb_ref[...]
    dt = jnp.maximum(x_raw, 0.0) + jnp.log(1.0 + jnp.exp(-jnp.abs(x_raw)))
    da = dt * a_ref[...]
    row = lax.broadcasted_iota(I32, (chunk, chunk), 0)
    col = lax.broadcasted_iota(I32, (chunk, chunk), 1)
    causal = row >= col
    tri = jnp.where(causal, 1.0, 0.0).astype(BF16)
    acum = jnp.zeros(da.shape, F32)
    rem = da
    for _ in range(3):
        part = rem.astype(BF16)
        acum = acum + jnp.dot(tri, part, preferred_element_type=F32)
        rem = rem - part.astype(F32)
    a2 = acum * LOG2E
    a2_t = a2.T
    src_t = a2_t - jnp.log2(dt.T)
    w_t = jnp.exp2(a2_t[:, chunk - 1:chunk] - src_t)
    e_end = jnp.exp2(a2[chunk - 1:chunk, :])
    e_all = jnp.dot(jnp.exp2(a2).astype(BF16), exp_ref[...], preferred_element_type=F32)
    head_of_lane = lax.broadcasted_iota(I32, (1, gw), 1) // SSM_HEAD_DIM

    for g in range(SSM_GROUPS):
        bg = xbc_ref[0, :, d_inner + g * D_STATE:d_inner + (g + 1) * D_STATE]
        cg = xbc_ref[0, :, d_inner + gn + g * D_STATE:d_inner + gn + (g + 1) * D_STATE]
        bt = bg.astype(F32).T
        cb = jnp.dot(cg, bt.astype(BF16), preferred_element_type=F32)
        st = state_scr[g]
        ys = jnp.dot(cg, st.astype(BF16), preferred_element_type=F32)
        xg = xbc_ref[0, :, g * gw:(g + 1) * gw].astype(F32)
        y = jnp.zeros((chunk, gw), F32)
        upd = jnp.zeros((D_STATE, gw), F32)
        sc = jnp.zeros((1, gw), F32)
        for r in range(rep):
            h = g * rep + r
            own = head_of_lane == r
            seg = a2[:, h:h + 1] - src_t[h:h + 1, :]
            m = (cb * jnp.exp2(jnp.where(causal, seg, -jnp.inf))).astype(BF16)
            xm = jnp.where(own, xg, 0.0).astype(BF16)
            y = y + jnp.dot(m, xm, preferred_element_type=F32)
            upd = upd + jnp.dot((bt * w_t[h:h + 1, :]).astype(BF16), xm, preferred_element_type=F32)
            sc = jnp.where(own, e_end[:, h:h + 1], sc)
        state_scr[g] = st * sc + upd
        gs = slice(g * gw, (g + 1) * gw)
        y = y + ys * e_all[:, gs] + xg * dsk_ref[:, gs]
        zg = z_ref[0, :, gs].astype(F32)
        y = y * (zg * _sigmoid(zg))
        y = y * lax.rsqrt(jnp.mean(y * y, axis=-1, keepdims=True) + RMS_EPS) * ng_ref[:, gs]
        o_ref[0, :, gs] = y.astype(o_ref.dtype)


def ssd(xbc, z, ha3, dt_bias, a_neg, d_skip, norm_g):
    bsz, s, c = xbc.shape
    d_inner = SSM_HEADS * SSM_HEAD_DIM
    chunk = min(SSD_CHUNK, s)
    gw = d_inner // SSM_GROUPS
    vec = lambda n: pl.BlockSpec((1, n), lambda bi, ci: (0, 0))
    expand = (jnp.arange(d_inner, dtype=I32)[None, :] // SSM_HEAD_DIM
              == jnp.arange(LANES, dtype=I32)[:, None]).astype(BF16)
    return pl.pallas_call(
        functools.partial(_ssd_kernel, chunk=chunk),
        grid=(bsz, s // chunk),
        in_specs=[pl.BlockSpec((1, chunk, c), lambda bi, ci: (bi, ci, 0)),
                  pl.BlockSpec((1, chunk, d_inner), lambda bi, ci: (bi, ci, 0)),
                  pl.BlockSpec((1, chunk, LANES), lambda bi, ci: (bi, ci, HA_DT // LANES)),
                  vec(LANES), vec(LANES), vec(d_inner), vec(d_inner),
                  pl.BlockSpec((LANES, d_inner), lambda bi, ci: (0, 0))],
        out_specs=pl.BlockSpec((1, chunk, d_inner), lambda bi, ci: (bi, ci, 0)),
        out_shape=jax.ShapeDtypeStruct((bsz, s, d_inner), BF16),
        scratch_shapes=[pltpu.VMEM((SSM_GROUPS, D_STATE, gw), F32)],
        compiler_params=_cp("parallel", "arbitrary"),
        name="ssd_scan",
    )(xbc, z, ha3, dt_bias, a_neg, d_skip, norm_g, expand)


def _xattn_kernel(q_ref, kv_ref, o_ref, *, scale):
    hd = XA_HEADS * XA_HEAD_DIM
    nt = (((1,), (1,)), ((), ()))
    for h in range(XA_HEADS):
        sl = slice(h * XA_HEAD_DIM, (h + 1) * XA_HEAD_DIM)
        q = q_ref[0, :, sl]
        k = kv_ref[0, :, sl]
        v = kv_ref[0, :, hd + h * XA_HEAD_DIM:hd + (h + 1) * XA_HEAD_DIM]
        s = lax.dot_general(q, k, nt, preferred_element_type=F32) * scale
        p = jnp.exp(s - jnp.max(s, axis=-1, keepdims=True))
        l = jnp.sum(p, axis=-1, keepdims=True)
        o = jnp.dot(p.astype(BF16), v, preferred_element_type=F32) / l
        o_ref[0, :, sl] = o.astype(o_ref.dtype)


def mem_attention(q, kv):
    bsz, s, hd = q.shape
    m = kv.shape[1]
    tq = min(XA_TQ, s)
    return pl.pallas_call(
        functools.partial(_xattn_kernel, scale=float(XA_HEAD_DIM ** -0.5)),
        grid=(bsz, s // tq),
        in_specs=[pl.BlockSpec((1, tq, hd), lambda bi, i: (bi, i, 0)),
                  pl.BlockSpec((1, m, 2 * hd), lambda bi, i: (bi, 0, 0))],
        out_specs=pl.BlockSpec((1, tq, hd), lambda bi, i: (bi, i, 0)),
        out_shape=jax.ShapeDtypeStruct((bsz, s, hd), BF16),
        compiler_params=_cp("parallel", "parallel"),
        name="mem_attention",
    )(q, kv)


def _layer_norm(v, g, b):
    mu = jnp.mean(v, axis=-1, keepdims=True)
    d = v - mu
    var = jnp.mean(d * d, axis=-1, keepdims=True)
    return d * lax.rsqrt(var + NORM_EPS) * g + b


def _merge_kernel(ol_ref, yn_ref, cm_ref, g_ref, x_ref, wuv_ref, wa_ref, wb_ref, wc_ref, wo_ref,
                  lg_ref, lb_ref, xo_ref, a_scr, *, alpha):
    tm, d = x_ref.shape
    for h in range(MLA_HEADS):
        a_scr[:, h * V_DIM:(h + 1) * V_DIM] = jnp.dot(
            ol_ref[:, h].reshape(tm, KV_RANK), wuv_ref[h],
            preferred_element_type=F32).astype(BF16)
    o_a = jnp.dot(a_scr[...], wa_ref[...], preferred_element_type=F32)
    o_b = jnp.dot(yn_ref[...], wb_ref[...], preferred_element_type=F32)
    o_c = jnp.dot(cm_ref[...], wc_ref[...], preferred_element_type=F32)
    g = g_ref[...].astype(F32)
    merged = (_sigmoid(g[:, :d]) * o_a + _sigmoid(g[:, d:2 * d]) * o_b + _sigmoid(g[:, 2 * d:]) * o_c)
    y = alpha * x_ref[...] + jnp.dot(merged.astype(BF16), wo_ref[...], preferred_element_type=F32)
    xo_ref[...] = _layer_norm(y, lg_ref[...], lb_ref[...])


def merge_ln(ol, yn, cm, g, x, wuv, wa, wb, wc, wo, ln_g, ln_b, alpha):
    t, d = x.shape
    tm = min(MERGE_TM, t)
    tq = ol.shape[2]
    row = lambda n: pl.BlockSpec((tm, n), lambda i: (i, 0))
    full = lambda a: pl.BlockSpec(a.shape, lambda i: (0,) * a.ndim, pipeline_mode=pl.Buffered(1))
    return pl.pallas_call(
        functools.partial(_merge_kernel, alpha=alpha),
        grid=(t // tm,),
        in_specs=[pl.BlockSpec((tm // tq, MLA_HEADS, tq, KV_RANK), lambda i: (i, 0, 0, 0)),
                  row(yn.shape[1]), row(cm.shape[1]), row(g.shape[1]), row(d),
                  full(wuv), full(wa), full(wb), full(wc), full(wo), full(ln_g), full(ln_b)],
        out_specs=row(d),
        out_shape=jax.ShapeDtypeStruct((t, d), F32),
        scratch_shapes=[pltpu.VMEM((tm, MLA_HEADS * V_DIM), BF16)],
        compiler_params=_cp("parallel"),
        name="merge_ln",
    )(ol, yn, cm, g, x, wuv, wa, wb, wc, wo, ln_g, ln_b)


def _first_max(v, expert):
    m = jnp.max(v, axis=0, keepdims=True)
    idx = jnp.min(jnp.where(v == m, expert, N_EXPERTS), axis=0, keepdims=True)
    return m, idx


def _route_kernel(x_ref, rwh_ref, rwl_ref, rbt_ref, info_ref, infot_ref, tinfo_ref, cnt_ref, carry_scr):
    tm = x_ref.shape[0]

    @pl.when(pl.program_id(0) == 0)
    def _():
        carry_scr[...] = jnp.zeros(carry_scr.shape, F32)

    x = x_ref[...]
    x_hi = x.astype(BF16)
    x_lo = (x - x_hi.astype(F32)).astype(BF16)
    logits = (jnp.dot(x_hi, rwh_ref[...], preferred_element_type=F32)
              + jnp.dot(x_lo, rwh_ref[...], preferred_element_type=F32)
              + jnp.dot(x_hi, rwl_ref[...], preferred_element_type=F32))
    scores = _sigmoid(logits.T[:N_EXPERTS, :])
    sel = scores + jnp.tile(rbt_ref[...], (1, tm // LANES))
    expert = lax.broadcasted_iota(I32, (N_EXPERTS, tm), 0)
    neg = -jnp.inf
    best_score = None
    best_group = None
    for j in range(N_EXPERT_GROUPS):
        in_j = (expert >= j * EXPERTS_PER_GROUP) & (expert < (j + 1) * EXPERTS_PER_GROUP)
        v = jnp.where(in_j, sel, neg)
        m1, i1 = _first_max(v, expert)
        m2, _ = _first_max(jnp.where(expert == i1, neg, v), expert)
        gs = m1 + m2
        if j == 0:
            best_score, best_group = gs, jnp.zeros_like(i1)
        else:
            better = gs > best_score
            best_score = jnp.where(better, gs, best_score)
            best_group = jnp.where(better, j, best_group)
    lo = best_group * EXPERTS_PER_GROUP
    v = jnp.where((expert >= lo) & (expert < lo + EXPERTS_PER_GROUP), sel, neg)
    _, e1 = _first_max(v, expert)
    _, e2 = _first_max(jnp.where(expert == e1, neg, v), expert)
    w1 = jnp.sum(jnp.where(expert == e1, scores, 0.0), axis=0, keepdims=True)
    w2 = jnp.sum(jnp.where(expert == e2, scores, 0.0), axis=0, keepdims=True)
    wsum = w1 + w2
    member = jnp.where(expert == e1, 1.0, jnp.where(expert == e2, 1.0, 0.0))
    row = lax.broadcasted_iota(I32, (tm, tm), 0)
    col = lax.broadcasted_iota(I32, (tm, tm), 1)
    earlier = jnp.where(row < col, 1.0, 0.0).astype(BF16)
    lrank = jnp.dot(member.astype(BF16), earlier, preferred_element_type=F32)
    n_col = jnp.sum(member, axis=1, keepdims=True)
    diag = (lax.broadcasted_iota(I32, (N_EXPERTS, LANES), 0)
            == lax.broadcasted_iota(I32, (N_EXPERTS, LANES), 1))
    n = jnp.sum(jnp.where(diag, n_col, 0.0), axis=0, keepdims=True)
    n8 = jnp.floor((n + (ROW_ALIGN - 1)) * (1.0 / ROW_ALIGN)) * ROW_ALIGN
    ua = lax.broadcasted_iota(I32, (LANES, LANES), 0)
    ub = lax.broadcasted_iota(I32, (LANES, LANES), 1)
    lower_experts = jnp.where(ua < ub, 1.0, 0.0).astype(BF16)
    n8_rows = jnp.broadcast_to(n8, (8, LANES))
    block_start = jnp.dot(n8_rows.astype(BF16), lower_experts, preferred_element_type=F32)
    start_col = jnp.sum(jnp.where(diag, block_start[0:1, :], 0.0), axis=1, keepdims=True)
    local_row = start_col + lrank
    j1 = jnp.sum(jnp.where(expert == e1, local_row, 0.0), axis=0, keepdims=True)
    j2 = jnp.sum(jnp.where(expert == e2, local_row, 0.0), axis=0, keepdims=True)
    carry = carry_scr[...]
    sub = lax.broadcasted_iota(I32, (8, LANES), 0)
    tinfo_ref[0] = jnp.where(sub == 0, n8_rows, jnp.where(sub == 1, block_start, jnp.where(sub == 2, carry, 0.0)))
    new_carry = carry + n8_rows
    carry_scr[...] = new_carry
    cnt_ref[...] = new_carry
    field = lax.broadcasted_iota(I32, (8, tm), 0)
    info_t = jnp.where(field == 0, e1.astype(F32),
             jnp.where(field == 1, e2.astype(F32),
             jnp.where(field == 2, w1 / wsum,
             jnp.where(field == 3, w2 / wsum,
             jnp.where(field == 4, j1,
             jnp.where(field == 5, j2, 0.0))))))
    infot_ref[0] = info_t
    info_ref[...] = jnp.concatenate([info_t, jnp.zeros((LANES - 8, tm), F32)], axis=0).T


def route(x, rw_hi, rw_lo, rb):
    t, d = x.shape
    tm = min(DISPATCH_TM, t)
    return pl.pallas_call(
        _route_kernel,
        grid=(t // tm,),
        in_specs=[pl.BlockSpec((tm, d), lambda i: (i, 0)),
                  pl.BlockSpec((d, LANES), lambda i: (0, 0)),
                  pl.BlockSpec((d, LANES), lambda i: (0, 0)),
                  pl.BlockSpec((N_EXPERTS, LANES), lambda i: (0, 0))],
        out_specs=[pl.BlockSpec((tm, LANES), lambda i: (i, 0)),
                   pl.BlockSpec((1, 8, tm), lambda i: (i, 0, 0)),
                   pl.BlockSpec((1, 8, LANES), lambda i: (i, 0, 0)),
                   pl.BlockSpec((8, LANES), lambda i: (0, 0))],
        out_shape=[jax.ShapeDtypeStruct((t, LANES), F32),
                   jax.ShapeDtypeStruct((t // tm, 8, tm), F32),
                   jax.ShapeDtypeStruct((t // tm, 8, LANES), F32),
                   jax.ShapeDtypeStruct((8, LANES), F32)],
        scratch_shapes=[pltpu.VMEM((8, LANES), F32)],
        compiler_params=_cp("arbitrary"),
        name="route",
    )(x, rw_hi, rw_lo, rb)


def _block_copies(plan_ref, local_ref, sorted_ref, sem, to_sorted, start):
    for e in range(N_EXPERTS):
        count = plan_ref[0, 0, e]
        loc0 = plan_ref[0, 0, N_EXPERTS + e]
        dst0 = plan_ref[0, 0, 2 * N_EXPERTS + e]

        def body(k, carry, loc0=loc0, dst0=dst0):
            loc = local_ref.at[pl.ds(pl.multiple_of(loc0 + k * ROW_ALIGN, ROW_ALIGN), ROW_ALIGN)]
            srt = sorted_ref.at[pl.ds(pl.multiple_of(dst0 + k * ROW_ALIGN, ROW_ALIGN), ROW_ALIGN)]
            cp = pltpu.make_async_copy(loc, srt, sem) if to_sorted else pltpu.make_async_copy(srt, loc, sem)
            if start:
                cp.start()
            else:
                cp.wait()
            return carry

        lax.fori_loop(0, count, body, 0)


def _dispatch_kernel(plan_ref, prev_plan_ref, infot_ref, x_ref, init_ref, o_ref, local_scr, sem):
    del init_ref
    tm = x_ref.shape[0]
    i = pl.program_id(0)
    slot = i % 2
    j1 = infot_ref[0, 4:5, :]
    j2 = infot_ref[0, 5:6, :]
    rowid = lax.broadcasted_iota(I32, (LOCAL_ROWS, tm), 0).astype(F32)
    sel = jnp.where(rowid == j1, 1.0, jnp.where(rowid == j2, 1.0, 0.0)).astype(BF16)
    local_scr[slot] = jnp.dot(sel, x_ref[...].astype(BF16), preferred_element_type=F32).astype(BF16)

    @pl.when(i > 0)
    def _():
        _block_copies(prev_plan_ref, local_scr.at[1 - slot], o_ref, sem.at[1 - slot], True, False)

    _block_copies(plan_ref, local_scr.at[slot], o_ref, sem.at[slot], True, True)

    @pl.when(i == pl.num_programs(0) - 1)
    def _():
        _block_copies(plan_ref, local_scr.at[slot], o_ref, sem.at[slot], True, False)


def dispatch_rows(x, info_t, plan, n_rows):
    t, d = x.shape
    tm = min(DISPATCH_TM, t)
    init = jnp.zeros((n_rows, d), BF16)
    return pl.pallas_call(
        _dispatch_kernel,
        grid=(t // tm,),
        in_specs=[pl.BlockSpec((1, 1, 3 * N_EXPERTS), lambda i: (i, 0, 0), memory_space=pltpu.SMEM),
                  pl.BlockSpec((1, 1, 3 * N_EXPERTS), lambda i: (jnp.maximum(i - 1, 0), 0, 0),
                               memory_space=pltpu.SMEM),
                  pl.BlockSpec((1, 8, tm), lambda i: (i, 0, 0)),
                  pl.BlockSpec((tm, d), lambda i: (i, 0)),
                  pl.BlockSpec(memory_space=pl.ANY)],
        out_specs=pl.BlockSpec(memory_space=pl.ANY),
        out_shape=jax.ShapeDtypeStruct((n_rows, d), BF16),
        scratch_shapes=[pltpu.VMEM((2, LOCAL_ROWS, d), BF16), pltpu.SemaphoreType.DMA((2,))],
        input_output_aliases={4: 0},
        compiler_params=_cp("arbitrary"),
        name="moe_dispatch",
    )(plan, plan, info_t, x, init)


def _expert_kernel(te_ref, nu_ref, xs_ref, w1_ref, w3_ref, w2_ref, o_ref, w13_scr, w2_scr):
    i = pl.program_id(0)
    f = w2_ref.shape[2]
    new_expert = (i == 0) | (te_ref[i] != te_ref[jnp.maximum(i - 1, 0)])

    @pl.when((i < nu_ref[0]) & new_expert)
    def _():
        w13_scr[:, :f] = w1_ref[0, 0].astype(BF16)
        w13_scr[:, f:] = w3_ref[0, 0].astype(BF16)
        w2_scr[...] = w2_ref[0, 0].astype(BF16)

    @pl.when(i < nu_ref[0])
    def _():
        h = jnp.dot(xs_ref[...], w13_scr[...], preferred_element_type=F32)
        h1 = h[:, :f]
        act = (h1 * _sigmoid(h1) * h[:, f:]).astype(BF16)
        o_ref[...] = jnp.dot(act, w2_scr[...], preferred_element_type=F32).astype(o_ref.dtype)

    @pl.when(i >= nu_ref[0])
    def _():
        o_ref[...] = jnp.zeros(o_ref.shape, o_ref.dtype)


def expert_ffn(xs, w1, w3, w2, layer, tile_expert, n_used):
    p, d = xs.shape
    f = w2.shape[2]
    n_tiles = p // MOE_TM
    grid_spec = pltpu.PrefetchScalarGridSpec(
        num_scalar_prefetch=2,
        grid=(n_tiles,),
        in_specs=[pl.BlockSpec((MOE_TM, d), lambda i, te, nu: (i, 0)),
                  pl.BlockSpec((1, 1, d, f), lambda i, te, nu: (layer, te[i], 0, 0)),
                  pl.BlockSpec((1, 1, d, f), lambda i, te, nu: (layer, te[i], 0, 0)),
                  pl.BlockSpec((1, 1, f, d), lambda i, te, nu: (layer, te[i], 0, 0))],
        out_specs=pl.BlockSpec((MOE_TM, d), lambda i, te, nu: (i, 0)),
        scratch_shapes=[pltpu.VMEM((d, 2 * f), BF16), pltpu.VMEM((f, d), BF16)],
    )
    return pl.pallas_call(
        _expert_kernel,
        grid_spec=grid_spec,
        out_shape=jax.ShapeDtypeStruct((p, d), BF16),
        compiler_params=_cp("arbitrary"),
        name="expert_ffn",
    )(tile_expert, n_used, xs, w1, w3, w2)


def _combine_kernel(plan_ref, next_plan_ref, info_ref, x_ref, ys_ref, lg_ref, lb_ref, xo_ref, xb_ref,
                    local_scr, sem, *, alpha):
    tm = x_ref.shape[0]
    i = pl.program_id(0)
    slot = i % 2

    @pl.when(i == 0)
    def _():
        _block_copies(plan_ref, local_scr.at[slot], ys_ref, sem.at[slot], False, True)

    @pl.when(i + 1 < pl.num_programs(0))
    def _():
        _block_copies(next_plan_ref, local_scr.at[1 - slot], ys_ref, sem.at[1 - slot], False, True)

    _block_copies(plan_ref, local_scr.at[slot], ys_ref, sem.at[slot], False, False)
    last = N_EXPERTS - 1
    used = plan_ref[0, 0, N_EXPERTS + last] + plan_ref[0, 0, last] * ROW_ALIGN
    rowid = lax.broadcasted_iota(I32, (LOCAL_ROWS, 1), 0)
    y = local_scr[slot]
    y = jnp.where(rowid < used, y, jnp.zeros_like(y))
    info = info_ref[...]
    col = lax.broadcasted_iota(I32, (tm, LOCAL_ROWS), 1).astype(F32)
    gate = jnp.where(col == info[:, 4:5], info[:, 2:3],
                     jnp.where(col == info[:, 5:6], info[:, 3:4], 0.0)).astype(BF16)
    moe = jnp.dot(gate, y, preferred_element_type=F32)
    out = _layer_norm(alpha * x_ref[...] + moe, lg_ref[...], lb_ref[...])
    xo_ref[...] = out
    xb_ref[...] = out.astype(BF16)


def combine_ln(plan, info, x, ys, ln_g, ln_b, alpha):
    t, d = x.shape
    tm = min(DISPATCH_TM, t)
    return pl.pallas_call(
        functools.partial(_combine_kernel, alpha=alpha),
        grid=(t // tm,),
        in_specs=[pl.BlockSpec((1, 1, 3 * N_EXPERTS), lambda i: (i, 0, 0), memory_space=pltpu.SMEM),
                  pl.BlockSpec((1, 1, 3 * N_EXPERTS), lambda i: (jnp.minimum(i + 1, t // tm - 1), 0, 0),
                               memory_space=pltpu.SMEM),
                  pl.BlockSpec((tm, LANES), lambda i: (i, 0)),
                  pl.BlockSpec((tm, d), lambda i: (i, 0)),
                  pl.BlockSpec(memory_space=pl.ANY),
                  pl.BlockSpec((1, d), lambda i: (0, 0)),
                  pl.BlockSpec((1, d), lambda i: (0, 0))],
        out_specs=[pl.BlockSpec((tm, d), lambda i: (i, 0)), pl.BlockSpec((tm, d), lambda i: (i, 0))],
        out_shape=[jax.ShapeDtypeStruct((t, d), F32), jax.ShapeDtypeStruct((t, d), BF16)],
        scratch_shapes=[pltpu.VMEM((2, LOCAL_ROWS, d), BF16), pltpu.SemaphoreType.DMA((2,))],
        compiler_params=_cp("arbitrary"),
        name="moe_combine_ln",
    )(plan, plan, info, x, ys, ln_g, ln_b)


def _prep_layer(l, p):
    w_in = p["w_in"][l]
    d = w_in.shape[0]
    o_dq = 0
    o_dkv = o_dq + Q_RANK
    o_z = o_dkv + KV_RANK + ROPE_DIM
    d_inner = SSM_HEADS * SSM_HEAD_DIM
    conv_ch = d_inner + 2 * SSM_GROUPS * D_STATE
    o_xbc = o_z + d_inner
    o_dt = o_xbc + conv_ch
    o_qm = o_dt + SSM_HEADS
    o_g = o_qm + XA_HEADS * XA_HEAD_DIM
    half = ROPE_DIM // 2
    kr0 = o_dkv + KV_RANK
    zeros = lambda n: jnp.zeros((d, n), F32)
    w_small = jnp.concatenate([
        w_in[:, o_dq:o_dq + Q_RANK],
        w_in[:, o_dkv:o_dkv + KV_RANK],
        w_in[:, kr0:kr0 + ROPE_DIM], zeros(LANES - ROPE_DIM),
        w_in[:, kr0 + half:kr0 + ROPE_DIM], w_in[:, kr0:kr0 + half], zeros(LANES - ROPE_DIM),
        w_in[:, o_dt:o_dt + SSM_HEADS], zeros(LANES - SSM_HEADS)], axis=1)
    w_uq = p["w_uq"][l].reshape(Q_RANK, MLA_HEADS, NOPE_DIM + ROPE_DIM)
    wq_nope = w_uq[:, :, :NOPE_DIM].reshape(Q_RANK, -1)
    wq_rope = w_uq[:, :, NOPE_DIM:]
    wq_rope_sw = jnp.concatenate([wq_rope[:, :, half:], wq_rope[:, :, :half]], axis=-1)
    wq = jnp.concatenate([wq_nope, wq_rope.reshape(Q_RANK, -1), wq_rope_sw.reshape(Q_RANK, -1)], axis=1)
    w_ukv = p["w_ukv"][l]
    wuk_t = jnp.transpose(w_ukv[:, :, :NOPE_DIM], (1, 2, 0))
    wuv = jnp.transpose(w_ukv[:, :, NOPE_DIM:], (1, 0, 2))
    pad_heads = lambda v, fill: jnp.concatenate(
        [v.astype(F32), jnp.full((LANES - SSM_HEADS,), fill, F32)]).reshape(1, LANES)
    bf = lambda a: a.astype(BF16)
    return dict(
        w_small=bf(w_small), w_z=bf(w_in[:, o_z:o_z + d_inner]), w_xbc=bf(w_in[:, o_xbc:o_xbc + conv_ch]),
        w_qm=bf(w_in[:, o_qm:o_g]), w_g=bf(w_in[:, o_g:]),
        q_norm=p["q_norm"][l].reshape(1, -1), kv_norm=p["kv_norm"][l].reshape(1, -1),
        wq=bf(wq), wuk_t=bf(wuk_t), wuv=bf(wuv),
        conv_w_half=0.5 * p["conv_w"][l], conv_b_half=0.5 * p["conv_b"][l].reshape(1, -1),
        dt_bias=pad_heads(p["dt_bias"][l], 0.0),
        a_neg=pad_heads(-jnp.exp(p["a_log"][l].astype(F32)), 0.0),
        d_skip=jnp.repeat(p["d_skip"][l].astype(F32), SSM_HEAD_DIM).reshape(1, -1),
        ssm_norm=p["ssm_norm"][l].reshape(1, -1),
        w_mem_kv=bf(p["w_mem_kv"][l]),
        wa=bf(p["w_proj_a"][l]), wb=bf(p["w_proj_b"][l]), wc=bf(p["w_proj_c"][l]), wo=bf(p["w_out"][l]),
        ln1_g=p["ln1_g"][l].reshape(1, -1), ln1_b=p["ln1_b"][l].reshape(1, -1),
        ln2_g=p["ln2_g"][l].reshape(1, -1), ln2_b=p["ln2_b"][l].reshape(1, -1),
    )


def _dispatch_plan(tinfo, counts, t):
    n_tok_tiles = tinfo.shape[0]
    n8 = tinfo[:, 0, :N_EXPERTS].astype(I32)
    block_start = tinfo[:, 1, :N_EXPERTS].astype(I32)
    carry = tinfo[:, 2, :N_EXPERTS].astype(I32)
    rows = counts[0, :N_EXPERTS].astype(I32)
    tiles = (rows + MOE_TM - 1) // MOE_TM
    tile_end = jnp.cumsum(tiles)
    row_start = (tile_end - tiles) * MOE_TM
    plan = jnp.concatenate([n8 // ROW_ALIGN, block_start, row_start[None, :] + carry], axis=1)
    max_rows = TOP_K * t + N_EXPERTS * (ROW_ALIGN - 1) * n_tok_tiles
    n_tiles = -(-max_rows // MOE_TM) + N_EXPERTS
    tile_ids = jnp.arange(n_tiles, dtype=I32)
    tile_expert = jnp.minimum(jnp.sum(tile_ids[:, None] >= tile_end[None, :], axis=1), N_EXPERTS - 1)
    n_used = tile_end[-1:].astype(I32)
    return plan.reshape(n_tok_tiles, 1, 3 * N_EXPERTS), tile_expert.astype(I32), n_used, n_tiles * MOE_TM


def kernel(x, mem, positions, w_in, q_norm, w_uq, kv_norm, w_ukv, w_proj_a, conv_w, conv_b, dt_bias, a_log,
           d_skip, ssm_norm, w_proj_b, w_mem_kv, w_proj_c, w_out, ln1_g, ln1_b, router_w, router_bias,
           exp_w1, exp_w3, exp_w2, ln2_g, ln2_b):
    params = dict(w_in=w_in, q_norm=q_norm, w_uq=w_uq, kv_norm=kv_norm, w_ukv=w_ukv, w_proj_a=w_proj_a,
                  conv_w=conv_w, conv_b=conv_b, dt_bias=dt_bias, a_log=a_log, d_skip=d_skip,
                  ssm_norm=ssm_norm, w_proj_b=w_proj_b, w_mem_kv=w_mem_kv, w_proj_c=w_proj_c, w_out=w_out,
                  ln1_g=ln1_g, ln1_b=ln1_b, exp_w1=exp_w1, exp_w3=exp_w3, exp_w2=exp_w2,
                  ln2_g=ln2_g, ln2_b=ln2_b)
    bsz, s, d = x.shape
    t = bsz * s
    depth = w_in.shape[0]
    alpha = float((2 * depth) ** 0.25)
    n_mem = mem.shape[1]

    inv = ROPE_THETA ** (-jnp.arange(0, ROPE_DIM, 2, dtype=F32) / ROPE_DIM)
    ang = positions.astype(F32)[..., None] * inv
    cos, sin = jnp.cos(ang), jnp.sin(ang)
    cosq = jnp.tile(jnp.concatenate([cos, cos], axis=-1), (1, 1, LANES // ROPE_DIM)).reshape(t, LANES)
    sinq = jnp.tile(jnp.concatenate([-sin, sin], axis=-1), (1, 1, LANES // ROPE_DIM)).reshape(t, LANES)

    rw = jnp.concatenate([router_w.astype(F32), jnp.zeros((d, LANES - N_EXPERTS), F32)], axis=1)
    rw_hi = rw.astype(BF16)
    rw_lo = (rw - rw_hi.astype(F32)).astype(BF16)
    rbt = jnp.broadcast_to(router_bias.astype(F32)[:, None], (N_EXPERTS, LANES))
    mem_b = mem.reshape(bsz * n_mem, d).astype(BF16)

    xf = x.reshape(t, d).astype(F32)
    xb = xf.astype(BF16)
    for l in range(depth):
        w = _prep_layer(l, params)
        ha = matmul(xb, w["w_small"], F32, "proj_small")
        z = matmul(xb, w["w_z"], BF16, "proj_z")
        xbc_c = proj_conv_silu(xb, w["w_xbc"], w["conv_w_half"], w["conv_b_half"], s).reshape(bsz, s, -1)
        qm = matmul(xb, w["w_qm"], BF16, "proj_qmem")
        g = matmul(xb, w["w_g"], BF16, "proj_gate")

        ql, qr, ck, kr = mla_prep(ha, cosq, sinq, w["q_norm"], w["kv_norm"], w["wq"], w["wuk_t"],
                                  min(ATT_TQ, s))
        o_lat = mla_attention(ql, qr, ck.reshape(bsz, s, KV_RANK), kr.reshape(bsz, s, ROPE_DIM))

        yn = ssd(xbc_c, z.reshape(bsz, s, -1), ha.reshape(bsz, s, HA_W), w["dt_bias"], w["a_neg"],
                 w["d_skip"], w["ssm_norm"]).reshape(t, -1)

        kv = matmul(mem_b, w["w_mem_kv"], BF16, "proj_memkv").reshape(bsz, n_mem, -1)
        cm = mem_attention(qm.reshape(bsz, s, -1), kv).reshape(t, -1)

        x1 = merge_ln(o_lat, yn, cm, g, xf, w["wuv"], w["wa"], w["wb"], w["wc"], w["wo"],
                           w["ln1_g"], w["ln1_b"], alpha)

        info, info_t, tinfo, counts = route(x1, rw_hi, rw_lo, rbt)
        plan, tile_expert, n_used, n_rows = _dispatch_plan(tinfo, counts, t)
        xs = dispatch_rows(x1, info_t, plan, n_rows)
        ys = expert_ffn(xs, exp_w1, exp_w3, exp_w2, l, tile_expert, n_used)
        xf, xb = combine_ln(plan, info, x1, ys, w["ln2_g"], w["ln2_b"], alpha)
    return xf.reshape(bsz, s, d)
```

```python
import functools

import jax
import jax.numpy as jnp
from jax import lax
from jax.experimental import pallas as pl
from jax.experimental.pallas import tpu as pltpu

F32 = jnp.float32
BF16 = jnp.bfloat16
I32 = jnp.int32

MLA_HEADS = 8
Q_RANK = 384
KV_RANK = 256
NOPE_DIM = 128
ROPE_DIM = 64
V_DIM = 128
ROPE_THETA = 10000.0
SSM_HEADS = 32
SSM_HEAD_DIM = 64
SSM_GROUPS = 8
D_STATE = 128
CONV_K = 4
XA_HEADS = 4
XA_HEAD_DIM = 256
N_EXPERTS = 16
N_EXPERT_GROUPS = 4
EXPERTS_PER_GROUP = 4
TOP_K = 2
NORM_EPS = 1e-5
RMS_EPS = 1e-6

LANES = 128
V7X_VMEM_LIMIT = 56 * 1024 * 1024

MM_TM = 2048
MM_TN = 1024
PREP_TM = 1024
ATT_TQ = 128
ATT_TK = 512
ATT_GROUP_ROWS = 512
ATT_UNROLL = 2
LOG2E = 1.4426950408889634
PROJ_CONV_TM = 1024
PROJ_CONV_TN = 1024
PROJ_CONV_SUB = 256
CONV_HALO = 16
SSD_CHUNK = 256
XA_TQ = 2048
MERGE_TM = 512
MOE_TM = 512
DISPATCH_TM = 512
ROW_ALIGN = 16
LOCAL_ROWS = 1280

HA_W = 1024
HA_DQ = 0
HA_C = 384
HA_KR = 640
HA_KRS = 768
HA_DT = 896


def _cp(*sem):
    return pltpu.CompilerParams(dimension_semantics=sem, vmem_limit_bytes=V7X_VMEM_LIMIT)


def _sigmoid(x):
    return 1.0 / (1.0 + jnp.exp(-x))


def _mm_kernel(x_ref, w_ref, o_ref):
    o_ref[...] = jnp.dot(x_ref[...], w_ref[...], preferred_element_type=F32).astype(o_ref.dtype)


def matmul(x, w, out_dtype, name):
    m, k = x.shape
    n = w.shape[1]
    tm = min(MM_TM, m)
    tn = min(MM_TN, n)
    return pl.pallas_call(
        _mm_kernel,
        grid=(n // tn, m // tm),
        in_specs=[pl.BlockSpec((tm, k), lambda j, i: (i, 0)),
                  pl.BlockSpec((k, tn), lambda j, i: (0, j))],
        out_specs=pl.BlockSpec((tm, tn), lambda j, i: (i, j)),
        out_shape=jax.ShapeDtypeStruct((m, n), out_dtype),
        compiler_params=_cp("parallel", "parallel"),
        name=name,
    )(x, w)


def _proj_conv_kernel(x_ref, xh_ref, w_ref, cw_ref, cb_ref, o_ref, *, tiles_per_seq):
    first = (pl.program_id(1) % tiles_per_seq) == 0
    tn = o_ref.shape[1]
    sub = min(PROJ_CONV_SUB, tn)
    def project(c):
        w = w_ref[:, c * sub:(c + 1) * sub]
        u = jnp.dot(x_ref[...], w, preferred_element_type=F32)
        halo = jnp.dot(xh_ref[...], w, preferred_element_type=F32)
        return jnp.concatenate([jnp.where(first, jnp.zeros_like(halo), halo), u], axis=0)

    n_sub = tn // sub
    nxt = project(0)
    for c in range(n_sub):
        cs = slice(c * sub, (c + 1) * sub)
        cw = cw_ref[:, cs]
        ext = nxt
        if c + 1 < n_sub:
            nxt = project(c + 1)
        prev = pltpu.roll(ext, 1, 0)
        near = cw[3:4, :] * ext + cw[2:3, :] * prev
        far = cw[1:2, :] * ext + cw[0:1, :] * prev
        half = (near + pltpu.roll(far, 2, 0))[CONV_HALO:, :] + cb_ref[:, cs]
        o_ref[:, cs] = (half * jnp.tanh(half) + half).astype(o_ref.dtype)


def proj_conv_silu(x, w, conv_w, conv_b, seq_len):
    m, k = x.shape
    n = w.shape[1]
    tm = min(PROJ_CONV_TM, seq_len)
    tn = min(PROJ_CONV_TN, n)
    hb = tm // CONV_HALO
    return pl.pallas_call(
        functools.partial(_proj_conv_kernel, tiles_per_seq=seq_len // tm),
        grid=(n // tn, m // tm),
        in_specs=[pl.BlockSpec((tm, k), lambda j, i: (i, 0)),
                  pl.BlockSpec((CONV_HALO, k), lambda j, i: (jnp.maximum(i * hb - 1, 0), 0)),
                  pl.BlockSpec((k, tn), lambda j, i: (0, j)),
                  pl.BlockSpec((CONV_K, tn), lambda j, i: (0, j)),
                  pl.BlockSpec((1, tn), lambda j, i: (0, j))],
        out_specs=pl.BlockSpec((tm, tn), lambda j, i: (i, j)),
        out_shape=jax.ShapeDtypeStruct((m, n), BF16),
        compiler_params=_cp("parallel", "parallel"),
        name="proj_xbc_conv",
    )(x, x, w, conv_w, conv_b)


def _mla_prep_kernel(ha_ref, cos_ref, sin_ref, qn_ref, kvn_ref, wq_ref, wuk_ref,
                     ql_ref, qr_ref, ck_ref, kr_ref, *, scale):
    nb, _, tq, _ = ql_ref.shape
    ha = ha_ref[...]
    dq = ha[:, HA_DQ:HA_DQ + Q_RANK]
    c_q = dq * lax.rsqrt(jnp.mean(dq * dq, axis=-1, keepdims=True) + RMS_EPS) * qn_ref[...]
    q = jnp.dot(c_q.astype(BF16), wq_ref[...], preferred_element_type=F32)
    n_nope = MLA_HEADS * NOPE_DIM
    n_rope = MLA_HEADS * ROPE_DIM
    cosq = jnp.tile(cos_ref[...], (1, n_rope // LANES))
    sinq = jnp.tile(sin_ref[...], (1, n_rope // LANES))
    q_rope = ((q[:, n_nope:n_nope + n_rope] * cosq + q[:, n_nope + n_rope:] * sinq) * scale).astype(BF16)
    for h in range(MLA_HEADS):
        qh = q[:, h * NOPE_DIM:(h + 1) * NOPE_DIM].astype(BF16)
        ql = jnp.dot(qh, wuk_ref[h], preferred_element_type=F32)
        ql_ref[:, h] = (ql * scale).astype(BF16).reshape(nb, tq, KV_RANK)
        qr_ref[:, h] = q_rope[:, h * ROPE_DIM:(h + 1) * ROPE_DIM].reshape(nb, tq, ROPE_DIM)
    c = ha[:, HA_C:HA_C + KV_RANK]
    c_kv = c * lax.rsqrt(jnp.mean(c * c, axis=-1, keepdims=True) + RMS_EPS) * kvn_ref[...]
    ck_ref[...] = c_kv.astype(BF16)
    k_rope = (ha[:, HA_KR:HA_KR + LANES] * cos_ref[...]
              + ha[:, HA_KRS:HA_KRS + LANES] * sin_ref[...])
    kr_ref[...] = k_rope[:, :ROPE_DIM].astype(BF16)


def mla_prep(ha, cosq, sinq, q_norm, kv_norm, wq, wuk_t, tq):
    t = ha.shape[0]
    tm = min(PREP_TM, t)
    nb = tm // tq
    scale = float((NOPE_DIM + ROPE_DIM) ** -0.5 * LOG2E)
    n_rope = MLA_HEADS * ROPE_DIM
    full = lambda shape: pl.BlockSpec(shape, lambda i: (0,) * len(shape))
    return pl.pallas_call(
        functools.partial(_mla_prep_kernel, scale=scale),
        grid=(t // tm,),
        in_specs=[pl.BlockSpec((tm, HA_W), lambda i: (i, 0)),
                  pl.BlockSpec((tm, LANES), lambda i: (i, 0)),
                  pl.BlockSpec((tm, LANES), lambda i: (i, 0)),
                  full((1, Q_RANK)), full((1, KV_RANK)),
                  full(wq.shape), full(wuk_t.shape)],
        out_specs=[pl.BlockSpec((nb, MLA_HEADS, tq, KV_RANK), lambda i: (i, 0, 0, 0)),
                   pl.BlockSpec((nb, MLA_HEADS, tq, ROPE_DIM), lambda i: (i, 0, 0, 0)),
                   pl.BlockSpec((tm, KV_RANK), lambda i: (i, 0)),
                   pl.BlockSpec((tm, ROPE_DIM), lambda i: (i, 0))],
        out_shape=[jax.ShapeDtypeStruct((t // tq, MLA_HEADS, tq, KV_RANK), BF16),
                   jax.ShapeDtypeStruct((t // tq, MLA_HEADS, tq, ROPE_DIM), BF16),
                   jax.ShapeDtypeStruct((t, KV_RANK), BF16),
                   jax.ShapeDtypeStruct((t, ROPE_DIM), BF16)],
        compiler_params=_cp("parallel"),
        name="mla_prep",
    )(ha, cosq, sinq, q_norm, kv_norm, wq, wuk_t)


def _mla_attn_kernel(ql_ref, qr_ref, ck_ref, kr_ref, o_ref, m_scr, l_scr, acc_scr, *, tq, tk, hpc):
    rc = hpc * tq
    n_groups = MLA_HEADS // hpc
    q_start = pl.program_id(1) * tq
    n_full = q_start // tk
    m_scr[...] = jnp.full(m_scr.shape, -jnp.inf, F32)
    l_scr[...] = jnp.zeros(l_scr.shape, F32)
    acc_scr[...] = jnp.zeros(acc_scr.shape, F32)
    nt = (((1,), (1,)), ((), ()))

    def step(j, masked, width=tk):
        ks = pl.multiple_of(j * tk, tk)
        ck = ck_ref[0, pl.ds(ks, width), :]
        kr = kr_ref[0, pl.ds(ks, width), :]
        if masked:
            q_pos = q_start + (lax.broadcasted_iota(I32, (rc, width), 0) & (tq - 1))
            k_pos = ks + lax.broadcasted_iota(I32, (rc, width), 1)
            visible = k_pos <= q_pos
        for c in range(n_groups):
            rs = slice(c * rc, (c + 1) * rc)
            ql = ql_ref[0, c * hpc:(c + 1) * hpc].reshape(rc, KV_RANK)
            qr = qr_ref[0, c * hpc:(c + 1) * hpc].reshape(rc, ROPE_DIM)
            s = (lax.dot_general(ql, ck, nt, preferred_element_type=F32)
                 + lax.dot_general(qr, kr, nt, preferred_element_type=F32))
            if masked:
                s = jnp.where(visible, s, -jnp.inf)
            m_prev = m_scr[rs]
            m_new = jnp.maximum(m_prev, jnp.max(s, axis=-1, keepdims=True))
            alpha = jnp.exp2(m_prev - m_new)
            p = jnp.exp2(s - jnp.tile(m_new, (1, width // LANES)))
            l_scr[rs] = alpha * l_scr[rs] + jnp.sum(p, axis=-1, keepdims=True)
            acc_scr[rs] = (jnp.tile(alpha, (1, KV_RANK // LANES)) * acc_scr[rs]
                           + jnp.dot(p.astype(BF16), ck, preferred_element_type=F32))
            m_scr[rs] = m_new

    def trip(jj, carry):
        for u in range(ATT_UNROLL):
            step(ATT_UNROLL * jj + u, False)
        return carry

    n_trips = n_full // ATT_UNROLL
    lax.fori_loop(0, n_trips, trip, 0)

    def single(j, carry):
        step(j, False)
        return carry

    lax.fori_loop(n_trips * ATT_UNROLL, n_full, single, 0)

    sub = (q_start - n_full * tk) // tq
    for v in range(tk // tq):
        @pl.when(sub == v)
        def _(v=v):
            step(n_full, True, (v + 1) * tq)
    out = acc_scr[...] / jnp.tile(l_scr[...], (1, KV_RANK // LANES))
    o_ref[0] = out.astype(o_ref.dtype).reshape(MLA_HEADS, tq, KV_RANK)


def mla_attention(ql, qr, ck, kr):
    _, _, tq, _ = ql.shape
    b, s, _ = ck.shape
    tk = min(ATT_TK, s)
    assert tk % tq == 0 and s % tk == 0 and tq & (tq - 1) == 0
    nq = s // tq
    rows = tq * MLA_HEADS
    hpc = max(1, min(MLA_HEADS, ATT_GROUP_ROWS // tq))
    qspec = lambda dim: pl.BlockSpec((1, MLA_HEADS, tq, dim), lambda bi, i: (bi * nq + i, 0, 0, 0))
    return pl.pallas_call(
        functools.partial(_mla_attn_kernel, tq=tq, tk=tk, hpc=hpc),
        grid=(b, nq),
        in_specs=[qspec(KV_RANK), qspec(ROPE_DIM),
                  pl.BlockSpec((1, s, KV_RANK), lambda bi, i: (bi, 0, 0)),
                  pl.BlockSpec((1, s, ROPE_DIM), lambda bi, i: (bi, 0, 0))],
        out_specs=qspec(KV_RANK),
        out_shape=jax.ShapeDtypeStruct(ql.shape, BF16),
        scratch_shapes=[pltpu.VMEM((rows, LANES), F32), pltpu.VMEM((rows, LANES), F32),
                        pltpu.VMEM((rows, KV_RANK), F32)],
        compiler_params=_cp("parallel", "parallel"),
        name="mla_attention",
    )(ql, qr, ck, kr)


def _ssd_kernel(xbc_ref, z_ref, dt_ref, dtb_ref, a_ref, dsk_ref, ng_ref, exp_ref, o_ref,
                state_scr, *, chunk):
    d_inner = SSM_HEADS * SSM_HEAD_DIM
    gn = SSM_GROUPS * D_STATE
    rep = SSM_HEADS // SSM_GROUPS
    gw = rep * SSM_HEAD_DIM

    @pl.when(pl.program_id(1) == 0)
    def _():
        state_scr[...] = jnp.zeros(state_scr.shape, F32)

    x_raw = dt_ref[0] + dtb_ref[...]
    dt = jnp.maximum(x_raw, 0.0) + jnp.log(1.0 + jnp.exp(-jnp.abs(x_raw)))
    da = dt * a_ref[...]
    row = lax.broadcasted_iota(I32, (chunk, chunk), 0)
    col = lax.broadcasted_iota(I32, (chunk, chunk), 1)
    causal = row >= col
    tri = jnp.where(causal, 1.0, 0.0).astype(BF16)
    acum = jnp.zeros(da.shape, F32)
    rem = da
    for _ in range(3):
        part = rem.astype(BF16)
        acum = acum + jnp.dot(tri, part, preferred_element_type=F32)
        rem = rem - part.astype(F32)
    a2 = acum * LOG2E
    a2_t = a2.T
    src_t = a2_t - jnp.log2(dt.T)
    w_t = jnp.exp2(a2_t[:, chunk - 1:chunk] - src_t)
    e_end = jnp.exp2(a2[chunk - 1:chunk, :])
    e_all = jnp.dot(jnp.exp2(a2).astype(BF16), exp_ref[...], preferred_element_type=F32)
    head_of_lane = lax.broadcasted_iota(I32, (1, gw), 1) // SSM_HEAD_DIM

    for g in range(SSM_GROUPS):
        bg = xbc_ref[0, :, d_inner + g * D_STATE:d_inner + (g + 1) * D_STATE]
        cg = xbc_ref[0, :, d_inner + gn + g * D_STATE:d_inner + gn + (g + 1) * D_STATE]
        bt = bg.astype(F32).T
        cb = jnp.dot(cg, bt.astype(BF16), preferred_element_type=F32)
        st = state_scr[g]
        ys = jnp.dot(cg, st.astype(BF16), preferred_element_type=F32)
        xg = xbc_ref[0, :, g * gw:(g + 1) * gw].astype(F32)
        y = jnp.zeros((chunk, gw), F32)
        upd = jnp.zeros((D_STATE, gw), F32)
        sc = jnp.zeros((1, gw), F32)
        for r in range(rep):
            h = g * rep + r
            own = head_of_lane == r
            seg = a2[:, h:h + 1] - src_t[h:h + 1, :]
            m = (cb * jnp.exp2(jnp.where(causal, seg, -jnp.inf))).astype(BF16)
            xm = jnp.where(own, xg, 0.0).astype(BF16)
            y = y + jnp.dot(m, xm, preferred_element_type=F32)
            upd = upd + jnp.dot((bt * w_t[h:h + 1, :]).astype(BF16), xm, preferred_element_type=F32)
            sc = jnp.where(own, e_end[:, h:h + 1], sc)
        state_scr[g] = st * sc + upd
        gs = slice(g * gw, (g + 1) * gw)
        y = y + ys * e_all[:, gs] + xg * dsk_ref[:, gs]
        zg = z_ref[0, :, gs].astype(F32)
        y = y * (zg * _sigmoid(zg))
        y = y * lax.rsqrt(jnp.mean(y * y, axis=-1, keepdims=True) + RMS_EPS) * ng_ref[:, gs]
        o_ref[0, :, gs] = y.astype(o_ref.dtype)


def ssd(xbc, z, ha3, dt_bias, a_neg, d_skip, norm_g):
    bsz, s, c = xbc.shape
    d_inner = SSM_HEADS * SSM_HEAD_DIM
    chunk = min(SSD_CHUNK, s)
    gw = d_inner // SSM_GROUPS
    vec = lambda n: pl.BlockSpec((1, n), lambda bi, ci: (0, 0))
    expand = (jnp.arange(d_inner, dtype=I32)[None, :] // SSM_HEAD_DIM
              == jnp.arange(LANES, dtype=I32)[:, None]).astype(BF16)
    return pl.pallas_call(
        functools.partial(_ssd_kernel, chunk=chunk),
        grid=(bsz, s // chunk),
        in_specs=[pl.BlockSpec((1, chunk, c), lambda bi, ci: (bi, ci, 0)),
                  pl.BlockSpec((1, chunk, d_inner), lambda bi, ci: (bi, ci, 0)),
                  pl.BlockSpec((1, chunk, LANES), lambda bi, ci: (bi, ci, HA_DT // LANES)),
                  vec(LANES), vec(LANES), vec(d_inner), vec(d_inner),
                  pl.BlockSpec((LANES, d_inner), lambda bi, ci: (0, 0))],
        out_specs=pl.BlockSpec((1, chunk, d_inner), lambda bi, ci: (bi, ci, 0)),
        out_shape=jax.ShapeDtypeStruct((bsz, s, d_inner), BF16),
        scratch_shapes=[pltpu.VMEM((SSM_GROUPS, D_STATE, gw), F32)],
        compiler_params=_cp("parallel", "arbitrary"),
        name="ssd_scan",
    )(xbc, z, ha3, dt_bias, a_neg, d_skip, norm_g, expand)


def _xattn_kernel(q_ref, kv_ref, o_ref, *, scale):
    hd = XA_HEADS * XA_HEAD_DIM
    nt = (((1,), (1,)), ((), ()))
    for h in range(XA_HEADS):
        sl = slice(h * XA_HEAD_DIM, (h + 1) * XA_HEAD_DIM)
        q = q_ref[0, :, sl]
        k = kv_ref[0, :, sl]
        v = kv_ref[0, :, hd + h * XA_HEAD_DIM:hd + (h + 1) * XA_HEAD_DIM]
        s = lax.dot_general(q, k, nt, preferred_element_type=F32) * scale
        p = jnp.exp(s - jnp.max(s, axis=-1, keepdims=True))
        l = jnp.sum(p, axis=-1, keepdims=True)
        o = jnp.dot(p.astype(BF16), v, preferred_element_type=F32) / l
        o_ref[0, :, sl] = o.astype(o_ref.dtype)


def mem_attention(q, kv):
    bsz, s, hd = q.shape
    m = kv.shape[1]
    tq = min(XA_TQ, s)
    return pl.pallas_call(
        functools.partial(_xattn_kernel, scale=float(XA_HEAD_DIM ** -0.5)),
        grid=(bsz, s // tq),
        in_specs=[pl.BlockSpec((1, tq, hd), lambda bi, i: (bi, i, 0)),
                  pl.BlockSpec((1, m, 2 * hd), lambda bi, i: (bi, 0, 0))],
        out_specs=pl.BlockSpec((1, tq, hd), lambda bi, i: (bi, i, 0)),
        out_shape=jax.ShapeDtypeStruct((bsz, s, hd), BF16),
        compiler_params=_cp("parallel", "parallel"),
        name="mem_attention",
    )(q, kv)


def _layer_norm(v, g, b):
    mu = jnp.mean(v, axis=-1, keepdims=True)
    d = v - mu
    var = jnp.mean(d * d, axis=-1, keepdims=True)
    return d * lax.rsqrt(var + NORM_EPS) * g + b


def _merge_kernel(ol_ref, yn_ref, cm_ref, g_ref, x_ref, wuv_ref, wa_ref, wb_ref, wc_ref, wo_ref,
                  lg_ref, lb_ref, xo_ref, a_scr, *, alpha):
    tm, d = x_ref.shape
    for h in range(MLA_HEADS):
        a_scr[:, h * V_DIM:(h + 1) * V_DIM] = jnp.dot(
            ol_ref[:, h].reshape(tm, KV_RANK), wuv_ref[h],
            preferred_element_type=F32).astype(BF16)
    o_a = jnp.dot(a_scr[...], wa_ref[...], preferred_element_type=F32)
    o_b = jnp.dot(yn_ref[...], wb_ref[...], preferred_element_type=F32)
    o_c = jnp.dot(cm_ref[...], wc_ref[...], preferred_element_type=F32)
    g = g_ref[...].astype(F32)
    merged = (_sigmoid(g[:, :d]) * o_a + _sigmoid(g[:, d:2 * d]) * o_b + _sigmoid(g[:, 2 * d:]) * o_c)
    y = alpha * x_ref[...] + jnp.dot(merged.astype(BF16), wo_ref[...], preferred_element_type=F32)
    xo_ref[...] = _layer_norm(y, lg_ref[...], lb_ref[...])


def merge_ln(ol, yn, cm, g, x, wuv, wa, wb, wc, wo, ln_g, ln_b, alpha):
    t, d = x.shape
    tm = min(MERGE_TM, t)
    tq = ol.shape[2]
    row = lambda n: pl.BlockSpec((tm, n), lambda i: (i, 0))
    full = lambda a: pl.BlockSpec(a.shape, lambda i: (0,) * a.ndim, pipeline_mode=pl.Buffered(1))
    return pl.pallas_call(
        functools.partial(_merge_kernel, alpha=alpha),
        grid=(t // tm,),
        in_specs=[pl.BlockSpec((tm // tq, MLA_HEADS, tq, KV_RANK), lambda i: (i, 0, 0, 0)),
                  row(yn.shape[1]), row(cm.shape[1]), row(g.shape[1]), row(d),
                  full(wuv), full(wa), full(wb), full(wc), full(wo), full(ln_g), full(ln_b)],
        out_specs=row(d),
        out_shape=jax.ShapeDtypeStruct((t, d), F32),
        scratch_shapes=[pltpu.VMEM((tm, MLA_HEADS * V_DIM), BF16)],
        compiler_params=_cp("parallel"),
        name="merge_ln",
    )(ol, yn, cm, g, x, wuv, wa, wb, wc, wo, ln_g, ln_b)


def _first_max(v, expert):
    m = jnp.max(v, axis=0, keepdims=True)
    idx = jnp.min(jnp.where(v == m, expert, N_EXPERTS), axis=0, keepdims=True)
    return m, idx


def _route_kernel(x_ref, rwh_ref, rwl_ref, rbt_ref, info_ref, infot_ref, tinfo_ref, cnt_ref, carry_scr):
    tm = x_ref.shape[0]

    @pl.when(pl.program_id(0) == 0)
    def _():
        carry_scr[...] = jnp.zeros(carry_scr.shape, F32)

    x = x_ref[...]
    x_hi = x.astype(BF16)
    x_lo = (x - x_hi.astype(F32)).astype(BF16)
    logits = (jnp.dot(x_hi, rwh_ref[...], preferred_element_type=F32)
              + jnp.dot(x_lo, rwh_ref[...], preferred_element_type=F32)
              + jnp.dot(x_hi, rwl_ref[...], preferred_element_type=F32))
    scores = _sigmoid(logits.T[:N_EXPERTS, :])
    sel = scores + jnp.tile(rbt_ref[...], (1, tm // LANES))
    expert = lax.broadcasted_iota(I32, (N_EXPERTS, tm), 0)
    neg = -jnp.inf
    best_score = None
    best_group = None
    for j in range(N_EXPERT_GROUPS):
        in_j = (expert >= j * EXPERTS_PER_GROUP) & (expert < (j + 1) * EXPERTS_PER_GROUP)
        v = jnp.where(in_j, sel, neg)
        m1, i1 = _first_max(v, expert)
        m2, _ = _first_max(jnp.where(expert == i1, neg, v), expert)
        gs = m1 + m2
        if j == 0:
            best_score, best_group = gs, jnp.zeros_like(i1)
        else:
            better = gs > best_score
            best_score = jnp.where(better, gs, best_score)
            best_group = jnp.where(better, j, best_group)
    lo = best_group * EXPERTS_PER_GROUP
    v = jnp.where((expert >= lo) & (expert < lo + EXPERTS_PER_GROUP), sel, neg)
    _, e1 = _first_max(v, expert)
    _, e2 = _first_max(jnp.where(expert == e1, neg, v), expert)
    w1 = jnp.sum(jnp.where(expert == e1, scores, 0.0), axis=0, keepdims=True)
    w2 = jnp.sum(jnp.where(expert == e2, scores, 0.0), axis=0, keepdims=True)
    wsum = w1 + w2
    member = jnp.where(expert == e1, 1.0, jnp.where(expert == e2, 1.0, 0.0))
    row = lax.broadcasted_iota(I32, (tm, tm), 0)
    col = lax.broadcasted_iota(I32, (tm, tm), 1)
    earlier = jnp.where(row < col, 1.0, 0.0).astype(BF16)
    lrank = jnp.dot(member.astype(BF16), earlier, preferred_element_type=F32)
    n_col = jnp.sum(member, axis=1, keepdims=True)
    diag = (lax.broadcasted_iota(I32, (N_EXPERTS, LANES), 0)
            == lax.broadcasted_iota(I32, (N_EXPERTS, LANES), 1))
    n = jnp.sum(jnp.where(diag, n_col, 0.0), axis=0, keepdims=True)
    n8 = jnp.floor((n + (ROW_ALIGN - 1)) * (1.0 / ROW_ALIGN)) * ROW_ALIGN
    ua = lax.broadcasted_iota(I32, (LANES, LANES), 0)
    ub = lax.broadcasted_iota(I32, (LANES, LANES), 1)
    lower_experts = jnp.where(ua < ub, 1.0, 0.0).astype(BF16)
    n8_rows = jnp.broadcast_to(n8, (8, LANES))
    block_start = jnp.dot(n8_rows.astype(BF16), lower_experts, preferred_element_type=F32)
    start_col = jnp.sum(jnp.where(diag, block_start[0:1, :], 0.0), axis=1, keepdims=True)
    local_row = start_col + lrank
    j1 = jnp.sum(jnp.where(expert == e1, local_row, 0.0), axis=0, keepdims=True)
    j2 = jnp.sum(jnp.where(expert == e2, local_row, 0.0), axis=0, keepdims=True)
    carry = carry_scr[...]
    sub = lax.broadcasted_iota(I32, (8, LANES), 0)
    tinfo_ref[0] = jnp.where(sub == 0, n8_rows, jnp.where(sub == 1, block_start, jnp.where(sub == 2, carry, 0.0)))
    new_carry = carry + n8_rows
    carry_scr[...] = new_carry
    cnt_ref[...] = new_carry
    field = lax.broadcasted_iota(I32, (8, tm), 0)
    info_t = jnp.where(field == 0, e1.astype(F32),
             jnp.where(field == 1, e2.astype(F32),
             jnp.where(field == 2, w1 / wsum,
             jnp.where(field == 3, w2 / wsum,
             jnp.where(field == 4, j1,
             jnp.where(field == 5, j2, 0.0))))))
    infot_ref[0] = info_t
    info_ref[...] = jnp.concatenate([info_t, jnp.zeros((LANES - 8, tm), F32)], axis=0).T


def route(x, rw_hi, rw_lo, rb):
    t, d = x.shape
    tm = min(DISPATCH_TM, t)
    return pl.pallas_call(
        _route_kernel,
        grid=(t // tm,),
        in_specs=[pl.BlockSpec((tm, d), lambda i: (i, 0)),
                  pl.BlockSpec((d, LANES), lambda i: (0, 0)),
                  pl.BlockSpec((d, LANES), lambda i: (0, 0)),
                  pl.BlockSpec((N_EXPERTS, LANES), lambda i: (0, 0))],
        out_specs=[pl.BlockSpec((tm, LANES), lambda i: (i, 0)),
                   pl.BlockSpec((1, 8, tm), lambda i: (i, 0, 0)),
                   pl.BlockSpec((1, 8, LANES), lambda i: (i, 0, 0)),
                   pl.BlockSpec((8, LANES), lambda i: (0, 0))],
        out_shape=[jax.ShapeDtypeStruct((t, LANES), F32),
                   jax.ShapeDtypeStruct((t // tm, 8, tm), F32),
                   jax.ShapeDtypeStruct((t // tm, 8, LANES), F32),
                   jax.ShapeDtypeStruct((8, LANES), F32)],
        scratch_shapes=[pltpu.VMEM((8, LANES), F32)],
        compiler_params=_cp("arbitrary"),
        name="route",
    )(x, rw_hi, rw_lo, rb)


def _block_copies(plan_ref, local_ref, sorted_ref, sem, to_sorted, start):
    for e in range(N_EXPERTS):
        count = plan_ref[0, 0, e]
        loc0 = plan_ref[0, 0, N_EXPERTS + e]
        dst0 = plan_ref[0, 0, 2 * N_EXPERTS + e]

        def body(k, carry, loc0=loc0, dst0=dst0):
            loc = local_ref.at[pl.ds(pl.multiple_of(loc0 + k * ROW_ALIGN, ROW_ALIGN), ROW_ALIGN)]
            srt = sorted_ref.at[pl.ds(pl.multiple_of(dst0 + k * ROW_ALIGN, ROW_ALIGN), ROW_ALIGN)]
            cp = pltpu.make_async_copy(loc, srt, sem) if to_sorted else pltpu.make_async_copy(srt, loc, sem)
            if start:
                cp.start()
            else:
                cp.wait()
            return carry

        lax.fori_loop(0, count, body, 0)


def _dispatch_kernel(plan_ref, prev_plan_ref, infot_ref, x_ref, init_ref, o_ref, local_scr, sem):
    del init_ref
    tm = x_ref.shape[0]
    i = pl.program_id(0)
    slot = i % 2
    j1 = infot_ref[0, 4:5, :]
    j2 = infot_ref[0, 5:6, :]
    rowid = lax.broadcasted_iota(I32, (LOCAL_ROWS, tm), 0).astype(F32)
    sel = jnp.where(rowid == j1, 1.0, jnp.where(rowid == j2, 1.0, 0.0)).astype(BF16)
    local_scr[slot] = jnp.dot(sel, x_ref[...].astype(BF16), preferred_element_type=F32).astype(BF16)

    @pl.when(i > 0)
    def _():
        _block_copies(prev_plan_ref, local_scr.at[1 - slot], o_ref, sem.at[1 - slot], True, False)

    _block_copies(plan_ref, local_scr.at[slot], o_ref, sem.at[slot], True, True)

    @pl.when(i == pl.num_programs(0) - 1)
    def _():
        _block_copies(plan_ref, local_scr.at[slot], o_ref, sem.at[slot], True, False)


def dispatch_rows(x, info_t, plan, n_rows):
    t, d = x.shape
    tm = min(DISPATCH_TM, t)
    init = jnp.zeros((n_rows, d), BF16)
    return pl.pallas_call(
        _dispatch_kernel,
        grid=(t // tm,),
        in_specs=[pl.BlockSpec((1, 1, 3 * N_EXPERTS), lambda i: (i, 0, 0), memory_space=pltpu.SMEM),
                  pl.BlockSpec((1, 1, 3 * N_EXPERTS), lambda i: (jnp.maximum(i - 1, 0), 0, 0),
                               memory_space=pltpu.SMEM),
                  pl.BlockSpec((1, 8, tm), lambda i: (i, 0, 0)),
                  pl.BlockSpec((tm, d), lambda i: (i, 0)),
                  pl.BlockSpec(memory_space=pl.ANY)],
        out_specs=pl.BlockSpec(memory_space=pl.ANY),
        out_shape=jax.ShapeDtypeStruct((n_rows, d), BF16),
        scratch_shapes=[pltpu.VMEM((2, LOCAL_ROWS, d), BF16), pltpu.SemaphoreType.DMA((2,))],
        input_output_aliases={4: 0},
        compiler_params=_cp("arbitrary"),
        name="moe_dispatch",
    )(plan, plan, info_t, x, init)


def _expert_kernel(te_ref, nu_ref, xs_ref, w1_ref, w3_ref, w2_ref, o_ref, w13_scr, w2_scr):
    i = pl.program_id(0)
    f = w2_ref.shape[2]
    new_expert = (i == 0) | (te_ref[i] != te_ref[jnp.maximum(i - 1, 0)])

    @pl.when((i < nu_ref[0]) & new_expert)
    def _():
        w13_scr[:, :f] = w1_ref[0, 0].astype(BF16)
        w13_scr[:, f:] = w3_ref[0, 0].astype(BF16)
        w2_scr[...] = w2_ref[0, 0].astype(BF16)

    @pl.when(i < nu_ref[0])
    def _():
        h = jnp.dot(xs_ref[...], w13_scr[...], preferred_element_type=F32)
        h1 = h[:, :f]
        act = (h1 * _sigmoid(h1) * h[:, f:]).astype(BF16)
        o_ref[...] = jnp.dot(act, w2_scr[...], preferred_element_type=F32).astype(o_ref.dtype)

    @pl.when(i >= nu_ref[0])
    def _():
        o_ref[...] = jnp.zeros(o_ref.shape, o_ref.dtype)


def expert_ffn(xs, w1, w3, w2, layer, tile_expert, n_used):
    p, d = xs.shape
    f = w2.shape[2]
    n_tiles = p // MOE_TM
    grid_spec = pltpu.PrefetchScalarGridSpec(
        num_scalar_prefetch=2,
        grid=(n_tiles,),
        in_specs=[pl.BlockSpec((MOE_TM, d), lambda i, te, nu: (i, 0)),
                  pl.BlockSpec((1, 1, d, f), lambda i, te, nu: (layer, te[i], 0, 0)),
                  pl.BlockSpec((1, 1, d, f), lambda i, te, nu: (layer, te[i], 0, 0)),
                  pl.BlockSpec((1, 1, f, d), lambda i, te, nu: (layer, te[i], 0, 0))],
        out_specs=pl.BlockSpec((MOE_TM, d), lambda i, te, nu: (i, 0)),
        scratch_shapes=[pltpu.VMEM((d, 2 * f), BF16), pltpu.VMEM((f, d), BF16)],
    )
    return pl.pallas_call(
        _expert_kernel,
        grid_spec=grid_spec,
        out_shape=jax.ShapeDtypeStruct((p, d), BF16),
        compiler_params=_cp("arbitrary"),
        name="expert_ffn",
    )(tile_expert, n_used, xs, w1, w3, w2)


def _combine_kernel(plan_ref, next_plan_ref, info_ref, x_ref, ys_ref, lg_ref, lb_ref, xo_ref, xb_ref,
                    local_scr, sem, *, alpha):
    tm = x_ref.shape[0]
    i = pl.program_id(0)
    slot = i % 2

    @pl.when(i == 0)
    def _():
        _block_copies(plan_ref, local_scr.at[slot], ys_ref, sem.at[slot], False, True)

    @pl.when(i + 1 < pl.num_programs(0))
    def _():
        _block_copies(next_plan_ref, local_scr.at[1 - slot], ys_ref, sem.at[1 - slot], False, True)

    _block_copies(plan_ref, local_scr.at[slot], ys_ref, sem.at[slot], False, False)
    last = N_EXPERTS - 1
    used = plan_ref[0, 0, N_EXPERTS + last] + plan_ref[0, 0, last] * ROW_ALIGN
    rowid = lax.broadcasted_iota(I32, (LOCAL_ROWS, 1), 0)
    y = local_scr[slot]
    y = jnp.where(rowid < used, y, jnp.zeros_like(y))
    info = info_ref[...]
    col = lax.broadcasted_iota(I32, (tm, LOCAL_ROWS), 1).astype(F32)
    gate = jnp.where(col == info[:, 4:5], info[:, 2:3],
                     jnp.where(col == info[:, 5:6], info[:, 3:4], 0.0)).astype(BF16)
    moe = jnp.dot(gate, y, preferred_element_type=F32)
    out = _layer_norm(alpha * x_ref[...] + moe, lg_ref[...], lb_ref[...])
    xo_ref[...] = out
    xb_ref[...] = out.astype(BF16)


def combine_ln(plan, info, x, ys, ln_g, ln_b, alpha):
    t, d = x.shape
    tm = min(DISPATCH_TM, t)
    return pl.pallas_call(
        functools.partial(_combine_kernel, alpha=alpha),
        grid=(t // tm,),
        in_specs=[pl.BlockSpec((1, 1, 3 * N_EXPERTS), lambda i: (i, 0, 0), memory_space=pltpu.SMEM),
                  pl.BlockSpec((1, 1, 3 * N_EXPERTS), lambda i: (jnp.minimum(i + 1, t // tm - 1), 0, 0),
                               memory_space=pltpu.SMEM),
                  pl.BlockSpec((tm, LANES), lambda i: (i, 0)),
                  pl.BlockSpec((tm, d), lambda i: (i, 0)),
                  pl.BlockSpec(memory_space=pl.ANY),
                  pl.BlockSpec((1, d), lambda i: (0, 0)),
                  pl.BlockSpec((1, d), lambda i: (0, 0))],
        out_specs=[pl.BlockSpec((tm, d), lambda i: (i, 0)), pl.BlockSpec((tm, d), lambda i: (i, 0))],
        out_shape=[jax.ShapeDtypeStruct((t, d), F32), jax.ShapeDtypeStruct((t, d), BF16)],
        scratch_shapes=[pltpu.VMEM((2, LOCAL_ROWS, d), BF16), pltpu.SemaphoreType.DMA((2,))],
        compiler_params=_cp("arbitrary"),
        name="moe_combine_ln",
    )(plan, plan, info, x, ys, ln_g, ln_b)


def _prep_layer(l, p):
    w_in = p["w_in"][l]
    d = w_in.shape[0]
    o_dq = 0
    o_dkv = o_dq + Q_RANK
    o_z = o_dkv + KV_RANK + ROPE_DIM
    d_inner = SSM_HEADS * SSM_HEAD_DIM
    conv_ch = d_inner + 2 * SSM_GROUPS * D_STATE
    o_xbc = o_z + d_inner
    o_dt = o_xbc + conv_ch
    o_qm = o_dt + SSM_HEADS
    o_g = o_qm + XA_HEADS * XA_HEAD_DIM
    half = ROPE_DIM // 2
    kr0 = o_dkv + KV_RANK
    zeros = lambda n: jnp.zeros((d, n), F32)
    w_small = jnp.concatenate([
        w_in[:, o_dq:o_dq + Q_RANK],
        w_in[:, o_dkv:o_dkv + KV_RANK],
        w_in[:, kr0:kr0 + ROPE_DIM], zeros(LANES - ROPE_DIM),
        w_in[:, kr0 + half:kr0 + ROPE_DIM], w_in[:, kr0:kr0 + half], zeros(LANES - ROPE_DIM),
        w_in[:, o_dt:o_dt + SSM_HEADS], zeros(LANES - SSM_HEADS)], axis=1)
    w_uq = p["w_uq"][l].reshape(Q_RANK, MLA_HEADS, NOPE_DIM + ROPE_DIM)
    wq_nope = w_uq[:, :, :NOPE_DIM].reshape(Q_RANK, -1)
    wq_rope = w_uq[:, :, NOPE_DIM:]
    wq_rope_sw = jnp.concatenate([wq_rope[:, :, half:], wq_rope[:, :, :half]], axis=-1)
    wq = jnp.concatenate([wq_nope, wq_rope.reshape(Q_RANK, -1), wq_rope_sw.reshape(Q_RANK, -1)], axis=1)
    w_ukv = p["w_ukv"][l]
    wuk_t = jnp.transpose(w_ukv[:, :, :NOPE_DIM], (1, 2, 0))
    wuv = jnp.transpose(w_ukv[:, :, NOPE_DIM:], (1, 0, 2))
    pad_heads = lambda v, fill: jnp.concatenate(
        [v.astype(F32), jnp.full((LANES - SSM_HEADS,), fill, F32)]).reshape(1, LANES)
    bf = lambda a: a.astype(BF16)
    return dict(
        w_small=bf(w_small), w_z=bf(w_in[:, o_z:o_z + d_inner]), w_xbc=bf(w_in[:, o_xbc:o_xbc + conv_ch]),
        w_qm=bf(w_in[:, o_qm:o_g]), w_g=bf(w_in[:, o_g:]),
        q_norm=p["q_norm"][l].reshape(1, -1), kv_norm=p["kv_norm"][l].reshape(1, -1),
        wq=bf(wq), wuk_t=bf(wuk_t), wuv=bf(wuv),
        conv_w_half=0.5 * p["conv_w"][l], conv_b_half=0.5 * p["conv_b"][l].reshape(1, -1),
        dt_bias=pad_heads(p["dt_bias"][l], 0.0),
        a_neg=pad_heads(-jnp.exp(p["a_log"][l].astype(F32)), 0.0),
        d_skip=jnp.repeat(p["d_skip"][l].astype(F32), SSM_HEAD_DIM).reshape(1, -1),
        ssm_norm=p["ssm_norm"][l].reshape(1, -1),
        w_mem_kv=bf(p["w_mem_kv"][l]),
        wa=bf(p["w_proj_a"][l]), wb=bf(p["w_proj_b"][l]), wc=bf(p["w_proj_c"][l]), wo=bf(p["w_out"][l]),
        ln1_g=p["ln1_g"][l].reshape(1, -1), ln1_b=p["ln1_b"][l].reshape(1, -1),
        ln2_g=p["ln2_g"][l].reshape(1, -1), ln2_b=p["ln2_b"][l].reshape(1, -1),
    )


def _dispatch_plan(tinfo, counts, t):
    n_tok_tiles = tinfo.shape[0]
    n8 = tinfo[:, 0, :N_EXPERTS].astype(I32)
    block_start = tinfo[:, 1, :N_EXPERTS].astype(I32)
    carry = tinfo[:, 2, :N_EXPERTS].astype(I32)
    rows = counts[0, :N_EXPERTS].astype(I32)
    tiles = (rows + MOE_TM - 1) // MOE_TM
    tile_end = jnp.cumsum(tiles)
    row_start = (tile_end - tiles) * MOE_TM
    plan = jnp.concatenate([n8 // ROW_ALIGN, block_start, row_start[None, :] + carry], axis=1)
    max_rows = TOP_K * t + N_EXPERTS * (ROW_ALIGN - 1) * n_tok_tiles
    n_tiles = -(-max_rows // MOE_TM) + N_EXPERTS
    tile_ids = jnp.arange(n_tiles, dtype=I32)
    tile_expert = jnp.minimum(jnp.sum(tile_ids[:, None] >= tile_end[None, :], axis=1), N_EXPERTS - 1)
    n_used = tile_end[-1:].astype(I32)
    return plan.reshape(n_tok_tiles, 1, 3 * N_EXPERTS), tile_expert.astype(I32), n_used, n_tiles * MOE_TM


def kernel(x, mem, positions, w_in, q_norm, w_uq, kv_norm, w_ukv, w_proj_a, conv_w, conv_b, dt_bias, a_log,
           d_skip, ssm_norm, w_proj_b, w_mem_kv, w_proj_c, w_out, ln1_g, ln1_b, router_w, router_bias,
           exp_w1, exp_w3, exp_w2, ln2_g, ln2_b):
    params = dict(w_in=w_in, q_norm=q_norm, w_uq=w_uq, kv_norm=kv_norm, w_ukv=w_ukv, w_proj_a=w_proj_a,
                  conv_w=conv_w, conv_b=conv_b, dt_bias=dt_bias, a_log=a_log, d_skip=d_skip,
                  ssm_norm=ssm_norm, w_proj_b=w_proj_b, w_mem_kv=w_mem_kv, w_proj_c=w_proj_c, w_out=w_out,
                  ln1_g=ln1_g, ln1_b=ln1_b, exp_w1=exp_w1, exp_w3=exp_w3, exp_w2=exp_w2,
                  ln2_g=ln2_g, ln2_b=ln2_b)
    bsz, s, d = x.shape
    t = bsz * s
    depth = w_in.shape[0]
    alpha = float((2 * depth) ** 0.25)
    n_mem = mem.shape[1]

    inv = ROPE_THETA ** (-jnp.arange(0, ROPE_DIM, 2, dtype=F32) / ROPE_DIM)
    ang = positions.astype(F32)[..., None] * inv
    cos, sin = jnp.cos(ang), jnp.sin(ang)
    cosq = jnp.tile(jnp.concatenate([cos, cos], axis=-1), (1, 1, LANES // ROPE_DIM)).reshape(t, LANES)
    sinq = jnp.tile(jnp.concatenate([-sin, sin], axis=-1), (1, 1, LANES // ROPE_DIM)).reshape(t, LANES)

    rw = jnp.concatenate([router_w.astype(F32), jnp.zeros((d, LANES - N_EXPERTS), F32)], axis=1)
    rw_hi = rw.astype(BF16)
    rw_lo = (rw - rw_hi.astype(F32)).astype(BF16)
    rbt = jnp.broadcast_to(router_bias.astype(F32)[:, None], (N_EXPERTS, LANES))
    mem_b = mem.reshape(bsz * n_mem, d).astype(BF16)

    xf = x.reshape(t, d).astype(F32)
    xb = xf.astype(BF16)
    for l in range(depth):
        w = _prep_layer(l, params)
        ha = matmul(xb, w["w_small"], F32, "proj_small")
        z = matmul(xb, w["w_z"], BF16, "proj_z")
        xbc_c = proj_conv_silu(xb, w["w_xbc"], w["conv_w_half"], w["conv_b_half"], s).reshape(bsz, s, -1)
        qm = matmul(xb, w["w_qm"], BF16, "proj_qmem")
        g = matmul(xb, w["w_g"], BF16, "proj_gate")

        ql, qr, ck, kr = mla_prep(ha, cosq, sinq, w["q_norm"], w["kv_norm"], w["wq"], w["wuk_t"],
                                  min(ATT_TQ, s))
        o_lat = mla_attention(ql, qr, ck.reshape(bsz, s, KV_RANK), kr.reshape(bsz, s, ROPE_DIM))

        yn = ssd(xbc_c, z.reshape(bsz, s, -1), ha.reshape(bsz, s, HA_W), w["dt_bias"], w["a_neg"],
                 w["d_skip"], w["ssm_norm"]).reshape(t, -1)

        kv = matmul(mem_b, w["w_mem_kv"], BF16, "proj_memkv").reshape(bsz, n_mem, -1)
        cm = mem_attention(qm.reshape(bsz, s, -1), kv).reshape(t, -1)

        x1 = merge_ln(o_lat, yn, cm, g, xf, w["wuv"], w["wa"], w["wb"], w["wc"], w["wo"],
                           w["ln1_g"], w["ln1_b"], alpha)

        info, info_t, tinfo, counts = route(x1, rw_hi, rw_lo, rbt)
        plan, tile_expert, n_used, n_rows = _dispatch_plan(tinfo, counts, t)
        xs = dispatch_rows(x1, info_t, plan, n_rows)
        ys = expert_ffn(xs, exp_w1, exp_w3, exp_w2, l, tile_expert, n_used)
        xf, xb = combine_ln(plan, info, x1, ys, w["ln2_g"], w["ln2_b"], alpha)
    return xf.reshape(bsz, s, d)
```

```python
import functools

import jax
import jax.numpy as jnp
from jax import lax
from jax.experimental import pallas as pl
from jax.experimental.pallas import tpu as pltpu

F32 = jnp.float32
BF16 = jnp.bfloat16
I32 = jnp.int32

MLA_HEADS = 8
Q_RANK = 384
KV_RANK = 256
NOPE_DIM = 128
ROPE_DIM = 64
V_DIM = 128
ROPE_THETA = 10000.0
SSM_HEADS = 32
SSM_HEAD_DIM = 64
SSM_GROUPS = 8
D_STATE = 128
CONV_K = 4
XA_HEADS = 4
XA_HEAD_DIM = 256
N_EXPERTS = 16
N_EXPERT_GROUPS = 4
EXPERTS_PER_GROUP = 4
TOP_K = 2
NORM_EPS = 1e-5
RMS_EPS = 1e-6

LANES = 128
V7X_VMEM_LIMIT = 56 * 1024 * 1024

MM_TM = 2048
MM_TN = 1024
PREP_TM = 1024
ATT_TQ = 128
ATT_TK = 512
ATT_GROUP_ROWS = 512
ATT_UNROLL = 2
LOG2E = 1.4426950408889634
PROJ_CONV_TM = 1024
PROJ_CONV_TN = 1024
PROJ_CONV_SUB = 256
CONV_HALO = 16
SSD_CHUNK = 256
XA_TQ = 2048
MERGE_TM = 512
MOE_TM = 512
DISPATCH_TM = 512
ROW_ALIGN = 16
LOCAL_ROWS = 1280

HA_W = 1024
HA_DQ = 0
HA_C = 384
HA_KR = 640
HA_KRS = 768
HA_DT = 896


def _cp(*sem):
    return pltpu.CompilerParams(dimension_semantics=sem, vmem_limit_bytes=V7X_VMEM_LIMIT)


def _sigmoid(x):
    return 1.0 / (1.0 + jnp.exp(-x))


def _mm_kernel(x_ref, w_ref, o_ref, *, silu):
    y = jnp.dot(x_ref[...], w_ref[...], preferred_element_type=F32)
    if silu:
        half = 0.5 * y
        y = half * jnp.tanh(half) + half
    o_ref[...] = y.astype(o_ref.dtype)


def matmul(x, w, out_dtype, name, silu=False):
    m, k = x.shape
    n = w.shape[1]
    tm = min(MM_TM, m)
    tn = min(MM_TN, n)
    return pl.pallas_call(
        functools.partial(_mm_kernel, silu=silu),
        grid=(n // tn, m // tm),
        in_specs=[pl.BlockSpec((tm, k), lambda j, i: (i, 0)),
                  pl.BlockSpec((k, tn), lambda j, i: (0, j))],
        out_specs=pl.BlockSpec((tm, tn), lambda j, i: (i, j)),
        out_shape=jax.ShapeDtypeStruct((m, n), out_dtype),
        compiler_params=_cp("parallel", "parallel"),
        name=name,
    )(x, w)


def _proj_conv_kernel(x_ref, xh_ref, w_ref, cw_ref, cb_ref, o_ref, *, tiles_per_seq):
    first = (pl.program_id(1) % tiles_per_seq) == 0
    tn = o_ref.shape[1]
    sub = min(PROJ_CONV_SUB, tn)
    def project(c):
        w = w_ref[:, c * sub:(c + 1) * sub]
        u = jnp.dot(x_ref[...], w, preferred_element_type=F32)
        halo = jnp.dot(xh_ref[...], w, preferred_element_type=F32)
        return jnp.concatenate([jnp.where(first, jnp.zeros_like(halo), halo), u], axis=0)

    n_sub = tn // sub
    nxt = project(0)
    for c in range(n_sub):
        cs = slice(c * sub, (c + 1) * sub)
        cw = cw_ref[:, cs]
        ext = nxt
        if c + 1 < n_sub:
            nxt = project(c + 1)
        prev = pltpu.roll(ext, 1, 0)
        near = cw[3:4, :] * ext + cw[2:3, :] * prev
        far = cw[1:2, :] * ext + cw[0:1, :] * prev
        half = (near + pltpu.roll(far, 2, 0))[CONV_HALO:, :] + cb_ref[:, cs]
        o_ref[:, cs] = (half * jnp.tanh(half) + half).astype(o_ref.dtype)


def proj_conv_silu(x, w, conv_w, conv_b, seq_len):
    m, k = x.shape
    n = w.shape[1]
    tm = min(PROJ_CONV_TM, seq_len)
    tn = min(PROJ_CONV_TN, n)
    hb = tm // CONV_HALO
    return pl.pallas_call(
        functools.partial(_proj_conv_kernel, tiles_per_seq=seq_len // tm),
        grid=(n // tn, m // tm),
        in_specs=[pl.BlockSpec((tm, k), lambda j, i: (i, 0)),
                  pl.BlockSpec((CONV_HALO, k), lambda j, i: (jnp.maximum(i * hb - 1, 0), 0)),
                  pl.BlockSpec((k, tn), lambda j, i: (0, j)),
                  pl.BlockSpec((CONV_K, tn), lambda j, i: (0, j)),
                  pl.BlockSpec((1, tn), lambda j, i: (0, j))],
        out_specs=pl.BlockSpec((tm, tn), lambda j, i: (i, j)),
        out_shape=jax.ShapeDtypeStruct((m, n), BF16),
        compiler_params=_cp("parallel", "parallel"),
        name="proj_xbc_conv",
    )(x, x, w, conv_w, conv_b)


def _mla_prep_kernel(ha_ref, cos_ref, sin_ref, qn_ref, kvn_ref, wq_ref, wuk_ref,
                     ql_ref, qr_ref, ck_ref, kr_ref, *, scale):
    nb, _, tq, _ = ql_ref.shape
    ha = ha_ref[...]
    dq = ha[:, HA_DQ:HA_DQ + Q_RANK]
    c_q = dq * lax.rsqrt(jnp.mean(dq * dq, axis=-1, keepdims=True) + RMS_EPS) * qn_ref[...]
    q = jnp.dot(c_q.astype(BF16), wq_ref[...], preferred_element_type=F32)
    n_nope = MLA_HEADS * NOPE_DIM
    n_rope = MLA_HEADS * ROPE_DIM
    cosq = jnp.tile(cos_ref[...], (1, n_rope // LANES))
    sinq = jnp.tile(sin_ref[...], (1, n_rope // LANES))
    q_rope = ((q[:, n_nope:n_nope + n_rope] * cosq + q[:, n_nope + n_rope:] * sinq) * scale).astype(BF16)
    for h in range(MLA_HEADS):
        qh = q[:, h * NOPE_DIM:(h + 1) * NOPE_DIM].astype(BF16)
        ql = jnp.dot(qh, wuk_ref[h], preferred_element_type=F32)
        ql_ref[:, h] = (ql * scale).astype(BF16).reshape(nb, tq, KV_RANK)
        qr_ref[:, h] = q_rope[:, h * ROPE_DIM:(h + 1) * ROPE_DIM].reshape(nb, tq, ROPE_DIM)
    c = ha[:, HA_C:HA_C + KV_RANK]
    c_kv = c * lax.rsqrt(jnp.mean(c * c, axis=-1, keepdims=True) + RMS_EPS) * kvn_ref[...]
    ck_ref[...] = c_kv.astype(BF16)
    k_rope = (ha[:, HA_KR:HA_KR + LANES] * cos_ref[...]
              + ha[:, HA_KRS:HA_KRS + LANES] * sin_ref[...])
    kr_ref[...] = k_rope[:, :ROPE_DIM].astype(BF16)


def mla_prep(ha, cosq, sinq, q_norm, kv_norm, wq, wuk_t, tq):
    t = ha.shape[0]
    tm = min(PREP_TM, t)
    nb = tm // tq
    scale = float((NOPE_DIM + ROPE_DIM) ** -0.5 * LOG2E)
    n_rope = MLA_HEADS * ROPE_DIM
    full = lambda shape: pl.BlockSpec(shape, lambda i: (0,) * len(shape))
    return pl.pallas_call(
        functools.partial(_mla_prep_kernel, scale=scale),
        grid=(t // tm,),
        in_specs=[pl.BlockSpec((tm, HA_W), lambda i: (i, 0)),
                  pl.BlockSpec((tm, LANES), lambda i: (i, 0)),
                  pl.BlockSpec((tm, LANES), lambda i: (i, 0)),
                  full((1, Q_RANK)), full((1, KV_RANK)),
                  full(wq.shape), full(wuk_t.shape)],
        out_specs=[pl.BlockSpec((nb, MLA_HEADS, tq, KV_RANK), lambda i: (i, 0, 0, 0)),
                   pl.BlockSpec((nb, MLA_HEADS, tq, ROPE_DIM), lambda i: (i, 0, 0, 0)),
                   pl.BlockSpec((tm, KV_RANK), lambda i: (i, 0)),
                   pl.BlockSpec((tm, ROPE_DIM), lambda i: (i, 0))],
        out_shape=[jax.ShapeDtypeStruct((t // tq, MLA_HEADS, tq, KV_RANK), BF16),
                   jax.ShapeDtypeStruct((t // tq, MLA_HEADS, tq, ROPE_DIM), BF16),
                   jax.ShapeDtypeStruct((t, KV_RANK), BF16),
                   jax.ShapeDtypeStruct((t, ROPE_DIM), BF16)],
        compiler_params=_cp("parallel"),
        name="mla_prep",
    )(ha, cosq, sinq, q_norm, kv_norm, wq, wuk_t)


def _mla_attn_kernel(ql_ref, qr_ref, ck_ref, kr_ref, o_ref, m_scr, l_scr, acc_scr, *, tq, tk, hpc):
    rc = hpc * tq
    n_groups = MLA_HEADS // hpc
    q_start = pl.program_id(1) * tq
    n_full = q_start // tk
    m_scr[...] = jnp.full(m_scr.shape, -jnp.inf, F32)
    l_scr[...] = jnp.zeros(l_scr.shape, F32)
    acc_scr[...] = jnp.zeros(acc_scr.shape, F32)
    nt = (((1,), (1,)), ((), ()))

    def step(j, masked, width=tk):
        ks = pl.multiple_of(j * tk, tk)
        ck = ck_ref[0, pl.ds(ks, width), :]
        kr = kr_ref[0, pl.ds(ks, width), :]
        if masked:
            q_pos = q_start + (lax.broadcasted_iota(I32, (rc, width), 0) & (tq - 1))
            k_pos = ks + lax.broadcasted_iota(I32, (rc, width), 1)
            visible = k_pos <= q_pos
        for c in range(n_groups):
            rs = slice(c * rc, (c + 1) * rc)
            ql = ql_ref[0, c * hpc:(c + 1) * hpc].reshape(rc, KV_RANK)
            qr = qr_ref[0, c * hpc:(c + 1) * hpc].reshape(rc, ROPE_DIM)
            s = (lax.dot_general(ql, ck, nt, preferred_element_type=F32)
                 + lax.dot_general(qr, kr, nt, preferred_element_type=F32))
            if masked:
                s = jnp.where(visible, s, -jnp.inf)
            m_prev = m_scr[rs]
            m_new = jnp.maximum(m_prev, jnp.max(s, axis=-1, keepdims=True))
            alpha = jnp.exp2(m_prev - m_new)
            p = jnp.exp2(s - jnp.tile(m_new, (1, width // LANES)))
            l_scr[rs] = alpha * l_scr[rs] + jnp.sum(p, axis=-1, keepdims=True)
            acc_scr[rs] = (jnp.tile(alpha, (1, KV_RANK // LANES)) * acc_scr[rs]
                           + jnp.dot(p.astype(BF16), ck, preferred_element_type=F32))
            m_scr[rs] = m_new

    def trip(jj, carry):
        for u in range(ATT_UNROLL):
            step(ATT_UNROLL * jj + u, False)
        return carry

    n_trips = n_full // ATT_UNROLL
    lax.fori_loop(0, n_trips, trip, 0)

    def single(j, carry):
        step(j, False)
        return carry

    lax.fori_loop(n_trips * ATT_UNROLL, n_full, single, 0)

    sub = (q_start - n_full * tk) // tq
    for v in range(tk // tq):
        @pl.when(sub == v)
        def _(v=v):
            step(n_full, True, (v + 1) * tq)
    out = acc_scr[...] / jnp.tile(l_scr[...], (1, KV_RANK // LANES))
    o_ref[0] = out.astype(o_ref.dtype).reshape(MLA_HEADS, tq, KV_RANK)


def mla_attention(ql, qr, ck, kr):
    _, _, tq, _ = ql.shape
    b, s, _ = ck.shape
    tk = min(ATT_TK, s)
    assert tk % tq == 0 and s % tk == 0 and tq & (tq - 1) == 0
    nq = s // tq
    rows = tq * MLA_HEADS
    hpc = max(1, min(MLA_HEADS, ATT_GROUP_ROWS // tq))
    qspec = lambda dim: pl.BlockSpec((1, MLA_HEADS, tq, dim), lambda bi, i: (bi * nq + i, 0, 0, 0))
    return pl.pallas_call(
        functools.partial(_mla_attn_kernel, tq=tq, tk=tk, hpc=hpc),
        grid=(b, nq),
        in_specs=[qspec(KV_RANK), qspec(ROPE_DIM),
                  pl.BlockSpec((1, s, KV_RANK), lambda bi, i: (bi, 0, 0)),
                  pl.BlockSpec((1, s, ROPE_DIM), lambda bi, i: (bi, 0, 0))],
        out_specs=qspec(KV_RANK),
        out_shape=jax.ShapeDtypeStruct(ql.shape, BF16),
        scratch_shapes=[pltpu.VMEM((rows, LANES), F32), pltpu.VMEM((rows, LANES), F32),
                        pltpu.VMEM((rows, KV_RANK), F32)],
        compiler_params=_cp("parallel", "parallel"),
        name="mla_attention",
    )(ql, qr, ck, kr)


def _ssd_kernel(xbc_ref, z_ref, dt_ref, dtb_ref, a_ref, dsk_ref, ng_ref, exp_ref, o_ref,
                state_scr, *, chunk):
    d_inner = SSM_HEADS * SSM_HEAD_DIM
    gn = SSM_GROUPS * D_STATE
    rep = SSM_HEADS // SSM_GROUPS
    gw = rep * SSM_HEAD_DIM

    @pl.when(pl.program_id(1) == 0)
    def _():
        state_scr[...] = jnp.zeros(state_scr.shape, F32)

    x_raw = dt_ref[0] + dtb_ref[...]
    dt = jnp.maximum(x_raw, 0.0) + jnp.log(1.0 + jnp.exp(-jnp.abs(x_raw)))
    da = dt * a_ref[...]
    row = lax.broadcasted_iota(I32, (chunk, chunk), 0)
    col = lax.broadcasted_iota(I32, (chunk, chunk), 1)
    causal = row >= col
    tri = jnp.where(causal, 1.0, 0.0).astype(BF16)
    acum = jnp.zeros(da.shape, F32)
    rem = da
    for _ in range(3):
        part = rem.astype(BF16)
        acum = acum + jnp.dot(tri, part, preferred_element_type=F32)
        rem = rem - part.astype(F32)
    a2 = acum * LOG2E
    a2_t = a2.T
    src_t = a2_t - jnp.log2(dt.T)
    w_t = jnp.exp2(a2_t[:, chunk - 1:chunk] - src_t)
    e_end = jnp.exp2(a2[chunk - 1:chunk, :])
    e_all = jnp.dot(jnp.exp2(a2).astype(BF16), exp_ref[...], preferred_element_type=F32)
    head_of_lane = lax.broadcasted_iota(I32, (1, gw), 1) // SSM_HEAD_DIM

    for g in range(SSM_GROUPS):
        bg = xbc_ref[0, :, d_inner + g * D_STATE:d_inner + (g + 1) * D_STATE]
        cg = xbc_ref[0, :, d_inner + gn + g * D_STATE:d_inner + gn + (g + 1) * D_STATE]
        bt = bg.astype(F32).T
        cb = jnp.dot(cg, bt.astype(BF16), preferred_element_type=F32)
        st = state_scr[g]
        ys = jnp.dot(cg, st.astype(BF16), preferred_element_type=F32)
        xg = xbc_ref[0, :, g * gw:(g + 1) * gw].astype(F32)
        y = jnp.zeros((chunk, gw), F32)
        upd = jnp.zeros((D_STATE, gw), F32)
        sc = jnp.zeros((1, gw), F32)
        for r in range(rep):
            h = g * rep + r
            own = head_of_lane == r
            seg = a2[:, h:h + 1] - src_t[h:h + 1, :]
            m = (cb * jnp.exp2(jnp.where(causal, seg, -jnp.inf))).astype(BF16)
            xm = jnp.where(own, xg, 0.0).astype(BF16)
            y = y + jnp.dot(m, xm, preferred_element_type=F32)
            upd = upd + jnp.dot((bt * w_t[h:h + 1, :]).astype(BF16), xm, preferred_element_type=F32)
            sc = jnp.where(own, e_end[:, h:h + 1], sc)
        state_scr[g] = st * sc + upd
        gs = slice(g * gw, (g + 1) * gw)
        y = y + ys * e_all[:, gs] + xg * dsk_ref[:, gs]
        y = y * z_ref[0, :, gs].astype(F32)
        y = y * lax.rsqrt(jnp.mean(y * y, axis=-1, keepdims=True) + RMS_EPS) * ng_ref[:, gs]
        o_ref[0, :, gs] = y.astype(o_ref.dtype)


def ssd(xbc, z, ha3, dt_bias, a_neg, d_skip, norm_g):
    bsz, s, c = xbc.shape
    d_inner = SSM_HEADS * SSM_HEAD_DIM
    chunk = min(SSD_CHUNK, s)
    gw = d_inner // SSM_GROUPS
    vec = lambda n: pl.BlockSpec((1, n), lambda bi, ci: (0, 0))
    expand = (jnp.arange(d_inner, dtype=I32)[None, :] // SSM_HEAD_DIM
              == jnp.arange(LANES, dtype=I32)[:, None]).astype(BF16)
    return pl.pallas_call(
        functools.partial(_ssd_kernel, chunk=chunk),
        grid=(bsz, s // chunk),
        in_specs=[pl.BlockSpec((1, chunk, c), lambda bi, ci: (bi, ci, 0)),
                  pl.BlockSpec((1, chunk, d_inner), lambda bi, ci: (bi, ci, 0)),
                  pl.BlockSpec((1, chunk, LANES), lambda bi, ci: (bi, ci, HA_DT // LANES)),
                  vec(LANES), vec(LANES), vec(d_inner), vec(d_inner),
                  pl.BlockSpec((LANES, d_inner), lambda bi, ci: (0, 0))],
        out_specs=pl.BlockSpec((1, chunk, d_inner), lambda bi, ci: (bi, ci, 0)),
        out_shape=jax.ShapeDtypeStruct((bsz, s, d_inner), BF16),
        scratch_shapes=[pltpu.VMEM((SSM_GROUPS, D_STATE, gw), F32)],
        compiler_params=_cp("parallel", "arbitrary"),
        name="ssd_scan",
    )(xbc, z, ha3, dt_bias, a_neg, d_skip, norm_g, expand)


def _xattn_kernel(q_ref, kv_ref, o_ref, *, scale):
    hd = XA_HEADS * XA_HEAD_DIM
    nt = (((1,), (1,)), ((), ()))
    for h in range(XA_HEADS):
        sl = slice(h * XA_HEAD_DIM, (h + 1) * XA_HEAD_DIM)
        q = q_ref[0, :, sl]
        k = kv_ref[0, :, sl]
        v = kv_ref[0, :, hd + h * XA_HEAD_DIM:hd + (h + 1) * XA_HEAD_DIM]
        s = lax.dot_general(q, k, nt, preferred_element_type=F32) * scale
        p = jnp.exp(s - jnp.max(s, axis=-1, keepdims=True))
        l = jnp.sum(p, axis=-1, keepdims=True)
        o = jnp.dot(p.astype(BF16), v, preferred_element_type=F32) / l
        o_ref[0, :, sl] = o.astype(o_ref.dtype)


def mem_attention(q, kv):
    bsz, s, hd = q.shape
    m = kv.shape[1]
    tq = min(XA_TQ, s)
    return pl.pallas_call(
        functools.partial(_xattn_kernel, scale=float(XA_HEAD_DIM ** -0.5)),
        grid=(bsz, s // tq),
        in_specs=[pl.BlockSpec((1, tq, hd), lambda bi, i: (bi, i, 0)),
                  pl.BlockSpec((1, m, 2 * hd), lambda bi, i: (bi, 0, 0))],
        out_specs=pl.BlockSpec((1, tq, hd), lambda bi, i: (bi, i, 0)),
        out_shape=jax.ShapeDtypeStruct((bsz, s, hd), BF16),
        compiler_params=_cp("parallel", "parallel"),
        name="mem_attention",
    )(q, kv)


def _layer_norm(v, g, b):
    mu = jnp.mean(v, axis=-1, keepdims=True)
    d = v - mu
    var = jnp.mean(d * d, axis=-1, keepdims=True)
    return d * lax.rsqrt(var + NORM_EPS) * g + b


def _merge_kernel(ol_ref, yn_ref, cm_ref, g_ref, x_ref, wuv_ref, wa_ref, wb_ref, wc_ref, wo_ref,
                  lg_ref, lb_ref, xo_ref, a_scr, *, alpha):
    tm, d = x_ref.shape
    for h in range(MLA_HEADS):
        a_scr[:, h * V_DIM:(h + 1) * V_DIM] = jnp.dot(
            ol_ref[:, h].reshape(tm, KV_RANK), wuv_ref[h],
            preferred_element_type=F32).astype(BF16)
    o_a = jnp.dot(a_scr[...], wa_ref[...], preferred_element_type=F32)
    o_b = jnp.dot(yn_ref[...], wb_ref[...], preferred_element_type=F32)
    o_c = jnp.dot(cm_ref[...], wc_ref[...], preferred_element_type=F32)
    g = g_ref[...].astype(F32)
    merged = (_sigmoid(g[:, :d]) * o_a + _sigmoid(g[:, d:2 * d]) * o_b + _sigmoid(g[:, 2 * d:]) * o_c)
    y = alpha * x_ref[...] + jnp.dot(merged.astype(BF16), wo_ref[...], preferred_element_type=F32)
    xo_ref[...] = _layer_norm(y, lg_ref[...], lb_ref[...])


def merge_ln(ol, yn, cm, g, x, wuv, wa, wb, wc, wo, ln_g, ln_b, alpha):
    t, d = x.shape
    tm = min(MERGE_TM, t)
    tq = ol.shape[2]
    row = lambda n: pl.BlockSpec((tm, n), lambda i: (i, 0))
    full = lambda a: pl.BlockSpec(a.shape, lambda i: (0,) * a.ndim, pipeline_mode=pl.Buffered(1))
    return pl.pallas_call(
        functools.partial(_merge_kernel, alpha=alpha),
        grid=(t // tm,),
        in_specs=[pl.BlockSpec((tm // tq, MLA_HEADS, tq, KV_RANK), lambda i: (i, 0, 0, 0)),
                  row(yn.shape[1]), row(cm.shape[1]), row(g.shape[1]), row(d),
                  full(wuv), full(wa), full(wb), full(wc), full(wo), full(ln_g), full(ln_b)],
        out_specs=row(d),
        out_shape=jax.ShapeDtypeStruct((t, d), F32),
        scratch_shapes=[pltpu.VMEM((tm, MLA_HEADS * V_DIM), BF16)],
        compiler_params=_cp("parallel"),
        name="merge_ln",
    )(ol, yn, cm, g, x, wuv, wa, wb, wc, wo, ln_g, ln_b)


def _first_max(v, expert):
    m = jnp.max(v, axis=0, keepdims=True)
    idx = jnp.min(jnp.where(v == m, expert, N_EXPERTS), axis=0, keepdims=True)
    return m, idx


def _route_kernel(x_ref, rwh_ref, rwl_ref, rbt_ref, info_ref, infot_ref, tinfo_ref, cnt_ref, carry_scr):
    tm = x_ref.shape[0]

    @pl.when(pl.program_id(0) == 0)
    def _():
        carry_scr[...] = jnp.zeros(carry_scr.shape, F32)

    x = x_ref[...]
    x_hi = x.astype(BF16)
    x_lo = (x - x_hi.astype(F32)).astype(BF16)
    logits = (jnp.dot(x_hi, rwh_ref[...], preferred_element_type=F32)
              + jnp.dot(x_lo, rwh_ref[...], preferred_element_type=F32)
              + jnp.dot(x_hi, rwl_ref[...], preferred_element_type=F32))
    scores = _sigmoid(logits.T[:N_EXPERTS, :])
    sel = scores + jnp.tile(rbt_ref[...], (1, tm // LANES))
    expert = lax.broadcasted_iota(I32, (N_EXPERTS, tm), 0)
    neg = -jnp.inf
    best_score = None
    best_group = None
    for j in range(N_EXPERT_GROUPS):
        in_j = (expert >= j * EXPERTS_PER_GROUP) & (expert < (j + 1) * EXPERTS_PER_GROUP)
        v = jnp.where(in_j, sel, neg)
        m1, i1 = _first_max(v, expert)
        m2, _ = _first_max(jnp.where(expert == i1, neg, v), expert)
        gs = m1 + m2
        if j == 0:
            best_score, best_group = gs, jnp.zeros_like(i1)
        else:
            better = gs > best_score
            best_score = jnp.where(better, gs, best_score)
            best_group = jnp.where(better, j, best_group)
    lo = best_group * EXPERTS_PER_GROUP
    v = jnp.where((expert >= lo) & (expert < lo + EXPERTS_PER_GROUP), sel, neg)
    _, e1 = _first_max(v, expert)
    _, e2 = _first_max(jnp.where(expert == e1, neg, v), expert)
    w1 = jnp.sum(jnp.where(expert == e1, scores, 0.0), axis=0, keepdims=True)
    w2 = jnp.sum(jnp.where(expert == e2, scores, 0.0), axis=0, keepdims=True)
    wsum = w1 + w2
    member = jnp.where(expert == e1, 1.0, jnp.where(expert == e2, 1.0, 0.0))
    row = lax.broadcasted_iota(I32, (tm, tm), 0)
    col = lax.broadcasted_iota(I32, (tm, tm), 1)
    earlier = jnp.where(row < col, 1.0, 0.0).astype(BF16)
    lrank = jnp.dot(member.astype(BF16), earlier, preferred_element_type=F32)
    n_col = jnp.sum(member, axis=1, keepdims=True)
    diag = (lax.broadcasted_iota(I32, (N_EXPERTS, LANES), 0)
            == lax.broadcasted_iota(I32, (N_EXPERTS, LANES), 1))
    n = jnp.sum(jnp.where(diag, n_col, 0.0), axis=0, keepdims=True)
    n8 = jnp.floor((n + (ROW_ALIGN - 1)) * (1.0 / ROW_ALIGN)) * ROW_ALIGN
    ua = lax.broadcasted_iota(I32, (LANES, LANES), 0)
    ub = lax.broadcasted_iota(I32, (LANES, LANES), 1)
    lower_experts = jnp.where(ua < ub, 1.0, 0.0).astype(BF16)
    n8_rows = jnp.broadcast_to(n8, (8, LANES))
    block_start = jnp.dot(n8_rows.astype(BF16), lower_experts, preferred_element_type=F32)
    start_col = jnp.sum(jnp.where(diag, block_start[0:1, :], 0.0), axis=1, keepdims=True)
    local_row = start_col + lrank
    j1 = jnp.sum(jnp.where(expert == e1, local_row, 0.0), axis=0, keepdims=True)
    j2 = jnp.sum(jnp.where(expert == e2, local_row, 0.0), axis=0, keepdims=True)
    carry = carry_scr[...]
    sub = lax.broadcasted_iota(I32, (8, LANES), 0)
    tinfo_ref[0] = jnp.where(sub == 0, n8_rows, jnp.where(sub == 1, block_start, jnp.where(sub == 2, carry, 0.0)))
    new_carry = carry + n8_rows
    carry_scr[...] = new_carry
    cnt_ref[...] = new_carry
    field = lax.broadcasted_iota(I32, (8, tm), 0)
    info_t = jnp.where(field == 0, e1.astype(F32),
             jnp.where(field == 1, e2.astype(F32),
             jnp.where(field == 2, w1 / wsum,
             jnp.where(field == 3, w2 / wsum,
             jnp.where(field == 4, j1,
             jnp.where(field == 5, j2, 0.0))))))
    infot_ref[0] = info_t
    info_ref[...] = jnp.concatenate([info_t, jnp.zeros((LANES - 8, tm), F32)], axis=0).T


def route(x, rw_hi, rw_lo, rb):
    t, d = x.shape
    tm = min(DISPATCH_TM, t)
    return pl.pallas_call(
        _route_kernel,
        grid=(t // tm,),
        in_specs=[pl.BlockSpec((tm, d), lambda i: (i, 0)),
                  pl.BlockSpec((d, LANES), lambda i: (0, 0)),
                  pl.BlockSpec((d, LANES), lambda i: (0, 0)),
                  pl.BlockSpec((N_EXPERTS, LANES), lambda i: (0, 0))],
        out_specs=[pl.BlockSpec((tm, LANES), lambda i: (i, 0)),
                   pl.BlockSpec((1, 8, tm), lambda i: (i, 0, 0)),
                   pl.BlockSpec((1, 8, LANES), lambda i: (i, 0, 0)),
                   pl.BlockSpec((8, LANES), lambda i: (0, 0))],
        out_shape=[jax.ShapeDtypeStruct((t, LANES), F32),
                   jax.ShapeDtypeStruct((t // tm, 8, tm), F32),
                   jax.ShapeDtypeStruct((t // tm, 8, LANES), F32),
                   jax.ShapeDtypeStruct((8, LANES), F32)],
        scratch_shapes=[pltpu.VMEM((8, LANES), F32)],
        compiler_params=_cp("arbitrary"),
        name="route",
    )(x, rw_hi, rw_lo, rb)


def _block_copies(plan_ref, local_ref, sorted_ref, sem, to_sorted, start):
    for e in range(N_EXPERTS):
        count = plan_ref[0, 0, e]
        loc0 = plan_ref[0, 0, N_EXPERTS + e]
        dst0 = plan_ref[0, 0, 2 * N_EXPERTS + e]

        def body(k, carry, loc0=loc0, dst0=dst0):
            loc = local_ref.at[pl.ds(pl.multiple_of(loc0 + k * ROW_ALIGN, ROW_ALIGN), ROW_ALIGN)]
            srt = sorted_ref.at[pl.ds(pl.multiple_of(dst0 + k * ROW_ALIGN, ROW_ALIGN), ROW_ALIGN)]
            cp = pltpu.make_async_copy(loc, srt, sem) if to_sorted else pltpu.make_async_copy(srt, loc, sem)
            if start:
                cp.start()
            else:
                cp.wait()
            return carry

        lax.fori_loop(0, count, body, 0)


def _dispatch_kernel(plan_ref, prev_plan_ref, infot_ref, x_ref, init_ref, o_ref, local_scr, sem):
    del init_ref
    tm = x_ref.shape[0]
    i = pl.program_id(0)
    slot = i % 2
    j1 = infot_ref[0, 4:5, :]
    j2 = infot_ref[0, 5:6, :]
    rowid = lax.broadcasted_iota(I32, (LOCAL_ROWS, tm), 0).astype(F32)
    sel = jnp.where(rowid == j1, 1.0, jnp.where(rowid == j2, 1.0, 0.0)).astype(BF16)
    local_scr[slot] = jnp.dot(sel, x_ref[...].astype(BF16), preferred_element_type=F32).astype(BF16)

    @pl.when(i > 0)
    def _():
        _block_copies(prev_plan_ref, local_scr.at[1 - slot], o_ref, sem.at[1 - slot], True, False)

    _block_copies(plan_ref, local_scr.at[slot], o_ref, sem.at[slot], True, True)

    @pl.when(i == pl.num_programs(0) - 1)
    def _():
        _block_copies(plan_ref, local_scr.at[slot], o_ref, sem.at[slot], True, False)


def dispatch_rows(x, info_t, plan, n_rows):
    t, d = x.shape
    tm = min(DISPATCH_TM, t)
    init = jnp.zeros((n_rows, d), BF16)
    return pl.pallas_call(
        _dispatch_kernel,
        grid=(t // tm,),
        in_specs=[pl.BlockSpec((1, 1, 3 * N_EXPERTS), lambda i: (i, 0, 0), memory_space=pltpu.SMEM),
                  pl.BlockSpec((1, 1, 3 * N_EXPERTS), lambda i: (jnp.maximum(i - 1, 0), 0, 0),
                               memory_space=pltpu.SMEM),
                  pl.BlockSpec((1, 8, tm), lambda i: (i, 0, 0)),
                  pl.BlockSpec((tm, d), lambda i: (i, 0)),
                  pl.BlockSpec(memory_space=pl.ANY)],
        out_specs=pl.BlockSpec(memory_space=pl.ANY),
        out_shape=jax.ShapeDtypeStruct((n_rows, d), BF16),
        scratch_shapes=[pltpu.VMEM((2, LOCAL_ROWS, d), BF16), pltpu.SemaphoreType.DMA((2,))],
        input_output_aliases={4: 0},
        compiler_params=_cp("arbitrary"),
        name="moe_dispatch",
    )(plan, plan, info_t, x, init)


def _expert_kernel(te_ref, nu_ref, xs_ref, w1_ref, w3_ref, w2_ref, o_ref, w13_scr, w2_scr):
    i = pl.program_id(0)
    f = w2_ref.shape[2]
    new_expert = (i == 0) | (te_ref[i] != te_ref[jnp.maximum(i - 1, 0)])

    @pl.when((i < nu_ref[0]) & new_expert)
    def _():
        w13_scr[:, :f] = w1_ref[0, 0].astype(BF16)
        w13_scr[:, f:] = w3_ref[0, 0].astype(BF16)
        w2_scr[...] = w2_ref[0, 0].astype(BF16)

    @pl.when(i < nu_ref[0])
    def _():
        h = jnp.dot(xs_ref[...], w13_scr[...], preferred_element_type=F32)
        h1 = h[:, :f]
        act = (h1 * _sigmoid(h1) * h[:, f:]).astype(BF16)
        o_ref[...] = jnp.dot(act, w2_scr[...], preferred_element_type=F32).astype(o_ref.dtype)

    @pl.when(i >= nu_ref[0])
    def _():
        o_ref[...] = jnp.zeros(o_ref.shape, o_ref.dtype)


def expert_ffn(xs, w1, w3, w2, layer, tile_expert, n_used):
    p, d = xs.shape
    f = w2.shape[2]
    n_tiles = p // MOE_TM
    grid_spec = pltpu.PrefetchScalarGridSpec(
        num_scalar_prefetch=2,
        grid=(n_tiles,),
        in_specs=[pl.BlockSpec((MOE_TM, d), lambda i, te, nu: (i, 0)),
                  pl.BlockSpec((1, 1, d, f), lambda i, te, nu: (layer, te[i], 0, 0)),
                  pl.BlockSpec((1, 1, d, f), lambda i, te, nu: (layer, te[i], 0, 0)),
                  pl.BlockSpec((1, 1, f, d), lambda i, te, nu: (layer, te[i], 0, 0))],
        out_specs=pl.BlockSpec((MOE_TM, d), lambda i, te, nu: (i, 0)),
        scratch_shapes=[pltpu.VMEM((d, 2 * f), BF16), pltpu.VMEM((f, d), BF16)],
    )
    return pl.pallas_call(
        _expert_kernel,
        grid_spec=grid_spec,
        out_shape=jax.ShapeDtypeStruct((p, d), BF16),
        compiler_params=_cp("arbitrary"),
        name="expert_ffn",
    )(tile_expert, n_used, xs, w1, w3, w2)


def _combine_kernel(plan_ref, next_plan_ref, info_ref, x_ref, ys_ref, lg_ref, lb_ref, xo_ref, xb_ref,
                    local_scr, sem, *, alpha):
    tm = x_ref.shape[0]
    i = pl.program_id(0)
    slot = i % 2

    @pl.when(i == 0)
    def _():
        _block_copies(plan_ref, local_scr.at[slot], ys_ref, sem.at[slot], False, True)

    @pl.when(i + 1 < pl.num_programs(0))
    def _():
        _block_copies(next_plan_ref, local_scr.at[1 - slot], ys_ref, sem.at[1 - slot], False, True)

    _block_copies(plan_ref, local_scr.at[slot], ys_ref, sem.at[slot], False, False)
    last = N_EXPERTS - 1
    used = plan_ref[0, 0, N_EXPERTS + last] + plan_ref[0, 0, last] * ROW_ALIGN
    rowid = lax.broadcasted_iota(I32, (LOCAL_ROWS, 1), 0)
    y = local_scr[slot]
    y = jnp.where(rowid < used, y, jnp.zeros_like(y))
    info = info_ref[...]
    col = lax.broadcasted_iota(I32, (tm, LOCAL_ROWS), 1).astype(F32)
    gate = jnp.where(col == info[:, 4:5], info[:, 2:3],
                     jnp.where(col == info[:, 5:6], info[:, 3:4], 0.0)).astype(BF16)
    moe = jnp.dot(gate, y, preferred_element_type=F32)
    out = _layer_norm(alpha * x_ref[...] + moe, lg_ref[...], lb_ref[...])
    xo_ref[...] = out
    xb_ref[...] = out.astype(BF16)


def combine_ln(plan, info, x, ys, ln_g, ln_b, alpha):
    t, d = x.shape
    tm = min(DISPATCH_TM, t)
    return pl.pallas_call(
        functools.partial(_combine_kernel, alpha=alpha),
        grid=(t // tm,),
        in_specs=[pl.BlockSpec((1, 1, 3 * N_EXPERTS), lambda i: (i, 0, 0), memory_space=pltpu.SMEM),
                  pl.BlockSpec((1, 1, 3 * N_EXPERTS), lambda i: (jnp.minimum(i + 1, t // tm - 1), 0, 0),
                               memory_space=pltpu.SMEM),
                  pl.BlockSpec((tm, LANES), lambda i: (i, 0)),
                  pl.BlockSpec((tm, d), lambda i: (i, 0)),
                  pl.BlockSpec(memory_space=pl.ANY),
                  pl.BlockSpec((1, d), lambda i: (0, 0)),
                  pl.BlockSpec((1, d), lambda i: (0, 0))],
        out_specs=[pl.BlockSpec((tm, d), lambda i: (i, 0)), pl.BlockSpec((tm, d), lambda i: (i, 0))],
        out_shape=[jax.ShapeDtypeStruct((t, d), F32), jax.ShapeDtypeStruct((t, d), BF16)],
        scratch_shapes=[pltpu.VMEM((2, LOCAL_ROWS, d), BF16), pltpu.SemaphoreType.DMA((2,))],
        compiler_params=_cp("arbitrary"),
        name="moe_combine_ln",
    )(plan, plan, info, x, ys, ln_g, ln_b)


def _prep_layer(l, p):
    w_in = p["w_in"][l]
    d = w_in.shape[0]
    o_dq = 0
    o_dkv = o_dq + Q_RANK
    o_z = o_dkv + KV_RANK + ROPE_DIM
    d_inner = SSM_HEADS * SSM_HEAD_DIM
    conv_ch = d_inner + 2 * SSM_GROUPS * D_STATE
    o_xbc = o_z + d_inner
    o_dt = o_xbc + conv_ch
    o_qm = o_dt + SSM_HEADS
    o_g = o_qm + XA_HEADS * XA_HEAD_DIM
    half = ROPE_DIM // 2
    kr0 = o_dkv + KV_RANK
    zeros = lambda n: jnp.zeros((d, n), F32)
    w_small = jnp.concatenate([
        w_in[:, o_dq:o_dq + Q_RANK],
        w_in[:, o_dkv:o_dkv + KV_RANK],
        w_in[:, kr0:kr0 + ROPE_DIM], zeros(LANES - ROPE_DIM),
        w_in[:, kr0 + half:kr0 + ROPE_DIM], w_in[:, kr0:kr0 + half], zeros(LANES - ROPE_DIM),
        w_in[:, o_dt:o_dt + SSM_HEADS], zeros(LANES - SSM_HEADS)], axis=1)
    w_uq = p["w_uq"][l].reshape(Q_RANK, MLA_HEADS, NOPE_DIM + ROPE_DIM)
    wq_nope = w_uq[:, :, :NOPE_DIM].reshape(Q_RANK, -1)
    wq_rope = w_uq[:, :, NOPE_DIM:]
    wq_rope_sw = jnp.concatenate([wq_rope[:, :, half:], wq_rope[:, :, :half]], axis=-1)
    wq = jnp.concatenate([wq_nope, wq_rope.reshape(Q_RANK, -1), wq_rope_sw.reshape(Q_RANK, -1)], axis=1)
    w_ukv = p["w_ukv"][l]
    wuk_t = jnp.transpose(w_ukv[:, :, :NOPE_DIM], (1, 2, 0))
    wuv = jnp.transpose(w_ukv[:, :, NOPE_DIM:], (1, 0, 2))
    pad_heads = lambda v, fill: jnp.concatenate(
        [v.astype(F32), jnp.full((LANES - SSM_HEADS,), fill, F32)]).reshape(1, LANES)
    bf = lambda a: a.astype(BF16)
    return dict(
        w_small=bf(w_small), w_z=bf(w_in[:, o_z:o_z + d_inner]), w_xbc=bf(w_in[:, o_xbc:o_xbc + conv_ch]),
        w_qm=bf(w_in[:, o_qm:o_g]), w_g=bf(w_in[:, o_g:]),
        q_norm=p["q_norm"][l].reshape(1, -1), kv_norm=p["kv_norm"][l].reshape(1, -1),
        wq=bf(wq), wuk_t=bf(wuk_t), wuv=bf(wuv),
        conv_w_half=0.5 * p["conv_w"][l], conv_b_half=0.5 * p["conv_b"][l].reshape(1, -1),
        dt_bias=pad_heads(p["dt_bias"][l], 0.0),
        a_neg=pad_heads(-jnp.exp(p["a_log"][l].astype(F32)), 0.0),
        d_skip=jnp.repeat(p["d_skip"][l].astype(F32), SSM_HEAD_DIM).reshape(1, -1),
        ssm_norm=p["ssm_norm"][l].reshape(1, -1),
        w_mem_kv=bf(p["w_mem_kv"][l]),
        wa=bf(p["w_proj_a"][l]), wb=bf(p["w_proj_b"][l]), wc=bf(p["w_proj_c"][l]), wo=bf(p["w_out"][l]),
        ln1_g=p["ln1_g"][l].reshape(1, -1), ln1_b=p["ln1_b"][l].reshape(1, -1),
        ln2_g=p["ln2_g"][l].reshape(1, -1), ln2_b=p["ln2_b"][l].reshape(1, -1),
    )


def _dispatch_plan(tinfo, counts, t):
    n_tok_tiles = tinfo.shape[0]
    n8 = tinfo[:, 0, :N_EXPERTS].astype(I32)
    block_start = tinfo[:, 1, :N_EXPERTS].astype(I32)
    carry = tinfo[:, 2, :N_EXPERTS].astype(I32)
    rows = counts[0, :N_EXPERTS].astype(I32)
    tiles = (rows + MOE_TM - 1) // MOE_TM
    tile_end = jnp.cumsum(tiles)
    row_start = (tile_end - tiles) * MOE_TM
    plan = jnp.concatenate([n8 // ROW_ALIGN, block_start, row_start[None, :] + carry], axis=1)
    max_rows = TOP_K * t + N_EXPERTS * (ROW_ALIGN - 1) * n_tok_tiles
    n_tiles = -(-max_rows // MOE_TM) + N_EXPERTS
    tile_ids = jnp.arange(n_tiles, dtype=I32)
    tile_expert = jnp.minimum(jnp.sum(tile_ids[:, None] >= tile_end[None, :], axis=1), N_EXPERTS - 1)
    n_used = tile_end[-1:].astype(I32)
    return plan.reshape(n_tok_tiles, 1, 3 * N_EXPERTS), tile_expert.astype(I32), n_used, n_tiles * MOE_TM


def kernel(x, mem, positions, w_in, q_norm, w_uq, kv_norm, w_ukv, w_proj_a, conv_w, conv_b, dt_bias, a_log,
           d_skip, ssm_norm, w_proj_b, w_mem_kv, w_proj_c, w_out, ln1_g, ln1_b, router_w, router_bias,
           exp_w1, exp_w3, exp_w2, ln2_g, ln2_b):
    params = dict(w_in=w_in, q_norm=q_norm, w_uq=w_uq, kv_norm=kv_norm, w_ukv=w_ukv, w_proj_a=w_proj_a,
                  conv_w=conv_w, conv_b=conv_b, dt_bias=dt_bias, a_log=a_log, d_skip=d_skip,
                  ssm_norm=ssm_norm, w_proj_b=w_proj_b, w_mem_kv=w_mem_kv, w_proj_c=w_proj_c, w_out=w_out,
                  ln1_g=ln1_g, ln1_b=ln1_b, exp_w1=exp_w1, exp_w3=exp_w3, exp_w2=exp_w2,
                  ln2_g=ln2_g, ln2_b=ln2_b)
    bsz, s, d = x.shape
    t = bsz * s
    depth = w_in.shape[0]
    alpha = float((2 * depth) ** 0.25)
    n_mem = mem.shape[1]

    inv = ROPE_THETA ** (-jnp.arange(0, ROPE_DIM, 2, dtype=F32) / ROPE_DIM)
    ang = positions.astype(F32)[..., None] * inv
    cos, sin = jnp.cos(ang), jnp.sin(ang)
    cosq = jnp.tile(jnp.concatenate([cos, cos], axis=-1), (1, 1, LANES // ROPE_DIM)).reshape(t, LANES)
    sinq = jnp.tile(jnp.concatenate([-sin, sin], axis=-1), (1, 1, LANES // ROPE_DIM)).reshape(t, LANES)

    rw = jnp.concatenate([router_w.astype(F32), jnp.zeros((d, LANES - N_EXPERTS), F32)], axis=1)
    rw_hi = rw.astype(BF16)
    rw_lo = (rw - rw_hi.astype(F32)).astype(BF16)
    rbt = jnp.broadcast_to(router_bias.astype(F32)[:, None], (N_EXPERTS, LANES))
    mem_b = mem.reshape(bsz * n_mem, d).astype(BF16)

    xf = x.reshape(t, d).astype(F32)
    xb = xf.astype(BF16)
    for l in range(depth):
        w = _prep_layer(l, params)
        ha = matmul(xb, w["w_small"], F32, "proj_small")
        z_act = matmul(xb, w["w_z"], BF16, "proj_z_silu", silu=True)
        xbc_c = proj_conv_silu(xb, w["w_xbc"], w["conv_w_half"], w["conv_b_half"], s).reshape(bsz, s, -1)
        qm = matmul(xb, w["w_qm"], BF16, "proj_qmem")
        g = matmul(xb, w["w_g"], BF16, "proj_gate")

        ql, qr, ck, kr = mla_prep(ha, cosq, sinq, w["q_norm"], w["kv_norm"], w["wq"], w["wuk_t"],
                                  min(ATT_TQ, s))
        o_lat = mla_attention(ql, qr, ck.reshape(bsz, s, KV_RANK), kr.reshape(bsz, s, ROPE_DIM))

        yn = ssd(xbc_c, z_act.reshape(bsz, s, -1), ha.reshape(bsz, s, HA_W), w["dt_bias"], w["a_neg"],
                 w["d_skip"], w["ssm_norm"]).reshape(t, -1)

        kv = matmul(mem_b, w["w_mem_kv"], BF16, "proj_memkv").reshape(bsz, n_mem, -1)
        cm = mem_attention(qm.reshape(bsz, s, -1), kv).reshape(t, -1)

        x1 = merge_ln(o_lat, yn, cm, g, xf, w["wuv"], w["wa"], w["wb"], w["wc"], w["wo"],
                           w["ln1_g"], w["ln1_b"], alpha)

        info, info_t, tinfo, counts = route(x1, rw_hi, rw_lo, rbt)
        plan, tile_expert, n_used, n_rows = _dispatch_plan(tinfo, counts, t)
        xs = dispatch_rows(x1, info_t, plan, n_rows)
        ys = expert_ffn(xs, exp_w1, exp_w3, exp_w2, l, tile_expert, n_used)
        xf, xb = combine_ln(plan, info, x1, ys, w["ln2_g"], w["ln2_b"], alpha)
    return xf.reshape(bsz, s, d)
```

```python
import functools

import jax
import jax.numpy as jnp
from jax import lax
from jax.experimental import pallas as pl
from jax.experimental.pallas import tpu as pltpu

F32 = jnp.float32
BF16 = jnp.bfloat16
I32 = jnp.int32

MLA_HEADS = 8
Q_RANK = 384
KV_RANK = 256
NOPE_DIM = 128
ROPE_DIM = 64
V_DIM = 128
ROPE_THETA = 10000.0
SSM_HEADS = 32
SSM_HEAD_DIM = 64
SSM_GROUPS = 8
D_STATE = 128
CONV_K = 4
XA_HEADS = 4
XA_HEAD_DIM = 256
N_EXPERTS = 16
N_EXPERT_GROUPS = 4
EXPERTS_PER_GROUP = 4
TOP_K = 2
NORM_EPS = 1e-5
RMS_EPS = 1e-6

LANES = 128
V7X_VMEM_LIMIT = 56 * 1024 * 1024

MM_TM = 2048
MM_TN = 1024
PREP_TM = 1024
ATT_TQ = 256
ATT_TK = 512
ATT_GROUP_ROWS = 512
ATT_UNROLL = 2
LOG2E = 1.4426950408889634
PROJ_CONV_TM = 1024
PROJ_CONV_TN = 1024
PROJ_CONV_SUB = 256
CONV_HALO = 16
SSD_CHUNK = 256
XA_TQ = 2048
MERGE_TM = 512
MOE_TM = 512
DISPATCH_TM = 512
ROW_ALIGN = 16
LOCAL_ROWS = 1280

HA_W = 1024
HA_DQ = 0
HA_C = 384
HA_KR = 640
HA_KRS = 768
HA_DT = 896


def _cp(*sem):
    return pltpu.CompilerParams(dimension_semantics=sem, vmem_limit_bytes=V7X_VMEM_LIMIT)


def _sigmoid(x):
    return 1.0 / (1.0 + jnp.exp(-x))


def _mm_kernel(x_ref, w_ref, o_ref, *, silu):
    y = jnp.dot(x_ref[...], w_ref[...], preferred_element_type=F32)
    if silu:
        half = 0.5 * y
        y = half * jnp.tanh(half) + half
    o_ref[...] = y.astype(o_ref.dtype)


def matmul(x, w, out_dtype, name, silu=False):
    m, k = x.shape
    n = w.shape[1]
    tm = min(MM_TM, m)
    tn = min(MM_TN, n)
    return pl.pallas_call(
        functools.partial(_mm_kernel, silu=silu),
        grid=(n // tn, m // tm),
        in_specs=[pl.BlockSpec((tm, k), lambda j, i: (i, 0)),
                  pl.BlockSpec((k, tn), lambda j, i: (0, j))],
        out_specs=pl.BlockSpec((tm, tn), lambda j, i: (i, j)),
        out_shape=jax.ShapeDtypeStruct((m, n), out_dtype),
        compiler_params=_cp("parallel", "parallel"),
        name=name,
    )(x, w)


def _proj_conv_kernel(x_ref, xh_ref, w_ref, cw_ref, cb_ref, o_ref, *, tiles_per_seq):
    first = (pl.program_id(1) % tiles_per_seq) == 0
    tn = o_ref.shape[1]
    sub = min(PROJ_CONV_SUB, tn)
    def project(c):
        w = w_ref[:, c * sub:(c + 1) * sub]
        u = jnp.dot(x_ref[...], w, preferred_element_type=F32)
        halo = jnp.dot(xh_ref[...], w, preferred_element_type=F32)
        return jnp.concatenate([jnp.where(first, jnp.zeros_like(halo), halo), u], axis=0)

    n_sub = tn // sub
    nxt = project(0)
    for c in range(n_sub):
        cs = slice(c * sub, (c + 1) * sub)
        cw = cw_ref[:, cs]
        ext = nxt
        if c + 1 < n_sub:
            nxt = project(c + 1)
        prev = pltpu.roll(ext, 1, 0)
        near = cw[3:4, :] * ext + cw[2:3, :] * prev
        far = cw[1:2, :] * ext + cw[0:1, :] * prev
        half = (near + pltpu.roll(far, 2, 0))[CONV_HALO:, :] + cb_ref[:, cs]
        o_ref[:, cs] = (half * jnp.tanh(half) + half).astype(o_ref.dtype)


def proj_conv_silu(x, w, conv_w, conv_b, seq_len):
    m, k = x.shape
    n = w.shape[1]
    tm = min(PROJ_CONV_TM, seq_len)
    tn = min(PROJ_CONV_TN, n)
    hb = tm // CONV_HALO
    return pl.pallas_call(
        functools.partial(_proj_conv_kernel, tiles_per_seq=seq_len // tm),
        grid=(n // tn, m // tm),
        in_specs=[pl.BlockSpec((tm, k), lambda j, i: (i, 0)),
                  pl.BlockSpec((CONV_HALO, k), lambda j, i: (jnp.maximum(i * hb - 1, 0), 0)),
                  pl.BlockSpec((k, tn), lambda j, i: (0, j)),
                  pl.BlockSpec((CONV_K, tn), lambda j, i: (0, j)),
                  pl.BlockSpec((1, tn), lambda j, i: (0, j))],
        out_specs=pl.BlockSpec((tm, tn), lambda j, i: (i, j)),
        out_shape=jax.ShapeDtypeStruct((m, n), BF16),
        compiler_params=_cp("parallel", "parallel"),
        name="proj_xbc_conv",
    )(x, x, w, conv_w, conv_b)


def _mla_prep_kernel(ha_ref, cos_ref, sin_ref, qn_ref, kvn_ref, wq_ref, wuk_ref,
                     ql_ref, qr_ref, ck_ref, kr_ref, *, scale):
    nb, _, tq, _ = ql_ref.shape
    ha = ha_ref[...]
    dq = ha[:, HA_DQ:HA_DQ + Q_RANK]
    c_q = dq * lax.rsqrt(jnp.mean(dq * dq, axis=-1, keepdims=True) + RMS_EPS) * qn_ref[...]
    q = jnp.dot(c_q.astype(BF16), wq_ref[...], preferred_element_type=F32)
    n_nope = MLA_HEADS * NOPE_DIM
    n_rope = MLA_HEADS * ROPE_DIM
    cosq = jnp.tile(cos_ref[...], (1, n_rope // LANES))
    sinq = jnp.tile(sin_ref[...], (1, n_rope // LANES))
    q_rope = ((q[:, n_nope:n_nope + n_rope] * cosq + q[:, n_nope + n_rope:] * sinq) * scale).astype(BF16)
    for h in range(MLA_HEADS):
        qh = q[:, h * NOPE_DIM:(h + 1) * NOPE_DIM].astype(BF16)
        ql = jnp.dot(qh, wuk_ref[h], preferred_element_type=F32)
        ql_ref[:, h] = (ql * scale).astype(BF16).reshape(nb, tq, KV_RANK)
        qr_ref[:, h] = q_rope[:, h * ROPE_DIM:(h + 1) * ROPE_DIM].reshape(nb, tq, ROPE_DIM)
    c = ha[:, HA_C:HA_C + KV_RANK]
    c_kv = c * lax.rsqrt(jnp.mean(c * c, axis=-1, keepdims=True) + RMS_EPS) * kvn_ref[...]
    ck_ref[...] = c_kv.astype(BF16)
    k_rope = (ha[:, HA_KR:HA_KR + LANES] * cos_ref[...]
              + ha[:, HA_KRS:HA_KRS + LANES] * sin_ref[...])
    kr_ref[...] = k_rope[:, :ROPE_DIM].astype(BF16)


def mla_prep(ha, cosq, sinq, q_norm, kv_norm, wq, wuk_t, tq):
    t = ha.shape[0]
    tm = min(PREP_TM, t)
    nb = tm // tq
    scale = float((NOPE_DIM + ROPE_DIM) ** -0.5 * LOG2E)
    n_rope = MLA_HEADS * ROPE_DIM
    full = lambda shape: pl.BlockSpec(shape, lambda i: (0,) * len(shape))
    return pl.pallas_call(
        functools.partial(_mla_prep_kernel, scale=scale),
        grid=(t // tm,),
        in_specs=[pl.BlockSpec((tm, HA_W), lambda i: (i, 0)),
                  pl.BlockSpec((tm, LANES), lambda i: (i, 0)),
                  pl.BlockSpec((tm, LANES), lambda i: (i, 0)),
                  full((1, Q_RANK)), full((1, KV_RANK)),
                  full(wq.shape), full(wuk_t.shape)],
        out_specs=[pl.BlockSpec((nb, MLA_HEADS, tq, KV_RANK), lambda i: (i, 0, 0, 0)),
                   pl.BlockSpec((nb, MLA_HEADS, tq, ROPE_DIM), lambda i: (i, 0, 0, 0)),
                   pl.BlockSpec((tm, KV_RANK), lambda i: (i, 0)),
                   pl.BlockSpec((tm, ROPE_DIM), lambda i: (i, 0))],
        out_shape=[jax.ShapeDtypeStruct((t // tq, MLA_HEADS, tq, KV_RANK), BF16),
                   jax.ShapeDtypeStruct((t // tq, MLA_HEADS, tq, ROPE_DIM), BF16),
                   jax.ShapeDtypeStruct((t, KV_RANK), BF16),
                   jax.ShapeDtypeStruct((t, ROPE_DIM), BF16)],
        compiler_params=_cp("parallel"),
        name="mla_prep",
    )(ha, cosq, sinq, q_norm, kv_norm, wq, wuk_t)


def _mla_attn_kernel(ql_ref, qr_ref, ck_ref, kr_ref, o_ref, m_scr, l_scr, acc_scr, *, tq, tk, hpc):
    rc = hpc * tq
    n_groups = MLA_HEADS // hpc
    q_start = pl.program_id(1) * tq
    n_full = q_start // tk
    m_scr[...] = jnp.full(m_scr.shape, -jnp.inf, F32)
    l_scr[...] = jnp.zeros(l_scr.shape, F32)
    acc_scr[...] = jnp.zeros(acc_scr.shape, F32)
    nt = (((1,), (1,)), ((), ()))

    def step(j, masked, width=tk):
        ks = pl.multiple_of(j * tk, tk)
        ck = ck_ref[0, pl.ds(ks, width), :]
        kr = kr_ref[0, pl.ds(ks, width), :]
        if masked:
            q_pos = q_start + (lax.broadcasted_iota(I32, (rc, width), 0) & (tq - 1))
            k_pos = ks + lax.broadcasted_iota(I32, (rc, width), 1)
            visible = k_pos <= q_pos
        for c in range(n_groups):
            rs = slice(c * rc, (c + 1) * rc)
            ql = ql_ref[0, c * hpc:(c + 1) * hpc].reshape(rc, KV_RANK)
            qr = qr_ref[0, c * hpc:(c + 1) * hpc].reshape(rc, ROPE_DIM)
            s = (lax.dot_general(ql, ck, nt, preferred_element_type=F32)
                 + lax.dot_general(qr, kr, nt, preferred_element_type=F32))
            if masked:
                s = jnp.where(visible, s, -jnp.inf)
            m_prev = m_scr[rs]
            m_new = jnp.maximum(m_prev, jnp.max(s, axis=-1, keepdims=True))
            alpha = jnp.exp2(m_prev - m_new)
            p = jnp.exp2(s - jnp.tile(m_new, (1, width // LANES)))
            l_scr[rs] = alpha * l_scr[rs] + jnp.sum(p, axis=-1, keepdims=True)
            acc_scr[rs] = (jnp.tile(alpha, (1, KV_RANK // LANES)) * acc_scr[rs]
                           + jnp.dot(p.astype(BF16), ck, preferred_element_type=F32))
            m_scr[rs] = m_new

    def trip(jj, carry):
        for u in range(ATT_UNROLL):
            step(ATT_UNROLL * jj + u, False)
        return carry

    n_trips = n_full // ATT_UNROLL
    lax.fori_loop(0, n_trips, trip, 0)

    def single(j, carry):
        step(j, False)
        return carry

    lax.fori_loop(n_trips * ATT_UNROLL, n_full, single, 0)

    sub = (q_start - n_full * tk) // tq
    for v in range(tk // tq):
        @pl.when(sub == v)
        def _(v=v):
            step(n_full, True, (v + 1) * tq)
    out = acc_scr[...] / jnp.tile(l_scr[...], (1, KV_RANK // LANES))
    o_ref[0] = out.astype(o_ref.dtype).reshape(MLA_HEADS, tq, KV_RANK)


def mla_attention(ql, qr, ck, kr):
    _, _, tq, _ = ql.shape
    b, s, _ = ck.shape
    tk = min(ATT_TK, s)
    assert tk % tq == 0 and s % tk == 0 and tq & (tq - 1) == 0
    nq = s // tq
    rows = tq * MLA_HEADS
    hpc = max(1, min(MLA_HEADS, ATT_GROUP_ROWS // tq))
    qspec = lambda dim: pl.BlockSpec((1, MLA_HEADS, tq, dim), lambda bi, i: (bi * nq + i, 0, 0, 0))
    return pl.pallas_call(
        functools.partial(_mla_attn_kernel, tq=tq, tk=tk, hpc=hpc),
        grid=(b, nq),
        in_specs=[qspec(KV_RANK), qspec(ROPE_DIM),
                  pl.BlockSpec((1, s, KV_RANK), lambda bi, i: (bi, 0, 0)),
                  pl.BlockSpec((1, s, ROPE_DIM), lambda bi, i: (bi, 0, 0))],
        out_specs=qspec(KV_RANK),
        out_shape=jax.ShapeDtypeStruct(ql.shape, BF16),
        scratch_shapes=[pltpu.VMEM((rows, LANES), F32), pltpu.VMEM((rows, LANES), F32),
                        pltpu.VMEM((rows, KV_RANK), F32)],
        compiler_params=_cp("parallel", "parallel"),
        name="mla_attention",
    )(ql, qr, ck, kr)


def _ssd_kernel(xbc_ref, z_ref, dt_ref, dtb_ref, a_ref, dsk_ref, ng_ref, exp_ref, o_ref,
                state_scr, *, chunk):
    d_inner = SSM_HEADS * SSM_HEAD_DIM
    gn = SSM_GROUPS * D_STATE
    rep = SSM_HEADS // SSM_GROUPS
    gw = rep * SSM_HEAD_DIM

    @pl.when(pl.program_id(1) == 0)
    def _():
        state_scr[...] = jnp.zeros(state_scr.shape, F32)

    x_raw = dt_ref[0] + dtb_ref[...]
    dt = jnp.maximum(x_raw, 0.0) + jnp.log(1.0 + jnp.exp(-jnp.abs(x_raw)))
    da = dt * a_ref[...]
    row = lax.broadcasted_iota(I32, (chunk, chunk), 0)
    col = lax.broadcasted_iota(I32, (chunk, chunk), 1)
    causal = row >= col
    tri = jnp.where(causal, 1.0, 0.0).astype(BF16)
    acum = jnp.zeros(da.shape, F32)
    rem = da
    for _ in range(3):
        part = rem.astype(BF16)
        acum = acum + jnp.dot(tri, part, preferred_element_type=F32)
        rem = rem - part.astype(F32)
    a2 = acum * LOG2E
    a2_t = a2.T
    src_t = a2_t - jnp.log2(dt.T)
    w_t = jnp.exp2(a2_t[:, chunk - 1:chunk] - src_t)
    e_end = jnp.exp2(a2[chunk - 1:chunk, :])
    e_all = jnp.dot(jnp.exp2(a2).astype(BF16), exp_ref[...], preferred_element_type=F32)
    head_of_lane = lax.broadcasted_iota(I32, (1, gw), 1) // SSM_HEAD_DIM

    for g in range(SSM_GROUPS):
        bg = xbc_ref[0, :, d_inner + g * D_STATE:d_inner + (g + 1) * D_STATE]
        cg = xbc_ref[0, :, d_inner + gn + g * D_STATE:d_inner + gn + (g + 1) * D_STATE]
        bt = bg.astype(F32).T
        cb = jnp.dot(cg, bt.astype(BF16), preferred_element_type=F32)
        st = state_scr[g]
        ys = jnp.dot(cg, st.astype(BF16), preferred_element_type=F32)
        xg = xbc_ref[0, :, g * gw:(g + 1) * gw].astype(F32)
        y = jnp.zeros((chunk, gw), F32)
        upd = jnp.zeros((D_STATE, gw), F32)
        sc = jnp.zeros((1, gw), F32)
        for r in range(rep):
            h = g * rep + r
            own = head_of_lane == r
            seg = a2[:, h:h + 1] - src_t[h:h + 1, :]
            m = (cb * jnp.exp2(jnp.where(causal, seg, -jnp.inf))).astype(BF16)
            xm = jnp.where(own, xg, 0.0).astype(BF16)
            y = y + jnp.dot(m, xm, preferred_element_type=F32)
            upd = upd + jnp.dot((bt * w_t[h:h + 1, :]).astype(BF16), xm, preferred_element_type=F32)
            sc = jnp.where(own, e_end[:, h:h + 1], sc)
        state_scr[g] = st * sc + upd
        gs = slice(g * gw, (g + 1) * gw)
        y = y + ys * e_all[:, gs] + xg * dsk_ref[:, gs]
        y = y * z_ref[0, :, gs].astype(F32)
        y = y * lax.rsqrt(jnp.mean(y * y, axis=-1, keepdims=True) + RMS_EPS) * ng_ref[:, gs]
        o_ref[0, :, gs] = y.astype(o_ref.dtype)


def ssd(xbc, z, ha3, dt_bias, a_neg, d_skip, norm_g):
    bsz, s, c = xbc.shape
    d_inner = SSM_HEADS * SSM_HEAD_DIM
    chunk = min(SSD_CHUNK, s)
    gw = d_inner // SSM_GROUPS
    vec = lambda n: pl.BlockSpec((1, n), lambda bi, ci: (0, 0))
    expand = (jnp.arange(d_inner, dtype=I32)[None, :] // SSM_HEAD_DIM
              == jnp.arange(LANES, dtype=I32)[:, None]).astype(BF16)
    return pl.pallas_call(
        functools.partial(_ssd_kernel, chunk=chunk),
        grid=(bsz, s // chunk),
        in_specs=[pl.BlockSpec((1, chunk, c), lambda bi, ci: (bi, ci, 0)),
                  pl.BlockSpec((1, chunk, d_inner), lambda bi, ci: (bi, ci, 0)),
                  pl.BlockSpec((1, chunk, LANES), lambda bi, ci: (bi, ci, HA_DT // LANES)),
                  vec(LANES), vec(LANES), vec(d_inner), vec(d_inner),
                  pl.BlockSpec((LANES, d_inner), lambda bi, ci: (0, 0))],
        out_specs=pl.BlockSpec((1, chunk, d_inner), lambda bi, ci: (bi, ci, 0)),
        out_shape=jax.ShapeDtypeStruct((bsz, s, d_inner), BF16),
        scratch_shapes=[pltpu.VMEM((SSM_GROUPS, D_STATE, gw), F32)],
        compiler_params=_cp("parallel", "arbitrary"),
        name="ssd_scan",
    )(xbc, z, ha3, dt_bias, a_neg, d_skip, norm_g, expand)


def _xattn_kernel(q_ref, kv_ref, o_ref, *, scale):
    hd = XA_HEADS * XA_HEAD_DIM
    nt = (((1,), (1,)), ((), ()))
    for h in range(XA_HEADS):
        sl = slice(h * XA_HEAD_DIM, (h + 1) * XA_HEAD_DIM)
        q = q_ref[0, :, sl]
        k = kv_ref[0, :, sl]
        v = kv_ref[0, :, hd + h * XA_HEAD_DIM:hd + (h + 1) * XA_HEAD_DIM]
        s = lax.dot_general(q, k, nt, preferred_element_type=F32) * scale
        p = jnp.exp(s - jnp.max(s, axis=-1, keepdims=True))
        l = jnp.sum(p, axis=-1, keepdims=True)
        o = jnp.dot(p.astype(BF16), v, preferred_element_type=F32) / l
        o_ref[0, :, sl] = o.astype(o_ref.dtype)


def mem_attention(q, kv):
    bsz, s, hd = q.shape
    m = kv.shape[1]
    tq = min(XA_TQ, s)
    return pl.pallas_call(
        functools.partial(_xattn_kernel, scale=float(XA_HEAD_DIM ** -0.5)),
        grid=(bsz, s // tq),
        in_specs=[pl.BlockSpec((1, tq, hd), lambda bi, i: (bi, i, 0)),
                  pl.BlockSpec((1, m, 2 * hd), lambda bi, i: (bi, 0, 0))],
        out_specs=pl.BlockSpec((1, tq, hd), lambda bi, i: (bi, i, 0)),
        out_shape=jax.ShapeDtypeStruct((bsz, s, hd), BF16),
        compiler_params=_cp("parallel", "parallel"),
        name="mem_attention",
    )(q, kv)


def _layer_norm(v, g, b):
    mu = jnp.mean(v, axis=-1, keepdims=True)
    d = v - mu
    var = jnp.mean(d * d, axis=-1, keepdims=True)
    return d * lax.rsqrt(var + NORM_EPS) * g + b


def _merge_kernel(ol_ref, yn_ref, cm_ref, g_ref, x_ref, wuv_ref, wa_ref, wb_ref, wc_ref, wo_ref,
                  lg_ref, lb_ref, xo_ref, a_scr, *, alpha):
    tm, d = x_ref.shape
    for h in range(MLA_HEADS):
        a_scr[:, h * V_DIM:(h + 1) * V_DIM] = jnp.dot(
            ol_ref[:, h].reshape(tm, KV_RANK), wuv_ref[h],
            preferred_element_type=F32).astype(BF16)
    o_a = jnp.dot(a_scr[...], wa_ref[...], preferred_element_type=F32)
    o_b = jnp.dot(yn_ref[...], wb_ref[...], preferred_element_type=F32)
    o_c = jnp.dot(cm_ref[...], wc_ref[...], preferred_element_type=F32)
    g = g_ref[...].astype(F32)
    merged = (_sigmoid(g[:, :d]) * o_a + _sigmoid(g[:, d:2 * d]) * o_b + _sigmoid(g[:, 2 * d:]) * o_c)
    y = alpha * x_ref[...] + jnp.dot(merged.astype(BF16), wo_ref[...], preferred_element_type=F32)
    xo_ref[...] = _layer_norm(y, lg_ref[...], lb_ref[...])


def merge_ln(ol, yn, cm, g, x, wuv, wa, wb, wc, wo, ln_g, ln_b, alpha):
    t, d = x.shape
    tm = min(MERGE_TM, t)
    tq = ol.shape[2]
    row = lambda n: pl.BlockSpec((tm, n), lambda i: (i, 0))
    full = lambda a: pl.BlockSpec(a.shape, lambda i: (0,) * a.ndim, pipeline_mode=pl.Buffered(1))
    return pl.pallas_call(
        functools.partial(_merge_kernel, alpha=alpha),
        grid=(t // tm,),
        in_specs=[pl.BlockSpec((tm // tq, MLA_HEADS, tq, KV_RANK), lambda i: (i, 0, 0, 0)),
                  row(yn.shape[1]), row(cm.shape[1]), row(g.shape[1]), row(d),
                  full(wuv), full(wa), full(wb), full(wc), full(wo), full(ln_g), full(ln_b)],
        out_specs=row(d),
        out_shape=jax.ShapeDtypeStruct((t, d), F32),
        scratch_shapes=[pltpu.VMEM((tm, MLA_HEADS * V_DIM), BF16)],
        compiler_params=_cp("parallel"),
        name="merge_ln",
    )(ol, yn, cm, g, x, wuv, wa, wb, wc, wo, ln_g, ln_b)


def _first_max(v, expert):
    m = jnp.max(v, axis=0, keepdims=True)
    idx = jnp.min(jnp.where(v == m, expert, N_EXPERTS), axis=0, keepdims=True)
    return m, idx


def _route_kernel(x_ref, rwh_ref, rwl_ref, rbt_ref, info_ref, infot_ref, tinfo_ref, cnt_ref, carry_scr):
    tm = x_ref.shape[0]

    @pl.when(pl.program_id(0) == 0)
    def _():
        carry_scr[...] = jnp.zeros(carry_scr.shape, F32)

    x = x_ref[...]
    x_hi = x.astype(BF16)
    x_lo = (x - x_hi.astype(F32)).astype(BF16)
    logits = (jnp.dot(x_hi, rwh_ref[...], preferred_element_type=F32)
              + jnp.dot(x_lo, rwh_ref[...], preferred_element_type=F32)
              + jnp.dot(x_hi, rwl_ref[...], preferred_element_type=F32))
    scores = _sigmoid(logits.T[:N_EXPERTS, :])
    sel = scores + jnp.tile(rbt_ref[...], (1, tm // LANES))
    expert = lax.broadcasted_iota(I32, (N_EXPERTS, tm), 0)
    neg = -jnp.inf
    best_score = None
    best_group = None
    for j in range(N_EXPERT_GROUPS):
        in_j = (expert >= j * EXPERTS_PER_GROUP) & (expert < (j + 1) * EXPERTS_PER_GROUP)
        v = jnp.where(in_j, sel, neg)
        m1, i1 = _first_max(v, expert)
        m2, _ = _first_max(jnp.where(expert == i1, neg, v), expert)
        gs = m1 + m2
        if j == 0:
            best_score, best_group = gs, jnp.zeros_like(i1)
        else:
            better = gs > best_score
            best_score = jnp.where(better, gs, best_score)
            best_group = jnp.where(better, j, best_group)
    lo = best_group * EXPERTS_PER_GROUP
    v = jnp.where((expert >= lo) & (expert < lo + EXPERTS_PER_GROUP), sel, neg)
    _, e1 = _first_max(v, expert)
    _, e2 = _first_max(jnp.where(expert == e1, neg, v), expert)
    w1 = jnp.sum(jnp.where(expert == e1, scores, 0.0), axis=0, keepdims=True)
    w2 = jnp.sum(jnp.where(expert == e2, scores, 0.0), axis=0, keepdims=True)
    wsum = w1 + w2
    member = jnp.where(expert == e1, 1.0, jnp.where(expert == e2, 1.0, 0.0))
    row = lax.broadcasted_iota(I32, (tm, tm), 0)
    col = lax.broadcasted_iota(I32, (tm, tm), 1)
    earlier = jnp.where(row < col, 1.0, 0.0).astype(BF16)
    lrank = jnp.dot(member.astype(BF16), earlier, preferred_element_type=F32)
    n_col = jnp.sum(member, axis=1, keepdims=True)
    diag = (lax.broadcasted_iota(I32, (N_EXPERTS, LANES), 0)
            == lax.broadcasted_iota(I32, (N_EXPERTS, LANES), 1))
    n = jnp.sum(jnp.where(diag, n_col, 0.0), axis=0, keepdims=True)
    n8 = jnp.floor((n + (ROW_ALIGN - 1)) * (1.0 / ROW_ALIGN)) * ROW_ALIGN
    ua = lax.broadcasted_iota(I32, (LANES, LANES), 0)
    ub = lax.broadcasted_iota(I32, (LANES, LANES), 1)
    lower_experts = jnp.where(ua < ub, 1.0, 0.0).astype(BF16)
    n8_rows = jnp.broadcast_to(n8, (8, LANES))
    block_start = jnp.dot(n8_rows.astype(BF16), lower_experts, preferred_element_type=F32)
    start_col = jnp.sum(jnp.where(diag, block_start[0:1, :], 0.0), axis=1, keepdims=True)
    local_row = start_col + lrank
    j1 = jnp.sum(jnp.where(expert == e1, local_row, 0.0), axis=0, keepdims=True)
    j2 = jnp.sum(jnp.where(expert == e2, local_row, 0.0), axis=0, keepdims=True)
    carry = carry_scr[...]
    sub = lax.broadcasted_iota(I32, (8, LANES), 0)
    tinfo_ref[0] = jnp.where(sub == 0, n8_rows, jnp.where(sub == 1, block_start, jnp.where(sub == 2, carry, 0.0)))
    new_carry = carry + n8_rows
    carry_scr[...] = new_carry
    cnt_ref[...] = new_carry
    field = lax.broadcasted_iota(I32, (8, tm), 0)
    info_t = jnp.where(field == 0, e1.astype(F32),
             jnp.where(field == 1, e2.astype(F32),
             jnp.where(field == 2, w1 / wsum,
             jnp.where(field == 3, w2 / wsum,
             jnp.where(field == 4, j1,
             jnp.where(field == 5, j2, 0.0))))))
    infot_ref[0] = info_t
    info_ref[...] = jnp.concatenate([info_t, jnp.zeros((LANES - 8, tm), F32)], axis=0).T


def route(x, rw_hi, rw_lo, rb):
    t, d = x.shape
    tm = min(DISPATCH_TM, t)
    return pl.pallas_call(
        _route_kernel,
        grid=(t // tm,),
        in_specs=[pl.BlockSpec((tm, d), lambda i: (i, 0)),
                  pl.BlockSpec((d, LANES), lambda i: (0, 0)),
                  pl.BlockSpec((d, LANES), lambda i: (0, 0)),
                  pl.BlockSpec((N_EXPERTS, LANES), lambda i: (0, 0))],
        out_specs=[pl.BlockSpec((tm, LANES), lambda i: (i, 0)),
                   pl.BlockSpec((1, 8, tm), lambda i: (i, 0, 0)),
                   pl.BlockSpec((1, 8, LANES), lambda i: (i, 0, 0)),
                   pl.BlockSpec((8, LANES), lambda i: (0, 0))],
        out_shape=[jax.ShapeDtypeStruct((t, LANES), F32),
                   jax.ShapeDtypeStruct((t // tm, 8, tm), F32),
                   jax.ShapeDtypeStruct((t // tm, 8, LANES), F32),
                   jax.ShapeDtypeStruct((8, LANES), F32)],
        scratch_shapes=[pltpu.VMEM((8, LANES), F32)],
        compiler_params=_cp("arbitrary"),
        name="route",
    )(x, rw_hi, rw_lo, rb)


def _block_copies(plan_ref, local_ref, sorted_ref, sem, to_sorted, start):
    for e in range(N_EXPERTS):
        count = plan_ref[0, 0, e]
        loc0 = plan_ref[0, 0, N_EXPERTS + e]
        dst0 = plan_ref[0, 0, 2 * N_EXPERTS + e]

        def body(k, carry, loc0=loc0, dst0=dst0):
            loc = local_ref.at[pl.ds(pl.multiple_of(loc0 + k * ROW_ALIGN, ROW_ALIGN), ROW_ALIGN)]
            srt = sorted_ref.at[pl.ds(pl.multiple_of(dst0 + k * ROW_ALIGN, ROW_ALIGN), ROW_ALIGN)]
            cp = pltpu.make_async_copy(loc, srt, sem) if to_sorted else pltpu.make_async_copy(srt, loc, sem)
            if start:
                cp.start()
            else:
                cp.wait()
            return carry

        lax.fori_loop(0, count, body, 0)


def _dispatch_kernel(plan_ref, prev_plan_ref, infot_ref, x_ref, init_ref, o_ref, local_scr, sem):
    del init_ref
    tm = x_ref.shape[0]
    i = pl.program_id(0)
    slot = i % 2
    j1 = infot_ref[0, 4:5, :]
    j2 = infot_ref[0, 5:6, :]
    rowid = lax.broadcasted_iota(I32, (LOCAL_ROWS, tm), 0).astype(F32)
    sel = jnp.where(rowid == j1, 1.0, jnp.where(rowid == j2, 1.0, 0.0)).astype(BF16)
    local_scr[slot] = jnp.dot(sel, x_ref[...].astype(BF16), preferred_element_type=F32).astype(BF16)

    @pl.when(i > 0)
    def _():
        _block_copies(prev_plan_ref, local_scr.at[1 - slot], o_ref, sem.at[1 - slot], True, False)

    _block_copies(plan_ref, local_scr.at[slot], o_ref, sem.at[slot], True, True)

    @pl.when(i == pl.num_programs(0) - 1)
    def _():
        _block_copies(plan_ref, local_scr.at[slot], o_ref, sem.at[slot], True, False)


def dispatch_rows(x, info_t, plan, n_rows):
    t, d = x.shape
    tm = min(DISPATCH_TM, t)
    init = jnp.zeros((n_rows, d), BF16)
    return pl.pallas_call(
        _dispatch_kernel,
        grid=(t // tm,),
        in_specs=[pl.BlockSpec((1, 1, 3 * N_EXPERTS), lambda i: (i, 0, 0), memory_space=pltpu.SMEM),
                  pl.BlockSpec((1, 1, 3 * N_EXPERTS), lambda i: (jnp.maximum(i - 1, 0), 0, 0),
                               memory_space=pltpu.SMEM),
                  pl.BlockSpec((1, 8, tm), lambda i: (i, 0, 0)),
                  pl.BlockSpec((tm, d), lambda i: (i, 0)),
                  pl.BlockSpec(memory_space=pl.ANY)],
        out_specs=pl.BlockSpec(memory_space=pl.ANY),
        out_shape=jax.ShapeDtypeStruct((n_rows, d), BF16),
        scratch_shapes=[pltpu.VMEM((2, LOCAL_ROWS, d), BF16), pltpu.SemaphoreType.DMA((2,))],
        input_output_aliases={4: 0},
        compiler_params=_cp("arbitrary"),
        name="moe_dispatch",
    )(plan, plan, info_t, x, init)


def _expert_kernel(te_ref, nu_ref, xs_ref, w1_ref, w3_ref, w2_ref, o_ref, w13_scr, w2_scr):
    i = pl.program_id(0)
    f = w2_ref.shape[2]
    new_expert = (i == 0) | (te_ref[i] != te_ref[jnp.maximum(i - 1, 0)])

    @pl.when((i < nu_ref[0]) & new_expert)
    def _():
        w13_scr[:, :f] = w1_ref[0, 0].astype(BF16)
        w13_scr[:, f:] = w3_ref[0, 0].astype(BF16)
        w2_scr[...] = w2_ref[0, 0].astype(BF16)

    @pl.when(i < nu_ref[0])
    def _():
        h = jnp.dot(xs_ref[...], w13_scr[...], preferred_element_type=F32)
        h1 = h[:, :f]
        act = (h1 * _sigmoid(h1) * h[:, f:]).astype(BF16)
        o_ref[...] = jnp.dot(act, w2_scr[...], preferred_element_type=F32).astype(o_ref.dtype)

    @pl.when(i >= nu_ref[0])
    def _():
        o_ref[...] = jnp.zeros(o_ref.shape, o_ref.dtype)


def expert_ffn(xs, w1, w3, w2, layer, tile_expert, n_used):
    p, d = xs.shape
    f = w2.shape[2]
    n_tiles = p // MOE_TM
    grid_spec = pltpu.PrefetchScalarGridSpec(
        num_scalar_prefetch=2,
        grid=(n_tiles,),
        in_specs=[pl.BlockSpec((MOE_TM, d), lambda i, te, nu: (i, 0)),
                  pl.BlockSpec((1, 1, d, f), lambda i, te, nu: (layer, te[i], 0, 0)),
                  pl.BlockSpec((1, 1, d, f), lambda i, te, nu: (layer, te[i], 0, 0)),
                  pl.BlockSpec((1, 1, f, d), lambda i, te, nu: (layer, te[i], 0, 0))],
        out_specs=pl.BlockSpec((MOE_TM, d), lambda i, te, nu: (i, 0)),
        scratch_shapes=[pltpu.VMEM((d, 2 * f), BF16), pltpu.VMEM((f, d), BF16)],
    )
    return pl.pallas_call(
        _expert_kernel,
        grid_spec=grid_spec,
        out_shape=jax.ShapeDtypeStruct((p, d), BF16),
        compiler_params=_cp("arbitrary"),
        name="expert_ffn",
    )(tile_expert, n_used, xs, w1, w3, w2)


def _combine_kernel(plan_ref, next_plan_ref, info_ref, x_ref, ys_ref, lg_ref, lb_ref, xo_ref, xb_ref,
                    local_scr, sem, *, alpha):
    tm = x_ref.shape[0]
    i = pl.program_id(0)
    slot = i % 2

    @pl.when(i == 0)
    def _():
        _block_copies(plan_ref, local_scr.at[slot], ys_ref, sem.at[slot], False, True)

    @pl.when(i + 1 < pl.num_programs(0))
    def _():
        _block_copies(next_plan_ref, local_scr.at[1 - slot], ys_ref, sem.at[1 - slot], False, True)

    _block_copies(plan_ref, local_scr.at[slot], ys_ref, sem.at[slot], False, False)
    last = N_EXPERTS - 1
    used = plan_ref[0, 0, N_EXPERTS + last] + plan_ref[0, 0, last] * ROW_ALIGN
    rowid = lax.broadcasted_iota(I32, (LOCAL_ROWS, 1), 0)
    y = local_scr[slot]
    y = jnp.where(rowid < used, y, jnp.zeros_like(y))
    info = info_ref[...]
    col = lax.broadcasted_iota(I32, (tm, LOCAL_ROWS), 1).astype(F32)
    gate = jnp.where(col == info[:, 4:5], info[:, 2:3],
                     jnp.where(col == info[:, 5:6], info[:, 3:4], 0.0)).astype(BF16)
    moe = jnp.dot(gate, y, preferred_element_type=F32)
    out = _layer_norm(alpha * x_ref[...] + moe, lg_ref[...], lb_ref[...])
    xo_ref[...] = out
    xb_ref[...] = out.astype(BF16)


def combine_ln(plan, info, x, ys, ln_g, ln_b, alpha):
    t, d = x.shape
    tm = min(DISPATCH_TM, t)
    return pl.pallas_call(
        functools.partial(_combine_kernel, alpha=alpha),
        grid=(t // tm,),
        in_specs=[pl.BlockSpec((1, 1, 3 * N_EXPERTS), lambda i: (i, 0, 0), memory_space=pltpu.SMEM),
                  pl.BlockSpec((1, 1, 3 * N_EXPERTS), lambda i: (jnp.minimum(i + 1, t // tm - 1), 0, 0),
                               memory_space=pltpu.SMEM),
                  pl.BlockSpec((tm, LANES), lambda i: (i, 0)),
                  pl.BlockSpec((tm, d), lambda i: (i, 0)),
                  pl.BlockSpec(memory_space=pl.ANY),
                  pl.BlockSpec((1, d), lambda i: (0, 0)),
                  pl.BlockSpec((1, d), lambda i: (0, 0))],
        out_specs=[pl.BlockSpec((tm, d), lambda i: (i, 0)), pl.BlockSpec((tm, d), lambda i: (i, 0))],
        out_shape=[jax.ShapeDtypeStruct((t, d), F32), jax.ShapeDtypeStruct((t, d), BF16)],
        scratch_shapes=[pltpu.VMEM((2, LOCAL_ROWS, d), BF16), pltpu.SemaphoreType.DMA((2,))],
        compiler_params=_cp("arbitrary"),
        name="moe_combine_ln",
    )(plan, plan, info, x, ys, ln_g, ln_b)


def _prep_layer(l, p):
    w_in = p["w_in"][l]
    d = w_in.shape[0]
    o_dq = 0
    o_dkv = o_dq + Q_RANK
    o_z = o_dkv + KV_RANK + ROPE_DIM
    d_inner = SSM_HEADS * SSM_HEAD_DIM
    conv_ch = d_inner + 2 * SSM_GROUPS * D_STATE
    o_xbc = o_z + d_inner
    o_dt = o_xbc + conv_ch
    o_qm = o_dt + SSM_HEADS
    o_g = o_qm + XA_HEADS * XA_HEAD_DIM
    half = ROPE_DIM // 2
    kr0 = o_dkv + KV_RANK
    zeros = lambda n: jnp.zeros((d, n), F32)
    w_small = jnp.concatenate([
        w_in[:, o_dq:o_dq + Q_RANK],
        w_in[:, o_dkv:o_dkv + KV_RANK],
        w_in[:, kr0:kr0 + ROPE_DIM], zeros(LANES - ROPE_DIM),
        w_in[:, kr0 + half:kr0 + ROPE_DIM], w_in[:, kr0:kr0 + half], zeros(LANES - ROPE_DIM),
        w_in[:, o_dt:o_dt + SSM_HEADS], zeros(LANES - SSM_HEADS)], axis=1)
    w_uq = p["w_uq"][l].reshape(Q_RANK, MLA_HEADS, NOPE_DIM + ROPE_DIM)
    wq_nope = w_uq[:, :, :NOPE_DIM].reshape(Q_RANK, -1)
    wq_rope = w_uq[:, :, NOPE_DIM:]
    wq_rope_sw = jnp.concatenate([wq_rope[:, :, half:], wq_rope[:, :, :half]], axis=-1)
    wq = jnp.concatenate([wq_nope, wq_rope.reshape(Q_RANK, -1), wq_rope_sw.reshape(Q_RANK, -1)], axis=1)
    w_ukv = p["w_ukv"][l]
    wuk_t = jnp.transpose(w_ukv[:, :, :NOPE_DIM], (1, 2, 0))
    wuv = jnp.transpose(w_ukv[:, :, NOPE_DIM:], (1, 0, 2))
    pad_heads = lambda v, fill: jnp.concatenate(
        [v.astype(F32), jnp.full((LANES - SSM_HEADS,), fill, F32)]).reshape(1, LANES)
    bf = lambda a: a.astype(BF16)
    return dict(
        w_small=bf(w_small), w_z=bf(w_in[:, o_z:o_z + d_inner]), w_xbc=bf(w_in[:, o_xbc:o_xbc + conv_ch]),
        w_qm=bf(w_in[:, o_qm:o_g]), w_g=bf(w_in[:, o_g:]),
        q_norm=p["q_norm"][l].reshape(1, -1), kv_norm=p["kv_norm"][l].reshape(1, -1),
        wq=bf(wq), wuk_t=bf(wuk_t), wuv=bf(wuv),
        conv_w_half=0.5 * p["conv_w"][l], conv_b_half=0.5 * p["conv_b"][l].reshape(1, -1),
        dt_bias=pad_heads(p["dt_bias"][l], 0.0),
        a_neg=pad_heads(-jnp.exp(p["a_log"][l].astype(F32)), 0.0),
        d_skip=jnp.repeat(p["d_skip"][l].astype(F32), SSM_HEAD_DIM).reshape(1, -1),
        ssm_norm=p["ssm_norm"][l].reshape(1, -1),
        w_mem_kv=bf(p["w_mem_kv"][l]),
        wa=bf(p["w_proj_a"][l]), wb=bf(p["w_proj_b"][l]), wc=bf(p["w_proj_c"][l]), wo=bf(p["w_out"][l]),
        ln1_g=p["ln1_g"][l].reshape(1, -1), ln1_b=p["ln1_b"][l].reshape(1, -1),
        ln2_g=p["ln2_g"][l].reshape(1, -1), ln2_b=p["ln2_b"][l].reshape(1, -1),
    )


def _dispatch_plan(tinfo, counts, t):
    n_tok_tiles = tinfo.shape[0]
    n8 = tinfo[:, 0, :N_EXPERTS].astype(I32)
    block_start = tinfo[:, 1, :N_EXPERTS].astype(I32)
    carry = tinfo[:, 2, :N_EXPERTS].astype(I32)
    rows = counts[0, :N_EXPERTS].astype(I32)
    tiles = (rows + MOE_TM - 1) // MOE_TM
    tile_end = jnp.cumsum(tiles)
    row_start = (tile_end - tiles) * MOE_TM
    plan = jnp.concatenate([n8 // ROW_ALIGN, block_start, row_start[None, :] + carry], axis=1)
    max_rows = TOP_K * t + N_EXPERTS * (ROW_ALIGN - 1) * n_tok_tiles
    n_tiles = -(-max_rows // MOE_TM) + N_EXPERTS
    tile_ids = jnp.arange(n_tiles, dtype=I32)
    tile_expert = jnp.minimum(jnp.sum(tile_ids[:, None] >= tile_end[None, :], axis=1), N_EXPERTS - 1)
    n_used = tile_end[-1:].astype(I32)
    return plan.reshape(n_tok_tiles, 1, 3 * N_EXPERTS), tile_expert.astype(I32), n_used, n_tiles * MOE_TM


def kernel(x, mem, positions, w_in, q_norm, w_uq, kv_norm, w_ukv, w_proj_a, conv_w, conv_b, dt_bias, a_log,
           d_skip, ssm_norm, w_proj_b, w_mem_kv, w_proj_c, w_out, ln1_g, ln1_b, router_w, router_bias,
           exp_w1, exp_w3, exp_w2, ln2_g, ln2_b):
    params = dict(w_in=w_in, q_norm=q_norm, w_uq=w_uq, kv_norm=kv_norm, w_ukv=w_ukv, w_proj_a=w_proj_a,
                  conv_w=conv_w, conv_b=conv_b, dt_bias=dt_bias, a_log=a_log, d_skip=d_skip,
                  ssm_norm=ssm_norm, w_proj_b=w_proj_b, w_mem_kv=w_mem_kv, w_proj_c=w_proj_c, w_out=w_out,
                  ln1_g=ln1_g, ln1_b=ln1_b, exp_w1=exp_w1, exp_w3=exp_w3, exp_w2=exp_w2,
                  ln2_g=ln2_g, ln2_b=ln2_b)
    bsz, s, d = x.shape
    t = bsz * s
    depth = w_in.shape[0]
    alpha = float((2 * depth) ** 0.25)
    n_mem = mem.shape[1]

    inv = ROPE_THETA ** (-jnp.arange(0, ROPE_DIM, 2, dtype=F32) / ROPE_DIM)
    ang = positions.astype(F32)[..., None] * inv
    cos, sin = jnp.cos(ang), jnp.sin(ang)
    cosq = jnp.tile(jnp.concatenate([cos, cos], axis=-1), (1, 1, LANES // ROPE_DIM)).reshape(t, LANES)
    sinq = jnp.tile(jnp.concatenate([-sin, sin], axis=-1), (1, 1, LANES // ROPE_DIM)).reshape(t, LANES)

    rw = jnp.concatenate([router_w.astype(F32), jnp.zeros((d, LANES - N_EXPERTS), F32)], axis=1)
    rw_hi = rw.astype(BF16)
    rw_lo = (rw - rw_hi.astype(F32)).astype(BF16)
    rbt = jnp.broadcast_to(router_bias.astype(F32)[:, None], (N_EXPERTS, LANES))
    mem_b = mem.reshape(bsz * n_mem, d).astype(BF16)

    xf = x.reshape(t, d).astype(F32)
    xb = xf.astype(BF16)
    for l in range(depth):
        w = _prep_layer(l, params)
        ha = matmul(xb, w["w_small"], F32, "proj_small")
        z_act = matmul(xb, w["w_z"], BF16, "proj_z_silu", silu=True)
        xbc_c = proj_conv_silu(xb, w["w_xbc"], w["conv_w_half"], w["conv_b_half"], s).reshape(bsz, s, -1)
        qm = matmul(xb, w["w_qm"], BF16, "proj_qmem")
        g = matmul(xb, w["w_g"], BF16, "proj_gate")

        ql, qr, ck, kr = mla_prep(ha, cosq, sinq, w["q_norm"], w["kv_norm"], w["wq"], w["wuk_t"],
                                  min(ATT_TQ, s))
        o_lat = mla_attention(ql, qr, ck.reshape(bsz, s, KV_RANK), kr.reshape(bsz, s, ROPE_DIM))

        yn = ssd(xbc_c, z_act.reshape(bsz, s, -1), ha.reshape(bsz, s, HA_W), w["dt_bias"], w["a_neg"],
                 w["d_skip"], w["ssm_norm"]).reshape(t, -1)

        kv = matmul(mem_b, w["w_mem_kv"], BF16, "proj_memkv").reshape(bsz, n_mem, -1)
        cm = mem_attention(qm.reshape(bsz, s, -1), kv).reshape(t, -1)

        x1 = merge_ln(o_lat, yn, cm, g, xf, w["wuv"], w["wa"], w["wb"], w["wc"], w["wo"],
                           w["ln1_g"], w["ln1_b"], alpha)

        info, info_t, tinfo, counts = route(x1, rw_hi, rw_lo, rbt)
        plan, tile_expert, n_used, n_rows = _dispatch_plan(tinfo, counts, t)
        xs = dispatch_rows(x1, info_t, plan, n_rows)
        ys = expert_ffn(xs, exp_w1, exp_w3, exp_w2, l, tile_expert, n_used)
        xf, xb = combine_ln(plan, info, x1, ys, w["ln2_g"], w["ln2_b"], alpha)
    return xf.reshape(bsz, s, d)
```

```python
import functools

import jax
import jax.numpy as jnp
from jax import lax
from jax.experimental import pallas as pl
from jax.experimental.pallas import tpu as pltpu

F32 = jnp.float32
BF16 = jnp.bfloat16
I32 = jnp.int32

MLA_HEADS = 8
Q_RANK = 384
KV_RANK = 256
NOPE_DIM = 128
ROPE_DIM = 64
V_DIM = 128
ROPE_THETA = 10000.0
SSM_HEADS = 32
SSM_HEAD_DIM = 64
SSM_GROUPS = 8
D_STATE = 128
CONV_K = 4
XA_HEADS = 4
XA_HEAD_DIM = 256
N_EXPERTS = 16
N_EXPERT_GROUPS = 4
EXPERTS_PER_GROUP = 4
TOP_K = 2
NORM_EPS = 1e-5
RMS_EPS = 1e-6

LANES = 128
V7X_VMEM_LIMIT = 56 * 1024 * 1024

MM_TM = 2048
MM_TN = 1024
PREP_TM = 1024
ATT_TQ = 512
ATT_TK = 512
ATT_GROUP_ROWS = 512
ATT_UNROLL = 2
LOG2E = 1.4426950408889634
PROJ_CONV_TM = 1024
PROJ_CONV_TN = 1024
PROJ_CONV_SUB = 256
CONV_HALO = 16
SSD_CHUNK = 256
XA_TQ = 2048
MERGE_TM = 512
MOE_TM = 512
DISPATCH_TM = 512
ROW_ALIGN = 16
LOCAL_ROWS = 1280

HA_W = 1024
HA_DQ = 0
HA_C = 384
HA_KR = 640
HA_KRS = 768
HA_DT = 896


def _cp(*sem):
    return pltpu.CompilerParams(dimension_semantics=sem, vmem_limit_bytes=V7X_VMEM_LIMIT)


def _sigmoid(x):
    return 1.0 / (1.0 + jnp.exp(-x))


def _mm_kernel(x_ref, w_ref, o_ref, *, silu):
    y = jnp.dot(x_ref[...], w_ref[...], preferred_element_type=F32)
    if silu:
        half = 0.5 * y
        y = half * jnp.tanh(half) + half
    o_ref[...] = y.astype(o_ref.dtype)


def matmul(x, w, out_dtype, name, silu=False):
    m, k = x.shape
    n = w.shape[1]
    tm = min(MM_TM, m)
    tn = min(MM_TN, n)
    return pl.pallas_call(
        functools.partial(_mm_kernel, silu=silu),
        grid=(n // tn, m // tm),
        in_specs=[pl.BlockSpec((tm, k), lambda j, i: (i, 0)),
                  pl.BlockSpec((k, tn), lambda j, i: (0, j))],
        out_specs=pl.BlockSpec((tm, tn), lambda j, i: (i, j)),
        out_shape=jax.ShapeDtypeStruct((m, n), out_dtype),
        compiler_params=_cp("parallel", "parallel"),
        name=name,
    )(x, w)


def _proj_conv_kernel(x_ref, xh_ref, w_ref, cw_ref, cb_ref, o_ref, *, tiles_per_seq):
    first = (pl.program_id(1) % tiles_per_seq) == 0
    tn = o_ref.shape[1]
    sub = min(PROJ_CONV_SUB, tn)
    def project(c):
        w = w_ref[:, c * sub:(c + 1) * sub]
        u = jnp.dot(x_ref[...], w, preferred_element_type=F32)
        halo = jnp.dot(xh_ref[...], w, preferred_element_type=F32)
        return jnp.concatenate([jnp.where(first, jnp.zeros_like(halo), halo), u], axis=0)

    n_sub = tn // sub
    nxt = project(0)
    for c in range(n_sub):
        cs = slice(c * sub, (c + 1) * sub)
        cw = cw_ref[:, cs]
        ext = nxt
        if c + 1 < n_sub:
            nxt = project(c + 1)
        prev = pltpu.roll(ext, 1, 0)
        near = cw[3:4, :] * ext + cw[2:3, :] * prev
        far = cw[1:2, :] * ext + cw[0:1, :] * prev
        half = (near + pltpu.roll(far, 2, 0))[CONV_HALO:, :] + cb_ref[:, cs]
        o_ref[:, cs] = (half * jnp.tanh(half) + half).astype(o_ref.dtype)


def proj_conv_silu(x, w, conv_w, conv_b, seq_len):
    m, k = x.shape
    n = w.shape[1]
    tm = min(PROJ_CONV_TM, seq_len)
    tn = min(PROJ_CONV_TN, n)
    hb = tm // CONV_HALO
    return pl.pallas_call(
        functools.partial(_proj_conv_kernel, tiles_per_seq=seq_len // tm),
        grid=(n // tn, m // tm),
        in_specs=[pl.BlockSpec((tm, k), lambda j, i: (i, 0)),
                  pl.BlockSpec((CONV_HALO, k), lambda j, i: (jnp.maximum(i * hb - 1, 0), 0)),
                  pl.BlockSpec((k, tn), lambda j, i: (0, j)),
                  pl.BlockSpec((CONV_K, tn), lambda j, i: (0, j)),
                  pl.BlockSpec((1, tn), lambda j, i: (0, j))],
        out_specs=pl.BlockSpec((tm, tn), lambda j, i: (i, j)),
        out_shape=jax.ShapeDtypeStruct((m, n), BF16),
        compiler_params=_cp("parallel", "parallel"),
        name="proj_xbc_conv",
    )(x, x, w, conv_w, conv_b)


def _mla_prep_kernel(ha_ref, cos_ref, sin_ref, qn_ref, kvn_ref, wq_ref, wuk_ref,
                     ql_ref, qr_ref, ck_ref, kr_ref, *, scale):
    nb, _, tq, _ = ql_ref.shape
    ha = ha_ref[...]
    dq = ha[:, HA_DQ:HA_DQ + Q_RANK]
    c_q = dq * lax.rsqrt(jnp.mean(dq * dq, axis=-1, keepdims=True) + RMS_EPS) * qn_ref[...]
    q = jnp.dot(c_q.astype(BF16), wq_ref[...], preferred_element_type=F32)
    n_nope = MLA_HEADS * NOPE_DIM
    n_rope = MLA_HEADS * ROPE_DIM
    cosq = jnp.tile(cos_ref[...], (1, n_rope // LANES))
    sinq = jnp.tile(sin_ref[...], (1, n_rope // LANES))
    q_rope = ((q[:, n_nope:n_nope + n_rope] * cosq + q[:, n_nope + n_rope:] * sinq) * scale).astype(BF16)
    for h in range(MLA_HEADS):
        qh = q[:, h * NOPE_DIM:(h + 1) * NOPE_DIM].astype(BF16)
        ql = jnp.dot(qh, wuk_ref[h], preferred_element_type=F32)
        ql_ref[:, h] = (ql * scale).astype(BF16).reshape(nb, tq, KV_RANK)
        qr_ref[:, h] = q_rope[:, h * ROPE_DIM:(h + 1) * ROPE_DIM].reshape(nb, tq, ROPE_DIM)
    c = ha[:, HA_C:HA_C + KV_RANK]
    c_kv = c * lax.rsqrt(jnp.mean(c * c, axis=-1, keepdims=True) + RMS_EPS) * kvn_ref[...]
    ck_ref[...] = c_kv.astype(BF16)
    k_rope = (ha[:, HA_KR:HA_KR + LANES] * cos_ref[...]
              + ha[:, HA_KRS:HA_KRS + LANES] * sin_ref[...])
    kr_ref[...] = k_rope[:, :ROPE_DIM].astype(BF16)


def mla_prep(ha, cosq, sinq, q_norm, kv_norm, wq, wuk_t, tq):
    t = ha.shape[0]
    tm = min(PREP_TM, t)
    nb = tm // tq
    scale = float((NOPE_DIM + ROPE_DIM) ** -0.5 * LOG2E)
    n_rope = MLA_HEADS * ROPE_DIM
    full = lambda shape: pl.BlockSpec(shape, lambda i: (0,) * len(shape))
    return pl.pallas_call(
        functools.partial(_mla_prep_kernel, scale=scale),
        grid=(t // tm,),
        in_specs=[pl.BlockSpec((tm, HA_W), lambda i: (i, 0)),
                  pl.BlockSpec((tm, LANES), lambda i: (i, 0)),
                  pl.BlockSpec((tm, LANES), lambda i: (i, 0)),
                  full((1, Q_RANK)), full((1, KV_RANK)),
                  full(wq.shape), full(wuk_t.shape)],
        out_specs=[pl.BlockSpec((nb, MLA_HEADS, tq, KV_RANK), lambda i: (i, 0, 0, 0)),
                   pl.BlockSpec((nb, MLA_HEADS, tq, ROPE_DIM), lambda i: (i, 0, 0, 0)),
                   pl.BlockSpec((tm, KV_RANK), lambda i: (i, 0)),
                   pl.BlockSpec((tm, ROPE_DIM), lambda i: (i, 0))],
        out_shape=[jax.ShapeDtypeStruct((t // tq, MLA_HEADS, tq, KV_RANK), BF16),
                   jax.ShapeDtypeStruct((t // tq, MLA_HEADS, tq, ROPE_DIM), BF16),
                   jax.ShapeDtypeStruct((t, KV_RANK), BF16),
                   jax.ShapeDtypeStruct((t, ROPE_DIM), BF16)],
        compiler_params=_cp("parallel"),
        name="mla_prep",
    )(ha, cosq, sinq, q_norm, kv_norm, wq, wuk_t)


def _mla_attn_kernel(ql_ref, qr_ref, ck_ref, kr_ref, o_ref, m_scr, l_scr, acc_scr, *, tq, tk, hpc):
    rc = hpc * tq
    n_groups = MLA_HEADS // hpc
    q_start = pl.program_id(1) * tq
    n_full = q_start // tk
    m_scr[...] = jnp.full(m_scr.shape, -jnp.inf, F32)
    l_scr[...] = jnp.zeros(l_scr.shape, F32)
    acc_scr[...] = jnp.zeros(acc_scr.shape, F32)
    nt = (((1,), (1,)), ((), ()))

    def step(j, masked, width=tk):
        ks = pl.multiple_of(j * tk, tk)
        ck = ck_ref[0, pl.ds(ks, width), :]
        kr = kr_ref[0, pl.ds(ks, width), :]
        if masked:
            q_pos = q_start + (lax.broadcasted_iota(I32, (rc, width), 0) & (tq - 1))
            k_pos = ks + lax.broadcasted_iota(I32, (rc, width), 1)
            visible = k_pos <= q_pos
        for c in range(n_groups):
            rs = slice(c * rc, (c + 1) * rc)
            ql = ql_ref[0, c * hpc:(c + 1) * hpc].reshape(rc, KV_RANK)
            qr = qr_ref[0, c * hpc:(c + 1) * hpc].reshape(rc, ROPE_DIM)
            s = (lax.dot_general(ql, ck, nt, preferred_element_type=F32)
                 + lax.dot_general(qr, kr, nt, preferred_element_type=F32))
            if masked:
                s = jnp.where(visible, s, -jnp.inf)
            m_prev = m_scr[rs]
            m_new = jnp.maximum(m_prev, jnp.max(s, axis=-1, keepdims=True))
            alpha = jnp.exp2(m_prev - m_new)
            p = jnp.exp2(s - jnp.tile(m_new, (1, width // LANES)))
            l_scr[rs] = alpha * l_scr[rs] + jnp.sum(p, axis=-1, keepdims=True)
            acc_scr[rs] = (jnp.tile(alpha, (1, KV_RANK // LANES)) * acc_scr[rs]
                           + jnp.dot(p.astype(BF16), ck, preferred_element_type=F32))
            m_scr[rs] = m_new

    def trip(jj, carry):
        for u in range(ATT_UNROLL):
            step(ATT_UNROLL * jj + u, False)
        return carry

    n_trips = n_full // ATT_UNROLL
    lax.fori_loop(0, n_trips, trip, 0)

    def single(j, carry):
        step(j, False)
        return carry

    lax.fori_loop(n_trips * ATT_UNROLL, n_full, single, 0)

    sub = (q_start - n_full * tk) // tq
    for v in range(tk // tq):
        @pl.when(sub == v)
        def _(v=v):
            step(n_full, True, (v + 1) * tq)
    out = acc_scr[...] / jnp.tile(l_scr[...], (1, KV_RANK // LANES))
    o_ref[0] = out.astype(o_ref.dtype).reshape(MLA_HEADS, tq, KV_RANK)


def mla_attention(ql, qr, ck, kr):
    _, _, tq, _ = ql.shape
    b, s, _ = ck.shape
    tk = min(ATT_TK, s)
    assert tk % tq == 0 and s % tk == 0 and tq & (tq - 1) == 0
    nq = s // tq
    rows = tq * MLA_HEADS
    hpc = max(1, min(MLA_HEADS, ATT_GROUP_ROWS // tq))
    qspec = lambda dim: pl.BlockSpec((1, MLA_HEADS, tq, dim), lambda bi, i: (bi * nq + i, 0, 0, 0))
    return pl.pallas_call(
        functools.partial(_mla_attn_kernel, tq=tq, tk=tk, hpc=hpc),
        grid=(b, nq),
        in_specs=[qspec(KV_RANK), qspec(ROPE_DIM),
                  pl.BlockSpec((1, s, KV_RANK), lambda bi, i: (bi, 0, 0)),
                  pl.BlockSpec((1, s, ROPE_DIM), lambda bi, i: (bi, 0, 0))],
        out_specs=qspec(KV_RANK),
        out_shape=jax.ShapeDtypeStruct(ql.shape, BF16),
        scratch_shapes=[pltpu.VMEM((rows, LANES), F32), pltpu.VMEM((rows, LANES), F32),
                        pltpu.VMEM((rows, KV_RANK), F32)],
        compiler_params=_cp("parallel", "parallel"),
        name="mla_attention",
    )(ql, qr, ck, kr)


def _ssd_kernel(xbc_ref, z_ref, dt_ref, dtb_ref, a_ref, dsk_ref, ng_ref, exp_ref, o_ref,
                state_scr, *, chunk):
    d_inner = SSM_HEADS * SSM_HEAD_DIM
    gn = SSM_GROUPS * D_STATE
    rep = SSM_HEADS // SSM_GROUPS
    gw = rep * SSM_HEAD_DIM

    @pl.when(pl.program_id(1) == 0)
    def _():
        state_scr[...] = jnp.zeros(state_scr.shape, F32)

    x_raw = dt_ref[0] + dtb_ref[...]
    dt = jnp.maximum(x_raw, 0.0) + jnp.log(1.0 + jnp.exp(-jnp.abs(x_raw)))
    da = dt * a_ref[...]
    row = lax.broadcasted_iota(I32, (chunk, chunk), 0)
    col = lax.broadcasted_iota(I32, (chunk, chunk), 1)
    causal = row >= col
    tri = jnp.where(causal, 1.0, 0.0).astype(BF16)
    acum = jnp.zeros(da.shape, F32)
    rem = da
    for _ in range(3):
        part = rem.astype(BF16)
        acum = acum + jnp.dot(tri, part, preferred_element_type=F32)
        rem = rem - part.astype(F32)
    a2 = acum * LOG2E
    a2_t = a2.T
    src_t = a2_t - jnp.log2(dt.T)
    w_t = jnp.exp2(a2_t[:, chunk - 1:chunk] - src_t)
    e_end = jnp.exp2(a2[chunk - 1:chunk, :])
    e_all = jnp.dot(jnp.exp2(a2).astype(BF16), exp_ref[...], preferred_element_type=F32)
    head_of_lane = lax.broadcasted_iota(I32, (1, gw), 1) // SSM_HEAD_DIM

    for g in range(SSM_GROUPS):
        bg = xbc_ref[0, :, d_inner + g * D_STATE:d_inner + (g + 1) * D_STATE]
        cg = xbc_ref[0, :, d_inner + gn + g * D_STATE:d_inner + gn + (g + 1) * D_STATE]
        bt = bg.astype(F32).T
        cb = jnp.dot(cg, bt.astype(BF16), preferred_element_type=F32)
        st = state_scr[g]
        ys = jnp.dot(cg, st.astype(BF16), preferred_element_type=F32)
        xg = xbc_ref[0, :, g * gw:(g + 1) * gw].astype(F32)
        y = jnp.zeros((chunk, gw), F32)
        upd = jnp.zeros((D_STATE, gw), F32)
        sc = jnp.zeros((1, gw), F32)
        for r in range(rep):
            h = g * rep + r
            own = head_of_lane == r
            seg = a2[:, h:h + 1] - src_t[h:h + 1, :]
            m = (cb * jnp.exp2(jnp.where(causal, seg, -jnp.inf))).astype(BF16)
            xm = jnp.where(own, xg, 0.0).astype(BF16)
            y = y + jnp.dot(m, xm, preferred_element_type=F32)
            upd = upd + jnp.dot((bt * w_t[h:h + 1, :]).astype(BF16), xm, preferred_element_type=F32)
            sc = jnp.where(own, e_end[:, h:h + 1], sc)
        state_scr[g] = st * sc + upd
        gs = slice(g * gw, (g + 1) * gw)
        y = y + ys * e_all[:, gs] + xg * dsk_ref[:, gs]
        y = y * z_ref[0, :, gs].astype(F32)
        y = y * lax.rsqrt(jnp.mean(y * y, axis=-1, keepdims=True) + RMS_EPS) * ng_ref[:, gs]
        o_ref[0, :, gs] = y.astype(o_ref.dtype)


def ssd(xbc, z, ha3, dt_bias, a_neg, d_skip, norm_g):
    bsz, s, c = xbc.shape
    d_inner = SSM_HEADS * SSM_HEAD_DIM
    chunk = min(SSD_CHUNK, s)
    gw = d_inner // SSM_GROUPS
    vec = lambda n: pl.BlockSpec((1, n), lambda bi, ci: (0, 0))
    expand = (jnp.arange(d_inner, dtype=I32)[None, :] // SSM_HEAD_DIM
              == jnp.arange(LANES, dtype=I32)[:, None]).astype(BF16)
    return pl.pallas_call(
        functools.partial(_ssd_kernel, chunk=chunk),
        grid=(bsz, s // chunk),
        in_specs=[pl.BlockSpec((1, chunk, c), lambda bi, ci: (bi, ci, 0)),
                  pl.BlockSpec((1, chunk, d_inner), lambda bi, ci: (bi, ci, 0)),
                  pl.BlockSpec((1, chunk, LANES), lambda bi, ci: (bi, ci, HA_DT // LANES)),
                  vec(LANES), vec(LANES), vec(d_inner), vec(d_inner),
                  pl.BlockSpec((LANES, d_inner), lambda bi, ci: (0, 0))],
        out_specs=pl.BlockSpec((1, chunk, d_inner), lambda bi, ci: (bi, ci, 0)),
        out_shape=jax.ShapeDtypeStruct((bsz, s, d_inner), BF16),
        scratch_shapes=[pltpu.VMEM((SSM_GROUPS, D_STATE, gw), F32)],
        compiler_params=_cp("parallel", "arbitrary"),
        name="ssd_scan",
    )(xbc, z, ha3, dt_bias, a_neg, d_skip, norm_g, expand)


def _xattn_kernel(q_ref, kv_ref, o_ref, *, scale):
    hd = XA_HEADS * XA_HEAD_DIM
    nt = (((1,), (1,)), ((), ()))
    for h in range(XA_HEADS):
        sl = slice(h * XA_HEAD_DIM, (h + 1) * XA_HEAD_DIM)
        q = q_ref[0, :, sl]
        k = kv_ref[0, :, sl]
        v = kv_ref[0, :, hd + h * XA_HEAD_DIM:hd + (h + 1) * XA_HEAD_DIM]
        s = lax.dot_general(q, k, nt, preferred_element_type=F32) * scale
        p = jnp.exp(s - jnp.max(s, axis=-1, keepdims=True))
        l = jnp.sum(p, axis=-1, keepdims=True)
        o = jnp.dot(p.astype(BF16), v, preferred_element_type=F32) / l
        o_ref[0, :, sl] = o.astype(o_ref.dtype)


def mem_attention(q, kv):
    bsz, s, hd = q.shape
    m = kv.shape[1]
    tq = min(XA_TQ, s)
    return pl.pallas_call(
        functools.partial(_xattn_kernel, scale=float(XA_HEAD_DIM ** -0.5)),
        grid=(bsz, s // tq),
        in_specs=[pl.BlockSpec((1, tq, hd), lambda bi, i: (bi, i, 0)),
                  pl.BlockSpec((1, m, 2 * hd), lambda bi, i: (bi, 0, 0))],
        out_specs=pl.BlockSpec((1, tq, hd), lambda bi, i: (bi, i, 0)),
        out_shape=jax.ShapeDtypeStruct((bsz, s, hd), BF16),
        compiler_params=_cp("parallel", "parallel"),
        name="mem_attention",
    )(q, kv)


def _layer_norm(v, g, b):
    mu = jnp.mean(v, axis=-1, keepdims=True)
    d = v - mu
    var = jnp.mean(d * d, axis=-1, keepdims=True)
    return d * lax.rsqrt(var + NORM_EPS) * g + b


def _merge_kernel(ol_ref, yn_ref, cm_ref, g_ref, x_ref, wuv_ref, wa_ref, wb_ref, wc_ref, wo_ref,
                  lg_ref, lb_ref, xo_ref, a_scr, *, alpha):
    tm, d = x_ref.shape
    for h in range(MLA_HEADS):
        a_scr[:, h * V_DIM:(h + 1) * V_DIM] = jnp.dot(
            ol_ref[:, h].reshape(tm, KV_RANK), wuv_ref[h],
            preferred_element_type=F32).astype(BF16)
    o_a = jnp.dot(a_scr[...], wa_ref[...], preferred_element_type=F32)
    o_b = jnp.dot(yn_ref[...], wb_ref[...], preferred_element_type=F32)
    o_c = jnp.dot(cm_ref[...], wc_ref[...], preferred_element_type=F32)
    g = g_ref[...].astype(F32)
    merged = (_sigmoid(g[:, :d]) * o_a + _sigmoid(g[:, d:2 * d]) * o_b + _sigmoid(g[:, 2 * d:]) * o_c)
    y = alpha * x_ref[...] + jnp.dot(merged.astype(BF16), wo_ref[...], preferred_element_type=F32)
    xo_ref[...] = _layer_norm(y, lg_ref[...], lb_ref[...])


def merge_ln(ol, yn, cm, g, x, wuv, wa, wb, wc, wo, ln_g, ln_b, alpha):
    t, d = x.shape
    tm = min(MERGE_TM, t)
    tq = ol.shape[2]
    row = lambda n: pl.BlockSpec((tm, n), lambda i: (i, 0))
    full = lambda a: pl.BlockSpec(a.shape, lambda i: (0,) * a.ndim, pipeline_mode=pl.Buffered(1))
    return pl.pallas_call(
        functools.partial(_merge_kernel, alpha=alpha),
        grid=(t // tm,),
        in_specs=[pl.BlockSpec((tm // tq, MLA_HEADS, tq, KV_RANK), lambda i: (i, 0, 0, 0)),
                  row(yn.shape[1]), row(cm.shape[1]), row(g.shape[1]), row(d),
                  full(wuv), full(wa), full(wb), full(wc), full(wo), full(ln_g), full(ln_b)],
        out_specs=row(d),
        out_shape=jax.ShapeDtypeStruct((t, d), F32),
        scratch_shapes=[pltpu.VMEM((tm, MLA_HEADS * V_DIM), BF16)],
        compiler_params=_cp("parallel"),
        name="merge_ln",
    )(ol, yn, cm, g, x, wuv, wa, wb, wc, wo, ln_g, ln_b)


def _first_max(v, expert):
    m = jnp.max(v, axis=0, keepdims=True)
    idx = jnp.min(jnp.where(v == m, expert, N_EXPERTS), axis=0, keepdims=True)
    return m, idx


def _route_kernel(x_ref, rwh_ref, rwl_ref, rbt_ref, info_ref, infot_ref, tinfo_ref, cnt_ref, carry_scr):
    tm = x_ref.shape[0]

    @pl.when(pl.program_id(0) == 0)
    def _():
        carry_scr[...] = jnp.zeros(carry_scr.shape, F32)

    x = x_ref[...]
    x_hi = x.astype(BF16)
    x_lo = (x - x_hi.astype(F32)).astype(BF16)
    logits = (jnp.dot(x_hi, rwh_ref[...], preferred_element_type=F32)
              + jnp.dot(x_lo, rwh_ref[...], preferred_element_type=F32)
              + jnp.dot(x_hi, rwl_ref[...], preferred_element_type=F32))
    scores = _sigmoid(logits.T[:N_EXPERTS, :])
    sel = scores + jnp.tile(rbt_ref[...], (1, tm // LANES))
    expert = lax.broadcasted_iota(I32, (N_EXPERTS, tm), 0)
    neg = -jnp.inf
    best_score = None
    best_group = None
    for j in range(N_EXPERT_GROUPS):
        in_j = (expert >= j * EXPERTS_PER_GROUP) & (expert < (j + 1) * EXPERTS_PER_GROUP)
        v = jnp.where(in_j, sel, neg)
        m1, i1 = _first_max(v, expert)
        m2, _ = _first_max(jnp.where(expert == i1, neg, v), expert)
        gs = m1 + m2
        if j == 0:
            best_score, best_group = gs, jnp.zeros_like(i1)
        else:
            better = gs > best_score
            best_score = jnp.where(better, gs, best_score)
            best_group = jnp.where(better, j, best_group)
    lo = best_group * EXPERTS_PER_GROUP
    v = jnp.where((expert >= lo) & (expert < lo + EXPERTS_PER_GROUP), sel, neg)
    _, e1 = _first_max(v, expert)
    _, e2 = _first_max(jnp.where(expert == e1, neg, v), expert)
    w1 = jnp.sum(jnp.where(expert == e1, scores, 0.0), axis=0, keepdims=True)
    w2 = jnp.sum(jnp.where(expert == e2, scores, 0.0), axis=0, keepdims=True)
    wsum = w1 + w2
    member = jnp.where(expert == e1, 1.0, jnp.where(expert == e2, 1.0, 0.0))
    row = lax.broadcasted_iota(I32, (tm, tm), 0)
    col = lax.broadcasted_iota(I32, (tm, tm), 1)
    earlier = jnp.where(row < col, 1.0, 0.0).astype(BF16)
    lrank = jnp.dot(member.astype(BF16), earlier, preferred_element_type=F32)
    n_col = jnp.sum(member, axis=1, keepdims=True)
    diag = (lax.broadcasted_iota(I32, (N_EXPERTS, LANES), 0)
            == lax.broadcasted_iota(I32, (N_EXPERTS, LANES), 1))
    n = jnp.sum(jnp.where(diag, n_col, 0.0), axis=0, keepdims=True)
    n8 = jnp.floor((n + (ROW_ALIGN - 1)) * (1.0 / ROW_ALIGN)) * ROW_ALIGN
    ua = lax.broadcasted_iota(I32, (LANES, LANES), 0)
    ub = lax.broadcasted_iota(I32, (LANES, LANES), 1)
    lower_experts = jnp.where(ua < ub, 1.0, 0.0).astype(BF16)
    n8_rows = jnp.broadcast_to(n8, (8, LANES))
    block_start = jnp.dot(n8_rows.astype(BF16), lower_experts, preferred_element_type=F32)
    start_col = jnp.sum(jnp.where(diag, block_start[0:1, :], 0.0), axis=1, keepdims=True)
    local_row = start_col + lrank
    j1 = jnp.sum(jnp.where(expert == e1, local_row, 0.0), axis=0, keepdims=True)
    j2 = jnp.sum(jnp.where(expert == e2, local_row, 0.0), axis=0, keepdims=True)
    carry = carry_scr[...]
    sub = lax.broadcasted_iota(I32, (8, LANES), 0)
    tinfo_ref[0] = jnp.where(sub == 0, n8_rows, jnp.where(sub == 1, block_start, jnp.where(sub == 2, carry, 0.0)))
    new_carry = carry + n8_rows
    carry_scr[...] = new_carry
    cnt_ref[...] = new_carry
    field = lax.broadcasted_iota(I32, (8, tm), 0)
    info_t = jnp.where(field == 0, e1.astype(F32),
             jnp.where(field == 1, e2.astype(F32),
             jnp.where(field == 2, w1 / wsum,
             jnp.where(field == 3, w2 / wsum,
             jnp.where(field == 4, j1,
             jnp.where(field == 5, j2, 0.0))))))
    infot_ref[0] = info_t
    info_ref[...] = jnp.concatenate([info_t, jnp.zeros((LANES - 8, tm), F32)], axis=0).T


def route(x, rw_hi, rw_lo, rb):
    t, d = x.shape
    tm = min(DISPATCH_TM, t)
    return pl.pallas_call(
        _route_kernel,
        grid=(t // tm,),
        in_specs=[pl.BlockSpec((tm, d), lambda i: (i, 0)),
                  pl.BlockSpec((d, LANES), lambda i: (0, 0)),
                  pl.BlockSpec((d, LANES), lambda i: (0, 0)),
                  pl.BlockSpec((N_EXPERTS, LANES), lambda i: (0, 0))],
        out_specs=[pl.BlockSpec((tm, LANES), lambda i: (i, 0)),
                   pl.BlockSpec((1, 8, tm), lambda i: (i, 0, 0)),
                   pl.BlockSpec((1, 8, LANES), lambda i: (i, 0, 0)),
                   pl.BlockSpec((8, LANES), lambda i: (0, 0))],
        out_shape=[jax.ShapeDtypeStruct((t, LANES), F32),
                   jax.ShapeDtypeStruct((t // tm, 8, tm), F32),
                   jax.ShapeDtypeStruct((t // tm, 8, LANES), F32),
                   jax.ShapeDtypeStruct((8, LANES), F32)],
        scratch_shapes=[pltpu.VMEM((8, LANES), F32)],
        compiler_params=_cp("arbitrary"),
        name="route",
    )(x, rw_hi, rw_lo, rb)


def _block_copies(plan_ref, local_ref, sorted_ref, sem, to_sorted, start):
    for e in range(N_EXPERTS):
        count = plan_ref[0, 0, e]
        loc0 = plan_ref[0, 0, N_EXPERTS + e]
        dst0 = plan_ref[0, 0, 2 * N_EXPERTS + e]

        def body(k, carry, loc0=loc0, dst0=dst0):
            loc = local_ref.at[pl.ds(pl.multiple_of(loc0 + k * ROW_ALIGN, ROW_ALIGN), ROW_ALIGN)]
            srt = sorted_ref.at[pl.ds(pl.multiple_of(dst0 + k * ROW_ALIGN, ROW_ALIGN), ROW_ALIGN)]
            cp = pltpu.make_async_copy(loc, srt, sem) if to_sorted else pltpu.make_async_copy(srt, loc, sem)
            if start:
                cp.start()
            else:
                cp.wait()
            return carry

        lax.fori_loop(0, count, body, 0)


def _dispatch_kernel(plan_ref, prev_plan_ref, infot_ref, x_ref, init_ref, o_ref, local_scr, sem):
    del init_ref
    tm = x_ref.shape[0]
    i = pl.program_id(0)
    slot = i % 2
    j1 = infot_ref[0, 4:5, :]
    j2 = infot_ref[0, 5:6, :]
    rowid = lax.broadcasted_iota(I32, (LOCAL_ROWS, tm), 0).astype(F32)
    sel = jnp.where(rowid == j1, 1.0, jnp.where(rowid == j2, 1.0, 0.0)).astype(BF16)
    local_scr[slot] = jnp.dot(sel, x_ref[...].astype(BF16), preferred_element_type=F32).astype(BF16)

    @pl.when(i > 0)
    def _():
        _block_copies(prev_plan_ref, local_scr.at[1 - slot], o_ref, sem.at[1 - slot], True, False)

    _block_copies(plan_ref, local_scr.at[slot], o_ref, sem.at[slot], True, True)

    @pl.when(i == pl.num_programs(0) - 1)
    def _():
        _block_copies(plan_ref, local_scr.at[slot], o_ref, sem.at[slot], True, False)


def dispatch_rows(x, info_t, plan, n_rows):
    t, d = x.shape
    tm = min(DISPATCH_TM, t)
    init = jnp.zeros((n_rows, d), BF16)
    return pl.pallas_call(
        _dispatch_kernel,
        grid=(t // tm,),
        in_specs=[pl.BlockSpec((1, 1, 3 * N_EXPERTS), lambda i: (i, 0, 0), memory_space=pltpu.SMEM),
                  pl.BlockSpec((1, 1, 3 * N_EXPERTS), lambda i: (jnp.maximum(i - 1, 0), 0, 0),
                               memory_space=pltpu.SMEM),
                  pl.BlockSpec((1, 8, tm), lambda i: (i, 0, 0)),
                  pl.BlockSpec((tm, d), lambda i: (i, 0)),
                  pl.BlockSpec(memory_space=pl.ANY)],
        out_specs=pl.BlockSpec(memory_space=pl.ANY),
        out_shape=jax.ShapeDtypeStruct((n_rows, d), BF16),
        scratch_shapes=[pltpu.VMEM((2, LOCAL_ROWS, d), BF16), pltpu.SemaphoreType.DMA((2,))],
        input_output_aliases={4: 0},
        compiler_params=_cp("arbitrary"),
        name="moe_dispatch",
    )(plan, plan, info_t, x, init)


def _expert_kernel(te_ref, nu_ref, xs_ref, w1_ref, w3_ref, w2_ref, o_ref, w13_scr, w2_scr):
    i = pl.program_id(0)
    f = w2_ref.shape[2]
    new_expert = (i == 0) | (te_ref[i] != te_ref[jnp.maximum(i - 1, 0)])

    @pl.when((i < nu_ref[0]) & new_expert)
    def _():
        w13_scr[:, :f] = w1_ref[0, 0].astype(BF16)
        w13_scr[:, f:] = w3_ref[0, 0].astype(BF16)
        w2_scr[...] = w2_ref[0, 0].astype(BF16)

    @pl.when(i < nu_ref[0])
    def _():
        h = jnp.dot(xs_ref[...], w13_scr[...], preferred_element_type=F32)
        h1 = h[:, :f]
        act = (h1 * _sigmoid(h1) * h[:, f:]).astype(BF16)
        o_ref[...] = jnp.dot(act, w2_scr[...], preferred_element_type=F32).astype(o_ref.dtype)

    @pl.when(i >= nu_ref[0])
    def _():
        o_ref[...] = jnp.zeros(o_ref.shape, o_ref.dtype)


def expert_ffn(xs, w1, w3, w2, layer, tile_expert, n_used):
    p, d = xs.shape
    f = w2.shape[2]
    n_tiles = p // MOE_TM
    grid_spec = pltpu.PrefetchScalarGridSpec(
        num_scalar_prefetch=2,
        grid=(n_tiles,),
        in_specs=[pl.BlockSpec((MOE_TM, d), lambda i, te, nu: (i, 0)),
                  pl.BlockSpec((1, 1, d, f), lambda i, te, nu: (layer, te[i], 0, 0)),
                  pl.BlockSpec((1, 1, d, f), lambda i, te, nu: (layer, te[i], 0, 0)),
                  pl.BlockSpec((1, 1, f, d), lambda i, te, nu: (layer, te[i], 0, 0))],
        out_specs=pl.BlockSpec((MOE_TM, d), lambda i, te, nu: (i, 0)),
        scratch_shapes=[pltpu.VMEM((d, 2 * f), BF16), pltpu.VMEM((f, d), BF16)],
    )
    return pl.pallas_call(
        _expert_kernel,
        grid_spec=grid_spec,
        out_shape=jax.ShapeDtypeStruct((p, d), BF16),
        compiler_params=_cp("arbitrary"),
        name="expert_ffn",
    )(tile_expert, n_used, xs, w1, w3, w2)


def _combine_kernel(plan_ref, next_plan_ref, info_ref, x_ref, ys_ref, lg_ref, lb_ref, xo_ref, xb_ref,
                    local_scr, sem, *, alpha):
    tm = x_ref.shape[0]
    i = pl.program_id(0)
    slot = i % 2

    @pl.when(i == 0)
    def _():
        _block_copies(plan_ref, local_scr.at[slot], ys_ref, sem.at[slot], False, True)

    @pl.when(i + 1 < pl.num_programs(0))
    def _():
        _block_copies(next_plan_ref, local_scr.at[1 - slot], ys_ref, sem.at[1 - slot], False, True)

    _block_copies(plan_ref, local_scr.at[slot], ys_ref, sem.at[slot], False, False)
    last = N_EXPERTS - 1
    used = plan_ref[0, 0, N_EXPERTS + last] + plan_ref[0, 0, last] * ROW_ALIGN
    rowid = lax.broadcasted_iota(I32, (LOCAL_ROWS, 1), 0)
    y = local_scr[slot]
    y = jnp.where(rowid < used, y, jnp.zeros_like(y))
    info = info_ref[...]
    col = lax.broadcasted_iota(I32, (tm, LOCAL_ROWS), 1).astype(F32)
    gate = jnp.where(col == info[:, 4:5], info[:, 2:3],
                     jnp.where(col == info[:, 5:6], info[:, 3:4], 0.0)).astype(BF16)
    moe = jnp.dot(gate, y, preferred_element_type=F32)
    out = _layer_norm(alpha * x_ref[...] + moe, lg_ref[...], lb_ref[...])
    xo_ref[...] = out
    xb_ref[...] = out.astype(BF16)


def combine_ln(plan, info, x, ys, ln_g, ln_b, alpha):
    t, d = x.shape
    tm = min(DISPATCH_TM, t)
    return pl.pallas_call(
        functools.partial(_combine_kernel, alpha=alpha),
        grid=(t // tm,),
        in_specs=[pl.BlockSpec((1, 1, 3 * N_EXPERTS), lambda i: (i, 0, 0), memory_space=pltpu.SMEM),
                  pl.BlockSpec((1, 1, 3 * N_EXPERTS), lambda i: (jnp.minimum(i + 1, t // tm - 1), 0, 0),
                               memory_space=pltpu.SMEM),
                  pl.BlockSpec((tm, LANES), lambda i: (i, 0)),
                  pl.BlockSpec((tm, d), lambda i: (i, 0)),
                  pl.BlockSpec(memory_space=pl.ANY),
                  pl.BlockSpec((1, d), lambda i: (0, 0)),
                  pl.BlockSpec((1, d), lambda i: (0, 0))],
        out_specs=[pl.BlockSpec((tm, d), lambda i: (i, 0)), pl.BlockSpec((tm, d), lambda i: (i, 0))],
        out_shape=[jax.ShapeDtypeStruct((t, d), F32), jax.ShapeDtypeStruct((t, d), BF16)],
        scratch_shapes=[pltpu.VMEM((2, LOCAL_ROWS, d), BF16), pltpu.SemaphoreType.DMA((2,))],
        compiler_params=_cp("arbitrary"),
        name="moe_combine_ln",
    )(plan, plan, info, x, ys, ln_g, ln_b)


def _prep_layer(l, p):
    w_in = p["w_in"][l]
    d = w_in.shape[0]
    o_dq = 0
    o_dkv = o_dq + Q_RANK
    o_z = o_dkv + KV_RANK + ROPE_DIM
    d_inner = SSM_HEADS * SSM_HEAD_DIM
    conv_ch = d_inner + 2 * SSM_GROUPS * D_STATE
    o_xbc = o_z + d_inner
    o_dt = o_xbc + conv_ch
    o_qm = o_dt + SSM_HEADS
    o_g = o_qm + XA_HEADS * XA_HEAD_DIM
    half = ROPE_DIM // 2
    kr0 = o_dkv + KV_RANK
    zeros = lambda n: jnp.zeros((d, n), F32)
    w_small = jnp.concatenate([
        w_in[:, o_dq:o_dq + Q_RANK],
        w_in[:, o_dkv:o_dkv + KV_RANK],
        w_in[:, kr0:kr0 + ROPE_DIM], zeros(LANES - ROPE_DIM),
        w_in[:, kr0 + half:kr0 + ROPE_DIM], w_in[:, kr0:kr0 + half], zeros(LANES - ROPE_DIM),
        w_in[:, o_dt:o_dt + SSM_HEADS], zeros(LANES - SSM_HEADS)], axis=1)
    w_uq = p["w_uq"][l].reshape(Q_RANK, MLA_HEADS, NOPE_DIM + ROPE_DIM)
    wq_nope = w_uq[:, :, :NOPE_DIM].reshape(Q_RANK, -1)
    wq_rope = w_uq[:, :, NOPE_DIM:]
    wq_rope_sw = jnp.concatenate([wq_rope[:, :, half:], wq_rope[:, :, :half]], axis=-1)
    wq = jnp.concatenate([wq_nope, wq_rope.reshape(Q_RANK, -1), wq_rope_sw.reshape(Q_RANK, -1)], axis=1)
    w_ukv = p["w_ukv"][l]
    wuk_t = jnp.transpose(w_ukv[:, :, :NOPE_DIM], (1, 2, 0))
    wuv = jnp.transpose(w_ukv[:, :, NOPE_DIM:], (1, 0, 2))
    pad_heads = lambda v, fill: jnp.concatenate(
        [v.astype(F32), jnp.full((LANES - SSM_HEADS,), fill, F32)]).reshape(1, LANES)
    bf = lambda a: a.astype(BF16)
    return dict(
        w_small=bf(w_small), w_z=bf(w_in[:, o_z:o_z + d_inner]), w_xbc=bf(w_in[:, o_xbc:o_xbc + conv_ch]),
        w_qm=bf(w_in[:, o_qm:o_g]), w_g=bf(w_in[:, o_g:]),
        q_norm=p["q_norm"][l].reshape(1, -1), kv_norm=p["kv_norm"][l].reshape(1, -1),
        wq=bf(wq), wuk_t=bf(wuk_t), wuv=bf(wuv),
        conv_w_half=0.5 * p["conv_w"][l], conv_b_half=0.5 * p["conv_b"][l].reshape(1, -1),
        dt_bias=pad_heads(p["dt_bias"][l], 0.0),
        a_neg=pad_heads(-jnp.exp(p["a_log"][l].astype(F32)), 0.0),
        d_skip=jnp.repeat(p["d_skip"][l].astype(F32), SSM_HEAD_DIM).reshape(1, -1),
        ssm_norm=p["ssm_norm"][l].reshape(1, -1),
        w_mem_kv=bf(p["w_mem_kv"][l]),
        wa=bf(p["w_proj_a"][l]), wb=bf(p["w_proj_b"][l]), wc=bf(p["w_proj_c"][l]), wo=bf(p["w_out"][l]),
        ln1_g=p["ln1_g"][l].reshape(1, -1), ln1_b=p["ln1_b"][l].reshape(1, -1),
        ln2_g=p["ln2_g"][l].reshape(1, -1), ln2_b=p["ln2_b"][l].reshape(1, -1),
    )


def _dispatch_plan(tinfo, counts, t):
    n_tok_tiles = tinfo.shape[0]
    n8 = tinfo[:, 0, :N_EXPERTS].astype(I32)
    block_start = tinfo[:, 1, :N_EXPERTS].astype(I32)
    carry = tinfo[:, 2, :N_EXPERTS].astype(I32)
    rows = counts[0, :N_EXPERTS].astype(I32)
    tiles = (rows + MOE_TM - 1) // MOE_TM
    tile_end = jnp.cumsum(tiles)
    row_start = (tile_end - tiles) * MOE_TM
    plan = jnp.concatenate([n8 // ROW_ALIGN, block_start, row_start[None, :] + carry], axis=1)
    max_rows = TOP_K * t + N_EXPERTS * (ROW_ALIGN - 1) * n_tok_tiles
    n_tiles = -(-max_rows // MOE_TM) + N_EXPERTS
    tile_ids = jnp.arange(n_tiles, dtype=I32)
    tile_expert = jnp.minimum(jnp.sum(tile_ids[:, None] >= tile_end[None, :], axis=1), N_EXPERTS - 1)
    n_used = tile_end[-1:].astype(I32)
    return plan.reshape(n_tok_tiles, 1, 3 * N_EXPERTS), tile_expert.astype(I32), n_used, n_tiles * MOE_TM


def kernel(x, mem, positions, w_in, q_norm, w_uq, kv_norm, w_ukv, w_proj_a, conv_w, conv_b, dt_bias, a_log,
           d_skip, ssm_norm, w_proj_b, w_mem_kv, w_proj_c, w_out, ln1_g, ln1_b, router_w, router_bias,
           exp_w1, exp_w3, exp_w2, ln2_g, ln2_b):
    params = dict(w_in=w_in, q_norm=q_norm, w_uq=w_uq, kv_norm=kv_norm, w_ukv=w_ukv, w_proj_a=w_proj_a,
                  conv_w=conv_w, conv_b=conv_b, dt_bias=dt_bias, a_log=a_log, d_skip=d_skip,
                  ssm_norm=ssm_norm, w_proj_b=w_proj_b, w_mem_kv=w_mem_kv, w_proj_c=w_proj_c, w_out=w_out,
                  ln1_g=ln1_g, ln1_b=ln1_b, exp_w1=exp_w1, exp_w3=exp_w3, exp_w2=exp_w2,
                  ln2_g=ln2_g, ln2_b=ln2_b)
    bsz, s, d = x.shape
    t = bsz * s
    depth = w_in.shape[0]
    alpha = float((2 * depth) ** 0.25)
    n_mem = mem.shape[1]

    inv = ROPE_THETA ** (-jnp.arange(0, ROPE_DIM, 2, dtype=F32) / ROPE_DIM)
    ang = positions.astype(F32)[..., None] * inv
    cos, sin = jnp.cos(ang), jnp.sin(ang)
    cosq = jnp.tile(jnp.concatenate([cos, cos], axis=-1), (1, 1, LANES // ROPE_DIM)).reshape(t, LANES)
    sinq = jnp.tile(jnp.concatenate([-sin, sin], axis=-1), (1, 1, LANES // ROPE_DIM)).reshape(t, LANES)

    rw = jnp.concatenate([router_w.astype(F32), jnp.zeros((d, LANES - N_EXPERTS), F32)], axis=1)
    rw_hi = rw.astype(BF16)
    rw_lo = (rw - rw_hi.astype(F32)).astype(BF16)
    rbt = jnp.broadcast_to(router_bias.astype(F32)[:, None], (N_EXPERTS, LANES))
    mem_b = mem.reshape(bsz * n_mem, d).astype(BF16)

    xf = x.reshape(t, d).astype(F32)
    xb = xf.astype(BF16)
    for l in range(depth):
        w = _prep_layer(l, params)
        ha = matmul(xb, w["w_small"], F32, "proj_small")
        z_act = matmul(xb, w["w_z"], BF16, "proj_z_silu", silu=True)
        xbc_c = proj_conv_silu(xb, w["w_xbc"], w["conv_w_half"], w["conv_b_half"], s).reshape(bsz, s, -1)
        qm = matmul(xb, w["w_qm"], BF16, "proj_qmem")
        g = matmul(xb, w["w_g"], BF16, "proj_gate")

        ql, qr, ck, kr = mla_prep(ha, cosq, sinq, w["q_norm"], w["kv_norm"], w["wq"], w["wuk_t"],
                                  min(ATT_TQ, s))
        o_lat = mla_attention(ql, qr, ck.reshape(bsz, s, KV_RANK), kr.reshape(bsz, s, ROPE_DIM))

        yn = ssd(xbc_c, z_act.reshape(bsz, s, -1), ha.reshape(bsz, s, HA_W), w["dt_bias"], w["a_neg"],
                 w["d_skip"], w["ssm_norm"]).reshape(t, -1)

        kv = matmul(mem_b, w["w_mem_kv"], BF16, "proj_memkv").reshape(bsz, n_mem, -1)
        cm = mem_attention(qm.reshape(bsz, s, -1), kv).reshape(t, -1)

        x1 = merge_ln(o_lat, yn, cm, g, xf, w["wuv"], w["wa"], w["wb"], w["wc"], w["wo"],
                           w["ln1_g"], w["ln1_b"], alpha)

        info, info_t, tinfo, counts = route(x1, rw_hi, rw_lo, rbt)
        plan, tile_expert, n_used, n_rows = _dispatch_plan(tinfo, counts, t)
        xs = dispatch_rows(x1, info_t, plan, n_rows)
        ys = expert_ffn(xs, exp_w1, exp_w3, exp_w2, l, tile_expert, n_used)
        xf, xb = combine_ln(plan, info, x1, ys, w["ln2_g"], w["ln2_b"], alpha)
    return xf.reshape(bsz, s, d)
```

```python
import functools

import jax
import jax.numpy as jnp
from jax import lax
from jax.experimental import pallas as pl
from jax.experimental.pallas import tpu as pltpu

F32 = jnp.float32
BF16 = jnp.bfloat16
I32 = jnp.int32

MLA_HEADS = 8
Q_RANK = 384
KV_RANK = 256
NOPE_DIM = 128
ROPE_DIM = 64
V_DIM = 128
ROPE_THETA = 10000.0
SSM_HEADS = 32
SSM_HEAD_DIM = 64
SSM_GROUPS = 8
D_STATE = 128
CONV_K = 4
XA_HEADS = 4
XA_HEAD_DIM = 256
N_EXPERTS = 16
N_EXPERT_GROUPS = 4
EXPERTS_PER_GROUP = 4
TOP_K = 2
NORM_EPS = 1e-5
RMS_EPS = 1e-6

LANES = 128
V7X_VMEM_LIMIT = 56 * 1024 * 1024

MM_TM = 2048
MM_TN = 1024
PREP_TM = 1024
ATT_TQ = 512
ATT_TK = 512
ATT_GROUP_ROWS = 1024
ATT_UNROLL = 2
LOG2E = 1.4426950408889634
PROJ_CONV_TM = 1024
PROJ_CONV_TN = 1024
PROJ_CONV_SUB = 256
CONV_HALO = 16
SSD_CHUNK = 256
XA_TQ = 2048
MERGE_TM = 512
MOE_TM = 512
DISPATCH_TM = 512
ROW_ALIGN = 16
LOCAL_ROWS = 1280

HA_W = 1024
HA_DQ = 0
HA_C = 384
HA_KR = 640
HA_KRS = 768
HA_DT = 896


def _cp(*sem):
    return pltpu.CompilerParams(dimension_semantics=sem, vmem_limit_bytes=V7X_VMEM_LIMIT)


def _sigmoid(x):
    return 1.0 / (1.0 + jnp.exp(-x))


def _mm_kernel(x_ref, w_ref, o_ref, *, silu):
    y = jnp.dot(x_ref[...], w_ref[...], preferred_element_type=F32)
    if silu:
        half = 0.5 * y
        y = half * jnp.tanh(half) + half
    o_ref[...] = y.astype(o_ref.dtype)


def matmul(x, w, out_dtype, name, silu=False):
    m, k = x.shape
    n = w.shape[1]
    tm = min(MM_TM, m)
    tn = min(MM_TN, n)
    return pl.pallas_call(
        functools.partial(_mm_kernel, silu=silu),
        grid=(n // tn, m // tm),
        in_specs=[pl.BlockSpec((tm, k), lambda j, i: (i, 0)),
                  pl.BlockSpec((k, tn), lambda j, i: (0, j))],
        out_specs=pl.BlockSpec((tm, tn), lambda j, i: (i, j)),
        out_shape=jax.ShapeDtypeStruct((m, n), out_dtype),
        compiler_params=_cp("parallel", "parallel"),
        name=name,
    )(x, w)


def _proj_conv_kernel(x_ref, xh_ref, w_ref, cw_ref, cb_ref, o_ref, *, tiles_per_seq):
    first = (pl.program_id(1) % tiles_per_seq) == 0
    tn = o_ref.shape[1]
    sub = min(PROJ_CONV_SUB, tn)
    def project(c):
        w = w_ref[:, c * sub:(c + 1) * sub]
        u = jnp.dot(x_ref[...], w, preferred_element_type=F32)
        halo = jnp.dot(xh_ref[...], w, preferred_element_type=F32)
        return jnp.concatenate([jnp.where(first, jnp.zeros_like(halo), halo), u], axis=0)

    n_sub = tn // sub
    nxt = project(0)
    for c in range(n_sub):
        cs = slice(c * sub, (c + 1) * sub)
        cw = cw_ref[:, cs]
        ext = nxt
        if c + 1 < n_sub:
            nxt = project(c + 1)
        prev = pltpu.roll(ext, 1, 0)
        near = cw[3:4, :] * ext + cw[2:3, :] * prev
        far = cw[1:2, :] * ext + cw[0:1, :] * prev
        half = (near + pltpu.roll(far, 2, 0))[CONV_HALO:, :] + cb_ref[:, cs]
        o_ref[:, cs] = (half * jnp.tanh(half) + half).astype(o_ref.dtype)


def proj_conv_silu(x, w, conv_w, conv_b, seq_len):
    m, k = x.shape
    n = w.shape[1]
    tm = min(PROJ_CONV_TM, seq_len)
    tn = min(PROJ_CONV_TN, n)
    hb = tm // CONV_HALO
    return pl.pallas_call(
        functools.partial(_proj_conv_kernel, tiles_per_seq=seq_len // tm),
        grid=(n // tn, m // tm),
        in_specs=[pl.BlockSpec((tm, k), lambda j, i: (i, 0)),
                  pl.BlockSpec((CONV_HALO, k), lambda j, i: (jnp.maximum(i * hb - 1, 0), 0)),
                  pl.BlockSpec((k, tn), lambda j, i: (0, j)),
                  pl.BlockSpec((CONV_K, tn), lambda j, i: (0, j)),
                  pl.BlockSpec((1, tn), lambda j, i: (0, j))],
        out_specs=pl.BlockSpec((tm, tn), lambda j, i: (i, j)),
        out_shape=jax.ShapeDtypeStruct((m, n), BF16),
        compiler_params=_cp("parallel", "parallel"),
        name="proj_xbc_conv",
    )(x, x, w, conv_w, conv_b)


def _mla_prep_kernel(ha_ref, cos_ref, sin_ref, qn_ref, kvn_ref, wq_ref, wuk_ref,
                     ql_ref, qr_ref, ck_ref, kr_ref, *, scale):
    nb, _, tq, _ = ql_ref.shape
    ha = ha_ref[...]
    dq = ha[:, HA_DQ:HA_DQ + Q_RANK]
    c_q = dq * lax.rsqrt(jnp.mean(dq * dq, axis=-1, keepdims=True) + RMS_EPS) * qn_ref[...]
    q = jnp.dot(c_q.astype(BF16), wq_ref[...], preferred_element_type=F32)
    n_nope = MLA_HEADS * NOPE_DIM
    n_rope = MLA_HEADS * ROPE_DIM
    cosq = jnp.tile(cos_ref[...], (1, n_rope // LANES))
    sinq = jnp.tile(sin_ref[...], (1, n_rope // LANES))
    q_rope = ((q[:, n_nope:n_nope + n_rope] * cosq + q[:, n_nope + n_rope:] * sinq) * scale).astype(BF16)
    for h in range(MLA_HEADS):
        qh = q[:, h * NOPE_DIM:(h + 1) * NOPE_DIM].astype(BF16)
        ql = jnp.dot(qh, wuk_ref[h], preferred_element_type=F32)
        ql_ref[:, h] = (ql * scale).astype(BF16).reshape(nb, tq, KV_RANK)
        qr_ref[:, h] = q_rope[:, h * ROPE_DIM:(h + 1) * ROPE_DIM].reshape(nb, tq, ROPE_DIM)
    c = ha[:, HA_C:HA_C + KV_RANK]
    c_kv = c * lax.rsqrt(jnp.mean(c * c, axis=-1, keepdims=True) + RMS_EPS) * kvn_ref[...]
    ck_ref[...] = c_kv.astype(BF16)
    k_rope = (ha[:, HA_KR:HA_KR + LANES] * cos_ref[...]
              + ha[:, HA_KRS:HA_KRS + LANES] * sin_ref[...])
    kr_ref[...] = k_rope[:, :ROPE_DIM].astype(BF16)


def mla_prep(ha, cosq, sinq, q_norm, kv_norm, wq, wuk_t, tq):
    t = ha.shape[0]
    tm = min(PREP_TM, t)
    nb = tm // tq
    scale = float((NOPE_DIM + ROPE_DIM) ** -0.5 * LOG2E)
    n_rope = MLA_HEADS * ROPE_DIM
    full = lambda shape: pl.BlockSpec(shape, lambda i: (0,) * len(shape))
    return pl.pallas_call(
        functools.partial(_mla_prep_kernel, scale=scale),
        grid=(t // tm,),
        in_specs=[pl.BlockSpec((tm, HA_W), lambda i: (i, 0)),
                  pl.BlockSpec((tm, LANES), lambda i: (i, 0)),
                  pl.BlockSpec((tm, LANES), lambda i: (i, 0)),
                  full((1, Q_RANK)), full((1, KV_RANK)),
                  full(wq.shape), full(wuk_t.shape)],
        out_specs=[pl.BlockSpec((nb, MLA_HEADS, tq, KV_RANK), lambda i: (i, 0, 0, 0)),
                   pl.BlockSpec((nb, MLA_HEADS, tq, ROPE_DIM), lambda i: (i, 0, 0, 0)),
                   pl.BlockSpec((tm, KV_RANK), lambda i: (i, 0)),
                   pl.BlockSpec((tm, ROPE_DIM), lambda i: (i, 0))],
        out_shape=[jax.ShapeDtypeStruct((t // tq, MLA_HEADS, tq, KV_RANK), BF16),
                   jax.ShapeDtypeStruct((t // tq, MLA_HEADS, tq, ROPE_DIM), BF16),
                   jax.ShapeDtypeStruct((t, KV_RANK), BF16),
                   jax.ShapeDtypeStruct((t, ROPE_DIM), BF16)],
        compiler_params=_cp("parallel"),
        name="mla_prep",
    )(ha, cosq, sinq, q_norm, kv_norm, wq, wuk_t)


def _mla_attn_kernel(ql_ref, qr_ref, ck_ref, kr_ref, o_ref, m_scr, l_scr, acc_scr, *, tq, tk, hpc):
    rc = hpc * tq
    n_groups = MLA_HEADS // hpc
    q_start = pl.program_id(1) * tq
    n_full = q_start // tk
    m_scr[...] = jnp.full(m_scr.shape, -jnp.inf, F32)
    l_scr[...] = jnp.zeros(l_scr.shape, F32)
    acc_scr[...] = jnp.zeros(acc_scr.shape, F32)
    nt = (((1,), (1,)), ((), ()))

    def step(j, masked, width=tk):
        ks = pl.multiple_of(j * tk, tk)
        ck = ck_ref[0, pl.ds(ks, width), :]
        kr = kr_ref[0, pl.ds(ks, width), :]
        if masked:
            q_pos = q_start + (lax.broadcasted_iota(I32, (rc, width), 0) & (tq - 1))
            k_pos = ks + lax.broadcasted_iota(I32, (rc, width), 1)
            visible = k_pos <= q_pos
        for c in range(n_groups):
            rs = slice(c * rc, (c + 1) * rc)
            ql = ql_ref[0, c * hpc:(c + 1) * hpc].reshape(rc, KV_RANK)
            qr = qr_ref[0, c * hpc:(c + 1) * hpc].reshape(rc, ROPE_DIM)
            s = (lax.dot_general(ql, ck, nt, preferred_element_type=F32)
                 + lax.dot_general(qr, kr, nt, preferred_element_type=F32))
            if masked:
                s = jnp.where(visible, s, -jnp.inf)
            m_prev = m_scr[rs]
            m_new = jnp.maximum(m_prev, jnp.max(s, axis=-1, keepdims=True))
            alpha = jnp.exp2(m_prev - m_new)
            p = jnp.exp2(s - jnp.tile(m_new, (1, width // LANES)))
            l_scr[rs] = alpha * l_scr[rs] + jnp.sum(p, axis=-1, keepdims=True)
            acc_scr[rs] = (jnp.tile(alpha, (1, KV_RANK // LANES)) * acc_scr[rs]
                           + jnp.dot(p.astype(BF16), ck, preferred_element_type=F32))
            m_scr[rs] = m_new

    def trip(jj, carry):
        for u in range(ATT_UNROLL):
            step(ATT_UNROLL * jj + u, False)
        return carry

    n_trips = n_full // ATT_UNROLL
    lax.fori_loop(0, n_trips, trip, 0)

    def single(j, carry):
        step(j, False)
        return carry

    lax.fori_loop(n_trips * ATT_UNROLL, n_full, single, 0)

    sub = (q_start - n_full * tk) // tq
    for v in range(tk // tq):
        @pl.when(sub == v)
        def _(v=v):
            step(n_full, True, (v + 1) * tq)
    out = acc_scr[...] / jnp.tile(l_scr[...], (1, KV_RANK // LANES))
    o_ref[0] = out.astype(o_ref.dtype).reshape(MLA_HEADS, tq, KV_RANK)


def mla_attention(ql, qr, ck, kr):
    _, _, tq, _ = ql.shape
    b, s, _ = ck.shape
    tk = min(ATT_TK, s)
    assert tk % tq == 0 and s % tk == 0 and tq & (tq - 1) == 0
    nq = s // tq
    rows = tq * MLA_HEADS
    hpc = max(1, min(MLA_HEADS, ATT_GROUP_ROWS // tq))
    qspec = lambda dim: pl.BlockSpec((1, MLA_HEADS, tq, dim), lambda bi, i: (bi * nq + i, 0, 0, 0))
    return pl.pallas_call(
        functools.partial(_mla_attn_kernel, tq=tq, tk=tk, hpc=hpc),
        grid=(b, nq),
        in_specs=[qspec(KV_RANK), qspec(ROPE_DIM),
                  pl.BlockSpec((1, s, KV_RANK), lambda bi, i: (bi, 0, 0)),
                  pl.BlockSpec((1, s, ROPE_DIM), lambda bi, i: (bi, 0, 0))],
        out_specs=qspec(KV_RANK),
        out_shape=jax.ShapeDtypeStruct(ql.shape, BF16),
        scratch_shapes=[pltpu.VMEM((rows, LANES), F32), pltpu.VMEM((rows, LANES), F32),
                        pltpu.VMEM((rows, KV_RANK), F32)],
        compiler_params=_cp("parallel", "parallel"),
        name="mla_attention",
    )(ql, qr, ck, kr)


def _ssd_kernel(xbc_ref, z_ref, dt_ref, dtb_ref, a_ref, dsk_ref, ng_ref, exp_ref, o_ref,
                state_scr, *, chunk):
    d_inner = SSM_HEADS * SSM_HEAD_DIM
    gn = SSM_GROUPS * D_STATE
    rep = SSM_HEADS // SSM_GROUPS
    gw = rep * SSM_HEAD_DIM

    @pl.when(pl.program_id(1) == 0)
    def _():
        state_scr[...] = jnp.zeros(state_scr.shape, F32)

    x_raw = dt_ref[0] + dtb_ref[...]
    dt = jnp.maximum(x_raw, 0.0) + jnp.log(1.0 + jnp.exp(-jnp.abs(x_raw)))
    da = dt * a_ref[...]
    row = lax.broadcasted_iota(I32, (chunk, chunk), 0)
    col = lax.broadcasted_iota(I32, (chunk, chunk), 1)
    causal = row >= col
    tri = jnp.where(causal, 1.0, 0.0).astype(BF16)
    acum = jnp.zeros(da.shape, F32)
    rem = da
    for _ in range(3):
        part = rem.astype(BF16)
        acum = acum + jnp.dot(tri, part, preferred_element_type=F32)
        rem = rem - part.astype(F32)
    a2 = acum * LOG2E
    a2_t = a2.T
    src_t = a2_t - jnp.log2(dt.T)
    w_t = jnp.exp2(a2_t[:, chunk - 1:chunk] - src_t)
    e_end = jnp.exp2(a2[chunk - 1:chunk, :])
    e_all = jnp.dot(jnp.exp2(a2).astype(BF16), exp_ref[...], preferred_element_type=F32)
    head_of_lane = lax.broadcasted_iota(I32, (1, gw), 1) // SSM_HEAD_DIM

    for g in range(SSM_GROUPS):
        bg = xbc_ref[0, :, d_inner + g * D_STATE:d_inner + (g + 1) * D_STATE]
        cg = xbc_ref[0, :, d_inner + gn + g * D_STATE:d_inner + gn + (g + 1) * D_STATE]
        bt = bg.astype(F32).T
        cb = jnp.dot(cg, bt.astype(BF16), preferred_element_type=F32)
        st = state_scr[g]
        ys = jnp.dot(cg, st.astype(BF16), preferred_element_type=F32)
        xg = xbc_ref[0, :, g * gw:(g + 1) * gw].astype(F32)
        y = jnp.zeros((chunk, gw), F32)
        upd = jnp.zeros((D_STATE, gw), F32)
        sc = jnp.zeros((1, gw), F32)
        for r in range(rep):
            h = g * rep + r
            own = head_of_lane == r
            seg = a2[:, h:h + 1] - src_t[h:h + 1, :]
            m = (cb * jnp.exp2(jnp.where(causal, seg, -jnp.inf))).astype(BF16)
            xm = jnp.where(own, xg, 0.0).astype(BF16)
            y = y + jnp.dot(m, xm, preferred_element_type=F32)
            upd = upd + jnp.dot((bt * w_t[h:h + 1, :]).astype(BF16), xm, preferred_element_type=F32)
            sc = jnp.where(own, e_end[:, h:h + 1], sc)
        state_scr[g] = st * sc + upd
        gs = slice(g * gw, (g + 1) * gw)
        y = y + ys * e_all[:, gs] + xg * dsk_ref[:, gs]
        y = y * z_ref[0, :, gs].astype(F32)
        y = y * lax.rsqrt(jnp.mean(y * y, axis=-1, keepdims=True) + RMS_EPS) * ng_ref[:, gs]
        o_ref[0, :, gs] = y.astype(o_ref.dtype)


def ssd(xbc, z, ha3, dt_bias, a_neg, d_skip, norm_g):
    bsz, s, c = xbc.shape
    d_inner = SSM_HEADS * SSM_HEAD_DIM
    chunk = min(SSD_CHUNK, s)
    gw = d_inner // SSM_GROUPS
    vec = lambda n: pl.BlockSpec((1, n), lambda bi, ci: (0, 0))
    expand = (jnp.arange(d_inner, dtype=I32)[None, :] // SSM_HEAD_DIM
              == jnp.arange(LANES, dtype=I32)[:, None]).astype(BF16)
    return pl.pallas_call(
        functools.partial(_ssd_kernel, chunk=chunk),
        grid=(bsz, s // chunk),
        in_specs=[pl.BlockSpec((1, chunk, c), lambda bi, ci: (bi, ci, 0)),
                  pl.BlockSpec((1, chunk, d_inner), lambda bi, ci: (bi, ci, 0)),
                  pl.BlockSpec((1, chunk, LANES), lambda bi, ci: (bi, ci, HA_DT // LANES)),
                  vec(LANES), vec(LANES), vec(d_inner), vec(d_inner),
                  pl.BlockSpec((LANES, d_inner), lambda bi, ci: (0, 0))],
        out_specs=pl.BlockSpec((1, chunk, d_inner), lambda bi, ci: (bi, ci, 0)),
        out_shape=jax.ShapeDtypeStruct((bsz, s, d_inner), BF16),
        scratch_shapes=[pltpu.VMEM((SSM_GROUPS, D_STATE, gw), F32)],
        compiler_params=_cp("parallel", "arbitrary"),
        name="ssd_scan",
    )(xbc, z, ha3, dt_bias, a_neg, d_skip, norm_g, expand)


def _xattn_kernel(q_ref, kv_ref, o_ref, *, scale):
    hd = XA_HEADS * XA_HEAD_DIM
    nt = (((1,), (1,)), ((), ()))
    for h in range(XA_HEADS):
        sl = slice(h * XA_HEAD_DIM, (h + 1) * XA_HEAD_DIM)
        q = q_ref[0, :, sl]
        k = kv_ref[0, :, sl]
        v = kv_ref[0, :, hd + h * XA_HEAD_DIM:hd + (h + 1) * XA_HEAD_DIM]
        s = lax.dot_general(q, k, nt, preferred_element_type=F32) * scale
        p = jnp.exp(s - jnp.max(s, axis=-1, keepdims=True))
        l = jnp.sum(p, axis=-1, keepdims=True)
        o = jnp.dot(p.astype(BF16), v, preferred_element_type=F32) / l
        o_ref[0, :, sl] = o.astype(o_ref.dtype)


def mem_attention(q, kv):
    bsz, s, hd = q.shape
    m = kv.shape[1]
    tq = min(XA_TQ, s)
    return pl.pallas_call(
        functools.partial(_xattn_kernel, scale=float(XA_HEAD_DIM ** -0.5)),
        grid=(bsz, s // tq),
        in_specs=[pl.BlockSpec((1, tq, hd), lambda bi, i: (bi, i, 0)),
                  pl.BlockSpec((1, m, 2 * hd), lambda bi, i: (bi, 0, 0))],
        out_specs=pl.BlockSpec((1, tq, hd), lambda bi, i: (bi, i, 0)),
        out_shape=jax.ShapeDtypeStruct((bsz, s, hd), BF16),
        compiler_params=_cp("parallel", "parallel"),
        name="mem_attention",
    )(q, kv)


def _layer_norm(v, g, b):
    mu = jnp.mean(v, axis=-1, keepdims=True)
    d = v - mu
    var = jnp.mean(d * d, axis=-1, keepdims=True)
    return d * lax.rsqrt(var + NORM_EPS) * g + b


def _merge_kernel(ol_ref, yn_ref, cm_ref, g_ref, x_ref, wuv_ref, wa_ref, wb_ref, wc_ref, wo_ref,
                  lg_ref, lb_ref, xo_ref, a_scr, *, alpha):
    tm, d = x_ref.shape
    for h in range(MLA_HEADS):
        a_scr[:, h * V_DIM:(h + 1) * V_DIM] = jnp.dot(
            ol_ref[:, h].reshape(tm, KV_RANK), wuv_ref[h],
            preferred_element_type=F32).astype(BF16)
    o_a = jnp.dot(a_scr[...], wa_ref[...], preferred_element_type=F32)
    o_b = jnp.dot(yn_ref[...], wb_ref[...], preferred_element_type=F32)
    o_c = jnp.dot(cm_ref[...], wc_ref[...], preferred_element_type=F32)
    g = g_ref[...].astype(F32)
    merged = (_sigmoid(g[:, :d]) * o_a + _sigmoid(g[:, d:2 * d]) * o_b + _sigmoid(g[:, 2 * d:]) * o_c)
    y = alpha * x_ref[...] + jnp.dot(merged.astype(BF16), wo_ref[...], preferred_element_type=F32)
    xo_ref[...] = _layer_norm(y, lg_ref[...], lb_ref[...])


def merge_ln(ol, yn, cm, g, x, wuv, wa, wb, wc, wo, ln_g, ln_b, alpha):
    t, d = x.shape
    tm = min(MERGE_TM, t)
    tq = ol.shape[2]
    row = lambda n: pl.BlockSpec((tm, n), lambda i: (i, 0))
    full = lambda a: pl.BlockSpec(a.shape, lambda i: (0,) * a.ndim, pipeline_mode=pl.Buffered(1))
    return pl.pallas_call(
        functools.partial(_merge_kernel, alpha=alpha),
        grid=(t // tm,),
        in_specs=[pl.BlockSpec((tm // tq, MLA_HEADS, tq, KV_RANK), lambda i: (i, 0, 0, 0)),
                  row(yn.shape[1]), row(cm.shape[1]), row(g.shape[1]), row(d),
                  full(wuv), full(wa), full(wb), full(wc), full(wo), full(ln_g), full(ln_b)],
        out_specs=row(d),
        out_shape=jax.ShapeDtypeStruct((t, d), F32),
        scratch_shapes=[pltpu.VMEM((tm, MLA_HEADS * V_DIM), BF16)],
        compiler_params=_cp("parallel"),
        name="merge_ln",
    )(ol, yn, cm, g, x, wuv, wa, wb, wc, wo, ln_g, ln_b)


def _first_max(v, expert):
    m = jnp.max(v, axis=0, keepdims=True)
    idx = jnp.min(jnp.where(v == m, expert, N_EXPERTS), axis=0, keepdims=True)
    return m, idx


def _route_kernel(x_ref, rwh_ref, rwl_ref, rbt_ref, info_ref, infot_ref, tinfo_ref, cnt_ref, carry_scr):
    tm = x_ref.shape[0]

    @pl.when(pl.program_id(0) == 0)
    def _():
        carry_scr[...] = jnp.zeros(carry_scr.shape, F32)

    x = x_ref[...]
    x_hi = x.astype(BF16)
    x_lo = (x - x_hi.astype(F32)).astype(BF16)
    logits = (jnp.dot(x_hi, rwh_ref[...], preferred_element_type=F32)
              + jnp.dot(x_lo, rwh_ref[...], preferred_element_type=F32)
              + jnp.dot(x_hi, rwl_ref[...], preferred_element_type=F32))
    scores = _sigmoid(logits.T[:N_EXPERTS, :])
    sel = scores + jnp.tile(rbt_ref[...], (1, tm // LANES))
    expert = lax.broadcasted_iota(I32, (N_EXPERTS, tm), 0)
    neg = -jnp.inf
    best_score = None
    best_group = None
    for j in range(N_EXPERT_GROUPS):
        in_j = (expert >= j * EXPERTS_PER_GROUP) & (expert < (j + 1) * EXPERTS_PER_GROUP)
        v = jnp.where(in_j, sel, neg)
        m1, i1 = _first_max(v, expert)
        m2, _ = _first_max(jnp.where(expert == i1, neg, v), expert)
        gs = m1 + m2
        if j == 0:
            best_score, best_group = gs, jnp.zeros_like(i1)
        else:
            better = gs > best_score
            best_score = jnp.where(better, gs, best_score)
            best_group = jnp.where(better, j, best_group)
    lo = best_group * EXPERTS_PER_GROUP
    v = jnp.where((expert >= lo) & (expert < lo + EXPERTS_PER_GROUP), sel, neg)
    _, e1 = _first_max(v, expert)
    _, e2 = _first_max(jnp.where(expert == e1, neg, v), expert)
    w1 = jnp.sum(jnp.where(expert == e1, scores, 0.0), axis=0, keepdims=True)
    w2 = jnp.sum(jnp.where(expert == e2, scores, 0.0), axis=0, keepdims=True)
    wsum = w1 + w2
    member = jnp.where(expert == e1, 1.0, jnp.where(expert == e2, 1.0, 0.0))
    row = lax.broadcasted_iota(I32, (tm, tm), 0)
    col = lax.broadcasted_iota(I32, (tm, tm), 1)
    earlier = jnp.where(row < col, 1.0, 0.0).astype(BF16)
    lrank = jnp.dot(member.astype(BF16), earlier, preferred_element_type=F32)
    n_col = jnp.sum(member, axis=1, keepdims=True)
    diag = (lax.broadcasted_iota(I32, (N_EXPERTS, LANES), 0)
            == lax.broadcasted_iota(I32, (N_EXPERTS, LANES), 1))
    n = jnp.sum(jnp.where(diag, n_col, 0.0), axis=0, keepdims=True)
    n8 = jnp.floor((n + (ROW_ALIGN - 1)) * (1.0 / ROW_ALIGN)) * ROW_ALIGN
    ua = lax.broadcasted_iota(I32, (LANES, LANES), 0)
    ub = lax.broadcasted_iota(I32, (LANES, LANES), 1)
    lower_experts = jnp.where(ua < ub, 1.0, 0.0).astype(BF16)
    n8_rows = jnp.broadcast_to(n8, (8, LANES))
    block_start = jnp.dot(n8_rows.astype(BF16), lower_experts, preferred_element_type=F32)
    start_col = jnp.sum(jnp.where(diag, block_start[0:1, :], 0.0), axis=1, keepdims=True)
    local_row = start_col + lrank
    j1 = jnp.sum(jnp.where(expert == e1, local_row, 0.0), axis=0, keepdims=True)
    j2 = jnp.sum(jnp.where(expert == e2, local_row, 0.0), axis=0, keepdims=True)
    carry = carry_scr[...]
    sub = lax.broadcasted_iota(I32, (8, LANES), 0)
    tinfo_ref[0] = jnp.where(sub == 0, n8_rows, jnp.where(sub == 1, block_start, jnp.where(sub == 2, carry, 0.0)))
    new_carry = carry + n8_rows
    carry_scr[...] = new_carry
    cnt_ref[...] = new_carry
    field = lax.broadcasted_iota(I32, (8, tm), 0)
    info_t = jnp.where(field == 0, e1.astype(F32),
             jnp.where(field == 1, e2.astype(F32),
             jnp.where(field == 2, w1 / wsum,
             jnp.where(field == 3, w2 / wsum,
             jnp.where(field == 4, j1,
             jnp.where(field == 5, j2, 0.0))))))
    infot_ref[0] = info_t
    info_ref[...] = jnp.concatenate([info_t, jnp.zeros((LANES - 8, tm), F32)], axis=0).T


def route(x, rw_hi, rw_lo, rb):
    t, d = x.shape
    tm = min(DISPATCH_TM, t)
    return pl.pallas_call(
        _route_kernel,
        grid=(t // tm,),
        in_specs=[pl.BlockSpec((tm, d), lambda i: (i, 0)),
                  pl.BlockSpec((d, LANES), lambda i: (0, 0)),
                  pl.BlockSpec((d, LANES), lambda i: (0, 0)),
                  pl.BlockSpec((N_EXPERTS, LANES), lambda i: (0, 0))],
        out_specs=[pl.BlockSpec((tm, LANES), lambda i: (i, 0)),
                   pl.BlockSpec((1, 8, tm), lambda i: (i, 0, 0)),
                   pl.BlockSpec((1, 8, LANES), lambda i: (i, 0, 0)),
                   pl.BlockSpec((8, LANES), lambda i: (0, 0))],
        out_shape=[jax.ShapeDtypeStruct((t, LANES), F32),
                   jax.ShapeDtypeStruct((t // tm, 8, tm), F32),
                   jax.ShapeDtypeStruct((t // tm, 8, LANES), F32),
                   jax.ShapeDtypeStruct((8, LANES), F32)],
        scratch_shapes=[pltpu.VMEM((8, LANES), F32)],
        compiler_params=_cp("arbitrary"),
        name="route",
    )(x, rw_hi, rw_lo, rb)


def _block_copies(plan_ref, local_ref, sorted_ref, sem, to_sorted, start):
    for e in range(N_EXPERTS):
        count = plan_ref[0, 0, e]
        loc0 = plan_ref[0, 0, N_EXPERTS + e]
        dst0 = plan_ref[0, 0, 2 * N_EXPERTS + e]

        def body(k, carry, loc0=loc0, dst0=dst0):
            loc = local_ref.at[pl.ds(pl.multiple_of(loc0 + k * ROW_ALIGN, ROW_ALIGN), ROW_ALIGN)]
            srt = sorted_ref.at[pl.ds(pl.multiple_of(dst0 + k * ROW_ALIGN, ROW_ALIGN), ROW_ALIGN)]
            cp = pltpu.make_async_copy(loc, srt, sem) if to_sorted else pltpu.make_async_copy(srt, loc, sem)
            if start:
                cp.start()
            else:
                cp.wait()
            return carry

        lax.fori_loop(0, count, body, 0)


def _dispatch_kernel(plan_ref, prev_plan_ref, infot_ref, x_ref, init_ref, o_ref, local_scr, sem):
    del init_ref
    tm = x_ref.shape[0]
    i = pl.program_id(0)
    slot = i % 2
    j1 = infot_ref[0, 4:5, :]
    j2 = infot_ref[0, 5:6, :]
    rowid = lax.broadcasted_iota(I32, (LOCAL_ROWS, tm), 0).astype(F32)
    sel = jnp.where(rowid == j1, 1.0, jnp.where(rowid == j2, 1.0, 0.0)).astype(BF16)
    local_scr[slot] = jnp.dot(sel, x_ref[...].astype(BF16), preferred_element_type=F32).astype(BF16)

    @pl.when(i > 0)
    def _():
        _block_copies(prev_plan_ref, local_scr.at[1 - slot], o_ref, sem.at[1 - slot], True, False)

    _block_copies(plan_ref, local_scr.at[slot], o_ref, sem.at[slot], True, True)

    @pl.when(i == pl.num_programs(0) - 1)
    def _():
        _block_copies(plan_ref, local_scr.at[slot], o_ref, sem.at[slot], True, False)


def dispatch_rows(x, info_t, plan, n_rows):
    t, d = x.shape
    tm = min(DISPATCH_TM, t)
    init = jnp.zeros((n_rows, d), BF16)
    return pl.pallas_call(
        _dispatch_kernel,
        grid=(t // tm,),
        in_specs=[pl.BlockSpec((1, 1, 3 * N_EXPERTS), lambda i: (i, 0, 0), memory_space=pltpu.SMEM),
                  pl.BlockSpec((1, 1, 3 * N_EXPERTS), lambda i: (jnp.maximum(i - 1, 0), 0, 0),
                               memory_space=pltpu.SMEM),
                  pl.BlockSpec((1, 8, tm), lambda i: (i, 0, 0)),
                  pl.BlockSpec((tm, d), lambda i: (i, 0)),
                  pl.BlockSpec(memory_space=pl.ANY)],
        out_specs=pl.BlockSpec(memory_space=pl.ANY),
        out_shape=jax.ShapeDtypeStruct((n_rows, d), BF16),
        scratch_shapes=[pltpu.VMEM((2, LOCAL_ROWS, d), BF16), pltpu.SemaphoreType.DMA((2,))],
        input_output_aliases={4: 0},
        compiler_params=_cp("arbitrary"),
        name="moe_dispatch",
    )(plan, plan, info_t, x, init)


def _expert_kernel(te_ref, nu_ref, xs_ref, w1_ref, w3_ref, w2_ref, o_ref, w13_scr, w2_scr):
    i = pl.program_id(0)
    f = w2_ref.shape[2]
    new_expert = (i == 0) | (te_ref[i] != te_ref[jnp.maximum(i - 1, 0)])

    @pl.when((i < nu_ref[0]) & new_expert)
    def _():
        w13_scr[:, :f] = w1_ref[0, 0].astype(BF16)
        w13_scr[:, f:] = w3_ref[0, 0].astype(BF16)
        w2_scr[...] = w2_ref[0, 0].astype(BF16)

    @pl.when(i < nu_ref[0])
    def _():
        h = jnp.dot(xs_ref[...], w13_scr[...], preferred_element_type=F32)
        h1 = h[:, :f]
        act = (h1 * _sigmoid(h1) * h[:, f:]).astype(BF16)
        o_ref[...] = jnp.dot(act, w2_scr[...], preferred_element_type=F32).astype(o_ref.dtype)

    @pl.when(i >= nu_ref[0])
    def _():
        o_ref[...] = jnp.zeros(o_ref.shape, o_ref.dtype)


def expert_ffn(xs, w1, w3, w2, layer, tile_expert, n_used):
    p, d = xs.shape
    f = w2.shape[2]
    n_tiles = p // MOE_TM
    grid_spec = pltpu.PrefetchScalarGridSpec(
        num_scalar_prefetch=2,
        grid=(n_tiles,),
        in_specs=[pl.BlockSpec((MOE_TM, d), lambda i, te, nu: (i, 0)),
                  pl.BlockSpec((1, 1, d, f), lambda i, te, nu: (layer, te[i], 0, 0)),
                  pl.BlockSpec((1, 1, d, f), lambda i, te, nu: (layer, te[i], 0, 0)),
                  pl.BlockSpec((1, 1, f, d), lambda i, te, nu: (layer, te[i], 0, 0))],
        out_specs=pl.BlockSpec((MOE_TM, d), lambda i, te, nu: (i, 0)),
        scratch_shapes=[pltpu.VMEM((d, 2 * f), BF16), pltpu.VMEM((f, d), BF16)],
    )
    return pl.pallas_call(
        _expert_kernel,
        grid_spec=grid_spec,
        out_shape=jax.ShapeDtypeStruct((p, d), BF16),
        compiler_params=_cp("arbitrary"),
        name="expert_ffn",
    )(tile_expert, n_used, xs, w1, w3, w2)


def _combine_kernel(plan_ref, next_plan_ref, info_ref, x_ref, ys_ref, lg_ref, lb_ref, xo_ref, xb_ref,
                    local_scr, sem, *, alpha):
    tm = x_ref.shape[0]
    i = pl.program_id(0)
    slot = i % 2

    @pl.when(i == 0)
    def _():
        _block_copies(plan_ref, local_scr.at[slot], ys_ref, sem.at[slot], False, True)

    @pl.when(i + 1 < pl.num_programs(0))
    def _():
        _block_copies(next_plan_ref, local_scr.at[1 - slot], ys_ref, sem.at[1 - slot], False, True)

    _block_copies(plan_ref, local_scr.at[slot], ys_ref, sem.at[slot], False, False)
    last = N_EXPERTS - 1
    used = plan_ref[0, 0, N_EXPERTS + last] + plan_ref[0, 0, last] * ROW_ALIGN
    rowid = lax.broadcasted_iota(I32, (LOCAL_ROWS, 1), 0)
    y = local_scr[slot]
    y = jnp.where(rowid < used, y, jnp.zeros_like(y))
    info = info_ref[...]
    col = lax.broadcasted_iota(I32, (tm, LOCAL_ROWS), 1).astype(F32)
    gate = jnp.where(col == info[:, 4:5], info[:, 2:3],
                     jnp.where(col == info[:, 5:6], info[:, 3:4], 0.0)).astype(BF16)
    moe = jnp.dot(gate, y, preferred_element_type=F32)
    out = _layer_norm(alpha * x_ref[...] + moe, lg_ref[...], lb_ref[...])
    xo_ref[...] = out
    xb_ref[...] = out.astype(BF16)


def combine_ln(plan, info, x, ys, ln_g, ln_b, alpha):
    t, d = x.shape
    tm = min(DISPATCH_TM, t)
    return pl.pallas_call(
        functools.partial(_combine_kernel, alpha=alpha),
        grid=(t // tm,),
        in_specs=[pl.BlockSpec((1, 1, 3 * N_EXPERTS), lambda i: (i, 0, 0), memory_space=pltpu.SMEM),
                  pl.BlockSpec((1, 1, 3 * N_EXPERTS), lambda i: (jnp.minimum(i + 1, t // tm - 1), 0, 0),
                               memory_space=pltpu.SMEM),
                  pl.BlockSpec((tm, LANES), lambda i: (i, 0)),
                  pl.BlockSpec((tm, d), lambda i: (i, 0)),
                  pl.BlockSpec(memory_space=pl.ANY),
                  pl.BlockSpec((1, d), lambda i: (0, 0)),
                  pl.BlockSpec((1, d), lambda i: (0, 0))],
        out_specs=[pl.BlockSpec((tm, d), lambda i: (i, 0)), pl.BlockSpec((tm, d), lambda i: (i, 0))],
        out_shape=[jax.ShapeDtypeStruct((t, d), F32), jax.ShapeDtypeStruct((t, d), BF16)],
        scratch_shapes=[pltpu.VMEM((2, LOCAL_ROWS, d), BF16), pltpu.SemaphoreType.DMA((2,))],
        compiler_params=_cp("arbitrary"),
        name="moe_combine_ln",
    )(plan, plan, info, x, ys, ln_g, ln_b)


def _prep_layer(l, p):
    w_in = p["w_in"][l]
    d = w_in.shape[0]
    o_dq = 0
    o_dkv = o_dq + Q_RANK
    o_z = o_dkv + KV_RANK + ROPE_DIM
    d_inner = SSM_HEADS * SSM_HEAD_DIM
    conv_ch = d_inner + 2 * SSM_GROUPS * D_STATE
    o_xbc = o_z + d_inner
    o_dt = o_xbc + conv_ch
    o_qm = o_dt + SSM_HEADS
    o_g = o_qm + XA_HEADS * XA_HEAD_DIM
    half = ROPE_DIM // 2
    kr0 = o_dkv + KV_RANK
    zeros = lambda n: jnp.zeros((d, n), F32)
    w_small = jnp.concatenate([
        w_in[:, o_dq:o_dq + Q_RANK],
        w_in[:, o_dkv:o_dkv + KV_RANK],
        w_in[:, kr0:kr0 + ROPE_DIM], zeros(LANES - ROPE_DIM),
        w_in[:, kr0 + half:kr0 + ROPE_DIM], w_in[:, kr0:kr0 + half], zeros(LANES - ROPE_DIM),
        w_in[:, o_dt:o_dt + SSM_HEADS], zeros(LANES - SSM_HEADS)], axis=1)
    w_uq = p["w_uq"][l].reshape(Q_RANK, MLA_HEADS, NOPE_DIM + ROPE_DIM)
    wq_nope = w_uq[:, :, :NOPE_DIM].reshape(Q_RANK, -1)
    wq_rope = w_uq[:, :, NOPE_DIM:]
    wq_rope_sw = jnp.concatenate([wq_rope[:, :, half:], wq_rope[:, :, :half]], axis=-1)
    wq = jnp.concatenate([wq_nope, wq_rope.reshape(Q_RANK, -1), wq_rope_sw.reshape(Q_RANK, -1)], axis=1)
    w_ukv = p["w_ukv"][l]
    wuk_t = jnp.transpose(w_ukv[:, :, :NOPE_DIM], (1, 2, 0))
    wuv = jnp.transpose(w_ukv[:, :, NOPE_DIM:], (1, 0, 2))
    pad_heads = lambda v, fill: jnp.concatenate(
        [v.astype(F32), jnp.full((LANES - SSM_HEADS,), fill, F32)]).reshape(1, LANES)
    bf = lambda a: a.astype(BF16)
    return dict(
        w_small=bf(w_small), w_z=bf(w_in[:, o_z:o_z + d_inner]), w_xbc=bf(w_in[:, o_xbc:o_xbc + conv_ch]),
        w_qm=bf(w_in[:, o_qm:o_g]), w_g=bf(w_in[:, o_g:]),
        q_norm=p["q_norm"][l].reshape(1, -1), kv_norm=p["kv_norm"][l].reshape(1, -1),
        wq=bf(wq), wuk_t=bf(wuk_t), wuv=bf(wuv),
        conv_w_half=0.5 * p["conv_w"][l], conv_b_half=0.5 * p["conv_b"][l].reshape(1, -1),
        dt_bias=pad_heads(p["dt_bias"][l], 0.0),
        a_neg=pad_heads(-jnp.exp(p["a_log"][l].astype(F32)), 0.0),
        d_skip=jnp.repeat(p["d_skip"][l].astype(F32), SSM_HEAD_DIM).reshape(1, -1),
        ssm_norm=p["ssm_norm"][l].reshape(1, -1),
        w_mem_kv=bf(p["w_mem_kv"][l]),
        wa=bf(p["w_proj_a"][l]), wb=bf(p["w_proj_b"][l]), wc=bf(p["w_proj_c"][l]), wo=bf(p["w_out"][l]),
        ln1_g=p["ln1_g"][l].reshape(1, -1), ln1_b=p["ln1_b"][l].reshape(1, -1),
        ln2_g=p["ln2_g"][l].reshape(1, -1), ln2_b=p["ln2_b"][l].reshape(1, -1),
    )


def _dispatch_plan(tinfo, counts, t):
    n_tok_tiles = tinfo.shape[0]
    n8 = tinfo[:, 0, :N_EXPERTS].astype(I32)
    block_start = tinfo[:, 1, :N_EXPERTS].astype(I32)
    carry = tinfo[:, 2, :N_EXPERTS].astype(I32)
    rows = counts[0, :N_EXPERTS].astype(I32)
    tiles = (rows + MOE_TM - 1) // MOE_TM
    tile_end = jnp.cumsum(tiles)
    row_start = (tile_end - tiles) * MOE_TM
    plan = jnp.concatenate([n8 // ROW_ALIGN, block_start, row_start[None, :] + carry], axis=1)
    max_rows = TOP_K * t + N_EXPERTS * (ROW_ALIGN - 1) * n_tok_tiles
    n_tiles = -(-max_rows // MOE_TM) + N_EXPERTS
    tile_ids = jnp.arange(n_tiles, dtype=I32)
    tile_expert = jnp.minimum(jnp.sum(tile_ids[:, None] >= tile_end[None, :], axis=1), N_EXPERTS - 1)
    n_used = tile_end[-1:].astype(I32)
    return plan.reshape(n_tok_tiles, 1, 3 * N_EXPERTS), tile_expert.astype(I32), n_used, n_tiles * MOE_TM


def kernel(x, mem, positions, w_in, q_norm, w_uq, kv_norm, w_ukv, w_proj_a, conv_w, conv_b, dt_bias, a_log,
           d_skip, ssm_norm, w_proj_b, w_mem_kv, w_proj_c, w_out, ln1_g, ln1_b, router_w, router_bias,
           exp_w1, exp_w3, exp_w2, ln2_g, ln2_b):
    params = dict(w_in=w_in, q_norm=q_norm, w_uq=w_uq, kv_norm=kv_norm, w_ukv=w_ukv, w_proj_a=w_proj_a,
                  conv_w=conv_w, conv_b=conv_b, dt_bias=dt_bias, a_log=a_log, d_skip=d_skip,
                  ssm_norm=ssm_norm, w_proj_b=w_proj_b, w_mem_kv=w_mem_kv, w_proj_c=w_proj_c, w_out=w_out,
                  ln1_g=ln1_g, ln1_b=ln1_b, exp_w1=exp_w1, exp_w3=exp_w3, exp_w2=exp_w2,
                  ln2_g=ln2_g, ln2_b=ln2_b)
    bsz, s, d = x.shape
    t = bsz * s
    depth = w_in.shape[0]
    alpha = float((2 * depth) ** 0.25)
    n_mem = mem.shape[1]

    inv = ROPE_THETA ** (-jnp.arange(0, ROPE_DIM, 2, dtype=F32) / ROPE_DIM)
    ang = positions.astype(F32)[..., None] * inv
    cos, sin = jnp.cos(ang), jnp.sin(ang)
    cosq = jnp.tile(jnp.concatenate([cos, cos], axis=-1), (1, 1, LANES // ROPE_DIM)).reshape(t, LANES)
    sinq = jnp.tile(jnp.concatenate([-sin, sin], axis=-1), (1, 1, LANES // ROPE_DIM)).reshape(t, LANES)

    rw = jnp.concatenate([router_w.astype(F32), jnp.zeros((d, LANES - N_EXPERTS), F32)], axis=1)
    rw_hi = rw.astype(BF16)
    rw_lo = (rw - rw_hi.astype(F32)).astype(BF16)
    rbt = jnp.broadcast_to(router_bias.astype(F32)[:, None], (N_EXPERTS, LANES))
    mem_b = mem.reshape(bsz * n_mem, d).astype(BF16)

    xf = x.reshape(t, d).astype(F32)
    xb = xf.astype(BF16)
    for l in range(depth):
        w = _prep_layer(l, params)
        ha = matmul(xb, w["w_small"], F32, "proj_small")
        z_act = matmul(xb, w["w_z"], BF16, "proj_z_silu", silu=True)
        xbc_c = proj_conv_silu(xb, w["w_xbc"], w["conv_w_half"], w["conv_b_half"], s).reshape(bsz, s, -1)
        qm = matmul(xb, w["w_qm"], BF16, "proj_qmem")
        g = matmul(xb, w["w_g"], BF16, "proj_gate")

        ql, qr, ck, kr = mla_prep(ha, cosq, sinq, w["q_norm"], w["kv_norm"], w["wq"], w["wuk_t"],
                                  min(ATT_TQ, s))
        o_lat = mla_attention(ql, qr, ck.reshape(bsz, s, KV_RANK), kr.reshape(bsz, s, ROPE_DIM))

        yn = ssd(xbc_c, z_act.reshape(bsz, s, -1), ha.reshape(bsz, s, HA_W), w["dt_bias"], w["a_neg"],
                 w["d_skip"], w["ssm_norm"]).reshape(t, -1)

        kv = matmul(mem_b, w["w_mem_kv"], BF16, "proj_memkv").reshape(bsz, n_mem, -1)
        cm = mem_attention(qm.reshape(bsz, s, -1), kv).reshape(t, -1)

        x1 = merge_ln(o_lat, yn, cm, g, xf, w["wuv"], w["wa"], w["wb"], w["wc"], w["wo"],
                           w["ln1_g"], w["ln1_b"], alpha)

        info, info_t, tinfo, counts = route(x1, rw_hi, rw_lo, rbt)
        plan, tile_expert, n_used, n_rows = _dispatch_plan(tinfo, counts, t)
        xs = dispatch_rows(x1, info_t, plan, n_rows)
        ys = expert_ffn(xs, exp_w1, exp_w3, exp_w2, l, tile_expert, n_used)
        xf, xb = combine_ln(plan, info, x1, ys, w["ln2_g"], w["ln2_b"], alpha)
    return xf.reshape(bsz, s, d)
```
